```python
import math
import jax, jax.numpy as jnp
from jax import lax
import numpy as np

D_MODEL = 1024
BATCH = 8
SEQ = 4096
DEPTH = 2

GRID_W = 64
CTX_LEN = 256
N_EVEN = (DEPTH + 1) // 2
N_ODD = DEPTH // 2
N_MOD = 6
EPS = 1e-6
CONV_WIDTH = 5
RWKV_HEADS = 8
RWKV_HEAD_DIM = 64
RWKV_WIDTH = RWKV_HEADS * RWKV_HEAD_DIM
RWKV_DECAY_RANK = 64
RWKV_A_RANK = 64
RWKV_GATE_RANK = 128
RWKV_GN_EPS = 64e-5
RWKV_PROJ = 3 * RWKV_WIDTH + 2 * RWKV_DECAY_RANK + 2 * RWKV_A_RANK + RWKV_GATE_RANK
HGRN_HEADS = 4
HGRN_DK = 128
HGRN_DV = 128
HGRN_KW = HGRN_HEADS * HGRN_DK
HGRN_VW = HGRN_HEADS * HGRN_DV
HGRN_CHUNK = 32
HGRN_PROJ = HGRN_KW + 2 * HGRN_KW + 2 * HGRN_VW
EVEN_PROJ = RWKV_PROJ + HGRN_PROJ
EVEN_OUT = RWKV_WIDTH + HGRN_VW
SSD_HEADS = 16
SSD_HEAD_DIM = 64
SSD_INNER = SSD_HEADS * SSD_HEAD_DIM
SSD_GROUPS = 2
SSD_STATE = 128
SSD_CHUNK = 64
SSD_CONV_CH = SSD_INNER + 2 * SSD_GROUPS * SSD_STATE
SSD_PROJ = SSD_INNER + SSD_CONV_CH + 2 * SSD_HEADS
MLSTM_HEADS = 4
MLSTM_DQK = 128
MLSTM_DV = 256
MLSTM_QK_W = MLSTM_HEADS * MLSTM_DQK
MLSTM_V_W = MLSTM_HEADS * MLSTM_DV
MLSTM_CHUNK = 64
MLSTM_M_INIT = -1e30
MLSTM_PROJ = 2 * MLSTM_QK_W + 2 * MLSTM_V_W + 4 * MLSTM_HEADS
ODD_PROJ = SSD_PROJ + MLSTM_PROJ
ODD_OUT = SSD_INNER + MLSTM_V_W
N_EXPERTS = 32
TOP_K = 4
D_FF_EXPERT = 1024
SWIGLU_LIMIT = 7.0
SWIGLU_ALPHA = 1.702
MOE_BLOCK = 256

kernel_name = 'bidir_hybrid_recurrent_moe_dit'

F32 = jnp.float32


def _split(t, sizes):
    return jnp.split(t, np.cumsum(sizes)[:-1].tolist(), axis=-1)


def rms_norm(x, gain):
    xf = x.astype(F32)
    y = xf * lax.rsqrt(jnp.mean(xf * xf, axis=-1, keepdims=True) + EPS)
    return (y * gain.astype(F32)).astype(x.dtype)


def head_layernorm(o, w, b, eps):
    mu = jnp.mean(o, axis=-1, keepdims=True)
    var = jnp.mean(jnp.square(o - mu), axis=-1, keepdims=True)
    return (o - mu) * lax.rsqrt(var + eps) * w + b


def centred_shift(p):
    pp = jnp.pad(p, ((0, 0), (1, 1), (0, 0)))
    return 0.5 * (pp[:, :-2] + pp[:, 2:])


def depthwise_conv_centred(u, w, b):
    ch = u.shape[-1]
    pad = w.shape[0] // 2
    y = lax.conv_general_dilated(u, w[:, None, :].astype(u.dtype), (1,), [(pad, pad)],
                                 dimension_numbers=('NWC', 'WIO', 'NWC'), feature_group_count=ch)
    return y + b


def to_col_major(t, rows):
    b, s, ch = t.shape
    return t.reshape(b, rows, GRID_W, ch).transpose(0, 2, 1, 3).reshape(b, s, ch)


def to_row_major(t, rows):
    b, s, ch = t.shape
    return t.reshape(b, GRID_W, rows, ch).transpose(0, 2, 1, 3).reshape(b, s, ch)


def dir_stack(tc, tl, per_dir):
    if per_dir:
        cf, cb, lf, lb = tc[:, :, 0], tc[:, :, 1], tl[:, :, 0], tl[:, :, 1]
    else:
        cf, cb, lf, lb = tc, tc, tl, tl
    fwd = jnp.concatenate([cf, lf], axis=1)
    bwd = jnp.concatenate([jnp.flip(cb, 1), jnp.flip(lb, 1)], axis=1)
    return jnp.stack([fwd, bwd], axis=0)


def dir_merge(y, lc, need_ctx):
    yl = y[0, :, lc:] + jnp.flip(y[1, :, lc:], 1)
    yc = (y[0, :, :lc] + jnp.flip(y[1, :, :lc], 1)) if need_ctx else None
    return yc, yl


def rwkv7_scan(r, w, k, v, kk, b):
    xs = tuple(jnp.moveaxis(t, 2, 0) for t in (r, w, k, v, kk, b))
    s0 = jnp.zeros(r.shape[:2] + r.shape[3:] + (r.shape[-1],), r.dtype)

    def step(S, inp):
        r_t, w_t, k_t, v_t, kk_t, b_t = inp
        sa = jnp.einsum('zbhvk,zbhk->zbhv', S, kk_t)
        S = S * w_t[..., None, :] - sa[..., :, None] * b_t[..., None, :] + v_t[..., :, None] * k_t[..., None, :]
        return S, jnp.einsum('zbhvk,zbhk->zbhv', S, r_t)

    _, o = lax.scan(step, s0, xs)
    return jnp.moveaxis(o, 0, 2)


def gla_chunked(q, k, v, log_f, chunk):
    z, L, h, dk = q.shape
    dv = v.shape[-1]
    n = L // chunk
    q, k, log_f = (t.reshape(z, n, chunk, h, dk) for t in (q, k, log_f))
    v = v.reshape(z, n, chunk, h, dv)
    b = jnp.cumsum(log_f, axis=2)
    q_in = q * jnp.exp(b)
    k_in = k * jnp.exp(-b)
    lower = jnp.tril(jnp.ones((chunk, chunk), bool))[:, :, None]
    A = jnp.where(lower, jnp.einsum('zcthd,zcshd->zctsh', q_in, k_in), 0.0)
    o_intra = jnp.einsum('zctsh,zcshv->zcthv', A, v)
    b_last = b[:, :, -1:]
    k_end = k * jnp.exp(b_last - b)
    chunk_decay = jnp.exp(b_last[:, :, 0])

    def step(S, inp):
        q_c, k_c, v_c, d_c = inp
        o_c = jnp.einsum('zthd,zhdv->zthv', q_c, S)
        S = d_c[..., None] * S + jnp.einsum('zshd,zshv->zhdv', k_c, v_c)
        return S, o_c

    s0 = jnp.zeros((z, h, dk, dv), q.dtype)
    _, o_inter = lax.scan(step, s0, tuple(jnp.moveaxis(t, 1, 0) for t in (q_in, k_end, v, chunk_decay)))
    return (o_intra + jnp.moveaxis(o_inter, 0, 1)).reshape(z, L, h, dv)


def ssd_chunked(x, log_a, Bm, Cm, chunk):
    z, L, g, hg, p = x.shape
    nst = Bm.shape[-1]
    n = L // chunk
    x = x.reshape(z, n, chunk, g, hg, p)
    log_a = log_a.reshape(z, n, chunk, g, hg)
    Bm = Bm.reshape(z, n, chunk, g, nst)
    Cm = Cm.reshape(z, n, chunk, g, nst)
    a_cum = jnp.cumsum(log_a, axis=2)
    lower = jnp.tril(jnp.ones((chunk, chunk), bool))[:, :, None, None]
    seg = a_cum[:, :, :, None] - a_cum[:, :, None, :]
    decay = jnp.exp(jnp.where(lower, seg, -jnp.inf))
    cb = jnp.einsum('zctgn,zcsgn->zctsg', Cm, Bm)
    y_intra = jnp.einsum('zctsg,zctsgh,zcsghp->zctghp', cb, decay, x)
    decay_end = jnp.exp(a_cum[:, :, -1:] - a_cum)
    decay_in = jnp.exp(a_cum)
    chunk_decay = jnp.exp(a_cum[:, :, -1])

    def step(hs, inp):
        c_c, b_c, x_c, din, dend, dch = inp
        y_c = jnp.einsum('ztgn,zghpn,ztgh->ztghp', c_c, hs, din)
        hs = dch[..., None, None] * hs + jnp.einsum('zsgn,zsgh,zsghp->zghpn', b_c, dend, x_c)
        return hs, y_c

    h0 = jnp.zeros((z, g, hg, p, nst), x.dtype)
    _, y_inter = lax.scan(step, h0, tuple(jnp.moveaxis(t, 1, 0) for t in (Cm, Bm, x, decay_in, decay_end, chunk_decay)))
    return (y_intra + jnp.moveaxis(y_inter, 0, 1)).reshape(z, L, g, hg, p)


def mlstm_chunked(q, k, v, log_i, log_f, chunk):
    z, L, h, dk = q.shape
    dv = v.shape[-1]
    n = L // chunk
    q = q.reshape(z, n, chunk, h, dk)
    k = k.reshape(z, n, chunk, h, dk)
    v = v.reshape(z, n, chunk, h, dv)
    log_i = log_i.reshape(z, n, chunk, h)
    F = jnp.cumsum(log_f.reshape(z, n, chunk, h), axis=2)
    F_last = F[:, :, -1]
    logw_end = F_last[:, :, None] - F + log_i
    m_end = jnp.max(logw_end, axis=2)
    w_end = jnp.exp(logw_end - m_end[:, :, None])
    lower = jnp.tril(jnp.ones((chunk, chunk), bool))[:, :, None]

    def step(carry, inp):
        Cs, ns, m = carry
        q_c, k_c, v_c, F_c, li_c, Fl_c, me_c, we_c = inp
        logw = jnp.where(lower, F_c[:, :, None] - F_c[:, None, :] + li_c[:, None, :], -jnp.inf)
        m_t = jnp.maximum(jnp.max(logw, axis=2), F_c + m[:, None])
        w_inter = jnp.exp(F_c + m[:, None] - m_t)
        scores = jnp.einsum('zthd,zshd->ztsh', q_c, k_c) * jnp.exp(logw - m_t[:, :, None])
        num = jnp.einsum('ztsh,zshv->zthv', scores, v_c) + w_inter[..., None] * jnp.einsum('zthd,zhdv->zthv', q_c, Cs)
        den = jnp.sum(scores, axis=2) + w_inter * jnp.einsum('zthd,zhd->zth', q_c, ns)
        h_c = num / jnp.maximum(jnp.abs(den), jnp.exp(-m_t))[..., None]
        m_new = jnp.maximum(Fl_c + m, me_c)
        s_old = jnp.exp(Fl_c + m - m_new)
        s_loc = jnp.exp(me_c - m_new)
        Cs = s_old[..., None, None] * Cs + s_loc[..., None, None] * jnp.einsum('zsh,zshd,zshv->zhdv', we_c, k_c, v_c)
        ns = s_old[..., None] * ns + s_loc[..., None] * jnp.einsum('zsh,zshd->zhd', we_c, k_c)
        return (Cs, ns, m_new), h_c

    init = (jnp.zeros((z, h, dk, dv), q.dtype), jnp.zeros((z, h, dk), q.dtype), jnp.full((z, h), MLSTM_M_INIT, q.dtype))
    _, hs = lax.scan(step, init, tuple(jnp.moveaxis(t, 1, 0) for t in (q, k, v, F, log_i, F_last, m_end, w_end)))
    return jnp.moveaxis(hs, 0, 1).reshape(z, L, h, dv)


def rwkv7_group(pc, pl, mu, w0, w2, a0, a2, g2, k_k, k_a, r_k, ln_w, ln_b, need_ctx):
    H, N = RWKV_HEADS, RWKV_HEAD_DIM
    kk_gain = k_k.astype(F32).reshape(H, N)
    ka_gain = k_a.astype(F32).reshape(H, N)

    def prep(p):
        bsz, L = p.shape[:2]
        p = p + (centred_shift(p) - p) * mu
        r, k, v, wd, ad, gd = _split(p, [RWKV_WIDTH] * 3 + [2 * RWKV_DECAY_RANK, 2 * RWKV_A_RANK, RWKV_GATE_RANK])
        w_pre = w0 + jnp.einsum('bldr,drc->bldc', jnp.tanh(wd.reshape(bsz, L, 2, RWKV_DECAY_RANK)), w2)
        decay = jnp.exp(-jnp.exp(-jax.nn.softplus(-w_pre.astype(F32)) - 0.5)).reshape(bsz, L, 2, H, N)
        a = jax.nn.sigmoid((a0 + jnp.einsum('bldr,drc->bldc', ad.reshape(bsz, L, 2, RWKV_A_RANK), a2)).astype(F32))
        a = a.reshape(bsz, L, 2, H, N)
        g = jax.nn.sigmoid(gd) @ g2
        r, k, v = (t.astype(F32).reshape(bsz, L, H, N) for t in (r, k, v))
        kk = k * kk_gain
        kk = kk * lax.rsqrt(jnp.sum(kk * kk, axis=-1, keepdims=True) + 1e-12)
        k_mod = k[:, :, None] * (1.0 + (a - 1.0) * ka_gain)
        return r, v, kk, decay, k_mod, kk[:, :, None] * a, g

    rc, vc, kkc, dc, kmc, bc, gc = prep(pc)
    rl, vl, kkl, dl, kml, bl, gl = prep(pl)
    o = rwkv7_scan(dir_stack(rc, rl, False), dir_stack(dc, dl, True), dir_stack(kmc, kml, True),
                   dir_stack(vc, vl, False), dir_stack(kkc, kkl, False), dir_stack(bc, bl, True))
    oc, ol = dir_merge(o, pc.shape[1], need_ctx)

    def finish(o, r, v, k_mod, g):
        o = head_layernorm(o, ln_w.astype(F32).reshape(H, N), ln_b.astype(F32).reshape(H, N), RWKV_GN_EPS)
        bonus = jnp.sum(r[:, :, None] * k_mod * r_k.astype(F32).reshape(H, N), axis=(2, 4))
        o = (o + bonus[..., None] * v).reshape(o.shape[0], o.shape[1], RWKV_WIDTH)
        return (o * g.astype(F32)).astype(g.dtype)

    yl = finish(ol, rl, vl, kml, gl)
    yc = finish(oc, rc, vc, kmc, gc) if need_ctx else None
    return yc, yl


def hgrn2_group(pc, pl, lb, norm_w, need_ctx):
    H, DK, DV = HGRN_HEADS, HGRN_DK, HGRN_DV
    lb = lb.astype(F32)

    def prep(p):
        bsz, L = p.shape[:2]
        q, f, i, g = _split(p, [HGRN_KW, 2 * HGRN_KW, HGRN_VW, HGRN_VW])
        q = (jax.nn.silu(q.astype(F32)) * DK ** -0.5).reshape(bsz, L, H, DK)
        f = f.astype(F32).reshape(bsz, L, 2, HGRN_KW)
        log_f = jnp.log(lb + (1.0 - lb) * jax.nn.sigmoid(f)).reshape(bsz, L, 2, H, DK)
        k = ((1.0 - lb) * jax.nn.sigmoid(-f)).reshape(bsz, L, 2, H, DK)
        return q, k, log_f, i.astype(F32).reshape(bsz, L, H, DV), g

    qc, kc, lfc, vc, gc = prep(pc)
    ql, kl, lfl, vl, gl = prep(pl)
    bsz, lc = pl.shape[0], pc.shape[1]
    L = lc + pl.shape[1]
    q = dir_stack(qc, ql, False).reshape(2 * bsz, L, H, DK)
    k = dir_stack(kc, kl, True).reshape(2 * bsz, L, H, DK)
    log_f = dir_stack(lfc, lfl, True).reshape(2 * bsz, L, H, DK)
    v = dir_stack(vc, vl, False).reshape(2 * bsz, L, H, DV)
    o = gla_chunked(q, k, v, log_f, HGRN_CHUNK).reshape(2, bsz, L, H, DV)
    oc, ol = dir_merge(o, lc, need_ctx)

    def finish(o, g):
        o = rms_norm(o, norm_w.reshape(H, DV)).reshape(o.shape[0], o.shape[1], HGRN_VW)
        return (o * jax.nn.silu(g.astype(F32))).astype(g.dtype)

    yl = finish(ol, gl)
    yc = finish(oc, gc) if need_ctx else None
    return yc, yl


def ssd_group(pc, pl, conv_w, conv_b, dt_bias, a_log, d_skip, norm_w, need_ctx):
    H, P, G, N = SSD_HEADS, SSD_HEAD_DIM, SSD_GROUPS, SSD_STATE
    hg = H // G

    def prep(p):
        bsz, L = p.shape[:2]
        zg, xbc, dt = _split(p, [SSD_INNER, SSD_CONV_CH, 2 * H])
        xbc = jax.nn.silu(depthwise_conv_centred(xbc, conv_w, conv_b))
        x, Bm, Cm = _split(xbc, [SSD_INNER, G * N, G * N])
        dt = jax.nn.softplus((dt.reshape(bsz, L, 2, H) + dt_bias).astype(F32))
        log_a = dt * -jnp.exp(a_log.astype(F32))
        x = x.astype(F32).reshape(bsz, L, H, P)
        x_dt = x[:, :, None] * dt[..., None]
        return zg, x, x_dt, log_a, Bm.astype(F32).reshape(bsz, L, G, N), Cm.astype(F32).reshape(bsz, L, G, N)

    zc, xc, xdc, lac, Bc, Cc = prep(pc)
    zl, xl, xdl, lal, Bl, Cl = prep(pl)
    bsz, lc = pl.shape[0], pc.shape[1]
    L = lc + pl.shape[1]
    y = ssd_chunked(dir_stack(xdc, xdl, True).reshape(2 * bsz, L, G, hg, P),
                    dir_stack(lac, lal, True).reshape(2 * bsz, L, G, hg),
                    dir_stack(Bc, Bl, False).reshape(2 * bsz, L, G, N),
                    dir_stack(Cc, Cl, False).reshape(2 * bsz, L, G, N), SSD_CHUNK)
    yc, yl = dir_merge(y.reshape(2, bsz, L, H, P), lc, need_ctx)

    def finish(y, x, zg):
        b_, l_ = y.shape[:2]
        y = (y + d_skip.astype(F32)[:, None] * x).reshape(b_, l_, SSD_INNER) * jax.nn.silu(zg.astype(F32))
        y = rms_norm(y.reshape(b_, l_, G, SSD_INNER // G), norm_w.reshape(G, SSD_INNER // G))
        return y.reshape(b_, l_, SSD_INNER).astype(zg.dtype)

    out_l = finish(yl, xl, zl)
    out_c = finish(yc, xc, zc) if need_ctx else None
    return out_c, out_l


def mlstm_group(pc, pl, conv_w, conv_b, i_bias, f_bias, norm_w, need_ctx):
    H, DK, DV = MLSTM_HEADS, MLSTM_DQK, MLSTM_DV

    def prep(p):
        bsz, L = p.shape[:2]
        qk, v, og, ig, fg = _split(p, [2 * MLSTM_QK_W, MLSTM_V_W, MLSTM_V_W, 2 * H, 2 * H])
        qk = jax.nn.silu(depthwise_conv_centred(qk, conv_w, conv_b)).astype(F32)
        q, k = jnp.split(qk, 2, axis=-1)
        q = q.reshape(bsz, L, H, DK)
        k = k.reshape(bsz, L, H, DK) * DK ** -0.5
        v = v.astype(F32).reshape(bsz, L, H, DV)
        log_i = ig.astype(F32).reshape(bsz, L, 2, H) + i_bias.astype(F32)
        log_f = jax.nn.log_sigmoid(fg.astype(F32).reshape(bsz, L, 2, H) + f_bias.astype(F32))
        return q, k, v, og, log_i, log_f

    qc, kc, vc, oc_g, lic, lfc = prep(pc)
    ql, kl, vl, ol_g, lil, lfl = prep(pl)
    bsz, lc = pl.shape[0], pc.shape[1]
    L = lc + pl.shape[1]
    h = mlstm_chunked(dir_stack(qc, ql, False).reshape(2 * bsz, L, H, DK),
                      dir_stack(kc, kl, False).reshape(2 * bsz, L, H, DK),
                      dir_stack(vc, vl, False).reshape(2 * bsz, L, H, DV),
                      dir_stack(lic, lil, True).reshape(2 * bsz, L, H),
                      dir_stack(lfc, lfl, True).reshape(2 * bsz, L, H), MLSTM_CHUNK)
    hc, hl = dir_merge(h.reshape(2, bsz, L, H, DV), lc, need_ctx)

    def finish(h, og):
        h = rms_norm(h, norm_w.reshape(H, DV)).reshape(h.shape[0], h.shape[1], MLSTM_V_W)
        return (h * jax.nn.sigmoid(og.astype(F32))).astype(og.dtype)

    yl = finish(hl, ol_g)
    yc = finish(hc, oc_g) if need_ctx else None
    return yc, yl


def moe_ffn(h, router_w, router_b, w_gate, b_gate, w_up, b_up, w_down, b_down):
    T, D = h.shape
    logits = (h @ router_w).astype(F32) + router_b.astype(F32)
    top_val, top_idx = lax.top_k(logits, TOP_K)
    probs = jax.nn.softmax(top_val, axis=-1)
    TK = T * TOP_K
    flat_e = top_idx.reshape(-1)
    order = jnp.argsort(flat_e)
    sorted_e = flat_e[order]
    sorted_tok = order // TOP_K
    sorted_w = probs.reshape(-1)[order].astype(h.dtype)
    counts = jnp.bincount(flat_e, length=N_EXPERTS)
    padded = (counts + MOE_BLOCK - 1) // MOE_BLOCK * MOE_BLOCK
    start = jnp.cumsum(counts) - counts
    ends = jnp.cumsum(padded)
    pstart = ends - padded
    dest = pstart[sorted_e] + jnp.arange(TK) - start[sorted_e]
    n_blocks = -(-TK // MOE_BLOCK) + N_EXPERTS
    block_expert = jnp.clip(jnp.searchsorted(ends, jnp.arange(n_blocks) * MOE_BLOCK, side='right'), 0, N_EXPERTS - 1)
    x_buf = jnp.zeros((n_blocks * MOE_BLOCK, D), h.dtype).at[dest].set(h[sorted_tok])

    def expert_block(args):
        xb, e = args
        g = xb @ w_gate[e] + b_gate[e]
        u = xb @ w_up[e] + b_up[e]
        g = jnp.minimum(g, SWIGLU_LIMIT)
        u = jnp.clip(u, -SWIGLU_LIMIT, SWIGLU_LIMIT)
        return (g * jax.nn.sigmoid(SWIGLU_ALPHA * g) * (u + 1.0)) @ w_down[e] + b_down[e]

    y_blocks = lax.map(expert_block, (x_buf.reshape(n_blocks, MOE_BLOCK, D), block_expert))
    y_sorted = y_blocks.reshape(n_blocks * MOE_BLOCK, D)[dest]
    return jnp.zeros((T, D), h.dtype).at[sorted_tok].add(y_sorted * sorted_w[:, None])


def setup_inputs(seed: int = 0) -> dict:
    key = jax.random.key(seed)
    keys = iter(jax.random.split(key, 64))
    D = D_MODEL

    def normal(shape, scale):
        return jax.random.normal(next(keys), shape, F32) * scale

    def gain(shape):
        return 1.0 + normal(shape, 0.05)

    def uniform(shape, lo, hi):
        return jax.random.uniform(next(keys), shape, F32, lo, hi)

    dt = jnp.exp(uniform((N_ODD, 2, SSD_HEADS), math.log(1e-3), math.log(1e-1)))
    return {
        'x': normal((BATCH, SEQ, D), 1.0),
        'c': normal((BATCH, D), 1.0),
        'ctx': normal((BATCH, CTX_LEN, D), 1.0),
        'c_ctx': normal((D,), 1.0),
        'ada_w': normal((DEPTH, D, N_MOD * D), 0.5 * D ** -0.5),
        'ada_b': normal((DEPTH, N_MOD * D), 0.02),
        'norm1_w': gain((DEPTH, D)),
        'norm2_w': gain((DEPTH, D)),
        'even_w_in': normal((N_EVEN, D, EVEN_PROJ), D ** -0.5),
        'even_w_out': normal((N_EVEN, EVEN_OUT, D), EVEN_OUT ** -0.5),
        'rwkv_mu': uniform((N_EVEN, RWKV_PROJ), 0.0, 1.0),
        'rwkv_w0': uniform((N_EVEN, 2, RWKV_WIDTH), -6.0, 0.0),
        'rwkv_w2': normal((N_EVEN, 2, RWKV_DECAY_RANK, RWKV_WIDTH), 0.5 * RWKV_DECAY_RANK ** -0.5),
        'rwkv_a0': normal((N_EVEN, 2, RWKV_WIDTH), 0.1),
        'rwkv_a2': normal((N_EVEN, 2, RWKV_A_RANK, RWKV_WIDTH), 0.5 * RWKV_A_RANK ** -0.5),
        'rwkv_g2': normal((N_EVEN, RWKV_GATE_RANK, RWKV_WIDTH), RWKV_GATE_RANK ** -0.5),
        'rwkv_k_k': 0.85 + normal((N_EVEN, RWKV_WIDTH), 0.05),
        'rwkv_k_a': gain((N_EVEN, RWKV_WIDTH)),
        'rwkv_r_k': normal((N_EVEN, RWKV_WIDTH), 0.1),
        'rwkv_ln_w': gain((N_EVEN, RWKV_WIDTH)),
        'rwkv_ln_b': normal((N_EVEN, RWKV_WIDTH), 0.02),
        'hgrn_lower_bounds': normal((DEPTH + 1, HGRN_KW), 0.1),
        'hgrn_norm_w': gain((N_EVEN, HGRN_VW)),
        'odd_w_in': normal((N_ODD, D, ODD_PROJ), D ** -0.5),
        'odd_w_out': normal((N_ODD, ODD_OUT, D), ODD_OUT ** -0.5),
        'ssd_conv_w': normal((N_ODD, CONV_WIDTH, SSD_CONV_CH), CONV_WIDTH ** -0.5),
        'ssd_conv_b': normal((N_ODD, SSD_CONV_CH), 0.02),
        'ssd_dt_bias': dt + jnp.log(-jnp.expm1(-dt)),
        'ssd_a_log': jnp.log(uniform((N_ODD, 2, SSD_HEADS), 1.0, 16.0)),
        'ssd_d': gain((N_ODD, SSD_HEADS)),
        'ssd_norm_w': gain((N_ODD, SSD_INNER)),
        'mlstm_conv_w': normal((N_ODD, CONV_WIDTH, 2 * MLSTM_QK_W), CONV_WIDTH ** -0.5),
        'mlstm_conv_b': normal((N_ODD, 2 * MLSTM_QK_W), 0.02),
        'mlstm_i_bias': normal((N_ODD, 2, MLSTM_HEADS), 0.1),
        'mlstm_f_bias': uniform((N_ODD, 2, MLSTM_HEADS), 3.0, 6.0),
        'mlstm_norm_w': gain((N_ODD, MLSTM_V_W)),
        'router_w': normal((DEPTH, D, N_EXPERTS), D ** -0.5),
        'router_b': normal((DEPTH, N_EXPERTS), 0.01),
        'exp_w_gate': normal((DEPTH, N_EXPERTS, D, D_FF_EXPERT), D ** -0.5),
        'exp_b_gate': normal((DEPTH, N_EXPERTS, D_FF_EXPERT), 0.02),
        'exp_w_up': normal((DEPTH, N_EXPERTS, D, D_FF_EXPERT), D ** -0.5),
        'exp_b_up': normal((DEPTH, N_EXPERTS, D_FF_EXPERT), 0.02),
        'exp_w_down': normal((DEPTH, N_EXPERTS, D_FF_EXPERT, D), D_FF_EXPERT ** -0.5),
        'exp_b_down': normal((DEPTH, N_EXPERTS, D), 0.02),
        'final_norm_w': gain((D,)),
    }


def reference(x, c, ctx, c_ctx, ada_w, ada_b, norm1_w, norm2_w, even_w_in, even_w_out, rwkv_mu, rwkv_w0,
              rwkv_w2, rwkv_a0, rwkv_a2, rwkv_g2, rwkv_k_k, rwkv_k_a, rwkv_r_k, rwkv_ln_w, rwkv_ln_b,
              hgrn_lower_bounds, hgrn_norm_w, odd_w_in, odd_w_out, ssd_conv_w, ssd_conv_b, ssd_dt_bias,
              ssd_a_log, ssd_d, ssd_norm_w, mlstm_conv_w, mlstm_conv_b, mlstm_i_bias, mlstm_f_bias,
              mlstm_norm_w, router_w, router_b, exp_w_gate, exp_b_gate, exp_w_up, exp_b_up, exp_w_down,
              exp_b_down, final_norm_w):
    bsz, seq, _ = x.shape
    rows = seq // GRID_W
    lc = ctx.shape[1]
    lower_bounds = jnp.cumsum(jax.nn.softmax(hgrn_lower_bounds.astype(F32), axis=0), axis=0)
    xl, xc = x, ctx
    for l in range(DEPTH):
        need_ctx = l < DEPTH - 1
        mod = (jax.nn.silu(c) @ ada_w[l] + ada_b[l]).reshape(bsz, N_MOD, 1, D_MODEL)
        mod_c = (jax.nn.silu(c_ctx) @ ada_w[l] + ada_b[l]).reshape(N_MOD, D_MODEL)
        hl = rms_norm(xl, norm1_w[l]) * (1.0 + mod[:, 1]) + mod[:, 0]
        hc = rms_norm(xc, norm1_w[l]) * (1.0 + mod_c[1]) + mod_c[0]
        if l % 2 == 0:
            e = l // 2
            pc_r, pc_h = _split(hc @ even_w_in[e], [RWKV_PROJ, HGRN_PROJ])
            pl_r, pl_h = _split(hl @ even_w_in[e], [RWKV_PROJ, HGRN_PROJ])
            rc, rl = rwkv7_group(pc_r, pl_r, rwkv_mu[e], rwkv_w0[e], rwkv_w2[e], rwkv_a0[e], rwkv_a2[e],
                                 rwkv_g2[e], rwkv_k_k[e], rwkv_k_a[e], rwkv_r_k[e], rwkv_ln_w[e], rwkv_ln_b[e], need_ctx)
            gc_, gl_ = hgrn2_group(pc_h, pl_h, lower_bounds[l], hgrn_norm_w[e], need_ctx)
            yl = jnp.concatenate([rl, gl_], axis=-1) @ even_w_out[e]
            yc = (jnp.concatenate([rc, gc_], axis=-1) @ even_w_out[e]) if need_ctx else None
        else:
            o = l // 2
            hl_cm = to_col_major(hl, rows)
            pc_s, pc_m = _split(hc @ odd_w_in[o], [SSD_PROJ, MLSTM_PROJ])
            pl_s, pl_m = _split(hl_cm @ odd_w_in[o], [SSD_PROJ, MLSTM_PROJ])
            sc, sl = ssd_group(pc_s, pl_s, ssd_conv_w[o], ssd_conv_b[o], ssd_dt_bias[o], ssd_a_log[o], ssd_d[o],
                               ssd_norm_w[o], need_ctx)
            mc, ml = mlstm_group(pc_m, pl_m, mlstm_conv_w[o], mlstm_conv_b[o], mlstm_i_bias[o], mlstm_f_bias[o],
                                 mlstm_norm_w[o], need_ctx)
            yl = to_row_major(jnp.concatenate([sl, ml], axis=-1) @ odd_w_out[o], rows)
            yc = (jnp.concatenate([sc, mc], axis=-1) @ odd_w_out[o]) if need_ctx else None
        xl = xl + mod[:, 2] * yl
        hl = rms_norm(xl, norm2_w[l]) * (1.0 + mod[:, 4]) + mod[:, 3]
        moe_args = (router_w[l], router_b[l], exp_w_gate[l], exp_b_gate[l], exp_w_up[l], exp_b_up[l],
                    exp_w_down[l], exp_b_down[l])
        if need_ctx:
            xc = xc + mod_c[2] * yc
            hc = rms_norm(xc, norm2_w[l]) * (1.0 + mod_c[4]) + mod_c[3]
            tokens = jnp.concatenate([hc.reshape(-1, D_MODEL), hl.reshape(-1, D_MODEL)], axis=0)
            f_out = moe_ffn(tokens, *moe_args)
            xc = xc + mod_c[5] * f_out[:bsz * lc].reshape(bsz, lc, D_MODEL)
            xl = xl + mod[:, 5] * f_out[bsz * lc:].reshape(bsz, seq, D_MODEL)
        else:
            f_out = moe_ffn(hl.reshape(-1, D_MODEL), *moe_args)
            xl = xl + mod[:, 5] * f_out.reshape(bsz, seq, D_MODEL)
    return rms_norm(xl, final_norm_w)
```

```python
import functools
import math

import jax
import jax.numpy as jnp
from jax import lax
from jax.experimental import pallas as pl
from jax.experimental.pallas import tpu as pltpu

F32 = jnp.float32
BF16 = jnp.bfloat16
I32 = jnp.int32
HI = lax.Precision.HIGHEST

D_MODEL = 1024
GRID_W = 64
N_MOD = 6
EPS = 1e-6
RW_H, RW_N = 8, 64
RW_W = RW_H * RW_N
RW_RANK = 64
RW_GATE_RANK = 128
RW_GN_EPS = 64e-5
RW_PROJ = 3 * RW_W + 4 * RW_RANK + RW_GATE_RANK
RW_CHUNK = 64
HG_H, HG_DK, HG_DV = 4, 128, 128
HG_W = HG_H * HG_DK
HG_CHUNK = 32
HG_PROJ = 5 * HG_W
SSD_H, SSD_P, SSD_G, SSD_N = 16, 64, 2, 128
SSD_INNER = SSD_H * SSD_P
SSD_CONV_CH = SSD_INNER + 2 * SSD_G * SSD_N
SSD_CHUNK = 64
ML_H, ML_DK, ML_DV = 4, 128, 256
ML_QK = ML_H * ML_DK
ML_VW = ML_H * ML_DV
ML_CHUNK = 64
ML_M_INIT = -1e30
CONV_K = 5
N_EXP = 32
TOP_K = 4
D_FF = 1024
SWIGLU_LIMIT = 7.0
SWIGLU_ALPHA = 1.702
MOE_BLOCK = 256

TOK_TILE = 256
SCAN_BLOCK = 256
HALO = 8
SMALL_W = 128
VMEM_LIMIT = 56 * 1024 * 1024


def _cparams(sem):
    return pltpu.CompilerParams(dimension_semantics=sem, vmem_limit_bytes=VMEM_LIMIT)


def _dot(a, b, precision=None):
    return jnp.dot(a, b, preferred_element_type=F32, precision=precision)


def _dot_nt(a, b, precision=None):
    return lax.dot_general(a, b, (((1,), (1,)), ((), ())), preferred_element_type=F32, precision=precision)


def _dot_tn(a, b, precision=None):
    return lax.dot_general(a, b, (((0,), (0,)), ((), ())), preferred_element_type=F32, precision=precision)


def _sigmoid(x):
    return 1.0 / (1.0 + jnp.exp(-x))


def _silu(x):
    return x * _sigmoid(x)


def _softplus(x):
    return jnp.maximum(x, 0.0) + jnp.log(1.0 + jnp.exp(-jnp.abs(x)))


def _dir_masks(z, c):
    ti = lax.broadcasted_iota(I32, (c, c), 0)
    si = lax.broadcasted_iota(I32, (c, c), 1)
    d = (si - ti) * jnp.where(z == 0, 1, -1)
    return d <= 0, d < 0


def _time_block(z, n, ctx_blocks, n_blocks):
    rev = jnp.where(n < ctx_blocks, ctx_blocks - 1 - n, n_blocks - 1 - (n - ctx_blocks))
    return jnp.where(z == 0, n, rev)


def _mod_kernel(c_ref, w_ref, b_ref, o_ref):
    o_ref[...] = _dot(_silu(c_ref[...]), w_ref[...]) + b_ref[...]


def ada_mod(cc, w, b):
    rows, d = cc.shape
    n = w.shape[1]
    tn = 1536
    return pl.pallas_call(
        _mod_kernel,
        out_shape=jax.ShapeDtypeStruct((rows, n), F32),
        grid=(n // tn,),
        in_specs=[pl.BlockSpec((rows, d), lambda j: (0, 0)),
                  pl.BlockSpec((d, tn), lambda j: (0, j)),
                  pl.BlockSpec((1, tn), lambda j: (0, j))],
        out_specs=pl.BlockSpec((rows, tn), lambda j: (0, j)),
        compiler_params=_cparams(("arbitrary",)),
        name="ada_mod",
    )(cc, w, b.reshape(1, n))


def _mod_table(c, c_ctx, w, b):
    bsz = c.shape[0]
    rows = -(-(bsz + 1) // 8) * 8
    cc = jnp.zeros((rows, D_MODEL), F32).at[:bsz].set(c).at[bsz].set(c_ctx)
    m = ada_mod(cc, w, b)
    mod = m[:bsz].reshape(bsz, 1, N_MOD, D_MODEL)
    mod_c = jnp.broadcast_to(m[bsz].reshape(1, 1, N_MOD, D_MODEL), (bsz, 1, N_MOD, D_MODEL))
    return jnp.concatenate([mod_c, mod], axis=1)


def _norm_mod(x, nw, m, shift_idx):
    y = x * lax.rsqrt(jnp.mean(x * x, axis=-1, keepdims=True) + EPS) * nw
    return y * (1.0 + m[shift_idx + 1:shift_idx + 2]) + m[shift_idx:shift_idx + 1]


def _in_proj_kernel(x_ref, mod_ref, nw_ref, *rest, n_out):
    w_refs, o_refs = rest[:n_out], rest[n_out:]
    h = _norm_mod(x_ref[0], nw_ref[...], mod_ref[0, 0], 0).astype(BF16)
    for w_ref, o_ref in zip(w_refs, o_refs):
        o_ref[0] = _dot(h, w_ref[...])


def in_proj(x, modtab, nw, slabs, ctx_len):
    bsz, lt, d = x.shape
    tm = TOK_TILE
    ctx_tiles = ctx_len // tm
    in_specs = [pl.BlockSpec((1, tm, d), lambda b, i: (b, i, 0)),
                pl.BlockSpec((1, 1, N_MOD, d), lambda b, i: (b, (i >= ctx_tiles).astype(I32), 0, 0)),
                pl.BlockSpec((1, d), lambda b, i: (0, 0))]
    in_specs += [pl.BlockSpec(w.shape, lambda b, i: (0, 0)) for w in slabs]
    return pl.pallas_call(
        functools.partial(_in_proj_kernel, n_out=len(slabs)),
        out_shape=[jax.ShapeDtypeStruct((bsz, lt, w.shape[1]), F32) for w in slabs],
        grid=(bsz, lt // tm),
        in_specs=in_specs,
        out_specs=[pl.BlockSpec((1, tm, w.shape[1]), lambda b, i: (b, i, 0)) for w in slabs],
        compiler_params=_cparams(("parallel", "parallel")),
        name="in_proj",
    )(x, modtab, nw.reshape(1, d), *slabs)


def _halo_specs(tm, width, lt):
    per = tm // HALO
    last = lt // HALO - 1
    return [pl.BlockSpec((1, tm, width), lambda b, i: (b, i, 0)),
            pl.BlockSpec((1, HALO, width), lambda b, i: (b, jnp.maximum(i * per - 1, 0), 0)),
            pl.BlockSpec((1, HALO, width), lambda b, i: (b, jnp.minimum((i + 1) * per, last), 0))]


def _segment_edges(i, tm, ctx_len, lt):
    row0 = i * tm
    has_prev = jnp.logical_and(row0 != 0, row0 != ctx_len)
    has_next = jnp.logical_and(row0 + tm != ctx_len, row0 + tm != lt)
    return has_prev, has_next


def _rwkv_prep_kernel(p_ref, pp_ref, pn_ref, mu_ref, w0_ref, w2_ref, a0_ref, a2_ref, g2_ref, kk_ref, ka_ref,
                      rk_ref, bd_ref, r_o, v_o, kk_o, g_o, bv_o, lw_o, km_o, bb_o, *, tm, ctx_len, lt):
    has_prev, has_next = _segment_edges(pl.program_id(1), tm, ctx_len, lt)
    p = p_ref[0]
    prev_row = jnp.where(has_prev, pp_ref[0, HALO - 1:HALO, :], 0.0)
    next_row = jnp.where(has_next, pn_ref[0, 0:1, :], 0.0)
    rid = lax.broadcasted_iota(I32, (tm, 1), 0)
    up = jnp.where(rid == 0, prev_row, pltpu.roll(p, 1, 0))
    dn = jnp.where(rid == tm - 1, next_row, pltpu.roll(p, tm - 1, 0))
    p = p + (0.5 * (up + dn) - p) * mu_ref[...]
    w = RW_W
    r, k, v = p[:, 0:w], p[:, w:2 * w], p[:, 2 * w:3 * w]
    o = 3 * w
    wd = jnp.tanh(p[:, o:o + 2 * RW_RANK])
    ad = p[:, o + 2 * RW_RANK:o + 4 * RW_RANK]
    gd = p[:, o + 4 * RW_RANK:o + 4 * RW_RANK + RW_GATE_RANK]
    bd = bd_ref[...]
    kk = k * kk_ref[...]
    kk = kk * lax.rsqrt(_dot(kk * kk, bd, HI) + 1e-12)
    r_o[0] = r
    v_o[0] = v
    kk_o[0] = kk
    g_o[0] = _dot(_sigmoid(gd), g2_ref[...])
    km_sum = jnp.zeros_like(k)
    for z in range(2):
        sl = slice(z * RW_RANK, (z + 1) * RW_RANK)
        w_pre = w0_ref[z:z + 1, :] + _dot(wd[:, sl], w2_ref[z])
        lw_o[z, 0] = -_sigmoid(w_pre) * math.exp(-0.5)
        a = _sigmoid(a0_ref[z:z + 1, :] + _dot(ad[:, sl], a2_ref[z]))
        km = k * (1.0 + (a - 1.0) * ka_ref[...])
        km_o[z, 0] = km
        bb_o[z, 0] = kk * a
        km_sum = km_sum + km
    bv_o[0] = _dot(r * km_sum * rk_ref[...], bd, HI) * v


def _head_block_ones(width, head):
    i = jnp.arange(width) // head
    return (i[:, None] == i[None, :]).astype(F32)


def rwkv_prep(p, mu, w0, w2, a0, a2, g2, k_k, k_a, r_k, ctx_len):
    bsz, lt, width = p.shape
    tm = TOK_TILE
    w = RW_W
    full = lambda a: pl.BlockSpec(a.shape, lambda b, i: (0,) * a.ndim)
    params = [mu.reshape(1, width), w0, w2, a0, a2, g2, k_k.reshape(1, w), k_a.reshape(1, w), r_k.reshape(1, w),
              _head_block_ones(w, RW_N)]
    one = jax.ShapeDtypeStruct((bsz, lt, w), F32)
    two = jax.ShapeDtypeStruct((2, bsz, lt, w), F32)
    one_spec = pl.BlockSpec((1, tm, w), lambda b, i: (b, i, 0))
    two_spec = pl.BlockSpec((2, 1, tm, w), lambda b, i: (0, b, i, 0))
    return pl.pallas_call(
        functools.partial(_rwkv_prep_kernel, tm=tm, ctx_len=ctx_len, lt=lt),
        out_shape=[one] * 5 + [two] * 3,
        grid=(bsz, lt // tm),
        in_specs=_halo_specs(tm, width, lt) + [full(a) for a in params],
        out_specs=[one_spec] * 5 + [two_spec] * 3,
        compiler_params=_cparams(("parallel", "parallel")),
        name="rwkv_prep",
    )(p, p, p, *params)


def _unit_lower_inverse(n_mat, eye, c):
    x = eye - n_mat
    p = _dot(n_mat, n_mat)
    span = 4
    while True:
        x = x + _dot(x, p)
        if span >= c:
            return x
        p = _dot(p, p)
        span *= 2


def _rwkv_scan_kernel(r_ref, v_ref, kk_ref, lw_ref, km_ref, bb_ref, o_ref, s_ref, *, tb, c):
    z = pl.program_id(0)
    nsub = tb // c

    @pl.when(pl.program_id(2) == 0)
    def _():
        s_ref[...] = jnp.zeros_like(s_ref)

    incl, strict = _dir_masks(z, c)
    incl_f, strict_f = incl.astype(F32), strict.astype(F32)
    eye = incl_f - strict_f

    def body(j, carry):
        jj = jnp.where(z == 0, j, nsub - 1 - j)
        rows = pl.ds(pl.multiple_of(jj * c, c), c)
        r, v, kk = r_ref[0, rows, :], v_ref[0, rows, :], kk_ref[0, rows, :]
        lw, km, bb = lw_ref[0, 0, rows, :], km_ref[0, 0, rows, :], bb_ref[0, 0, rows, :]
        cum = _dot(incl_f, lw, HI)
        tot = jnp.sum(lw, axis=0, keepdims=True)
        e_neg = jnp.exp(-cum)
        e_end = jnp.exp(tot - cum)
        a_t = kk * jnp.exp(cum - lw)
        r_t = r * jnp.exp(cum)
        b_t, k_t = bb * e_neg, km * e_neg
        b_p, k_p = bb * e_end, km * e_end
        e_tot = jnp.exp(tot)
        outs = []
        for h in range(RW_H):
            hs = slice(h * RW_N, (h + 1) * RW_N)
            ar = jnp.concatenate([a_t[:, hs], r_t[:, hs]], axis=0)
            bk = jnp.concatenate([b_t[:, hs], k_t[:, hs]], axis=0)
            sc = _dot_nt(ar, bk)
            s = s_ref[h]
            ars = _dot_nt(ar, s)
            vh = v[:, hs]
            m_ab, m_ak = sc[:c, :c] * strict_f, sc[:c, c:] * strict_f
            n_rbk = jnp.concatenate([sc[c:, :c] * incl_f, sc[c:, c:] * incl_f], axis=1)
            rhs = ars[:c] + _dot(m_ak, vh)
            u = -_dot(_unit_lower_inverse(m_ab, eye, c), rhs)
            uv = jnp.concatenate([u, vh], axis=0)
            outs.append(ars[c:] + _dot(n_rbk, uv))
            bkp = jnp.concatenate([b_p[:, hs], k_p[:, hs]], axis=0)
            s_ref[h] = s * e_tot[:, hs] + _dot_tn(uv, bkp)
        o_ref[0, 0, rows, :] = jnp.concatenate(outs, axis=1)
        return carry

    lax.fori_loop(0, nsub, body, 0)


def rwkv_scan(r, v, kk, lw, km, bb, ctx_len):
    bsz, lt, w = r.shape
    tb = SCAN_BLOCK
    nb, cb = lt // tb, ctx_len // tb
    tmap = lambda z, b, n: _time_block(z, n, cb, nb)
    one = pl.BlockSpec((1, tb, w), lambda z, b, n: (b, tmap(z, b, n), 0))
    two = pl.BlockSpec((1, 1, tb, w), lambda z, b, n: (z, b, tmap(z, b, n), 0))
    return pl.pallas_call(
        functools.partial(_rwkv_scan_kernel, tb=tb, c=RW_CHUNK),
        out_shape=jax.ShapeDtypeStruct((2, bsz, lt, w), F32),
        grid=(2, bsz, nb),
        in_specs=[one, one, one, two, two, two],
        out_specs=two,
        scratch_shapes=[pltpu.VMEM((RW_H, RW_N, RW_N), F32)],
        compiler_params=_cparams(("parallel", "parallel", "arbitrary")),
        name="rwkv_scan",
    )(r, v, kk, lw, km, bb)


def _gla_kernel(q_ref, f_ref, i_ref, lb_ref, o_ref, s_ref, *, tb, c):
    z = pl.program_id(0)
    nsub = tb // c

    @pl.when(pl.program_id(2) == 0)
    def _():
        s_ref[...] = jnp.zeros_like(s_ref)

    incl, _ = _dir_masks(z, c)
    incl_f = incl.astype(F32)
    lb = lb_ref[...]

    def body(j, carry):
        jj = jnp.where(z == 0, j, nsub - 1 - j)
        rows = pl.ds(pl.multiple_of(jj * c, c), c)
        q = _silu(q_ref[0, rows, :]) * HG_DK ** -0.5
        f = f_ref[0, rows, :]
        v = i_ref[0, rows, :]
        log_f = jnp.log(lb + (1.0 - lb) * _sigmoid(f))
        k = (1.0 - lb) * _sigmoid(-f)
        cum = _dot(incl_f, log_f, HI)
        tot = jnp.sum(log_f, axis=0, keepdims=True)
        q_in = q * jnp.exp(cum)
        k_in = k * jnp.exp(-cum)
        k_end = k * jnp.exp(tot - cum)
        e_tot = jnp.exp(tot)
        outs = []
        for h in range(HG_H):
            hs = slice(h * HG_DK, (h + 1) * HG_DK)
            a = _dot_nt(q_in[:, hs], k_in[:, hs]) * incl_f
            s = s_ref[h]
            outs.append(_dot(a, v[:, hs]) + _dot_nt(q_in[:, hs], s))
            s_ref[h] = s * e_tot[:, hs] + _dot_tn(v[:, hs], k_end[:, hs])
        o_ref[0, 0, rows, :] = jnp.concatenate(outs, axis=1)
        return carry

    lax.fori_loop(0, nsub, body, 0)


def gla_scan(ph, lb, ctx_len):
    bsz, lt, _ = ph.shape
    w = HG_W
    tb = SCAN_BLOCK
    nb, cb = lt // tb, ctx_len // tb
    tmap = lambda z, n: _time_block(z, n, cb, nb)
    return pl.pallas_call(
        functools.partial(_gla_kernel, tb=tb, c=HG_CHUNK),
        out_shape=jax.ShapeDtypeStruct((2, bsz, lt, w), F32),
        grid=(2, bsz, nb),
        in_specs=[pl.BlockSpec((1, tb, w), lambda z, b, n: (b, tmap(z, n), 0)),
                  pl.BlockSpec((1, tb, w), lambda z, b, n: (b, tmap(z, n), 1 + z)),
                  pl.BlockSpec((1, tb, w), lambda z, b, n: (b, tmap(z, n), 3)),
                  pl.BlockSpec((1, w), lambda z, b, n: (0, 0))],
        out_specs=pl.BlockSpec((1, 1, tb, w), lambda z, b, n: (z, b, tmap(z, n), 0)),
        scratch_shapes=[pltpu.VMEM((HG_H, HG_DV, HG_DK), F32)],
        compiler_params=_cparams(("parallel", "parallel", "arbitrary")),
        name="gla_scan",
    )(ph, ph, ph, lb.reshape(1, w))


def _group_rms(y, groups, gain):
    width = y.shape[-1] // groups
    parts = []
    for g in range(groups):
        yg = y[:, g * width:(g + 1) * width]
        parts.append(yg * lax.rsqrt(jnp.mean(yg * yg, axis=-1, keepdims=True) + EPS))
    return jnp.concatenate(parts, axis=1) * gain


def _even_out_kernel(x_ref, mod_ref, or_ref, bv_ref, g_ref, og_ref, gate_ref, lnw_ref, lnb_ref, nw_ref, bd_ref,
                     w_ref, o_ref):
    bd = bd_ref[...]
    o = or_ref[0, 0] + or_ref[1, 0]
    mu = _dot(o, bd, HI) * (1.0 / RW_N)
    oc = o - mu
    var = _dot(oc * oc, bd, HI) * (1.0 / RW_N)
    yr = (oc * lax.rsqrt(var + RW_GN_EPS) * lnw_ref[...] + lnb_ref[...] + bv_ref[0]) * g_ref[0]
    og = og_ref[0, 0] + og_ref[1, 0]
    yh = _group_rms(og, HG_H, nw_ref[...]) * _silu(gate_ref[0])
    y = jnp.concatenate([yr, yh], axis=1).astype(BF16)
    o_ref[0] = x_ref[0] + mod_ref[0, 0][2:3] * _dot(y, w_ref[...])


def even_out(x, modtab, o_r, bv, g, o_g, ph, ln_w, ln_b, norm_w, w_out, ctx_len):
    bsz, lt, d = x.shape
    tm = TOK_TILE
    ctx_tiles = ctx_len // tm
    w = RW_W
    tok = lambda width, col=0: pl.BlockSpec((1, tm, width), lambda b, i: (b, i, col))
    two = pl.BlockSpec((2, 1, tm, w), lambda b, i: (0, b, i, 0))
    row = lambda width: pl.BlockSpec((1, width), lambda b, i: (0, 0))
    return pl.pallas_call(
        _even_out_kernel,
        out_shape=jax.ShapeDtypeStruct((bsz, lt, d), F32),
        grid=(bsz, lt // tm),
        in_specs=[tok(d),
                  pl.BlockSpec((1, 1, N_MOD, d), lambda b, i: (b, (i >= ctx_tiles).astype(I32), 0, 0)),
                  two, tok(w), tok(w), two, tok(HG_W, 4), row(w), row(w), row(HG_W),
                  pl.BlockSpec((w, w), lambda b, i: (0, 0)),
                  pl.BlockSpec(w_out.shape, lambda b, i: (0, 0))],
        out_specs=tok(d),
        compiler_params=_cparams(("parallel", "parallel")),
        name="even_out",
    )(x, modtab, o_r, bv, g, o_g, ph, ln_w.reshape(1, w), ln_b.reshape(1, w), norm_w.reshape(1, HG_W),
      _head_block_ones(w, RW_N), w_out)


def _conv_silu(x_ref, xp_ref, xn_ref, w_ref, b_ref, has_prev, has_next, tm):
    prev = jnp.where(has_prev, xp_ref[0], 0.0)
    nxt = jnp.where(has_next, xn_ref[0], 0.0)
    ext = jnp.concatenate([prev, x_ref[0], nxt], axis=0)
    rows = tm + 2 * HALO
    pad = CONV_K // 2
    acc = b_ref[...] + jnp.zeros((tm, ext.shape[1]), F32)
    for j in range(CONV_K):
        d = j - pad
        sh = ext if d == 0 else pltpu.roll(ext, (-d) % rows, 0)
        acc = acc + sh[HALO:HALO + tm] * w_ref[j:j + 1, :]
    return _silu(acc)


def _odd_prep_kernel(xs_ref, xsp_ref, xsn_ref, xm_ref, xmp_ref, xmn_ref, ws_ref, bs_ref, wm_ref, bm_ref,
                     os_ref, om_ref, *, tm, ctx_len, lt):
    has_prev, has_next = _segment_edges(pl.program_id(1), tm, ctx_len, lt)
    os_ref[0] = _conv_silu(xs_ref, xsp_ref, xsn_ref, ws_ref, bs_ref, has_prev, has_next, tm)
    om_ref[0] = _conv_silu(xm_ref, xmp_ref, xmn_ref, wm_ref, bm_ref, has_prev, has_next, tm)


def odd_prep(xbc, qk, ws, bs, wm, bm, ctx_len):
    bsz, lt, cs = xbc.shape
    cm = wm.shape[1]
    tm = TOK_TILE
    full = lambda a: pl.BlockSpec(a.shape, lambda b, i: (0,) * a.ndim)
    params = [ws, bs.reshape(1, cs), wm, bm.reshape(1, cm)]
    return pl.pallas_call(
        functools.partial(_odd_prep_kernel, tm=tm, ctx_len=ctx_len, lt=lt),
        out_shape=[jax.ShapeDtypeStruct((bsz, lt, cs), F32), jax.ShapeDtypeStruct((bsz, lt, cm), F32)],
        grid=(bsz, lt // tm),
        in_specs=_halo_specs(tm, cs, lt) + _halo_specs(tm, cm, lt) + [full(a) for a in params],
        out_specs=[pl.BlockSpec((1, tm, cs), lambda b, i: (b, i, 0)),
                   pl.BlockSpec((1, tm, cm), lambda b, i: (b, i, 0))],
        compiler_params=_cparams(("parallel", "parallel")),
        name="odd_prep",
    )(xbc, xbc, xbc, qk, qk, qk, *params)


def _ssd_kernel(x_ref, b_ref, c_ref, sm_ref, dtb_ref, alog_ref, ex_ref, o_ref, s_ref, *, tb, c):
    z = pl.program_id(0)
    nsub = tb // c
    hg = SSD_H // SSD_G

    @pl.when(pl.program_id(2) == 0)
    def _():
        s_ref[...] = jnp.zeros_like(s_ref)

    incl, _ = _dir_masks(z, c)
    incl_f = incl.astype(F32)
    ex = ex_ref[...]

    def body(j, carry):
        jj = jnp.where(z == 0, j, nsub - 1 - j)
        rows = pl.ds(pl.multiple_of(jj * c, c), c)
        x = x_ref[0, rows, :]
        bm = b_ref[0, rows, :]
        cm = c_ref[0, rows, :]
        dt = _softplus(sm_ref[0, rows, :] + dtb_ref[0])
        la = -dt * jnp.exp(alog_ref[0])
        a_cum = _dot(incl_f, la, HI)
        a_tot = jnp.sum(la, axis=0, keepdims=True)
        a_cum_t = a_cum.T
        wide = _dot(jnp.concatenate([dt, jnp.exp(a_cum), jnp.exp(a_tot - a_cum)], axis=0), ex, HI)
        xdt = x * wide[:c]
        din, dend = wide[c:2 * c], wide[2 * c:]
        e_tot = jnp.exp(a_tot)
        outs = []
        for g in range(SSD_G):
            bg = bm[:, g * SSD_N:(g + 1) * SSD_N]
            cg = cm[:, g * SSD_N:(g + 1) * SSD_N]
            cb = _dot_nt(cg, bg)
            for hh in range(hg):
                h = g * hg + hh
                hs = slice(h * SSD_P, (h + 1) * SSD_P)
                seg = a_cum[:, h:h + 1] - a_cum_t[h:h + 1, :]
                wmat = cb * jnp.where(incl, jnp.exp(seg), 0.0)
                s = s_ref[h]
                outs.append(_dot(wmat, xdt[:, hs]) + din[:, hs] * _dot_nt(cg, s))
                s_ref[h] = s * e_tot[:, h:h + 1] + _dot_tn(xdt[:, hs] * dend[:, hs], bg)
        o_ref[0, 0, rows, :] = jnp.concatenate(outs, axis=1)
        return carry

    lax.fori_loop(0, nsub, body, 0)


def _head_expand(heads, width, rows):
    h = jnp.arange(rows)[:, None]
    col = jnp.arange(heads * width)[None, :] // width
    return (h == col).astype(F32)


def ssd_scan(xbc, small, dtb, alog, ctx_len):
    bsz, lt, _ = xbc.shape
    tb = SCAN_BLOCK
    nb, cb = lt // tb, ctx_len // tb
    tmap = lambda z, n: _time_block(z, n, cb, nb)
    gw = SSD_G * SSD_N
    return pl.pallas_call(
        functools.partial(_ssd_kernel, tb=tb, c=SSD_CHUNK),
        out_shape=jax.ShapeDtypeStruct((2, bsz, lt, SSD_INNER), F32),
        grid=(2, bsz, nb),
        in_specs=[pl.BlockSpec((1, tb, SSD_INNER), lambda z, b, n: (b, tmap(z, n), 0)),
                  pl.BlockSpec((1, tb, gw), lambda z, b, n: (b, tmap(z, n), SSD_INNER // gw)),
                  pl.BlockSpec((1, tb, gw), lambda z, b, n: (b, tmap(z, n), SSD_INNER // gw + 1)),
                  pl.BlockSpec((1, tb, SMALL_W), lambda z, b, n: (b, tmap(z, n), z)),
                  pl.BlockSpec((1, 1, SMALL_W), lambda z, b, n: (z, 0, 0)),
                  pl.BlockSpec((1, 1, SMALL_W), lambda z, b, n: (z, 0, 0)),
                  pl.BlockSpec((SMALL_W, SSD_INNER), lambda z, b, n: (0, 0))],
        out_specs=pl.BlockSpec((1, 1, tb, SSD_INNER), lambda z, b, n: (z, b, tmap(z, n), 0)),
        scratch_shapes=[pltpu.VMEM((SSD_H, SSD_P, SSD_N), F32)],
        compiler_params=_cparams(("parallel", "parallel", "arbitrary")),
        name="ssd_scan",
    )(xbc, xbc, xbc, small, dtb, alog, _head_expand(SSD_H, SSD_P, SMALL_W))


ML_I_LANE = SSD_H
ML_F_LANE = SSD_H + ML_H


def _mlstm_kernel(q_ref, k_ref, v_ref, sm_ref, gb_ref, o_ref, cs_ref, ns_ref, m_ref, *, tb, c):
    z = pl.program_id(0)
    nsub = tb // c

    @pl.when(pl.program_id(2) == 0)
    def _():
        cs_ref[...] = jnp.zeros_like(cs_ref)
        ns_ref[...] = jnp.zeros_like(ns_ref)
        m_ref[...] = jnp.full(m_ref.shape, ML_M_INIT, F32)

    incl, _ = _dir_masks(z, c)
    incl_f = incl.astype(F32)
    lane = lax.broadcasted_iota(I32, (1, SMALL_W), 1)
    is_f = jnp.logical_and(lane >= ML_F_LANE, lane < ML_F_LANE + ML_H)

    def body(j, carry):
        jj = jnp.where(z == 0, j, nsub - 1 - j)
        rows = pl.ds(pl.multiple_of(jj * c, c), c)
        q = q_ref[0, rows, :]
        k = k_ref[0, rows, :] * ML_DK ** -0.5
        v = v_ref[0, rows, :]
        pre = sm_ref[0, rows, :] + gb_ref[0]
        gates = jnp.where(is_f, -_softplus(-pre), pre)
        f_cum = _dot(incl_f, gates, HI)
        f_tot = jnp.sum(gates, axis=0, keepdims=True)
        f_cum_t = f_cum.T
        gates_t = gates.T
        outs = []
        for h in range(ML_H):
            li, lf = ML_I_LANE + h, ML_F_LANE + h
            f_col = f_cum[:, lf:lf + 1]
            f_last = f_tot[:, lf:lf + 1]
            li_col = gates[:, li:li + 1]
            logw_end = f_last - f_col + li_col
            m_end = jnp.max(logw_end, axis=0, keepdims=True)
            w_end = jnp.exp(logw_end - m_end)
            m_prev = m_ref[h]
            logw = jnp.where(incl, f_col - f_cum_t[lf:lf + 1, :] + gates_t[li:li + 1, :], -jnp.inf)
            m_t = jnp.maximum(jnp.max(logw, axis=1, keepdims=True), f_col + m_prev)
            w_inter = jnp.exp(f_col + m_prev - m_t)
            qh = q[:, h * ML_DK:(h + 1) * ML_DK]
            kh = k[:, h * ML_DK:(h + 1) * ML_DK]
            vh = v[:, h * ML_DV:(h + 1) * ML_DV]
            scores = _dot_nt(qh, kh) * jnp.exp(logw - m_t)
            cs = cs_ref[h]
            ns = ns_ref[h]
            num = _dot(scores, vh) + w_inter * _dot(qh, cs)
            den = jnp.sum(scores, axis=1, keepdims=True) + w_inter * jnp.sum(qh * ns, axis=1, keepdims=True)
            outs.append(num / jnp.maximum(jnp.abs(den), jnp.exp(-m_t)))
            m_new = jnp.maximum(f_last + m_prev, m_end)
            s_old = jnp.exp(f_last + m_prev - m_new)
            s_loc = jnp.exp(m_end - m_new)
            kw = kh * w_end
            cs_ref[h] = s_old * cs + s_loc * _dot_tn(kw, vh)
            ns_ref[h] = s_old * ns + s_loc * jnp.sum(kw, axis=0, keepdims=True)
            m_ref[h] = m_new
        o_ref[0, 0, rows, :] = jnp.concatenate(outs, axis=1)
        return carry

    lax.fori_loop(0, nsub, body, 0)


def mlstm_scan(qk, pm, small, gate_bias, ctx_len):
    bsz, lt, _ = qk.shape
    tb = SCAN_BLOCK
    nb, cb = lt // tb, ctx_len // tb
    tmap = lambda z, n: _time_block(z, n, cb, nb)
    return pl.pallas_call(
        functools.partial(_mlstm_kernel, tb=tb, c=ML_CHUNK),
        out_shape=jax.ShapeDtypeStruct((2, bsz, lt, ML_VW), F32),
        grid=(2, bsz, nb),
        in_specs=[pl.BlockSpec((1, tb, ML_QK), lambda z, b, n: (b, tmap(z, n), 0)),
                  pl.BlockSpec((1, tb, ML_QK), lambda z, b, n: (b, tmap(z, n), 1)),
                  pl.BlockSpec((1, tb, ML_VW), lambda z, b, n: (b, tmap(z, n), 1)),
                  pl.BlockSpec((1, tb, SMALL_W), lambda z, b, n: (b, tmap(z, n), z)),
                  pl.BlockSpec((1, 1, SMALL_W), lambda z, b, n: (z, 0, 0))],
        out_specs=pl.BlockSpec((1, 1, tb, ML_VW), lambda z, b, n: (z, b, tmap(z, n), 0)),
        scratch_shapes=[pltpu.VMEM((ML_H, ML_DK, ML_DV), F32), pltpu.VMEM((ML_H, 1, ML_DK), F32),
                        pltpu.VMEM((ML_H, 1, 1), F32)],
        compiler_params=_cparams(("parallel", "parallel", "arbitrary")),
        name="mlstm_scan",
    )(qk, qk, pm, small, gate_bias)


def _odd_out_kernel(x_ref, mod_ref, ys_ref, xc_ref, zg_ref, ym_ref, og_ref, dsk_ref, nws_ref, nwm_ref, w_ref, o_ref):
    y = (ys_ref[0, 0] + ys_ref[1, 0] + dsk_ref[...] * xc_ref[0]) * _silu(zg_ref[0])
    ys = _group_rms(y, SSD_G, nws_ref[...])
    hm = ym_ref[0, 0] + ym_ref[1, 0]
    ym = _group_rms(hm, ML_H, nwm_ref[...]) * _sigmoid(og_ref[0])
    yy = jnp.concatenate([ys, ym], axis=1).astype(BF16)
    o_ref[0] = x_ref[0] + mod_ref[0, 0][2:3] * _dot(yy, w_ref[...])


def odd_out(x, modtab, y_s, xbc, ps, y_m, pm, d_skip, nw_s, nw_m, w_out, ctx_len):
    bsz, lt, d = x.shape
    tm = TOK_TILE
    ct = ctx_len // tm
    seq = lt - ctx_len
    tok = lambda width, col=0: pl.BlockSpec((1, tm, width), lambda b, i: (b, i + ct, col))
    two = lambda width: pl.BlockSpec((2, 1, tm, width), lambda b, i: (0, b, i + ct, 0))
    row = lambda width: pl.BlockSpec((1, width), lambda b, i: (0, 0))
    return pl.pallas_call(
        _odd_out_kernel,
        out_shape=jax.ShapeDtypeStruct((bsz, seq, d), F32),
        grid=(bsz, seq // tm),
        in_specs=[tok(d), pl.BlockSpec((1, 1, N_MOD, d), lambda b, i: (b, 1, 0, 0)),
                  two(SSD_INNER), tok(SSD_INNER), tok(SSD_INNER), two(ML_VW), tok(ML_VW, 2),
                  row(SSD_INNER), row(SSD_INNER), row(ML_VW),
                  pl.BlockSpec(w_out.shape, lambda b, i: (0, 0))],
        out_specs=pl.BlockSpec((1, tm, d), lambda b, i: (b, i, 0)),
        compiler_params=_cparams(("parallel", "parallel")),
        name="odd_out",
    )(x, modtab, y_s, xbc, ps, y_m, pm, d_skip.reshape(1, SSD_INNER), nw_s.reshape(1, SSD_INNER),
      nw_m.reshape(1, ML_VW), w_out)


def _router_kernel(x_ref, mod_ref, nw_ref, rw_ref, rb_ref, h_ref, idx_ref, prob_ref, rank_ref, cnt_ref, carry_ref,
                   *, tm):
    first = jnp.logical_and(pl.program_id(0) == 0, pl.program_id(1) == 0)

    @pl.when(first)
    def _():
        carry_ref[...] = jnp.zeros_like(carry_ref)

    h = _norm_mod(x_ref[0], nw_ref[...], mod_ref[0, 0], 3)
    h_ref[0] = h
    logits = _dot(h, rw_ref[...], HI) + rb_ref[...]
    lane = lax.broadcasted_iota(I32, (tm, N_EXP), 1).astype(F32)
    work = logits
    sel = jnp.zeros((tm, N_EXP), F32)
    vals, idxs, hots = [], [], []
    for _ in range(TOP_K):
        m = jnp.max(work, axis=1, keepdims=True)
        idx = jnp.min(jnp.where(work == m, lane, float(N_EXP)), axis=1, keepdims=True)
        hot = lane == idx
        vals.append(m)
        idxs.append(idx)
        hots.append(hot.astype(F32))
        sel = sel + hots[-1]
        work = jnp.where(hot, -jnp.inf, work)
    ex = [jnp.exp(v - vals[0]) for v in vals]
    tot = ex[0] + ex[1] + ex[2] + ex[3]
    ti = lax.broadcasted_iota(I32, (tm, tm), 0)
    si = lax.broadcasted_iota(I32, (tm, tm), 1)
    before = _dot((si < ti).astype(BF16), sel.astype(BF16)) + carry_ref[...]
    ranks = [jnp.sum(before * hot, axis=1, keepdims=True) for hot in hots]
    idx_ref[0] = jnp.concatenate(idxs, axis=1).astype(I32)
    prob_ref[0] = jnp.concatenate([e / tot for e in ex], axis=1)
    rank_ref[0] = jnp.concatenate(ranks, axis=1).astype(I32)
    carry_ref[...] = carry_ref[...] + jnp.sum(sel, axis=0, keepdims=True)
    cnt_ref[...] = carry_ref[...]


def router(x, modtab, nw, rw, rb, ctx_len):
    bsz, lt, d = x.shape
    tm = TOK_TILE
    ctx_tiles = ctx_len // tm
    tok = lambda width, dt: (jax.ShapeDtypeStruct((bsz, lt, width), dt),
                             pl.BlockSpec((1, tm, width), lambda b, i: (b, i, 0)))
    outs = [tok(d, F32), tok(TOP_K, I32), tok(TOP_K, F32), tok(TOP_K, I32),
            (jax.ShapeDtypeStruct((1, N_EXP), F32), pl.BlockSpec((1, N_EXP), lambda b, i: (0, 0)))]
    return pl.pallas_call(
        functools.partial(_router_kernel, tm=tm),
        out_shape=[o[0] for o in outs],
        grid=(bsz, lt // tm),
        in_specs=[pl.BlockSpec((1, tm, d), lambda b, i: (b, i, 0)),
                  pl.BlockSpec((1, 1, N_MOD, d), lambda b, i: (b, (i >= ctx_tiles).astype(I32), 0, 0)),
                  pl.BlockSpec((1, d), lambda b, i: (0, 0)),
                  pl.BlockSpec((d, N_EXP), lambda b, i: (0, 0)),
                  pl.BlockSpec((1, N_EXP), lambda b, i: (0, 0))],
        out_specs=[o[1] for o in outs],
        scratch_shapes=[pltpu.VMEM((1, N_EXP), F32)],
        compiler_params=_cparams(("arbitrary", "arbitrary")),
        name="router",
    )(x, modtab, nw.reshape(1, d), rw, rb.reshape(1, N_EXP))


def _gather_rows(tok_ref, blk, src_ref, dst_ref, sem, n_rows):
    def issue(r, carry):
        pltpu.make_async_copy(src_ref.at[pl.ds(tok_ref[blk, r], 1)], dst_ref.at[pl.ds(r, 1)], sem).start()
        return carry

    lax.fori_loop(0, n_rows, issue, 0, unroll=8)


def _expert_kernel(be_ref, tok_ref, h_ref, wg_ref, bg_ref, wu_ref, bu_ref, wd_ref, bd_ref, o_ref, xbuf, sem, *, nblk):
    i = pl.program_id(0)
    slot = i % 2

    @pl.when(i == 0)
    def _():
        _gather_rows(tok_ref, 0, h_ref, xbuf.at[0], sem.at[0], MOE_BLOCK)

    @pl.when(i + 1 < nblk)
    def _():
        _gather_rows(tok_ref, i + 1, h_ref, xbuf.at[1 - slot], sem.at[1 - slot], MOE_BLOCK)

    pltpu.make_async_copy(xbuf.at[slot], xbuf.at[slot], sem.at[slot]).wait()
    xb = xbuf[slot].astype(BF16)
    g = _dot(xb, wg_ref[0]) + bg_ref[0]
    u = _dot(xb, wu_ref[0]) + bu_ref[0]
    g = jnp.minimum(g, SWIGLU_LIMIT)
    u = jnp.clip(u, -SWIGLU_LIMIT, SWIGLU_LIMIT)
    act = (g * _sigmoid(SWIGLU_ALPHA * g) * (u + 1.0)).astype(BF16)
    o_ref[...] = _dot(act, wd_ref[0]) + bd_ref[0]


def expert_ffn(block_expert, sorted_tok, h, wg, bg, wu, bu, wd, bd):
    nblk = block_expert.shape[0]
    d, f = wg.shape[1], wg.shape[2]
    wspec = lambda r, c: pl.BlockSpec((1, r, c), lambda i, be, tok: (be[i], 0, 0))
    grid_spec = pltpu.PrefetchScalarGridSpec(
        num_scalar_prefetch=2,
        grid=(nblk,),
        in_specs=[pl.BlockSpec(memory_space=pl.ANY),
                  wspec(d, f), wspec(1, f), wspec(d, f), wspec(1, f), wspec(f, d), wspec(1, d)],
        out_specs=pl.BlockSpec((MOE_BLOCK, d), lambda i, be, tok: (i, 0)),
        scratch_shapes=[pltpu.VMEM((2, MOE_BLOCK, d), F32), pltpu.SemaphoreType.DMA((2,))],
    )
    return pl.pallas_call(
        functools.partial(_expert_kernel, nblk=nblk),
        out_shape=jax.ShapeDtypeStruct((nblk * MOE_BLOCK, d), F32),
        grid_spec=grid_spec,
        compiler_params=_cparams(("arbitrary",)),
        name="expert_ffn",
    )(block_expert, sorted_tok, h, wg, bg.reshape(N_EXP, 1, f), wu, bu.reshape(N_EXP, 1, f), wd,
      bd.reshape(N_EXP, 1, d))


def _combine_kernel(dest_ref, x_ref, mod_ref, prob_ref, y_ref, fw_ref, o_ref, ybuf, sem, *, tm, ntile, final_norm):
    i = pl.program_id(0) * pl.num_programs(1) + pl.program_id(1)
    slot = i % 2

    def gather(tile, s):
        def issue(r, carry):
            for k in range(TOP_K):
                pltpu.make_async_copy(y_ref.at[pl.ds(dest_ref[tile, r * TOP_K + k], 1)],
                                      ybuf.at[s, k, pl.ds(r, 1)], sem.at[s]).start()
            return carry

        lax.fori_loop(0, tm, issue, 0, unroll=4)

    @pl.when(i == 0)
    def _():
        gather(0, 0)

    @pl.when(i + 1 < ntile)
    def _():
        gather(i + 1, 1 - slot)

    pltpu.make_async_copy(ybuf.at[slot], ybuf.at[slot], sem.at[slot]).wait()
    p = prob_ref[0]
    acc = ybuf[slot, 0] * p[:, 0:1]
    for k in range(1, TOP_K):
        acc = acc + ybuf[slot, k] * p[:, k:k + 1]
    out = x_ref[0] + mod_ref[0, 0][5:6] * acc
    if final_norm:
        out = out * lax.rsqrt(jnp.mean(out * out, axis=-1, keepdims=True) + EPS) * fw_ref[...]
    o_ref[0] = out


def combine(dest, x, modtab, probs, y_sorted, final_w, ctx_len, final_norm):
    bsz, lt, d = x.shape
    tm = TOK_TILE
    ctx_tiles = ctx_len // tm
    ntile = bsz * (lt // tm)
    grid_spec = pltpu.PrefetchScalarGridSpec(
        num_scalar_prefetch=1,
        grid=(bsz, lt // tm),
        in_specs=[pl.BlockSpec((1, tm, d), lambda b, i, dst: (b, i, 0)),
                  pl.BlockSpec((1, 1, N_MOD, d), lambda b, i, dst: (b, (i >= ctx_tiles).astype(I32), 0, 0)),
                  pl.BlockSpec((1, tm, TOP_K), lambda b, i, dst: (b, i, 0)),
                  pl.BlockSpec(memory_space=pl.ANY),
                  pl.BlockSpec((1, d), lambda b, i, dst: (0, 0))],
        out_specs=pl.BlockSpec((1, tm, d), lambda b, i, dst: (b, i, 0)),
        scratch_shapes=[pltpu.VMEM((2, TOP_K, tm, d), F32), pltpu.SemaphoreType.DMA((2,))],
    )
    return pl.pallas_call(
        functools.partial(_combine_kernel, tm=tm, ntile=ntile, final_norm=final_norm),
        out_shape=jax.ShapeDtypeStruct((bsz, lt, d), F32),
        grid_spec=grid_spec,
        compiler_params=_cparams(("arbitrary", "arbitrary")),
        name="combine",
    )(dest, x, modtab, probs, y_sorted, final_w.reshape(1, d))


def moe_layer(x, modtab, nw, rw, rb, wg, bg, wu, bu, wd, bd, final_w, ctx_len, final_norm):
    bsz, lt, d = x.shape
    t = bsz * lt
    h, idx, probs, rank, counts = router(x, modtab, nw, rw, rb, ctx_len)
    counts = counts.reshape(N_EXP).astype(I32)
    padded = (counts + MOE_BLOCK - 1) // MOE_BLOCK * MOE_BLOCK
    ends = jnp.cumsum(padded)
    pstart = ends - padded
    idx = idx.reshape(t, TOP_K)
    dest = pstart[idx] + rank.reshape(t, TOP_K)
    nblk = -(-t * TOP_K // MOE_BLOCK) + N_EXP
    block_expert = jnp.clip(jnp.searchsorted(ends, jnp.arange(nblk, dtype=I32) * MOE_BLOCK, side='right'),
                            0, N_EXP - 1).astype(I32)
    tok_ids = jnp.broadcast_to(jnp.arange(t, dtype=I32)[:, None], (t, TOP_K))
    sorted_tok = jnp.zeros((nblk * MOE_BLOCK,), I32).at[dest.reshape(-1)].set(tok_ids.reshape(-1))
    y_sorted = expert_ffn(block_expert, sorted_tok.reshape(nblk, MOE_BLOCK), h.reshape(t, d),
                          wg.astype(BF16), bg, wu.astype(BF16), bu, wd.astype(BF16), bd)
    return combine(dest.reshape(t // TOK_TILE, TOK_TILE * TOP_K), x, modtab, probs, y_sorted, final_w, ctx_len,
                   final_norm)


def even_layer(x, modtab, norm1_w, w_in, w_out, mu, w0, w2, a0, a2, g2, k_k, k_a, r_k, ln_w, ln_b, lb, hg_norm_w,
               ctx_len):
    w_in = w_in.astype(BF16)
    pr, ph = in_proj(x, modtab, norm1_w, [w_in[:, :RW_PROJ], w_in[:, RW_PROJ:]], ctx_len)
    r, v, kk, g, bv, lw, km, bb = rwkv_prep(pr, mu, w0, w2, a0, a2, g2, k_k, k_a, r_k, ctx_len)
    o_r = rwkv_scan(r, v, kk, lw, km, bb, ctx_len)
    o_g = gla_scan(ph, lb, ctx_len)
    return even_out(x, modtab, o_r, bv, g, o_g, ph, ln_w, ln_b, hg_norm_w, w_out.astype(BF16), ctx_len)


def _dir_slab(cols, width):
    halves = []
    for z in range(2):
        parts = [c[:, z * (c.shape[1] // 2):(z + 1) * (c.shape[1] // 2)] for c in cols]
        used = sum(p.shape[1] for p in parts)
        parts.append(jnp.zeros((cols[0].shape[0], width - used), cols[0].dtype))
        halves.append(jnp.concatenate(parts, axis=1))
    return jnp.concatenate(halves, axis=1)


def _dir_rows(vals, width):
    rows = jnp.concatenate(vals, axis=1)
    return jnp.pad(rows, ((0, 0), (0, width - rows.shape[1]))).reshape(2, 1, width)


def odd_layer(x, modtab, norm1_w, w_in, w_out, s_conv_w, s_conv_b, dt_bias, a_log, d_skip, s_norm_w, m_conv_w,
              m_conv_b, i_bias, f_bias, m_norm_w, ctx_len):
    o = 0
    cols = {}
    for name, width in (("zg", SSD_INNER), ("xbc", SSD_CONV_CH), ("dt", 2 * SSD_H), ("qk", 2 * ML_QK),
                        ("v", ML_VW), ("og", ML_VW), ("ig", 2 * ML_H), ("fg", 2 * ML_H)):
        cols[name] = w_in[:, o:o + width]
        o += width
    slab_m = jnp.concatenate([cols["qk"], cols["v"], cols["og"]], axis=1).astype(BF16)
    slab_small = _dir_slab([cols["dt"], cols["ig"], cols["fg"]], SMALL_W).astype(BF16)
    pxbc, zg, pm, small = in_proj(x, modtab, norm1_w, [cols["xbc"].astype(BF16), cols["zg"].astype(BF16), slab_m,
                                                       slab_small], ctx_len)
    xbc, qk = odd_prep(pxbc, pm, s_conv_w, s_conv_b, m_conv_w, m_conv_b, ctx_len)
    zeros_g = jnp.zeros((2, 2 * ML_H), F32)
    y_s = ssd_scan(xbc, small, _dir_rows([dt_bias, zeros_g], SMALL_W), _dir_rows([a_log, zeros_g], SMALL_W), ctx_len)
    y_m = mlstm_scan(qk, pm, small, _dir_rows([jnp.zeros((2, SSD_H), F32), i_bias, f_bias], SMALL_W), ctx_len)
    return odd_out(x, modtab, y_s, xbc, zg, y_m, pm, jnp.repeat(d_skip, SSD_P), s_norm_w, m_norm_w,
                   w_out.astype(BF16), ctx_len)


def _to_col_major(t):
    b, s, ch = t.shape
    return t.reshape(b, s // GRID_W, GRID_W, ch).transpose(0, 2, 1, 3).reshape(b, s, ch)


def _to_row_major(t):
    b, s, ch = t.shape
    return t.reshape(b, GRID_W, s // GRID_W, ch).transpose(0, 2, 1, 3).reshape(b, s, ch)


def kernel(x, c, ctx, c_ctx, ada_w, ada_b, norm1_w, norm2_w, even_w_in, even_w_out, rwkv_mu, rwkv_w0, rwkv_w2, rwkv_a0, rwkv_a2, rwkv_g2, rwkv_k_k, rwkv_k_a, rwkv_r_k, rwkv_ln_w, rwkv_ln_b, hgrn_lower_bounds, hgrn_norm_w, odd_w_in, odd_w_out, ssd_conv_w, ssd_conv_b, ssd_dt_bias, ssd_a_log, ssd_d, ssd_norm_w, mlstm_conv_w, mlstm_conv_b, mlstm_i_bias, mlstm_f_bias, mlstm_norm_w, router_w, router_b, exp_w_gate, exp_b_gate, exp_w_up, exp_b_up, exp_w_down, exp_b_down, final_norm_w):
    lc = ctx.shape[1]
    lower_bounds = jnp.cumsum(jax.nn.softmax(hgrn_lower_bounds.astype(F32), axis=0), axis=0)
    moe = lambda l: (norm2_w[l], router_w[l], router_b[l], exp_w_gate[l], exp_b_gate[l], exp_w_up[l], exp_b_up[l],
                     exp_w_down[l], exp_b_down[l], final_norm_w)
    xa = jnp.concatenate([ctx, x], axis=1)
    modtab = _mod_table(c, c_ctx, ada_w[0], ada_b[0])
    xa = even_layer(xa, modtab, norm1_w[0], even_w_in[0], even_w_out[0], rwkv_mu[0], rwkv_w0[0], rwkv_w2[0],
                    rwkv_a0[0], rwkv_a2[0], rwkv_g2[0], rwkv_k_k[0], rwkv_k_a[0], rwkv_r_k[0], rwkv_ln_w[0],
                    rwkv_ln_b[0], lower_bounds[0], hgrn_norm_w[0], lc)
    xa = moe_layer(xa, modtab, *moe(0), lc, False)
    xa = jnp.concatenate([xa[:, :lc], _to_col_major(xa[:, lc:])], axis=1)
    modtab = _mod_table(c, c_ctx, ada_w[1], ada_b[1])
    xl = odd_layer(xa, modtab, norm1_w[1], odd_w_in[0], odd_w_out[0], ssd_conv_w[0], ssd_conv_b[0], ssd_dt_bias[0],
                   ssd_a_log[0], ssd_d[0], ssd_norm_w[0], mlstm_conv_w[0], mlstm_conv_b[0], mlstm_i_bias[0],
                   mlstm_f_bias[0], mlstm_norm_w[0], lc)
    xl = moe_layer(xl, modtab, *moe(1), 0, True)
    return _to_row_major(xl)
```

```python
import functools
import math

import jax
import jax.numpy as jnp
from jax import lax
from jax.experimental import pallas as pl
from jax.experimental.pallas import tpu as pltpu

F32 = jnp.float32
BF16 = jnp.bfloat16
I32 = jnp.int32
HI = lax.Precision.HIGHEST

D_MODEL = 1024
GRID_W = 64
N_MOD = 6
EPS = 1e-6
RW_H, RW_N = 8, 64
RW_W = RW_H * RW_N
RW_RANK = 64
RW_GATE_RANK = 128
RW_GN_EPS = 64e-5
RW_PROJ = 3 * RW_W + 4 * RW_RANK + RW_GATE_RANK
RW_CHUNK = 64
HG_H, HG_DK, HG_DV = 4, 128, 128
HG_W = HG_H * HG_DK
HG_CHUNK = 32
HG_PROJ = 5 * HG_W
SSD_H, SSD_P, SSD_G, SSD_N = 16, 64, 2, 128
SSD_INNER = SSD_H * SSD_P
SSD_CONV_CH = SSD_INNER + 2 * SSD_G * SSD_N
SSD_CHUNK = 64
ML_H, ML_DK, ML_DV = 4, 128, 256
ML_QK = ML_H * ML_DK
ML_VW = ML_H * ML_DV
ML_CHUNK = 64
ML_M_INIT = -1e30
CONV_K = 5
N_EXP = 32
TOP_K = 4
D_FF = 1024
SWIGLU_LIMIT = 7.0
SWIGLU_ALPHA = 1.702
MOE_BLOCK = 256

TOK_TILE = 256
SCAN_BLOCK = 256
HALO = 8
SMALL_W = 128
VMEM_LIMIT = 56 * 1024 * 1024


def _cparams(sem):
    return pltpu.CompilerParams(dimension_semantics=sem, vmem_limit_bytes=VMEM_LIMIT)


def _dot(a, b, precision=None):
    return jnp.dot(a, b, preferred_element_type=F32, precision=precision)


def _dot_nt(a, b, precision=None):
    return lax.dot_general(a, b, (((1,), (1,)), ((), ())), preferred_element_type=F32, precision=precision)


def _dot_tn(a, b, precision=None):
    return lax.dot_general(a, b, (((0,), (0,)), ((), ())), preferred_element_type=F32, precision=precision)


def _sigmoid(x):
    return 1.0 / (1.0 + jnp.exp(-x))


def _silu(x):
    return x * _sigmoid(x)


def _softplus(x):
    return jnp.maximum(x, 0.0) + jnp.log(1.0 + jnp.exp(-jnp.abs(x)))


def _dir_masks(z, c):
    ti = lax.broadcasted_iota(I32, (c, c), 0)
    si = lax.broadcasted_iota(I32, (c, c), 1)
    d = (si - ti) * jnp.where(z == 0, 1, -1)
    return d <= 0, d < 0


def _time_block(z, n, ctx_blocks, n_blocks):
    rev = jnp.where(n < ctx_blocks, ctx_blocks - 1 - n, n_blocks - 1 - (n - ctx_blocks))
    return jnp.where(z == 0, n, rev)


def _mod_kernel(c_ref, w_ref, b_ref, o_ref):
    o_ref[...] = _dot(_silu(c_ref[...]), w_ref[...]) + b_ref[...]


def ada_mod(cc, w, b):
    rows, d = cc.shape
    n = w.shape[1]
    tn = 1536
    return pl.pallas_call(
        _mod_kernel,
        out_shape=jax.ShapeDtypeStruct((rows, n), F32),
        grid=(n // tn,),
        in_specs=[pl.BlockSpec((rows, d), lambda j: (0, 0)),
                  pl.BlockSpec((d, tn), lambda j: (0, j)),
                  pl.BlockSpec((1, tn), lambda j: (0, j))],
        out_specs=pl.BlockSpec((rows, tn), lambda j: (0, j)),
        compiler_params=_cparams(("arbitrary",)),
        name="ada_mod",
    )(cc, w, b.reshape(1, n))


def _mod_table(c, c_ctx, w, b):
    bsz = c.shape[0]
    rows = -(-(bsz + 1) // 8) * 8
    cc = jnp.zeros((rows, D_MODEL), F32).at[:bsz].set(c).at[bsz].set(c_ctx)
    m = ada_mod(cc, w, b)
    mod = m[:bsz].reshape(bsz, 1, N_MOD, D_MODEL)
    mod_c = jnp.broadcast_to(m[bsz].reshape(1, 1, N_MOD, D_MODEL), (bsz, 1, N_MOD, D_MODEL))
    return jnp.concatenate([mod_c, mod], axis=1)


def _norm_mod(x, nw, m, shift_idx):
    y = x * lax.rsqrt(jnp.mean(x * x, axis=-1, keepdims=True) + EPS) * nw
    return y * (1.0 + m[shift_idx + 1:shift_idx + 2]) + m[shift_idx:shift_idx + 1]


def _in_proj_kernel(x_ref, mod_ref, nw_ref, *rest, n_out):
    w_refs, o_refs = rest[:n_out], rest[n_out:]
    h = _norm_mod(x_ref[0], nw_ref[...], mod_ref[0, 0], 0).astype(BF16)
    for w_ref, o_ref in zip(w_refs, o_refs):
        o_ref[0] = _dot(h, w_ref[...])


def in_proj(x, modtab, nw, slabs, ctx_len):
    bsz, lt, d = x.shape
    tm = TOK_TILE
    ctx_tiles = ctx_len // tm
    in_specs = [pl.BlockSpec((1, tm, d), lambda b, i: (b, i, 0)),
                pl.BlockSpec((1, 1, N_MOD, d), lambda b, i: (b, (i >= ctx_tiles).astype(I32), 0, 0)),
                pl.BlockSpec((1, d), lambda b, i: (0, 0))]
    in_specs += [pl.BlockSpec(w.shape, lambda b, i: (0, 0)) for w in slabs]
    return pl.pallas_call(
        functools.partial(_in_proj_kernel, n_out=len(slabs)),
        out_shape=[jax.ShapeDtypeStruct((bsz, lt, w.shape[1]), F32) for w in slabs],
        grid=(bsz, lt // tm),
        in_specs=in_specs,
        out_specs=[pl.BlockSpec((1, tm, w.shape[1]), lambda b, i: (b, i, 0)) for w in slabs],
        compiler_params=_cparams(("parallel", "parallel")),
        name="in_proj",
    )(x, modtab, nw.reshape(1, d), *slabs)


def _halo_specs(tm, width, lt):
    per = tm // HALO
    last = lt // HALO - 1
    return [pl.BlockSpec((1, tm, width), lambda b, i: (b, i, 0)),
            pl.BlockSpec((1, HALO, width), lambda b, i: (b, jnp.maximum(i * per - 1, 0), 0)),
            pl.BlockSpec((1, HALO, width), lambda b, i: (b, jnp.minimum((i + 1) * per, last), 0))]


def _segment_edges(i, tm, ctx_len, lt):
    row0 = i * tm
    has_prev = jnp.logical_and(row0 != 0, row0 != ctx_len)
    has_next = jnp.logical_and(row0 + tm != ctx_len, row0 + tm != lt)
    return has_prev, has_next


def _rwkv_prep_kernel(p_ref, pp_ref, pn_ref, mu_ref, w0_ref, w2_ref, a0_ref, a2_ref, g2_ref, kk_ref, ka_ref,
                      rk_ref, bd_ref, r_o, v_o, kk_o, g_o, bv_o, lw_o, km_o, bb_o, *, tm, ctx_len, lt):
    has_prev, has_next = _segment_edges(pl.program_id(1), tm, ctx_len, lt)
    p = p_ref[0]
    prev_row = jnp.where(has_prev, pp_ref[0, HALO - 1:HALO, :], 0.0)
    next_row = jnp.where(has_next, pn_ref[0, 0:1, :], 0.0)
    rid = lax.broadcasted_iota(I32, (tm, 1), 0)
    up = jnp.where(rid == 0, prev_row, pltpu.roll(p, 1, 0))
    dn = jnp.where(rid == tm - 1, next_row, pltpu.roll(p, tm - 1, 0))
    p = p + (0.5 * (up + dn) - p) * mu_ref[...]
    w = RW_W
    r, k, v = p[:, 0:w], p[:, w:2 * w], p[:, 2 * w:3 * w]
    o = 3 * w
    wd = jnp.tanh(p[:, o:o + 2 * RW_RANK])
    ad = p[:, o + 2 * RW_RANK:o + 4 * RW_RANK]
    gd = p[:, o + 4 * RW_RANK:o + 4 * RW_RANK + RW_GATE_RANK]
    bd = bd_ref[...]
    kk = k * kk_ref[...]
    kk = kk * lax.rsqrt(_dot(kk * kk, bd, HI) + 1e-12)
    r_o[0] = r
    v_o[0] = v
    kk_o[0] = kk
    g_o[0] = _dot(_sigmoid(gd), g2_ref[...])
    km_sum = jnp.zeros_like(k)
    for z in range(2):
        sl = slice(z * RW_RANK, (z + 1) * RW_RANK)
        w_pre = w0_ref[z:z + 1, :] + _dot(wd[:, sl], w2_ref[z])
        lw_o[z, 0] = -_sigmoid(w_pre) * math.exp(-0.5)
        a = _sigmoid(a0_ref[z:z + 1, :] + _dot(ad[:, sl], a2_ref[z]))
        km = k * (1.0 + (a - 1.0) * ka_ref[...])
        km_o[z, 0] = km
        bb_o[z, 0] = kk * a
        km_sum = km_sum + km
    bv_o[0] = _dot(r * km_sum * rk_ref[...], bd, HI) * v


def _head_block_ones(width, head):
    i = jnp.arange(width) // head
    return (i[:, None] == i[None, :]).astype(F32)


def rwkv_prep(p, mu, w0, w2, a0, a2, g2, k_k, k_a, r_k, ctx_len):
    bsz, lt, width = p.shape
    tm = TOK_TILE
    w = RW_W
    full = lambda a: pl.BlockSpec(a.shape, lambda b, i: (0,) * a.ndim)
    params = [mu.reshape(1, width), w0, w2, a0, a2, g2, k_k.reshape(1, w), k_a.reshape(1, w), r_k.reshape(1, w),
              _head_block_ones(w, RW_N)]
    one = jax.ShapeDtypeStruct((bsz, lt, w), F32)
    two = jax.ShapeDtypeStruct((2, bsz, lt, w), F32)
    one_spec = pl.BlockSpec((1, tm, w), lambda b, i: (b, i, 0))
    two_spec = pl.BlockSpec((2, 1, tm, w), lambda b, i: (0, b, i, 0))
    return pl.pallas_call(
        functools.partial(_rwkv_prep_kernel, tm=tm, ctx_len=ctx_len, lt=lt),
        out_shape=[one] * 5 + [two] * 3,
        grid=(bsz, lt // tm),
        in_specs=_halo_specs(tm, width, lt) + [full(a) for a in params],
        out_specs=[one_spec] * 5 + [two_spec] * 3,
        compiler_params=_cparams(("parallel", "parallel")),
        name="rwkv_prep",
    )(p, p, p, *params)


def _unit_lower_inverses(n_mats, eye, c):
    xs = [eye - n for n in n_mats]
    ps = [_dot(n, n) for n in n_mats]
    span = 4
    while True:
        xs = [x + _dot(x, p) for x, p in zip(xs, ps)]
        if span >= c:
            return xs
        ps = [_dot(p, p) for p in ps]
        span *= 2


def _rwkv_scan_kernel(r_ref, v_ref, kk_ref, lw_ref, km_ref, bb_ref, o_ref, s_ref, *, tb, c):
    z = pl.program_id(0)
    nsub = tb // c

    @pl.when(pl.program_id(2) == 0)
    def _():
        s_ref[...] = jnp.zeros_like(s_ref)

    incl, strict = _dir_masks(z, c)
    incl_f, strict_f = incl.astype(F32), strict.astype(F32)
    eye = incl_f - strict_f

    def body(j, s_old):
        jj = jnp.where(z == 0, j, nsub - 1 - j)
        rows = pl.ds(pl.multiple_of(jj * c, c), c)
        r, v, kk = r_ref[0, rows, :], v_ref[0, rows, :], kk_ref[0, rows, :]
        lw, km, bb = lw_ref[0, 0, rows, :], km_ref[0, 0, rows, :], bb_ref[0, 0, rows, :]
        cum = _dot(incl_f, lw, HI)
        tot = jnp.sum(lw, axis=0, keepdims=True)
        e_neg = jnp.exp(-cum)
        e_end = jnp.exp(tot - cum)
        a_t = kk * jnp.exp(cum - lw)
        r_t = r * jnp.exp(cum)
        b_t, k_t = bb * e_neg, km * e_neg
        b_p, k_p = bb * e_end, km * e_end
        e_tot = jnp.exp(tot)
        heads = [slice(h * RW_N, (h + 1) * RW_N) for h in range(RW_H)]
        ar =[jnp.concatenate([a_t[:, hs], r_t[:, hs]], axis=0) for hs in heads]
        bk = [jnp.concatenate([b_t[:, hs], k_t[:, hs]], axis=0) for hs in heads]
        bkp = [jnp.concatenate([b_p[:, hs], k_p[:, hs]], axis=0) for hs in heads]
        vh = [v[:, hs] for hs in heads]
        sc = [_dot_nt(a, b) for a, b in zip(ar, bk)]
        ars = [_dot_nt(a, s) for a, s in zip(ar, s_old)]
        m_ak_v = [_dot(x[:c, c:] * strict_f, vv) for x, vv in zip(sc, vh)]
        inv = _unit_lower_inverses([x[:c, :c] * strict_f for x in sc], eye, c)
        u = [-_dot(i, a[:c] + m) for i, a, m in zip(inv, ars, m_ak_v)]
        uv = [jnp.concatenate([uu, vv], axis=0) for uu, vv in zip(u, vh)]
        n_rbk = [jnp.concatenate([x[c:, :c] * incl_f, x[c:, c:] * incl_f], axis=1) for x in sc]
        outs = [a[c:] + _dot(n, w) for a, n, w in zip(ars, n_rbk, uv)]
        o_ref[0, 0, rows, :] = jnp.concatenate(outs, axis=1)
        return tuple(s * e_tot[:, hs] + _dot_tn(w, b) for s, hs, w, b in zip(s_old, heads, uv, bkp))

    s_fin = lax.fori_loop(0, nsub, body, tuple(s_ref[h] for h in range(RW_H)), unroll=2)
    for h in range(RW_H):
        s_ref[h] = s_fin[h]


def rwkv_scan(r, v, kk, lw, km, bb, ctx_len):
    bsz, lt, w = r.shape
    tb = SCAN_BLOCK
    nb, cb = lt // tb, ctx_len // tb
    tmap = lambda z, b, n: _time_block(z, n, cb, nb)
    one = pl.BlockSpec((1, tb, w), lambda z, b, n: (b, tmap(z, b, n), 0))
    two = pl.BlockSpec((1, 1, tb, w), lambda z, b, n: (z, b, tmap(z, b, n), 0))
    return pl.pallas_call(
        functools.partial(_rwkv_scan_kernel, tb=tb, c=RW_CHUNK),
        out_shape=jax.ShapeDtypeStruct((2, bsz, lt, w), F32),
        grid=(2, bsz, nb),
        in_specs=[one, one, one, two, two, two],
        out_specs=two,
        scratch_shapes=[pltpu.VMEM((RW_H, RW_N, RW_N), F32)],
        compiler_params=_cparams(("parallel", "parallel", "arbitrary")),
        name="rwkv_scan",
    )(r, v, kk, lw, km, bb)


def _gla_kernel(q_ref, f_ref, i_ref, lb_ref, o_ref, s_ref, *, tb, c):
    z = pl.program_id(0)
    nsub = tb // c

    @pl.when(pl.program_id(2) == 0)
    def _():
        s_ref[...] = jnp.zeros_like(s_ref)

    incl, _ = _dir_masks(z, c)
    incl_f = incl.astype(F32)
    lb = lb_ref[...]

    def body(j, s_old):
        jj = jnp.where(z == 0, j, nsub - 1 - j)
        rows = pl.ds(pl.multiple_of(jj * c, c), c)
        q = _silu(q_ref[0, rows, :]) * HG_DK ** -0.5
        f = f_ref[0, rows, :]
        v = i_ref[0, rows, :]
        log_f = jnp.log(lb + (1.0 - lb) * _sigmoid(f))
        k = (1.0 - lb) * _sigmoid(-f)
        cum = _dot(incl_f, log_f, HI)
        tot = jnp.sum(log_f, axis=0, keepdims=True)
        q_in = q * jnp.exp(cum)
        k_in = k * jnp.exp(-cum)
        k_end = k * jnp.exp(tot - cum)
        e_tot = jnp.exp(tot)
        heads = [slice(h * HG_DK, (h + 1) * HG_DK) for h in range(HG_H)]
        a = [_dot_nt(q_in[:, hs], k_in[:, hs]) * incl_f for hs in heads]
        inter = [_dot_nt(q_in[:, hs], s) for hs, s in zip(heads, s_old)]
        grow = [_dot_tn(v[:, hs], k_end[:, hs]) for hs in heads]
        outs = [_dot(aa, v[:, hs]) + ii for aa, hs, ii in zip(a, heads, inter)]
        o_ref[0, 0, rows, :] = jnp.concatenate(outs, axis=1)
        return tuple(s * e_tot[:, hs] + g for s, hs, g in zip(s_old, heads, grow))

    s_fin = lax.fori_loop(0, nsub, body, tuple(s_ref[h] for h in range(HG_H)), unroll=4)
    for h in range(HG_H):
        s_ref[h] = s_fin[h]


def gla_scan(ph, lb, ctx_len):
    bsz, lt, _ = ph.shape
    w = HG_W
    tb = SCAN_BLOCK
    nb, cb = lt // tb, ctx_len // tb
    tmap = lambda z, n: _time_block(z, n, cb, nb)
    return pl.pallas_call(
        functools.partial(_gla_kernel, tb=tb, c=HG_CHUNK),
        out_shape=jax.ShapeDtypeStruct((2, bsz, lt, w), F32),
        grid=(2, bsz, nb),
        in_specs=[pl.BlockSpec((1, tb, w), lambda z, b, n: (b, tmap(z, n), 0)),
                  pl.BlockSpec((1, tb, w), lambda z, b, n: (b, tmap(z, n), 1 + z)),
                  pl.BlockSpec((1, tb, w), lambda z, b, n: (b, tmap(z, n), 3)),
                  pl.BlockSpec((1, w), lambda z, b, n: (0, 0))],
        out_specs=pl.BlockSpec((1, 1, tb, w), lambda z, b, n: (z, b, tmap(z, n), 0)),
        scratch_shapes=[pltpu.VMEM((HG_H, HG_DV, HG_DK), F32)],
        compiler_params=_cparams(("parallel", "parallel", "arbitrary")),
        name="gla_scan",
    )(ph, ph, ph, lb.reshape(1, w))


def _group_rms(y, groups, gain):
    width = y.shape[-1] // groups
    parts = []
    for g in range(groups):
        yg = y[:, g * width:(g + 1) * width]
        parts.append(yg * lax.rsqrt(jnp.mean(yg * yg, axis=-1, keepdims=True) + EPS))
    return jnp.concatenate(parts, axis=1) * gain


def _even_out_kernel(x_ref, mod_ref, or_ref, bv_ref, g_ref, og_ref, gate_ref, lnw_ref, lnb_ref, nw_ref, bd_ref,
                     w_ref, o_ref):
    bd = bd_ref[...]
    o = or_ref[0, 0] + or_ref[1, 0]
    mu = _dot(o, bd, HI) * (1.0 / RW_N)
    oc = o - mu
    var = _dot(oc * oc, bd, HI) * (1.0 / RW_N)
    yr = (oc * lax.rsqrt(var + RW_GN_EPS) * lnw_ref[...] + lnb_ref[...] + bv_ref[0]) * g_ref[0]
    og = og_ref[0, 0] + og_ref[1, 0]
    yh = _group_rms(og, HG_H, nw_ref[...]) * _silu(gate_ref[0])
    y = jnp.concatenate([yr, yh], axis=1).astype(BF16)
    o_ref[0] = x_ref[0] + mod_ref[0, 0][2:3] * _dot(y, w_ref[...])


def even_out(x, modtab, o_r, bv, g, o_g, ph, ln_w, ln_b, norm_w, w_out, ctx_len):
    bsz, lt, d = x.shape
    tm = TOK_TILE
    ctx_tiles = ctx_len // tm
    w = RW_W
    tok = lambda width, col=0: pl.BlockSpec((1, tm, width), lambda b, i: (b, i, col))
    two = pl.BlockSpec((2, 1, tm, w), lambda b, i: (0, b, i, 0))
    row = lambda width: pl.BlockSpec((1, width), lambda b, i: (0, 0))
    return pl.pallas_call(
        _even_out_kernel,
        out_shape=jax.ShapeDtypeStruct((bsz, lt, d), F32),
        grid=(bsz, lt // tm),
        in_specs=[tok(d),
                  pl.BlockSpec((1, 1, N_MOD, d), lambda b, i: (b, (i >= ctx_tiles).astype(I32), 0, 0)),
                  two, tok(w), tok(w), two, tok(HG_W, 4), row(w), row(w), row(HG_W),
                  pl.BlockSpec((w, w), lambda b, i: (0, 0)),
                  pl.BlockSpec(w_out.shape, lambda b, i: (0, 0))],
        out_specs=tok(d),
        compiler_params=_cparams(("parallel", "parallel")),
        name="even_out",
    )(x, modtab, o_r, bv, g, o_g, ph, ln_w.reshape(1, w), ln_b.reshape(1, w), norm_w.reshape(1, HG_W),
      _head_block_ones(w, RW_N), w_out)


def _conv_silu(x_ref, xp_ref, xn_ref, w_ref, b_ref, has_prev, has_next, tm):
    prev = jnp.where(has_prev, xp_ref[0], 0.0)
    nxt = jnp.where(has_next, xn_ref[0], 0.0)
    ext = jnp.concatenate([prev, x_ref[0], nxt], axis=0)
    rows = tm + 2 * HALO
    pad = CONV_K // 2
    acc = b_ref[...] + jnp.zeros((tm, ext.shape[1]), F32)
    for j in range(CONV_K):
        d = j - pad
        sh = ext if d == 0 else pltpu.roll(ext, (-d) % rows, 0)
        acc = acc + sh[HALO:HALO + tm] * w_ref[j:j + 1, :]
    return _silu(acc)


def _odd_prep_kernel(xs_ref, xsp_ref, xsn_ref, xm_ref, xmp_ref, xmn_ref, ws_ref, bs_ref, wm_ref, bm_ref,
                     os_ref, om_ref, *, tm, ctx_len, lt):
    has_prev, has_next = _segment_edges(pl.program_id(1), tm, ctx_len, lt)
    os_ref[0] = _conv_silu(xs_ref, xsp_ref, xsn_ref, ws_ref, bs_ref, has_prev, has_next, tm)
    om_ref[0] = _conv_silu(xm_ref, xmp_ref, xmn_ref, wm_ref, bm_ref, has_prev, has_next, tm)


def odd_prep(xbc, qk, ws, bs, wm, bm, ctx_len):
    bsz, lt, cs = xbc.shape
    cm = wm.shape[1]
    tm = TOK_TILE
    full = lambda a: pl.BlockSpec(a.shape, lambda b, i: (0,) * a.ndim)
    params = [ws, bs.reshape(1, cs), wm, bm.reshape(1, cm)]
    return pl.pallas_call(
        functools.partial(_odd_prep_kernel, tm=tm, ctx_len=ctx_len, lt=lt),
        out_shape=[jax.ShapeDtypeStruct((bsz, lt, cs), F32), jax.ShapeDtypeStruct((bsz, lt, cm), F32)],
        grid=(bsz, lt // tm),
        in_specs=_halo_specs(tm, cs, lt) + _halo_specs(tm, cm, lt) + [full(a) for a in params],
        out_specs=[pl.BlockSpec((1, tm, cs), lambda b, i: (b, i, 0)),
                   pl.BlockSpec((1, tm, cm), lambda b, i: (b, i, 0))],
        compiler_params=_cparams(("parallel", "parallel")),
        name="odd_prep",
    )(xbc, xbc, xbc, qk, qk, qk, *params)


def _ssd_kernel(x_ref, b_ref, c_ref, sm_ref, dtb_ref, alog_ref, ex_ref, o_ref, s_ref, *, tb, c):
    z = pl.program_id(0)
    nsub = tb // c
    hg = SSD_H // SSD_G

    @pl.when(pl.program_id(2) == 0)
    def _():
        s_ref[...] = jnp.zeros_like(s_ref)

    incl, _ = _dir_masks(z, c)
    incl_f = incl.astype(F32)
    ex = ex_ref[...]

    gw = hg * SSD_P

    def body(j, s_old):
        jj = jnp.where(z == 0, j, nsub - 1 - j)
        rows = pl.ds(pl.multiple_of(jj * c, c), c)
        x = x_ref[0, rows, :]
        bm = b_ref[0, rows, :]
        cm = c_ref[0, rows, :]
        dt = _softplus(sm_ref[0, rows, :] + dtb_ref[0])
        la = -dt * jnp.exp(alog_ref[0])
        a_cum = _dot(incl_f, la, HI)
        a_tot = jnp.sum(la, axis=0, keepdims=True)
        a_cum_t = a_cum.T
        e_tot_col = jnp.exp(jnp.where(z == 0, a_cum_t[:, c - 1:c], a_cum_t[:, 0:1]))
        wide = _dot(jnp.concatenate([dt, jnp.exp(a_cum), jnp.exp(a_tot - a_cum)], axis=0), ex, HI)
        xdt = x * wide[:c]
        din, dend = wide[c:2 * c], wide[2 * c:]
        xd_end = xdt * dend
        bgs = [bm[:, g * SSD_N:(g + 1) * SSD_N] for g in range(SSD_G)]
        cgs = [cm[:, g * SSD_N:(g + 1) * SSD_N] for g in range(SSD_G)]
        cb = [_dot_nt(cg, bg) for cg, bg in zip(cgs, bgs)]
        inter = [_dot_nt(cg, s) for cg, s in zip(cgs, s_old)]
        grow = [_dot_tn(xd_end[:, g * gw:(g + 1) * gw], bg) for g, bg in enumerate(bgs)]
        intra = []
        for h in range(SSD_H):
            seg = a_cum[:, h:h + 1] - a_cum_t[h:h + 1, :]
            wmat = cb[h // hg] * jnp.where(incl, jnp.exp(seg), 0.0)
            intra.append(_dot(wmat, xdt[:, h * SSD_P:(h + 1) * SSD_P]))
        o_ref[0, 0, rows, :] = jnp.concatenate(intra, axis=1) + din * jnp.concatenate(inter, axis=1)
        new = []
        for g in range(SSD_G):
            decay = jnp.concatenate([jnp.broadcast_to(e_tot_col[h:h + 1, :], (SSD_P, 1))
                                     for h in range(g * hg, (g + 1) * hg)], axis=0)
            new.append(s_old[g] * decay + grow[g])
        return tuple(new)

    s_fin = lax.fori_loop(0, nsub, body, tuple(s_ref[g] for g in range(SSD_G)), unroll=2)
    for g in range(SSD_G):
        s_ref[g] = s_fin[g]


def _head_expand(heads, width, rows):
    h = jnp.arange(rows)[:, None]
    col = jnp.arange(heads * width)[None, :] // width
    return (h == col).astype(F32)


def ssd_scan(xbc, small, dtb, alog, ctx_len):
    bsz, lt, _ = xbc.shape
    tb = SCAN_BLOCK
    nb, cb = lt // tb, ctx_len // tb
    tmap = lambda z, n: _time_block(z, n, cb, nb)
    gw = SSD_G * SSD_N
    return pl.pallas_call(
        functools.partial(_ssd_kernel, tb=tb, c=SSD_CHUNK),
        out_shape=jax.ShapeDtypeStruct((2, bsz, lt, SSD_INNER), F32),
        grid=(2, bsz, nb),
        in_specs=[pl.BlockSpec((1, tb, SSD_INNER), lambda z, b, n: (b, tmap(z, n), 0)),
                  pl.BlockSpec((1, tb, gw), lambda z, b, n: (b, tmap(z, n), SSD_INNER // gw)),
                  pl.BlockSpec((1, tb, gw), lambda z, b, n: (b, tmap(z, n), SSD_INNER // gw + 1)),
                  pl.BlockSpec((1, tb, SMALL_W), lambda z, b, n: (b, tmap(z, n), z)),
                  pl.BlockSpec((1, 1, SMALL_W), lambda z, b, n: (z, 0, 0)),
                  pl.BlockSpec((1, 1, SMALL_W), lambda z, b, n: (z, 0, 0)),
                  pl.BlockSpec((SMALL_W, SSD_INNER), lambda z, b, n: (0, 0))],
        out_specs=pl.BlockSpec((1, 1, tb, SSD_INNER), lambda z, b, n: (z, b, tmap(z, n), 0)),
        scratch_shapes=[pltpu.VMEM((SSD_G, SSD_H // SSD_G * SSD_P, SSD_N), F32)],
        compiler_params=_cparams(("parallel", "parallel", "arbitrary")),
        name="ssd_scan",
    )(xbc, xbc, xbc, small, dtb, alog, _head_expand(SSD_H, SSD_P, SMALL_W))


ML_I_LANE = SSD_H
ML_F_LANE = SSD_H + ML_H


def _mlstm_kernel(q_ref, k_ref, v_ref, sm_ref, gb_ref, o_ref, cs_ref, ns_ref, m_ref, *, tb, c):
    z = pl.program_id(0)
    nsub = tb // c

    @pl.when(pl.program_id(2) == 0)
    def _():
        cs_ref[...] = jnp.zeros_like(cs_ref)
        ns_ref[...] = jnp.zeros_like(ns_ref)
        m_ref[...] = jnp.full(m_ref.shape, ML_M_INIT, F32)

    incl, _ = _dir_masks(z, c)
    incl_f = incl.astype(F32)
    lane = lax.broadcasted_iota(I32, (1, SMALL_W), 1)
    is_f = jnp.logical_and(lane >= ML_F_LANE, lane < ML_F_LANE + ML_H)

    def body(j, carry):
        cs_old, ns_old, m_old = carry
        jj = jnp.where(z == 0, j, nsub - 1 - j)
        rows = pl.ds(pl.multiple_of(jj * c, c), c)
        q = q_ref[0, rows, :]
        k = k_ref[0, rows, :] * ML_DK ** -0.5
        v = v_ref[0, rows, :]
        pre = sm_ref[0, rows, :] + gb_ref[0]
        gates = jnp.where(is_f, -_softplus(-pre), pre)
        f_cum = _dot(incl_f, gates, HI)
        f_tot = jnp.sum(gates, axis=0, keepdims=True)
        f_cum_t = f_cum.T
        gates_t = gates.T
        qs = [q[:, h * ML_DK:(h + 1) * ML_DK] for h in range(ML_H)]
        ks = [k[:, h * ML_DK:(h + 1) * ML_DK] for h in range(ML_H)]
        vs = [v[:, h * ML_DV:(h + 1) * ML_DV] for h in range(ML_H)]
        qk = [_dot_nt(a, b) for a, b in zip(qs, ks)]
        q_cs = [_dot(a, s) for a, s in zip(qs, cs_old)]
        outs, cs_new, ns_new, m_new_all = [], [], [], []
        for h in range(ML_H):
            li, lf = ML_I_LANE + h, ML_F_LANE + h
            f_col = f_cum[:, lf:lf + 1]
            f_last = f_tot[:, lf:lf + 1]
            logw_end = f_last - f_col + gates[:, li:li + 1]
            m_end = jnp.max(logw_end, axis=0, keepdims=True)
            kw = ks[h] * jnp.exp(logw_end - m_end)
            m_prev = m_old[h]
            logw = jnp.where(incl, f_col - f_cum_t[lf:lf + 1, :] + gates_t[li:li + 1, :], -jnp.inf)
            m_t = jnp.maximum(jnp.max(logw, axis=1, keepdims=True), f_col + m_prev)
            w_inter = jnp.exp(f_col + m_prev - m_t)
            scores = qk[h] * jnp.exp(logw - m_t)
            num = _dot(scores, vs[h]) + w_inter * q_cs[h]
            den = (jnp.sum(scores, axis=1, keepdims=True)
                   + w_inter * jnp.sum(qs[h] * ns_old[h], axis=1, keepdims=True))
            outs.append(num / jnp.maximum(jnp.abs(den), jnp.exp(-m_t)))
            m_new = jnp.maximum(f_last + m_prev, m_end)
            s_keep = jnp.exp(f_last + m_prev - m_new)
            s_loc = jnp.exp(m_end - m_new)
            cs_new.append(s_keep * cs_old[h] + s_loc * _dot_tn(kw, vs[h]))
            ns_new.append(s_keep * ns_old[h] + s_loc * jnp.sum(kw, axis=0, keepdims=True))
            m_new_all.append(m_new)
        o_ref[0, 0, rows, :] = jnp.concatenate(outs, axis=1)
        return tuple(cs_new), tuple(ns_new), tuple(m_new_all)

    heads = range(ML_H)
    init = (tuple(cs_ref[h] for h in heads), tuple(ns_ref[h] for h in heads), tuple(m_ref[h] for h in heads))
    cs_fin, ns_fin, m_fin = lax.fori_loop(0, nsub, body, init, unroll=2)
    for h in heads:
        cs_ref[h] = cs_fin[h]
        ns_ref[h] = ns_fin[h]
        m_ref[h] = m_fin[h]


def mlstm_scan(qk, pm, small, gate_bias, ctx_len):
    bsz, lt, _ = qk.shape
    tb = SCAN_BLOCK
    nb, cb = lt // tb, ctx_len // tb
    tmap = lambda z, n: _time_block(z, n, cb, nb)
    return pl.pallas_call(
        functools.partial(_mlstm_kernel, tb=tb, c=ML_CHUNK),
        out_shape=jax.ShapeDtypeStruct((2, bsz, lt, ML_VW), F32),
        grid=(2, bsz, nb),
        in_specs=[pl.BlockSpec((1, tb, ML_QK), lambda z, b, n: (b, tmap(z, n), 0)),
                  pl.BlockSpec((1, tb, ML_QK), lambda z, b, n: (b, tmap(z, n), 1)),
                  pl.BlockSpec((1, tb, ML_VW), lambda z, b, n: (b, tmap(z, n), 1)),
                  pl.BlockSpec((1, tb, SMALL_W), lambda z, b, n: (b, tmap(z, n), z)),
                  pl.BlockSpec((1, 1, SMALL_W), lambda z, b, n: (z, 0, 0))],
        out_specs=pl.BlockSpec((1, 1, tb, ML_VW), lambda z, b, n: (z, b, tmap(z, n), 0)),
        scratch_shapes=[pltpu.VMEM((ML_H, ML_DK, ML_DV), F32), pltpu.VMEM((ML_H, 1, ML_DK), F32),
                        pltpu.VMEM((ML_H, 1, 1), F32)],
        compiler_params=_cparams(("parallel", "parallel", "arbitrary")),
        name="mlstm_scan",
    )(qk, qk, pm, small, gate_bias)


def _odd_out_kernel(x_ref, mod_ref, ys_ref, xc_ref, zg_ref, ym_ref, og_ref, dsk_ref, nws_ref, nwm_ref, w_ref, o_ref):
    y = (ys_ref[0, 0] + ys_ref[1, 0] + dsk_ref[...] * xc_ref[0]) * _silu(zg_ref[0])
    ys = _group_rms(y, SSD_G, nws_ref[...])
    hm = ym_ref[0, 0] + ym_ref[1, 0]
    ym = _group_rms(hm, ML_H, nwm_ref[...]) * _sigmoid(og_ref[0])
    yy = jnp.concatenate([ys, ym], axis=1).astype(BF16)
    o_ref[0] = x_ref[0] + mod_ref[0, 0][2:3] * _dot(yy, w_ref[...])


def odd_out(x, modtab, y_s, xbc, ps, y_m, pm, d_skip, nw_s, nw_m, w_out, ctx_len):
    bsz, lt, d = x.shape
    tm = TOK_TILE
    ct = ctx_len // tm
    seq = lt - ctx_len
    tok = lambda width, col=0: pl.BlockSpec((1, tm, width), lambda b, i: (b, i + ct, col))
    two = lambda width: pl.BlockSpec((2, 1, tm, width), lambda b, i: (0, b, i + ct, 0))
    row = lambda width: pl.BlockSpec((1, width), lambda b, i: (0, 0))
    return pl.pallas_call(
        _odd_out_kernel,
        out_shape=jax.ShapeDtypeStruct((bsz, seq, d), F32),
        grid=(bsz, seq // tm),
        in_specs=[tok(d), pl.BlockSpec((1, 1, N_MOD, d), lambda b, i: (b, 1, 0, 0)),
                  two(SSD_INNER), tok(SSD_INNER), tok(SSD_INNER), two(ML_VW), tok(ML_VW, 2),
                  row(SSD_INNER), row(SSD_INNER), row(ML_VW),
                  pl.BlockSpec(w_out.shape, lambda b, i: (0, 0))],
        out_specs=pl.BlockSpec((1, tm, d), lambda b, i: (b, i, 0)),
        compiler_params=_cparams(("parallel", "parallel")),
        name="odd_out",
    )(x, modtab, y_s, xbc, ps, y_m, pm, d_skip.reshape(1, SSD_INNER), nw_s.reshape(1, SSD_INNER),
      nw_m.reshape(1, ML_VW), w_out)


def _router_kernel(x_ref, mod_ref, nw_ref, rw_ref, rb_ref, h_ref, idx_ref, prob_ref, rank_ref, cnt_ref, carry_ref,
                   *, tm):
    first = jnp.logical_and(pl.program_id(0) == 0, pl.program_id(1) == 0)

    @pl.when(first)
    def _():
        carry_ref[...] = jnp.zeros_like(carry_ref)

    h = _norm_mod(x_ref[0], nw_ref[...], mod_ref[0, 0], 3)
    h_ref[0] = h
    logits = _dot(h, rw_ref[...], HI) + rb_ref[...]
    lane = lax.broadcasted_iota(I32, (tm, N_EXP), 1).astype(F32)
    work = logits
    sel = jnp.zeros((tm, N_EXP), F32)
    vals, idxs, hots = [], [], []
    for _ in range(TOP_K):
        m = jnp.max(work, axis=1, keepdims=True)
        idx = jnp.min(jnp.where(work == m, lane, float(N_EXP)), axis=1, keepdims=True)
        hot = lane == idx
        vals.append(m)
        idxs.append(idx)
        hots.append(hot.astype(F32))
        sel = sel + hots[-1]
        work = jnp.where(hot, -jnp.inf, work)
    ex = [jnp.exp(v - vals[0]) for v in vals]
    tot = ex[0] + ex[1] + ex[2] + ex[3]
    ti = lax.broadcasted_iota(I32, (tm, tm), 0)
    si = lax.broadcasted_iota(I32, (tm, tm), 1)
    before = _dot((si < ti).astype(BF16), sel.astype(BF16)) + carry_ref[...]
    ranks = [jnp.sum(before * hot, axis=1, keepdims=True) for hot in hots]
    idx_ref[0] = jnp.concatenate(idxs, axis=1).astype(I32)
    prob_ref[0] = jnp.concatenate([e / tot for e in ex], axis=1)
    rank_ref[0] = jnp.concatenate(ranks, axis=1).astype(I32)
    carry_ref[...] = carry_ref[...] + jnp.sum(sel, axis=0, keepdims=True)
    cnt_ref[...] = carry_ref[...]


def router(x, modtab, nw, rw, rb, ctx_len):
    bsz, lt, d = x.shape
    tm = TOK_TILE
    ctx_tiles = ctx_len // tm
    tok = lambda width, dt: (jax.ShapeDtypeStruct((bsz, lt, width), dt),
                             pl.BlockSpec((1, tm, width), lambda b, i: (b, i, 0)))
    outs = [tok(d, F32), tok(TOP_K, I32), tok(TOP_K, F32), tok(TOP_K, I32),
            (jax.ShapeDtypeStruct((1, N_EXP), F32), pl.BlockSpec((1, N_EXP), lambda b, i: (0, 0)))]
    return pl.pallas_call(
        functools.partial(_router_kernel, tm=tm),
        out_shape=[o[0] for o in outs],
        grid=(bsz, lt // tm),
        in_specs=[pl.BlockSpec((1, tm, d), lambda b, i: (b, i, 0)),
                  pl.BlockSpec((1, 1, N_MOD, d), lambda b, i: (b, (i >= ctx_tiles).astype(I32), 0, 0)),
                  pl.BlockSpec((1, d), lambda b, i: (0, 0)),
                  pl.BlockSpec((d, N_EXP), lambda b, i: (0, 0)),
                  pl.BlockSpec((1, N_EXP), lambda b, i: (0, 0))],
        out_specs=[o[1] for o in outs],
        scratch_shapes=[pltpu.VMEM((1, N_EXP), F32)],
        compiler_params=_cparams(("arbitrary", "arbitrary")),
        name="router",
    )(x, modtab, nw.reshape(1, d), rw, rb.reshape(1, N_EXP))


def _gather_rows(tok_ref, blk, src_ref, dst_ref, sem, n_rows):
    def issue(r, carry):
        pltpu.make_async_copy(src_ref.at[pl.ds(tok_ref[blk, r], 1)], dst_ref.at[pl.ds(r, 1)], sem).start()
        return carry

    lax.fori_loop(0, n_rows, issue, 0, unroll=8)


def _expert_kernel(be_ref, tok_ref, h_ref, wg_ref, bg_ref, wu_ref, bu_ref, wd_ref, bd_ref, o_ref, xbuf, sem, *, nblk):
    i = pl.program_id(0)
    slot = i % 2

    @pl.when(i == 0)
    def _():
        _gather_rows(tok_ref, 0, h_ref, xbuf.at[0], sem.at[0], MOE_BLOCK)

    @pl.when(i + 1 < nblk)
    def _():
        _gather_rows(tok_ref, i + 1, h_ref, xbuf.at[1 - slot], sem.at[1 - slot], MOE_BLOCK)

    pltpu.make_async_copy(xbuf.at[slot], xbuf.at[slot], sem.at[slot]).wait()
    xb = xbuf[slot].astype(BF16)
    g = _dot(xb, wg_ref[0]) + bg_ref[0]
    u = _dot(xb, wu_ref[0]) + bu_ref[0]
    g = jnp.minimum(g, SWIGLU_LIMIT)
    u = jnp.clip(u, -SWIGLU_LIMIT, SWIGLU_LIMIT)
    act = (g * _sigmoid(SWIGLU_ALPHA * g) * (u + 1.0)).astype(BF16)
    o_ref[...] = _dot(act, wd_ref[0]) + bd_ref[0]


def expert_ffn(block_expert, sorted_tok, h, wg, bg, wu, bu, wd, bd):
    nblk = block_expert.shape[0]
    d, f = wg.shape[1], wg.shape[2]
    wspec = lambda r, c: pl.BlockSpec((1, r, c), lambda i, be, tok: (be[i], 0, 0))
    grid_spec = pltpu.PrefetchScalarGridSpec(
        num_scalar_prefetch=2,
        grid=(nblk,),
        in_specs=[pl.BlockSpec(memory_space=pl.ANY),
                  wspec(d, f), wspec(1, f), wspec(d, f), wspec(1, f), wspec(f, d), wspec(1, d)],
        out_specs=pl.BlockSpec((MOE_BLOCK, d), lambda i, be, tok: (i, 0)),
        scratch_shapes=[pltpu.VMEM((2, MOE_BLOCK, d), F32), pltpu.SemaphoreType.DMA((2,))],
    )
    return pl.pallas_call(
        functools.partial(_expert_kernel, nblk=nblk),
        out_shape=jax.ShapeDtypeStruct((nblk * MOE_BLOCK, d), F32),
        grid_spec=grid_spec,
        compiler_params=_cparams(("arbitrary",)),
        name="expert_ffn",
    )(block_expert, sorted_tok, h, wg, bg.reshape(N_EXP, 1, f), wu, bu.reshape(N_EXP, 1, f), wd,
      bd.reshape(N_EXP, 1, d))


def _combine_kernel(dest_ref, x_ref, mod_ref, prob_ref, y_ref, fw_ref, o_ref, ybuf, sem, *, tm, ntile, final_norm):
    i = pl.program_id(0) * pl.num_programs(1) + pl.program_id(1)
    slot = i % 2

    def gather(tile, s):
        def issue(r, carry):
            for k in range(TOP_K):
                pltpu.make_async_copy(y_ref.at[pl.ds(dest_ref[tile, r * TOP_K + k], 1)],
                                      ybuf.at[s, k, pl.ds(r, 1)], sem.at[s]).start()
            return carry

        lax.fori_loop(0, tm, issue, 0, unroll=4)

    @pl.when(i == 0)
    def _():
        gather(0, 0)

    @pl.when(i + 1 < ntile)
    def _():
        gather(i + 1, 1 - slot)

    pltpu.make_async_copy(ybuf.at[slot], ybuf.at[slot], sem.at[slot]).wait()
    p = prob_ref[0]
    acc = ybuf[slot, 0] * p[:, 0:1]
    for k in range(1, TOP_K):
        acc = acc + ybuf[slot, k] * p[:, k:k + 1]
    out = x_ref[0] + mod_ref[0, 0][5:6] * acc
    if final_norm:
        out = out * lax.rsqrt(jnp.mean(out * out, axis=-1, keepdims=True) + EPS) * fw_ref[...]
    o_ref[0] = out


def combine(dest, x, modtab, probs, y_sorted, final_w, ctx_len, final_norm):
    bsz, lt, d = x.shape
    tm = TOK_TILE
    ctx_tiles = ctx_len // tm
    ntile = bsz * (lt // tm)
    grid_spec = pltpu.PrefetchScalarGridSpec(
        num_scalar_prefetch=1,
        grid=(bsz, lt // tm),
        in_specs=[pl.BlockSpec((1, tm, d), lambda b, i, dst: (b, i, 0)),
                  pl.BlockSpec((1, 1, N_MOD, d), lambda b, i, dst: (b, (i >= ctx_tiles).astype(I32), 0, 0)),
                  pl.BlockSpec((1, tm, TOP_K), lambda b, i, dst: (b, i, 0)),
                  pl.BlockSpec(memory_space=pl.ANY),
                  pl.BlockSpec((1, d), lambda b, i, dst: (0, 0))],
        out_specs=pl.BlockSpec((1, tm, d), lambda b, i, dst: (b, i, 0)),
        scratch_shapes=[pltpu.VMEM((2, TOP_K, tm, d), F32), pltpu.SemaphoreType.DMA((2,))],
    )
    return pl.pallas_call(
        functools.partial(_combine_kernel, tm=tm, ntile=ntile, final_norm=final_norm),
        out_shape=jax.ShapeDtypeStruct((bsz, lt, d), F32),
        grid_spec=grid_spec,
        compiler_params=_cparams(("arbitrary", "arbitrary")),
        name="combine",
    )(dest, x, modtab, probs, y_sorted, final_w.reshape(1, d))


def moe_layer(x, modtab, nw, rw, rb, wg, bg, wu, bu, wd, bd, final_w, ctx_len, final_norm):
    bsz, lt, d = x.shape
    t = bsz * lt
    h, idx, probs, rank, counts = router(x, modtab, nw, rw, rb, ctx_len)
    counts = counts.reshape(N_EXP).astype(I32)
    padded = (counts + MOE_BLOCK - 1) // MOE_BLOCK * MOE_BLOCK
    ends = jnp.cumsum(padded)
    pstart = ends - padded
    idx = idx.reshape(t, TOP_K)
    dest = pstart[idx] + rank.reshape(t, TOP_K)
    nblk = -(-t * TOP_K // MOE_BLOCK) + N_EXP
    block_row0 = jnp.arange(nblk, dtype=I32) * MOE_BLOCK
    block_expert = jnp.minimum(jnp.sum((ends[None, :] <= block_row0[:, None]).astype(I32), axis=1), N_EXP - 1)
    tok_ids = jnp.broadcast_to(jnp.arange(t, dtype=I32)[:, None], (t, TOP_K))
    sorted_tok = jnp.zeros((nblk * MOE_BLOCK,), I32).at[dest.reshape(-1)].set(tok_ids.reshape(-1))
    y_sorted = expert_ffn(block_expert, sorted_tok.reshape(nblk, MOE_BLOCK), h.reshape(t, d),
                          wg.astype(BF16), bg, wu.astype(BF16), bu, wd.astype(BF16), bd)
    return combine(dest.reshape(t // TOK_TILE, TOK_TILE * TOP_K), x, modtab, probs, y_sorted, final_w, ctx_len,
                   final_norm)


def even_layer(x, modtab, norm1_w, w_in, w_out, mu, w0, w2, a0, a2, g2, k_k, k_a, r_k, ln_w, ln_b, lb, hg_norm_w,
               ctx_len):
    w_in = w_in.astype(BF16)
    pr, ph = in_proj(x, modtab, norm1_w, [w_in[:, :RW_PROJ], w_in[:, RW_PROJ:]], ctx_len)
    r, v, kk, g, bv, lw, km, bb = rwkv_prep(pr, mu, w0, w2, a0, a2, g2, k_k, k_a, r_k, ctx_len)
    o_r = rwkv_scan(r, v, kk, lw, km, bb, ctx_len)
    o_g = gla_scan(ph, lb, ctx_len)
    return even_out(x, modtab, o_r, bv, g, o_g, ph, ln_w, ln_b, hg_norm_w, w_out.astype(BF16), ctx_len)


def _dir_slab(cols, width):
    halves = []
    for z in range(2):
        parts = [c[:, z * (c.shape[1] // 2):(z + 1) * (c.shape[1] // 2)] for c in cols]
        used = sum(p.shape[1] for p in parts)
        parts.append(jnp.zeros((cols[0].shape[0], width - used), cols[0].dtype))
        halves.append(jnp.concatenate(parts, axis=1))
    return jnp.concatenate(halves, axis=1)


def _dir_rows(vals, width):
    rows = jnp.concatenate(vals, axis=1)
    return jnp.pad(rows, ((0, 0), (0, width - rows.shape[1]))).reshape(2, 1, width)


def odd_layer(x, modtab, norm1_w, w_in, w_out, s_conv_w, s_conv_b, dt_bias, a_log, d_skip, s_norm_w, m_conv_w,
              m_conv_b, i_bias, f_bias, m_norm_w, ctx_len):
    o = 0
    cols = {}
    for name, width in (("zg", SSD_INNER), ("xbc", SSD_CONV_CH), ("dt", 2 * SSD_H), ("qk", 2 * ML_QK),
                        ("v", ML_VW), ("og", ML_VW), ("ig", 2 * ML_H), ("fg", 2 * ML_H)):
        cols[name] = w_in[:, o:o + width]
        o += width
    slab_m = jnp.concatenate([cols["qk"], cols["v"], cols["og"]], axis=1).astype(BF16)
    slab_small = _dir_slab([cols["dt"], cols["ig"], cols["fg"]], SMALL_W).astype(BF16)
    pxbc, zg, pm, small = in_proj(x, modtab, norm1_w, [cols["xbc"].astype(BF16), cols["zg"].astype(BF16), slab_m,
                                                       slab_small], ctx_len)
    xbc, qk = odd_prep(pxbc, pm, s_conv_w, s_conv_b, m_conv_w, m_conv_b, ctx_len)
    zeros_g = jnp.zeros((2, 2 * ML_H), F32)
    y_s = ssd_scan(xbc, small, _dir_rows([dt_bias, zeros_g], SMALL_W), _dir_rows([a_log, zeros_g], SMALL_W), ctx_len)
    y_m = mlstm_scan(qk, pm, small, _dir_rows([jnp.zeros((2, SSD_H), F32), i_bias, f_bias], SMALL_W), ctx_len)
    return odd_out(x, modtab, y_s, xbc, zg, y_m, pm, jnp.repeat(d_skip, SSD_P), s_norm_w, m_norm_w,
                   w_out.astype(BF16), ctx_len)


def _to_col_major(t):
    b, s, ch = t.shape
    return t.reshape(b, s // GRID_W, GRID_W, ch).transpose(0, 2, 1, 3).reshape(b, s, ch)


def _to_row_major(t):
    b, s, ch = t.shape
    return t.reshape(b, GRID_W, s // GRID_W, ch).transpose(0, 2, 1, 3).reshape(b, s, ch)


def kernel(x, c, ctx, c_ctx, ada_w, ada_b, norm1_w, norm2_w, even_w_in, even_w_out, rwkv_mu, rwkv_w0, rwkv_w2, rwkv_a0, rwkv_a2, rwkv_g2, rwkv_k_k, rwkv_k_a, rwkv_r_k, rwkv_ln_w, rwkv_ln_b, hgrn_lower_bounds, hgrn_norm_w, odd_w_in, odd_w_out, ssd_conv_w, ssd_conv_b, ssd_dt_bias, ssd_a_log, ssd_d, ssd_norm_w, mlstm_conv_w, mlstm_conv_b, mlstm_i_bias, mlstm_f_bias, mlstm_norm_w, router_w, router_b, exp_w_gate, exp_b_gate, exp_w_up, exp_b_up, exp_w_down, exp_b_down, final_norm_w):
    lc = ctx.shape[1]
    lower_bounds = jnp.cumsum(jax.nn.softmax(hgrn_lower_bounds.astype(F32), axis=0), axis=0)
    moe = lambda l: (norm2_w[l], router_w[l], router_b[l], exp_w_gate[l], exp_b_gate[l], exp_w_up[l], exp_b_up[l],
                     exp_w_down[l], exp_b_down[l], final_norm_w)
    xa = jnp.concatenate([ctx, x], axis=1)
    modtab = _mod_table(c, c_ctx, ada_w[0], ada_b[0])
    xa = even_layer(xa, modtab, norm1_w[0], even_w_in[0], even_w_out[0], rwkv_mu[0], rwkv_w0[0], rwkv_w2[0],
                    rwkv_a0[0], rwkv_a2[0], rwkv_g2[0], rwkv_k_k[0], rwkv_k_a[0], rwkv_r_k[0], rwkv_ln_w[0],
                    rwkv_ln_b[0], lower_bounds[0], hgrn_norm_w[0], lc)
    xa = moe_layer(xa, modtab, *moe(0), lc, False)
    xa = jnp.concatenate([xa[:, :lc], _to_col_major(xa[:, lc:])], axis=1)
    modtab = _mod_table(c, c_ctx, ada_w[1], ada_b[1])
    xl = odd_layer(xa, modtab, norm1_w[1], odd_w_in[0], odd_w_out[0], ssd_conv_w[0], ssd_conv_b[0], ssd_dt_bias[0],
                   ssd_a_log[0], ssd_d[0], ssd_norm_w[0], mlstm_conv_w[0], mlstm_conv_b[0], mlstm_i_bias[0],
                   mlstm_f_bias[0], mlstm_norm_w[0], lc)
    xl = moe_layer(xl, modtab, *moe(1), 0, True)
    return _to_row_major(xl)
```

```python
import functools
import math

import jax
import jax.numpy as jnp
from jax import lax
from jax.experimental import pallas as pl
from jax.experimental.pallas import tpu as pltpu

F32 = jnp.float32
BF16 = jnp.bfloat16
I32 = jnp.int32
HI = lax.Precision.HIGHEST

D_MODEL = 1024
GRID_W = 64
N_MOD = 6
EPS = 1e-6
RW_H, RW_N = 8, 64
RW_W = RW_H * RW_N
RW_RANK = 64
RW_GATE_RANK = 128
RW_GN_EPS = 64e-5
RW_PROJ = 3 * RW_W + 4 * RW_RANK + RW_GATE_RANK
RW_CHUNK = 64
HG_H, HG_DK, HG_DV = 4, 128, 128
HG_W = HG_H * HG_DK
HG_CHUNK = 32
HG_PROJ = 5 * HG_W
SSD_H, SSD_P, SSD_G, SSD_N = 16, 64, 2, 128
SSD_INNER = SSD_H * SSD_P
SSD_CONV_CH = SSD_INNER + 2 * SSD_G * SSD_N
SSD_CHUNK = 64
ML_H, ML_DK, ML_DV = 4, 128, 256
ML_QK = ML_H * ML_DK
ML_VW = ML_H * ML_DV
ML_CHUNK = 64
ML_M_INIT = -1e30
CONV_K = 5
N_EXP = 32
TOP_K = 4
D_FF = 1024
SWIGLU_LIMIT = 7.0
SWIGLU_ALPHA = 1.702
MOE_BLOCK = 512

TOK_TILE = 256
SCAN_BLOCK = 256
HALO = 8
SMALL_W = 128
VMEM_LIMIT = 56 * 1024 * 1024


def _cparams(sem):
    return pltpu.CompilerParams(dimension_semantics=sem, vmem_limit_bytes=VMEM_LIMIT)


def _dot(a, b, precision=None):
    return jnp.dot(a, b, preferred_element_type=F32, precision=precision)


def _dot_nt(a, b, precision=None):
    return lax.dot_general(a, b, (((1,), (1,)), ((), ())), preferred_element_type=F32, precision=precision)


def _dot_tn(a, b, precision=None):
    return lax.dot_general(a, b, (((0,), (0,)), ((), ())), preferred_element_type=F32, precision=precision)


def _bf16_parts(x):
    hi = x.astype(BF16)
    rest = x - hi.astype(F32)
    mid = rest.astype(BF16)
    return hi, mid, (rest - mid.astype(F32)).astype(BF16)


def _dot_into_01(x, sel):
    sel = sel.astype(BF16)
    p0, p1, p2 = _bf16_parts(x)
    return _dot(p0, sel) + _dot(p1, sel) + _dot(p2, sel)


def _dot_01_into(sel, x):
    sel = sel.astype(BF16)
    p0, p1, p2 = _bf16_parts(x)
    return _dot(sel, p0) + _dot(sel, p1) + _dot(sel, p2)


def _sigmoid(x):
    return 1.0 / (1.0 + jnp.exp(-x))


def _silu(x):
    return x * _sigmoid(x)


def _softplus(x):
    return jnp.maximum(x, 0.0) + jnp.log(1.0 + jnp.exp(-jnp.abs(x)))


def _dir_masks(z, c):
    ti = lax.broadcasted_iota(I32, (c, c), 0)
    si = lax.broadcasted_iota(I32, (c, c), 1)
    d = (si - ti) * jnp.where(z == 0, 1, -1)
    return d <= 0, d < 0


def _time_block(z, n, ctx_blocks, n_blocks):
    rev = jnp.where(n < ctx_blocks, ctx_blocks - 1 - n, n_blocks - 1 - (n - ctx_blocks))
    return jnp.where(z == 0, n, rev)


def _mod_kernel(c_ref, w_ref, b_ref, o_ref):
    o_ref[...] = _dot(_silu(c_ref[...]), w_ref[...]) + b_ref[...]


def ada_mod(cc, w, b):
    rows, d = cc.shape
    n = w.shape[1]
    tn = 1536
    return pl.pallas_call(
        _mod_kernel,
        out_shape=jax.ShapeDtypeStruct((rows, n), F32),
        grid=(n // tn,),
        in_specs=[pl.BlockSpec((rows, d), lambda j: (0, 0)),
                  pl.BlockSpec((d, tn), lambda j: (0, j)),
                  pl.BlockSpec((1, tn), lambda j: (0, j))],
        out_specs=pl.BlockSpec((rows, tn), lambda j: (0, j)),
        compiler_params=_cparams(("arbitrary",)),
        name="ada_mod",
    )(cc, w, b.reshape(1, n))


def _mod_table(c, c_ctx, w, b):
    bsz = c.shape[0]
    rows = -(-(bsz + 1) // 8) * 8
    cc = jnp.zeros((rows, D_MODEL), F32).at[:bsz].set(c).at[bsz].set(c_ctx)
    m = ada_mod(cc, w, b)
    mod = m[:bsz].reshape(bsz, 1, N_MOD, D_MODEL)
    mod_c = jnp.broadcast_to(m[bsz].reshape(1, 1, N_MOD, D_MODEL), (bsz, 1, N_MOD, D_MODEL))
    return jnp.concatenate([mod_c, mod], axis=1)


def _norm_mod(x, nw, m, shift_idx):
    y = x * lax.rsqrt(jnp.mean(x * x, axis=-1, keepdims=True) + EPS) * nw
    return y * (1.0 + m[shift_idx + 1:shift_idx + 2]) + m[shift_idx:shift_idx + 1]


def _in_proj_kernel(x_ref, mod_ref, nw_ref, *rest, n_out):
    w_refs, o_refs = rest[:n_out], rest[n_out:]
    h = _norm_mod(x_ref[0], nw_ref[...], mod_ref[0, 0], 0).astype(BF16)
    for w_ref, o_ref in zip(w_refs, o_refs):
        o_ref[0] = _dot(h, w_ref[...])


def in_proj(x, modtab, nw, slabs, ctx_len):
    bsz, lt, d = x.shape
    tm = TOK_TILE
    ctx_tiles = ctx_len // tm
    in_specs = [pl.BlockSpec((1, tm, d), lambda b, i: (b, i, 0)),
                pl.BlockSpec((1, 1, N_MOD, d), lambda b, i: (b, (i >= ctx_tiles).astype(I32), 0, 0)),
                pl.BlockSpec((1, d), lambda b, i: (0, 0))]
    in_specs += [pl.BlockSpec(w.shape, lambda b, i: (0, 0)) for w in slabs]
    return pl.pallas_call(
        functools.partial(_in_proj_kernel, n_out=len(slabs)),
        out_shape=[jax.ShapeDtypeStruct((bsz, lt, w.shape[1]), F32) for w in slabs],
        grid=(bsz, lt // tm),
        in_specs=in_specs,
        out_specs=[pl.BlockSpec((1, tm, w.shape[1]), lambda b, i: (b, i, 0)) for w in slabs],
        compiler_params=_cparams(("parallel", "parallel")),
        name="in_proj",
    )(x, modtab, nw.reshape(1, d), *slabs)


def _halo_specs(tm, width, lt):
    per = tm // HALO
    last = lt // HALO - 1
    return [pl.BlockSpec((1, tm, width), lambda b, i: (b, i, 0)),
            pl.BlockSpec((1, HALO, width), lambda b, i: (b, jnp.maximum(i * per - 1, 0), 0)),
            pl.BlockSpec((1, HALO, width), lambda b, i: (b, jnp.minimum((i + 1) * per, last), 0))]


def _segment_edges(i, tm, ctx_len, lt):
    row0 = i * tm
    has_prev = jnp.logical_and(row0 != 0, row0 != ctx_len)
    has_next = jnp.logical_and(row0 + tm != ctx_len, row0 + tm != lt)
    return has_prev, has_next


def _rwkv_prep_kernel(p_ref, pp_ref, pn_ref, mu_ref, w0_ref, w2_ref, a0_ref, a2_ref, g2_ref, kk_ref, ka_ref,
                      rk_ref, bd_ref, r_o, v_o, kk_o, g_o, bv_o, lw_o, km_o, bb_o, *, tm, ctx_len, lt):
    has_prev, has_next = _segment_edges(pl.program_id(1), tm, ctx_len, lt)
    p = p_ref[0]
    prev_row = jnp.where(has_prev, pp_ref[0, HALO - 1:HALO, :], 0.0)
    next_row = jnp.where(has_next, pn_ref[0, 0:1, :], 0.0)
    rid = lax.broadcasted_iota(I32, (tm, 1), 0)
    up = jnp.where(rid == 0, prev_row, pltpu.roll(p, 1, 0))
    dn = jnp.where(rid == tm - 1, next_row, pltpu.roll(p, tm - 1, 0))
    p = p + (0.5 * (up + dn) - p) * mu_ref[...]
    w = RW_W
    r, k, v = p[:, 0:w], p[:, w:2 * w], p[:, 2 * w:3 * w]
    o = 3 * w
    wd = jnp.tanh(p[:, o:o + 2 * RW_RANK])
    ad = p[:, o + 2 * RW_RANK:o + 4 * RW_RANK]
    gd = p[:, o + 4 * RW_RANK:o + 4 * RW_RANK + RW_GATE_RANK]
    bd = bd_ref[...]
    kk = k * kk_ref[...]
    kk = kk * lax.rsqrt(_dot_into_01(kk * kk, bd) + 1e-12)
    r_o[0] = r
    v_o[0] = v
    kk_o[0] = kk
    g_o[0] = _dot(_sigmoid(gd), g2_ref[...])
    km_sum = jnp.zeros_like(k)
    for z in range(2):
        sl = slice(z * RW_RANK, (z + 1) * RW_RANK)
        w_pre = w0_ref[z:z + 1, :] + _dot(wd[:, sl], w2_ref[z])
        lw_o[z, 0] = -_sigmoid(w_pre) * math.exp(-0.5)
        a = _sigmoid(a0_ref[z:z + 1, :] + _dot(ad[:, sl], a2_ref[z]))
        km = k * (1.0 + (a - 1.0) * ka_ref[...])
        km_o[z, 0] = km
        bb_o[z, 0] = kk * a
        km_sum = km_sum + km
    bv_o[0] = _dot_into_01(r * km_sum * rk_ref[...], bd) * v


def _head_block_ones(width, head):
    i = jnp.arange(width) // head
    return (i[:, None] == i[None, :]).astype(F32)


def rwkv_prep(p, mu, w0, w2, a0, a2, g2, k_k, k_a, r_k, ctx_len):
    bsz, lt, width = p.shape
    tm = TOK_TILE
    w = RW_W
    full = lambda a: pl.BlockSpec(a.shape, lambda b, i: (0,) * a.ndim)
    params = [mu.reshape(1, width), w0, w2, a0, a2, g2, k_k.reshape(1, w), k_a.reshape(1, w), r_k.reshape(1, w),
              _head_block_ones(w, RW_N)]
    one = jax.ShapeDtypeStruct((bsz, lt, w), F32)
    two = jax.ShapeDtypeStruct((2, bsz, lt, w), F32)
    one_spec = pl.BlockSpec((1, tm, w), lambda b, i: (b, i, 0))
    two_spec = pl.BlockSpec((2, 1, tm, w), lambda b, i: (0, b, i, 0))
    return pl.pallas_call(
        functools.partial(_rwkv_prep_kernel, tm=tm, ctx_len=ctx_len, lt=lt),
        out_shape=[one] * 5 + [two] * 3,
        grid=(bsz, lt // tm),
        in_specs=_halo_specs(tm, width, lt) + [full(a) for a in params],
        out_specs=[one_spec] * 5 + [two_spec] * 3,
        compiler_params=_cparams(("parallel", "parallel")),
        name="rwkv_prep",
    )(p, p, p, *params)


def _unit_lower_inverses(n_mats, eye, c):
    xs = [eye - n for n in n_mats]
    ps = [_dot(n, n) for n in n_mats]
    span = 4
    while True:
        xs = [x + _dot(x, p) for x, p in zip(xs, ps)]
        if span >= c:
            return xs
        ps = [_dot(p, p) for p in ps]
        span *= 2


def _rwkv_scan_kernel(r_ref, v_ref, kk_ref, lw_ref, km_ref, bb_ref, o_ref, s_ref, *, tb, c):
    z = pl.program_id(0)
    nsub = tb // c

    @pl.when(pl.program_id(2) == 0)
    def _():
        s_ref[...] = jnp.zeros_like(s_ref)

    incl, strict = _dir_masks(z, c)
    incl_f, strict_f = incl.astype(F32), strict.astype(F32)
    eye = incl_f - strict_f

    def body(j, s_old):
        jj = jnp.where(z == 0, j, nsub - 1 - j)
        rows = pl.ds(pl.multiple_of(jj * c, c), c)
        r, v, kk = r_ref[0, rows, :], v_ref[0, rows, :], kk_ref[0, rows, :]
        lw, km, bb = lw_ref[0, 0, rows, :], km_ref[0, 0, rows, :], bb_ref[0, 0, rows, :]
        cum = _dot_01_into(incl_f, lw)
        tot = jnp.sum(lw, axis=0, keepdims=True)
        e_neg = jnp.exp(-cum)
        e_end = jnp.exp(tot - cum)
        a_t = kk * jnp.exp(cum - lw)
        r_t = r * jnp.exp(cum)
        b_t, k_t = bb * e_neg, km * e_neg
        b_p, k_p = bb * e_end, km * e_end
        e_tot = jnp.exp(tot)
        heads = [slice(h * RW_N, (h + 1) * RW_N) for h in range(RW_H)]
        ar =[jnp.concatenate([a_t[:, hs], r_t[:, hs]], axis=0) for hs in heads]
        bk = [jnp.concatenate([b_t[:, hs], k_t[:, hs]], axis=0) for hs in heads]
        bkp = [jnp.concatenate([b_p[:, hs], k_p[:, hs]], axis=0) for hs in heads]
        vh = [v[:, hs] for hs in heads]
        sc = [_dot_nt(a, b) for a, b in zip(ar, bk)]
        ars = [_dot_nt(a, s) for a, s in zip(ar, s_old)]
        m_ak_v = [_dot(x[:c, c:] * strict_f, vv) for x, vv in zip(sc, vh)]
        inv = _unit_lower_inverses([x[:c, :c] * strict_f for x in sc], eye, c)
        u = [-_dot(i, a[:c] + m) for i, a, m in zip(inv, ars, m_ak_v)]
        uv = [jnp.concatenate([uu, vv], axis=0) for uu, vv in zip(u, vh)]
        n_rbk = [jnp.concatenate([x[c:, :c] * incl_f, x[c:, c:] * incl_f], axis=1) for x in sc]
        outs = [a[c:] + _dot(n, w) for a, n, w in zip(ars, n_rbk, uv)]
        o_ref[0, 0, rows, :] = jnp.concatenate(outs, axis=1)
        return tuple(s * e_tot[:, hs] + _dot_tn(w, b) for s, hs, w, b in zip(s_old, heads, uv, bkp))

    s_fin = lax.fori_loop(0, nsub, body, tuple(s_ref[h] for h in range(RW_H)), unroll=2)
    for h in range(RW_H):
        s_ref[h] = s_fin[h]


def rwkv_scan(r, v, kk, lw, km, bb, ctx_len):
    bsz, lt, w = r.shape
    tb = SCAN_BLOCK
    nb, cb = lt // tb, ctx_len // tb
    tmap = lambda z, b, n: _time_block(z, n, cb, nb)
    one = pl.BlockSpec((1, tb, w), lambda z, b, n: (b, tmap(z, b, n), 0))
    two = pl.BlockSpec((1, 1, tb, w), lambda z, b, n: (z, b, tmap(z, b, n), 0))
    return pl.pallas_call(
        functools.partial(_rwkv_scan_kernel, tb=tb, c=RW_CHUNK),
        out_shape=jax.ShapeDtypeStruct((2, bsz, lt, w), F32),
        grid=(2, bsz, nb),
        in_specs=[one, one, one, two, two, two],
        out_specs=two,
        scratch_shapes=[pltpu.VMEM((RW_H, RW_N, RW_N), F32)],
        compiler_params=_cparams(("parallel", "parallel", "arbitrary")),
        name="rwkv_scan",
    )(r, v, kk, lw, km, bb)


def _gla_kernel(q_ref, f_ref, i_ref, lb_ref, o_ref, s_ref, *, tb, c):
    z = pl.program_id(0)
    nsub = tb // c

    @pl.when(pl.program_id(2) == 0)
    def _():
        s_ref[...] = jnp.zeros_like(s_ref)

    incl, _ = _dir_masks(z, c)
    incl_f = incl.astype(F32)
    lb = lb_ref[...]

    def body(j, s_old):
        jj = jnp.where(z == 0, j, nsub - 1 - j)
        rows = pl.ds(pl.multiple_of(jj * c, c), c)
        q = _silu(q_ref[0, rows, :]) * HG_DK ** -0.5
        f = f_ref[0, rows, :]
        v = i_ref[0, rows, :]
        log_f = jnp.log(lb + (1.0 - lb) * _sigmoid(f))
        k = (1.0 - lb) * _sigmoid(-f)
        cum = _dot_01_into(incl_f, log_f)
        tot = jnp.sum(log_f, axis=0, keepdims=True)
        q_in = q * jnp.exp(cum)
        k_in = k * jnp.exp(-cum)
        k_end = k * jnp.exp(tot - cum)
        e_tot = jnp.exp(tot)
        heads = [slice(h * HG_DK, (h + 1) * HG_DK) for h in range(HG_H)]
        a = [_dot_nt(q_in[:, hs], k_in[:, hs]) * incl_f for hs in heads]
        inter = [_dot_nt(q_in[:, hs], s) for hs, s in zip(heads, s_old)]
        grow = [_dot_tn(v[:, hs], k_end[:, hs]) for hs in heads]
        outs = [_dot(aa, v[:, hs]) + ii for aa, hs, ii in zip(a, heads, inter)]
        o_ref[0, 0, rows, :] = jnp.concatenate(outs, axis=1)
        return tuple(s * e_tot[:, hs] + g for s, hs, g in zip(s_old, heads, grow))

    s_fin = lax.fori_loop(0, nsub, body, tuple(s_ref[h] for h in range(HG_H)), unroll=4)
    for h in range(HG_H):
        s_ref[h] = s_fin[h]


def gla_scan(ph, lb, ctx_len):
    bsz, lt, _ = ph.shape
    w = HG_W
    tb = SCAN_BLOCK
    nb, cb = lt // tb, ctx_len // tb
    tmap = lambda z, n: _time_block(z, n, cb, nb)
    return pl.pallas_call(
        functools.partial(_gla_kernel, tb=tb, c=HG_CHUNK),
        out_shape=jax.ShapeDtypeStruct((2, bsz, lt, w), F32),
        grid=(2, bsz, nb),
        in_specs=[pl.BlockSpec((1, tb, w), lambda z, b, n: (b, tmap(z, n), 0)),
                  pl.BlockSpec((1, tb, w), lambda z, b, n: (b, tmap(z, n), 1 + z)),
                  pl.BlockSpec((1, tb, w), lambda z, b, n: (b, tmap(z, n), 3)),
                  pl.BlockSpec((1, w), lambda z, b, n: (0, 0))],
        out_specs=pl.BlockSpec((1, 1, tb, w), lambda z, b, n: (z, b, tmap(z, n), 0)),
        scratch_shapes=[pltpu.VMEM((HG_H, HG_DV, HG_DK), F32)],
        compiler_params=_cparams(("parallel", "parallel", "arbitrary")),
        name="gla_scan",
    )(ph, ph, ph, lb.reshape(1, w))


def _group_rms(y, groups, gain):
    width = y.shape[-1] // groups
    parts = []
    for g in range(groups):
        yg = y[:, g * width:(g + 1) * width]
        parts.append(yg * lax.rsqrt(jnp.mean(yg * yg, axis=-1, keepdims=True) + EPS))
    return jnp.concatenate(parts, axis=1) * gain


def _even_out_kernel(x_ref, mod_ref, or_ref, bv_ref, g_ref, og_ref, gate_ref, lnw_ref, lnb_ref, nw_ref, bd_ref,
                     w_ref, o_ref):
    bd = bd_ref[...]
    o = or_ref[0, 0] + or_ref[1, 0]
    mu = _dot_into_01(o, bd) * (1.0 / RW_N)
    oc = o - mu
    var = _dot_into_01(oc * oc, bd) * (1.0 / RW_N)
    yr = (oc * lax.rsqrt(var + RW_GN_EPS) * lnw_ref[...] + lnb_ref[...] + bv_ref[0]) * g_ref[0]
    og = og_ref[0, 0] + og_ref[1, 0]
    yh = _group_rms(og, HG_H, nw_ref[...]) * _silu(gate_ref[0])
    y = jnp.concatenate([yr, yh], axis=1).astype(BF16)
    o_ref[0] = x_ref[0] + mod_ref[0, 0][2:3] * _dot(y, w_ref[...])


def even_out(x, modtab, o_r, bv, g, o_g, ph, ln_w, ln_b, norm_w, w_out, ctx_len):
    bsz, lt, d = x.shape
    tm = TOK_TILE
    ctx_tiles = ctx_len // tm
    w = RW_W
    tok = lambda width, col=0: pl.BlockSpec((1, tm, width), lambda b, i: (b, i, col))
    two = pl.BlockSpec((2, 1, tm, w), lambda b, i: (0, b, i, 0))
    row = lambda width: pl.BlockSpec((1, width), lambda b, i: (0, 0))
    return pl.pallas_call(
        _even_out_kernel,
        out_shape=jax.ShapeDtypeStruct((bsz, lt, d), F32),
        grid=(bsz, lt // tm),
        in_specs=[tok(d),
                  pl.BlockSpec((1, 1, N_MOD, d), lambda b, i: (b, (i >= ctx_tiles).astype(I32), 0, 0)),
                  two, tok(w), tok(w), two, tok(HG_W, 4), row(w), row(w), row(HG_W),
                  pl.BlockSpec((w, w), lambda b, i: (0, 0)),
                  pl.BlockSpec(w_out.shape, lambda b, i: (0, 0))],
        out_specs=tok(d),
        compiler_params=_cparams(("parallel", "parallel")),
        name="even_out",
    )(x, modtab, o_r, bv, g, o_g, ph, ln_w.reshape(1, w), ln_b.reshape(1, w), norm_w.reshape(1, HG_W),
      _head_block_ones(w, RW_N), w_out)


def _conv_silu(x_ref, xp_ref, xn_ref, w_ref, b_ref, has_prev, has_next, tm):
    prev = jnp.where(has_prev, xp_ref[0], 0.0)
    nxt = jnp.where(has_next, xn_ref[0], 0.0)
    ext = jnp.concatenate([prev, x_ref[0], nxt], axis=0)
    rows = tm + 2 * HALO
    pad = CONV_K // 2
    acc = b_ref[...] + jnp.zeros((tm, ext.shape[1]), F32)
    for j in range(CONV_K):
        d = j - pad
        sh = ext if d == 0 else pltpu.roll(ext, (-d) % rows, 0)
        acc = acc + sh[HALO:HALO + tm] * w_ref[j:j + 1, :]
    return _silu(acc)


def _odd_prep_kernel(xs_ref, xsp_ref, xsn_ref, xm_ref, xmp_ref, xmn_ref, ws_ref, bs_ref, wm_ref, bm_ref,
                     os_ref, om_ref, *, tm, ctx_len, lt):
    has_prev, has_next = _segment_edges(pl.program_id(1), tm, ctx_len, lt)
    os_ref[0] = _conv_silu(xs_ref, xsp_ref, xsn_ref, ws_ref, bs_ref, has_prev, has_next, tm)
    om_ref[0] = _conv_silu(xm_ref, xmp_ref, xmn_ref, wm_ref, bm_ref, has_prev, has_next, tm)


def odd_prep(xbc, qk, ws, bs, wm, bm, ctx_len):
    bsz, lt, cs = xbc.shape
    cm = wm.shape[1]
    tm = TOK_TILE
    full = lambda a: pl.BlockSpec(a.shape, lambda b, i: (0,) * a.ndim)
    params = [ws, bs.reshape(1, cs), wm, bm.reshape(1, cm)]
    return pl.pallas_call(
        functools.partial(_odd_prep_kernel, tm=tm, ctx_len=ctx_len, lt=lt),
        out_shape=[jax.ShapeDtypeStruct((bsz, lt, cs), F32), jax.ShapeDtypeStruct((bsz, lt, cm), F32)],
        grid=(bsz, lt // tm),
        in_specs=_halo_specs(tm, cs, lt) + _halo_specs(tm, cm, lt) + [full(a) for a in params],
        out_specs=[pl.BlockSpec((1, tm, cs), lambda b, i: (b, i, 0)),
                   pl.BlockSpec((1, tm, cm), lambda b, i: (b, i, 0))],
        compiler_params=_cparams(("parallel", "parallel")),
        name="odd_prep",
    )(xbc, xbc, xbc, qk, qk, qk, *params)


def _ssd_kernel(x_ref, b_ref, c_ref, sm_ref, dtb_ref, alog_ref, ex_ref, o_ref, s_ref, *, tb, c):
    z = pl.program_id(0)
    nsub = tb // c
    hg = SSD_H // SSD_G

    @pl.when(pl.program_id(2) == 0)
    def _():
        s_ref[...] = jnp.zeros_like(s_ref)

    incl, _ = _dir_masks(z, c)
    incl_f = incl.astype(F32)
    ex = ex_ref[...]

    gw = hg * SSD_P

    def body(j, s_old):
        jj = jnp.where(z == 0, j, nsub - 1 - j)
        rows = pl.ds(pl.multiple_of(jj * c, c), c)
        x = x_ref[0, rows, :]
        bm = b_ref[0, rows, :]
        cm = c_ref[0, rows, :]
        dt = _softplus(sm_ref[0, rows, :] + dtb_ref[0])
        la = -dt * jnp.exp(alog_ref[0])
        a_cum = _dot_01_into(incl_f, la)
        a_tot = jnp.sum(la, axis=0, keepdims=True)
        a_cum_t = a_cum.T
        e_tot_col = jnp.exp(jnp.where(z == 0, a_cum_t[:, c - 1:c], a_cum_t[:, 0:1]))
        wide = _dot_into_01(jnp.concatenate([dt, jnp.exp(a_cum), jnp.exp(a_tot - a_cum)], axis=0), ex)
        xdt = x * wide[:c]
        din, dend = wide[c:2 * c], wide[2 * c:]
        xd_end = xdt * dend
        bgs = [bm[:, g * SSD_N:(g + 1) * SSD_N] for g in range(SSD_G)]
        cgs = [cm[:, g * SSD_N:(g + 1) * SSD_N] for g in range(SSD_G)]
        cb = [_dot_nt(cg, bg) for cg, bg in zip(cgs, bgs)]
        inter = [_dot_nt(cg, s) for cg, s in zip(cgs, s_old)]
        grow = [_dot_tn(xd_end[:, g * gw:(g + 1) * gw], bg) for g, bg in enumerate(bgs)]
        intra = []
        for h in range(SSD_H):
            seg = a_cum[:, h:h + 1] - a_cum_t[h:h + 1, :]
            wmat = cb[h // hg] * jnp.where(incl, jnp.exp(seg), 0.0)
            intra.append(_dot(wmat, xdt[:, h * SSD_P:(h + 1) * SSD_P]))
        o_ref[0, 0, rows, :] = jnp.concatenate(intra, axis=1) + din * jnp.concatenate(inter, axis=1)
        new = []
        for g in range(SSD_G):
            decay = jnp.concatenate([jnp.broadcast_to(e_tot_col[h:h + 1, :], (SSD_P, 1))
                                     for h in range(g * hg, (g + 1) * hg)], axis=0)
            new.append(s_old[g] * decay + grow[g])
        return tuple(new)

    s_fin = lax.fori_loop(0, nsub, body, tuple(s_ref[g] for g in range(SSD_G)), unroll=2)
    for g in range(SSD_G):
        s_ref[g] = s_fin[g]


def _head_expand(heads, width, rows):
    h = jnp.arange(rows)[:, None]
    col = jnp.arange(heads * width)[None, :] // width
    return (h == col).astype(F32)


def ssd_scan(xbc, small, dtb, alog, ctx_len):
    bsz, lt, _ = xbc.shape
    tb = SCAN_BLOCK
    nb, cb = lt // tb, ctx_len // tb
    tmap = lambda z, n: _time_block(z, n, cb, nb)
    gw = SSD_G * SSD_N
    return pl.pallas_call(
        functools.partial(_ssd_kernel, tb=tb, c=SSD_CHUNK),
        out_shape=jax.ShapeDtypeStruct((2, bsz, lt, SSD_INNER), F32),
        grid=(2, bsz, nb),
        in_specs=[pl.BlockSpec((1, tb, SSD_INNER), lambda z, b, n: (b, tmap(z, n), 0)),
                  pl.BlockSpec((1, tb, gw), lambda z, b, n: (b, tmap(z, n), SSD_INNER // gw)),
                  pl.BlockSpec((1, tb, gw), lambda z, b, n: (b, tmap(z, n), SSD_INNER // gw + 1)),
                  pl.BlockSpec((1, tb, SMALL_W), lambda z, b, n: (b, tmap(z, n), z)),
                  pl.BlockSpec((1, 1, SMALL_W), lambda z, b, n: (z, 0, 0)),
                  pl.BlockSpec((1, 1, SMALL_W), lambda z, b, n: (z, 0, 0)),
                  pl.BlockSpec((SMALL_W, SSD_INNER), lambda z, b, n: (0, 0))],
        out_specs=pl.BlockSpec((1, 1, tb, SSD_INNER), lambda z, b, n: (z, b, tmap(z, n), 0)),
        scratch_shapes=[pltpu.VMEM((SSD_G, SSD_H // SSD_G * SSD_P, SSD_N), F32)],
        compiler_params=_cparams(("parallel", "parallel", "arbitrary")),
        name="ssd_scan",
    )(xbc, xbc, xbc, small, dtb, alog, _head_expand(SSD_H, SSD_P, SMALL_W))


ML_I_LANE = SSD_H
ML_F_LANE = SSD_H + ML_H


def _mlstm_kernel(q_ref, k_ref, v_ref, sm_ref, gb_ref, o_ref, cs_ref, ns_ref, m_ref, *, tb, c):
    z = pl.program_id(0)
    nsub = tb // c

    @pl.when(pl.program_id(2) == 0)
    def _():
        cs_ref[...] = jnp.zeros_like(cs_ref)
        ns_ref[...] = jnp.zeros_like(ns_ref)
        m_ref[...] = jnp.full(m_ref.shape, ML_M_INIT, F32)

    incl, _ = _dir_masks(z, c)
    incl_f = incl.astype(F32)
    lane = lax.broadcasted_iota(I32, (1, SMALL_W), 1)
    is_f = jnp.logical_and(lane >= ML_F_LANE, lane < ML_F_LANE + ML_H)

    def body(j, carry):
        cs_old, ns_old, m_old = carry
        jj = jnp.where(z == 0, j, nsub - 1 - j)
        rows = pl.ds(pl.multiple_of(jj * c, c), c)
        q = q_ref[0, rows, :]
        k = k_ref[0, rows, :] * ML_DK ** -0.5
        v = v_ref[0, rows, :]
        pre = sm_ref[0, rows, :] + gb_ref[0]
        gates = jnp.where(is_f, -_softplus(-pre), pre)
        f_cum = _dot(incl_f, gates, HI)
        f_tot = jnp.sum(gates, axis=0, keepdims=True)
        f_cum_t = f_cum.T
        gates_t = gates.T
        qs = [q[:, h * ML_DK:(h + 1) * ML_DK] for h in range(ML_H)]
        ks = [k[:, h * ML_DK:(h + 1) * ML_DK] for h in range(ML_H)]
        vs = [v[:, h * ML_DV:(h + 1) * ML_DV] for h in range(ML_H)]
        qk = [_dot_nt(a, b) for a, b in zip(qs, ks)]
        q_cs = [_dot(a, s) for a, s in zip(qs, cs_old)]
        outs, cs_new, ns_new, m_new_all = [], [], [], []
        for h in range(ML_H):
            li, lf = ML_I_LANE + h, ML_F_LANE + h
            f_col = f_cum[:, lf:lf + 1]
            f_last = f_tot[:, lf:lf + 1]
            logw_end = f_last - f_col + gates[:, li:li + 1]
            m_end = jnp.max(logw_end, axis=0, keepdims=True)
            kw = ks[h] * jnp.exp(logw_end - m_end)
            m_prev = m_old[h]
            logw = jnp.where(incl, f_col - f_cum_t[lf:lf + 1, :] + gates_t[li:li + 1, :], -jnp.inf)
            m_t = jnp.maximum(jnp.max(logw, axis=1, keepdims=True), f_col + m_prev)
            w_inter = jnp.exp(f_col + m_prev - m_t)
            scores = qk[h] * jnp.exp(logw - m_t)
            num = _dot(scores, vs[h]) + w_inter * q_cs[h]
            den = (jnp.sum(scores, axis=1, keepdims=True)
                   + w_inter * jnp.sum(qs[h] * ns_old[h], axis=1, keepdims=True))
            outs.append(num / jnp.maximum(jnp.abs(den), jnp.exp(-m_t)))
            m_new = jnp.maximum(f_last + m_prev, m_end)
            s_keep = jnp.exp(f_last + m_prev - m_new)
            s_loc = jnp.exp(m_end - m_new)
            cs_new.append(s_keep * cs_old[h] + s_loc * _dot_tn(kw, vs[h]))
            ns_new.append(s_keep * ns_old[h] + s_loc * jnp.sum(kw, axis=0, keepdims=True))
            m_new_all.append(m_new)
        o_ref[0, 0, rows, :] = jnp.concatenate(outs, axis=1)
        return tuple(cs_new), tuple(ns_new), tuple(m_new_all)

    heads = range(ML_H)
    init = (tuple(cs_ref[h] for h in heads), tuple(ns_ref[h] for h in heads), tuple(m_ref[h] for h in heads))
    cs_fin, ns_fin, m_fin = lax.fori_loop(0, nsub, body, init, unroll=2)
    for h in heads:
        cs_ref[h] = cs_fin[h]
        ns_ref[h] = ns_fin[h]
        m_ref[h] = m_fin[h]


def mlstm_scan(qk, pm, small, gate_bias, ctx_len):
    bsz, lt, _ = qk.shape
    tb = SCAN_BLOCK
    nb, cb = lt // tb, ctx_len // tb
    tmap = lambda z, n: _time_block(z, n, cb, nb)
    return pl.pallas_call(
        functools.partial(_mlstm_kernel, tb=tb, c=ML_CHUNK),
        out_shape=jax.ShapeDtypeStruct((2, bsz, lt, ML_VW), F32),
        grid=(2, bsz, nb),
        in_specs=[pl.BlockSpec((1, tb, ML_QK), lambda z, b, n: (b, tmap(z, n), 0)),
                  pl.BlockSpec((1, tb, ML_QK), lambda z, b, n: (b, tmap(z, n), 1)),
                  pl.BlockSpec((1, tb, ML_VW), lambda z, b, n: (b, tmap(z, n), 1)),
                  pl.BlockSpec((1, tb, SMALL_W), lambda z, b, n: (b, tmap(z, n), z)),
                  pl.BlockSpec((1, 1, SMALL_W), lambda z, b, n: (z, 0, 0))],
        out_specs=pl.BlockSpec((1, 1, tb, ML_VW), lambda z, b, n: (z, b, tmap(z, n), 0)),
        scratch_shapes=[pltpu.VMEM((ML_H, ML_DK, ML_DV), F32), pltpu.VMEM((ML_H, 1, ML_DK), F32),
                        pltpu.VMEM((ML_H, 1, 1), F32)],
        compiler_params=_cparams(("parallel", "parallel", "arbitrary")),
        name="mlstm_scan",
    )(qk, qk, pm, small, gate_bias)


def _odd_out_kernel(x_ref, mod_ref, ys_ref, xc_ref, zg_ref, ym_ref, og_ref, dsk_ref, nws_ref, nwm_ref, w_ref, o_ref):
    y = (ys_ref[0, 0] + ys_ref[1, 0] + dsk_ref[...] * xc_ref[0]) * _silu(zg_ref[0])
    ys = _group_rms(y, SSD_G, nws_ref[...])
    hm = ym_ref[0, 0] + ym_ref[1, 0]
    ym = _group_rms(hm, ML_H, nwm_ref[...]) * _sigmoid(og_ref[0])
    yy = jnp.concatenate([ys, ym], axis=1).astype(BF16)
    o_ref[0] = x_ref[0] + mod_ref[0, 0][2:3] * _dot(yy, w_ref[...])


def odd_out(x, modtab, y_s, xbc, ps, y_m, pm, d_skip, nw_s, nw_m, w_out, ctx_len):
    bsz, lt, d = x.shape
    tm = TOK_TILE
    ct = ctx_len // tm
    seq = lt - ctx_len
    tok = lambda width, col=0: pl.BlockSpec((1, tm, width), lambda b, i: (b, i + ct, col))
    two = lambda width: pl.BlockSpec((2, 1, tm, width), lambda b, i: (0, b, i + ct, 0))
    row = lambda width: pl.BlockSpec((1, width), lambda b, i: (0, 0))
    return pl.pallas_call(
        _odd_out_kernel,
        out_shape=jax.ShapeDtypeStruct((bsz, seq, d), F32),
        grid=(bsz, seq // tm),
        in_specs=[tok(d), pl.BlockSpec((1, 1, N_MOD, d), lambda b, i: (b, 1, 0, 0)),
                  two(SSD_INNER), tok(SSD_INNER), tok(SSD_INNER), two(ML_VW), tok(ML_VW, 2),
                  row(SSD_INNER), row(SSD_INNER), row(ML_VW),
                  pl.BlockSpec(w_out.shape, lambda b, i: (0, 0))],
        out_specs=pl.BlockSpec((1, tm, d), lambda b, i: (b, i, 0)),
        compiler_params=_cparams(("parallel", "parallel")),
        name="odd_out",
    )(x, modtab, y_s, xbc, ps, y_m, pm, d_skip.reshape(1, SSD_INNER), nw_s.reshape(1, SSD_INNER),
      nw_m.reshape(1, ML_VW), w_out)


def _router_kernel(x_ref, mod_ref, nw_ref, rw_ref, rb_ref, h_ref, idx_ref, prob_ref, rank_ref, cnt_ref, carry_ref,
                   *, tm):
    first = jnp.logical_and(pl.program_id(0) == 0, pl.program_id(1) == 0)

    @pl.when(first)
    def _():
        carry_ref[...] = jnp.zeros_like(carry_ref)

    h = _norm_mod(x_ref[0], nw_ref[...], mod_ref[0, 0], 3)
    h_ref[0] = h
    logits = _dot(h, rw_ref[...], HI) + rb_ref[...]
    lane = lax.broadcasted_iota(I32, (tm, N_EXP), 1).astype(F32)
    work = logits
    sel = jnp.zeros((tm, N_EXP), F32)
    vals, idxs, hots = [], [], []
    for _ in range(TOP_K):
        m = jnp.max(work, axis=1, keepdims=True)
        idx = jnp.min(jnp.where(work == m, lane, float(N_EXP)), axis=1, keepdims=True)
        hot = lane == idx
        vals.append(m)
        idxs.append(idx)
        hots.append(hot.astype(F32))
        sel = sel + hots[-1]
        work = jnp.where(hot, -jnp.inf, work)
    ex = [jnp.exp(v - vals[0]) for v in vals]
    tot = ex[0] + ex[1] + ex[2] + ex[3]
    ti = lax.broadcasted_iota(I32, (tm, tm), 0)
    si = lax.broadcasted_iota(I32, (tm, tm), 1)
    before = _dot((si < ti).astype(BF16), sel.astype(BF16)) + carry_ref[...]
    ranks = [jnp.sum(before * hot, axis=1, keepdims=True) for hot in hots]
    idx_ref[0] = jnp.concatenate(idxs, axis=1).astype(I32)
    prob_ref[0] = jnp.concatenate([e / tot for e in ex], axis=1)
    rank_ref[0] = jnp.concatenate(ranks, axis=1).astype(I32)
    carry_ref[...] = carry_ref[...] + jnp.sum(sel, axis=0, keepdims=True)
    cnt_ref[...] = carry_ref[...]


def router(x, modtab, nw, rw, rb, ctx_len):
    bsz, lt, d = x.shape
    tm = TOK_TILE
    ctx_tiles = ctx_len // tm
    tok = lambda width, dt: (jax.ShapeDtypeStruct((bsz, lt, width), dt),
                             pl.BlockSpec((1, tm, width), lambda b, i: (b, i, 0)))
    outs = [tok(d, F32), tok(TOP_K, I32), tok(TOP_K, F32), tok(TOP_K, I32),
            (jax.ShapeDtypeStruct((1, N_EXP), F32), pl.BlockSpec((1, N_EXP), lambda b, i: (0, 0)))]
    return pl.pallas_call(
        functools.partial(_router_kernel, tm=tm),
        out_shape=[o[0] for o in outs],
        grid=(bsz, lt // tm),
        in_specs=[pl.BlockSpec((1, tm, d), lambda b, i: (b, i, 0)),
                  pl.BlockSpec((1, 1, N_MOD, d), lambda b, i: (b, (i >= ctx_tiles).astype(I32), 0, 0)),
                  pl.BlockSpec((1, d), lambda b, i: (0, 0)),
                  pl.BlockSpec((d, N_EXP), lambda b, i: (0, 0)),
                  pl.BlockSpec((1, N_EXP), lambda b, i: (0, 0))],
        out_specs=[o[1] for o in outs],
        scratch_shapes=[pltpu.VMEM((1, N_EXP), F32)],
        compiler_params=_cparams(("arbitrary", "arbitrary")),
        name="router",
    )(x, modtab, nw.reshape(1, d), rw, rb.reshape(1, N_EXP))


def _gather_rows(tok_ref, blk, src_ref, dst_ref, sem, n_rows):
    def issue(r, carry):
        pltpu.make_async_copy(src_ref.at[pl.ds(tok_ref[blk, r], 1)], dst_ref.at[pl.ds(r, 1)], sem).start()
        return carry

    lax.fori_loop(0, n_rows, issue, 0, unroll=8)


def _expert_kernel(be_ref, tok_ref, h_ref, wg_ref, bg_ref, wu_ref, bu_ref, wd_ref, bd_ref, o_ref, xbuf, wg_bf, wu_bf,
                   wd_bf, sem, *, nblk):
    i = pl.program_id(0)
    slot = i % 2

    @pl.when(i == 0)
    def _():
        _gather_rows(tok_ref, 0, h_ref, xbuf.at[0], sem.at[0], MOE_BLOCK)

    @pl.when(i + 1 < nblk)
    def _():
        _gather_rows(tok_ref, i + 1, h_ref, xbuf.at[1 - slot], sem.at[1 - slot], MOE_BLOCK)

    @pl.when(jnp.logical_or(i == 0, be_ref[i] != be_ref[jnp.maximum(i - 1, 0)]))
    def _():
        wg_bf[...] = wg_ref[0].astype(BF16)
        wu_bf[...] = wu_ref[0].astype(BF16)
        wd_bf[...] = wd_ref[0].astype(BF16)

    pltpu.make_async_copy(xbuf.at[slot], xbuf.at[slot], sem.at[slot]).wait()
    xb = xbuf[slot].astype(BF16)
    g = _dot(xb, wg_bf[...]) + bg_ref[0]
    u = _dot(xb, wu_bf[...]) + bu_ref[0]
    g = jnp.minimum(g, SWIGLU_LIMIT)
    u = jnp.clip(u, -SWIGLU_LIMIT, SWIGLU_LIMIT)
    act = (g * _sigmoid(SWIGLU_ALPHA * g) * (u + 1.0)).astype(BF16)
    o_ref[...] = _dot(act, wd_bf[...]) + bd_ref[0]


def expert_ffn(block_expert, sorted_tok, h, wg, bg, wu, bu, wd, bd):
    nblk = block_expert.shape[0]
    d, f = wg.shape[1], wg.shape[2]
    wspec = lambda r, c: pl.BlockSpec((1, r, c), lambda i, be, tok: (be[i], 0, 0))
    grid_spec = pltpu.PrefetchScalarGridSpec(
        num_scalar_prefetch=2,
        grid=(nblk,),
        in_specs=[pl.BlockSpec(memory_space=pl.ANY),
                  wspec(d, f), wspec(1, f), wspec(d, f), wspec(1, f), wspec(f, d), wspec(1, d)],
        out_specs=pl.BlockSpec((MOE_BLOCK, d), lambda i, be, tok: (i, 0)),
        scratch_shapes=[pltpu.VMEM((2, MOE_BLOCK, d), F32), pltpu.VMEM((d, f), BF16), pltpu.VMEM((d, f), BF16),
                        pltpu.VMEM((f, d), BF16), pltpu.SemaphoreType.DMA((2,))],
    )
    return pl.pallas_call(
        functools.partial(_expert_kernel, nblk=nblk),
        out_shape=jax.ShapeDtypeStruct((nblk * MOE_BLOCK, d), F32),
        grid_spec=grid_spec,
        compiler_params=_cparams(("arbitrary",)),
        name="expert_ffn",
    )(block_expert, sorted_tok, h, wg, bg.reshape(N_EXP, 1, f), wu, bu.reshape(N_EXP, 1, f), wd,
      bd.reshape(N_EXP, 1, d))


def _combine_kernel(dest_ref, x_ref, mod_ref, prob_ref, y_ref, fw_ref, o_ref, ybuf, sem, *, tm, ntile, final_norm):
    i = pl.program_id(0) * pl.num_programs(1) + pl.program_id(1)
    slot = i % 2

    def gather(tile, s):
        def issue(r, carry):
            for k in range(TOP_K):
                pltpu.make_async_copy(y_ref.at[pl.ds(dest_ref[tile, r * TOP_K + k], 1)],
                                      ybuf.at[s, k, pl.ds(r, 1)], sem.at[s]).start(priority=k % 2)
            return carry

        lax.fori_loop(0, tm, issue, 0, unroll=4)

    @pl.when(i == 0)
    def _():
        gather(0, 0)

    @pl.when(i + 1 < ntile)
    def _():
        gather(i + 1, 1 - slot)

    pltpu.make_async_copy(ybuf.at[slot], ybuf.at[slot], sem.at[slot]).wait()
    p = prob_ref[0]
    acc = ybuf[slot, 0] * p[:, 0:1]
    for k in range(1, TOP_K):
        acc = acc + ybuf[slot, k] * p[:, k:k + 1]
    out = x_ref[0] + mod_ref[0, 0][5:6] * acc
    if final_norm:
        out = out * lax.rsqrt(jnp.mean(out * out, axis=-1, keepdims=True) + EPS) * fw_ref[...]
    o_ref[0] = out


def combine(dest, x, modtab, probs, y_sorted, final_w, ctx_len, final_norm):
    bsz, lt, d = x.shape
    tm = TOK_TILE
    ctx_tiles = ctx_len // tm
    ntile = bsz * (lt // tm)
    grid_spec = pltpu.PrefetchScalarGridSpec(
        num_scalar_prefetch=1,
        grid=(bsz, lt // tm),
        in_specs=[pl.BlockSpec((1, tm, d), lambda b, i, dst: (b, i, 0)),
                  pl.BlockSpec((1, 1, N_MOD, d), lambda b, i, dst: (b, (i >= ctx_tiles).astype(I32), 0, 0)),
                  pl.BlockSpec((1, tm, TOP_K), lambda b, i, dst: (b, i, 0)),
                  pl.BlockSpec(memory_space=pl.ANY),
                  pl.BlockSpec((1, d), lambda b, i, dst: (0, 0))],
        out_specs=pl.BlockSpec((1, tm, d), lambda b, i, dst: (b, i, 0)),
        scratch_shapes=[pltpu.VMEM((2, TOP_K, tm, d), F32), pltpu.SemaphoreType.DMA((2,))],
    )
    return pl.pallas_call(
        functools.partial(_combine_kernel, tm=tm, ntile=ntile, final_norm=final_norm),
        out_shape=jax.ShapeDtypeStruct((bsz, lt, d), F32),
        grid_spec=grid_spec,
        compiler_params=_cparams(("arbitrary", "arbitrary")),
        name="combine",
    )(dest, x, modtab, probs, y_sorted, final_w.reshape(1, d))


def moe_layer(x, modtab, nw, rw, rb, wg, bg, wu, bu, wd, bd, final_w, ctx_len, final_norm):
    bsz, lt, d = x.shape
    t = bsz * lt
    h, idx, probs, rank, counts = router(x, modtab, nw, rw, rb, ctx_len)
    counts = counts.reshape(N_EXP).astype(I32)
    padded = (counts + MOE_BLOCK - 1) // MOE_BLOCK * MOE_BLOCK
    ends = jnp.cumsum(padded)
    pstart = ends - padded
    idx = idx.reshape(t, TOP_K)
    dest = pstart[idx] + rank.reshape(t, TOP_K)
    nblk = -(-t * TOP_K // MOE_BLOCK) + N_EXP
    block_row0 = jnp.arange(nblk, dtype=I32) * MOE_BLOCK
    block_expert = jnp.minimum(jnp.sum((ends[None, :] <= block_row0[:, None]).astype(I32), axis=1), N_EXP - 1)
    tok_ids = jnp.broadcast_to(jnp.arange(t, dtype=I32)[:, None], (t, TOP_K))
    sorted_tok = jnp.zeros((nblk * MOE_BLOCK,), I32).at[dest.reshape(-1)].set(
        tok_ids.reshape(-1), unique_indices=True, mode="promise_in_bounds")
    y_sorted = expert_ffn(block_expert, sorted_tok.reshape(nblk, MOE_BLOCK), h.reshape(t, d), wg, bg, wu, bu, wd, bd)
    return combine(dest.reshape(t // TOK_TILE, TOK_TILE * TOP_K), x, modtab, probs, y_sorted, final_w, ctx_len,
                   final_norm)


def even_layer(x, modtab, norm1_w, w_in, w_out, mu, w0, w2, a0, a2, g2, k_k, k_a, r_k, ln_w, ln_b, lb, hg_norm_w,
               ctx_len):
    w_in = w_in.astype(BF16)
    pr, ph = in_proj(x, modtab, norm1_w, [w_in[:, :RW_PROJ], w_in[:, RW_PROJ:]], ctx_len)
    r, v, kk, g, bv, lw, km, bb = rwkv_prep(pr, mu, w0, w2, a0, a2, g2, k_k, k_a, r_k, ctx_len)
    o_r = rwkv_scan(r, v, kk, lw, km, bb, ctx_len)
    o_g = gla_scan(ph, lb, ctx_len)
    return even_out(x, modtab, o_r, bv, g, o_g, ph, ln_w, ln_b, hg_norm_w, w_out.astype(BF16), ctx_len)


def _dir_slab(cols, width):
    halves = []
    for z in range(2):
        parts = [c[:, z * (c.shape[1] // 2):(z + 1) * (c.shape[1] // 2)] for c in cols]
        used = sum(p.shape[1] for p in parts)
        parts.append(jnp.zeros((cols[0].shape[0], width - used), cols[0].dtype))
        halves.append(jnp.concatenate(parts, axis=1))
    return jnp.concatenate(halves, axis=1)


def _dir_rows(vals, width):
    rows = jnp.concatenate(vals, axis=1)
    return jnp.pad(rows, ((0, 0), (0, width - rows.shape[1]))).reshape(2, 1, width)


def odd_layer(x, modtab, norm1_w, w_in, w_out, s_conv_w, s_conv_b, dt_bias, a_log, d_skip, s_norm_w, m_conv_w,
              m_conv_b, i_bias, f_bias, m_norm_w, ctx_len):
    o = 0
    cols = {}
    for name, width in (("zg", SSD_INNER), ("xbc", SSD_CONV_CH), ("dt", 2 * SSD_H), ("qk", 2 * ML_QK),
                        ("v", ML_VW), ("og", ML_VW), ("ig", 2 * ML_H), ("fg", 2 * ML_H)):
        cols[name] = w_in[:, o:o + width]
        o += width
    slab_m = jnp.concatenate([cols["qk"], cols["v"], cols["og"]], axis=1).astype(BF16)
    slab_small = _dir_slab([cols["dt"], cols["ig"], cols["fg"]], SMALL_W).astype(BF16)
    pxbc, zg, pm, small = in_proj(x, modtab, norm1_w, [cols["xbc"].astype(BF16), cols["zg"].astype(BF16), slab_m,
                                                       slab_small], ctx_len)
    xbc, qk = odd_prep(pxbc, pm, s_conv_w, s_conv_b, m_conv_w, m_conv_b, ctx_len)
    zeros_g = jnp.zeros((2, 2 * ML_H), F32)
    y_s = ssd_scan(xbc, small, _dir_rows([dt_bias, zeros_g], SMALL_W), _dir_rows([a_log, zeros_g], SMALL_W), ctx_len)
    y_m = mlstm_scan(qk, pm, small, _dir_rows([jnp.zeros((2, SSD_H), F32), i_bias, f_bias], SMALL_W), ctx_len)
    return odd_out(x, modtab, y_s, xbc, zg, y_m, pm, jnp.repeat(d_skip, SSD_P), s_norm_w, m_norm_w,
                   w_out.astype(BF16), ctx_len)


def _to_col_major(t):
    b, s, ch = t.shape
    return t.reshape(b, s // GRID_W, GRID_W, ch).transpose(0, 2, 1, 3).reshape(b, s, ch)


def _to_row_major(t):
    b, s, ch = t.shape
    return t.reshape(b, GRID_W, s // GRID_W, ch).transpose(0, 2, 1, 3).reshape(b, s, ch)


def kernel(x, c, ctx, c_ctx, ada_w, ada_b, norm1_w, norm2_w, even_w_in, even_w_out, rwkv_mu, rwkv_w0, rwkv_w2, rwkv_a0, rwkv_a2, rwkv_g2, rwkv_k_k, rwkv_k_a, rwkv_r_k, rwkv_ln_w, rwkv_ln_b, hgrn_lower_bounds, hgrn_norm_w, odd_w_in, odd_w_out, ssd_conv_w, ssd_conv_b, ssd_dt_bias, ssd_a_log, ssd_d, ssd_norm_w, mlstm_conv_w, mlstm_conv_b, mlstm_i_bias, mlstm_f_bias, mlstm_norm_w, router_w, router_b, exp_w_gate, exp_b_gate, exp_w_up, exp_b_up, exp_w_down, exp_b_down, final_norm_w):
    lc = ctx.shape[1]
    lower_bounds = jnp.cumsum(jax.nn.softmax(hgrn_lower_bounds.astype(F32), axis=0), axis=0)
    moe = lambda l: (norm2_w[l], router_w[l], router_b[l], exp_w_gate[l], exp_b_gate[l], exp_w_up[l], exp_b_up[l],
                     exp_w_down[l], exp_b_down[l], final_norm_w)
    xa = jnp.concatenate([ctx, x], axis=1)
    modtab = _mod_table(c, c_ctx, ada_w[0], ada_b[0])
    xa = even_layer(xa, modtab, norm1_w[0], even_w_in[0], even_w_out[0], rwkv_mu[0], rwkv_w0[0], rwkv_w2[0],
                    rwkv_a0[0], rwkv_a2[0], rwkv_g2[0], rwkv_k_k[0], rwkv_k_a[0], rwkv_r_k[0], rwkv_ln_w[0],
                    rwkv_ln_b[0], lower_bounds[0], hgrn_norm_w[0], lc)
    xa = moe_layer(xa, modtab, *moe(0), lc, False)
    xa = jnp.concatenate([xa[:, :lc], _to_col_major(xa[:, lc:])], axis=1)
    modtab = _mod_table(c, c_ctx, ada_w[1], ada_b[1])
    xl = odd_layer(xa, modtab, norm1_w[1], odd_w_in[0], odd_w_out[0], ssd_conv_w[0], ssd_conv_b[0], ssd_dt_bias[0],
                   ssd_a_log[0], ssd_d[0], ssd_norm_w[0], mlstm_conv_w[0], mlstm_conv_b[0], mlstm_i_bias[0],
                   mlstm_f_bias[0], mlstm_norm_w[0], lc)
    xl = moe_layer(xl, modtab, *moe(1), 0, True)
    return _to_row_major(xl)
```

```python
import functools
import math

import jax
import jax.numpy as jnp
from jax import lax
from jax.experimental import pallas as pl
from jax.experimental.pallas import tpu as pltpu
from jax.experimental.pallas import tpu_sc as plsc

F32 = jnp.float32
BF16 = jnp.bfloat16
I32 = jnp.int32
HI = lax.Precision.HIGHEST

D_MODEL = 1024
GRID_W = 64
N_MOD = 6
EPS = 1e-6
RW_H, RW_N = 8, 64
RW_W = RW_H * RW_N
RW_RANK = 64
RW_GATE_RANK = 128
RW_GN_EPS = 64e-5
RW_PROJ = 3 * RW_W + 4 * RW_RANK + RW_GATE_RANK
RW_CHUNK = 64
HG_H, HG_DK, HG_DV = 4, 128, 128
HG_W = HG_H * HG_DK
HG_CHUNK = 32
HG_PROJ = 5 * HG_W
SSD_H, SSD_P, SSD_G, SSD_N = 16, 64, 2, 128
SSD_INNER = SSD_H * SSD_P
SSD_CONV_CH = SSD_INNER + 2 * SSD_G * SSD_N
SSD_CHUNK = 64
ML_H, ML_DK, ML_DV = 4, 128, 256
ML_QK = ML_H * ML_DK
ML_VW = ML_H * ML_DV
ML_CHUNK = 64
ML_M_INIT = -1e30
CONV_K = 5
N_EXP = 32
TOP_K = 4
D_FF = 1024
SWIGLU_LIMIT = 7.0
SWIGLU_ALPHA = 1.702
MOE_BLOCK = 512

TOK_TILE = 256
SCAN_BLOCK = 256
HALO = 8
SMALL_W = 128
VMEM_LIMIT = 56 * 1024 * 1024
SC_CORES, SC_SUBCORES = 2, 16
SC_GATHER_ROWS = 32


def _cparams(sem):
    return pltpu.CompilerParams(dimension_semantics=sem, vmem_limit_bytes=VMEM_LIMIT)


def _dot(a, b, precision=None):
    return jnp.dot(a, b, preferred_element_type=F32, precision=precision)


def _dot_nt(a, b, precision=None):
    return lax.dot_general(a, b, (((1,), (1,)), ((), ())), preferred_element_type=F32, precision=precision)


def _dot_tn(a, b, precision=None):
    return lax.dot_general(a, b, (((0,), (0,)), ((), ())), preferred_element_type=F32, precision=precision)


def _bf16_parts(x):
    hi = x.astype(BF16)
    rest = x - hi.astype(F32)
    mid = rest.astype(BF16)
    return hi, mid, (rest - mid.astype(F32)).astype(BF16)


def _dot_into_01(x, sel):
    sel = sel.astype(BF16)
    p0, p1, p2 = _bf16_parts(x)
    return _dot(p0, sel) + _dot(p1, sel) + _dot(p2, sel)


def _dot_01_into(sel, x):
    sel = sel.astype(BF16)
    p0, p1, p2 = _bf16_parts(x)
    return _dot(sel, p0) + _dot(sel, p1) + _dot(sel, p2)


def _sigmoid(x):
    return 1.0 / (1.0 + jnp.exp(-x))


def _silu(x):
    return x * _sigmoid(x)


def _softplus(x):
    return jnp.maximum(x, 0.0) + jnp.log(1.0 + jnp.exp(-jnp.abs(x)))


def _dir_masks(z, c):
    ti = lax.broadcasted_iota(I32, (c, c), 0)
    si = lax.broadcasted_iota(I32, (c, c), 1)
    d = (si - ti) * jnp.where(z == 0, 1, -1)
    return d <= 0, d < 0


def _time_block(z, n, ctx_blocks, n_blocks):
    rev = jnp.where(n < ctx_blocks, ctx_blocks - 1 - n, n_blocks - 1 - (n - ctx_blocks))
    return jnp.where(z == 0, n, rev)


def _mod_kernel(c_ref, w_ref, b_ref, o_ref):
    o_ref[...] = _dot(_silu(c_ref[...]), w_ref[...]) + b_ref[...]


def ada_mod(cc, w, b):
    rows, d = cc.shape
    n = w.shape[1]
    tn = 1536
    return pl.pallas_call(
        _mod_kernel,
        out_shape=jax.ShapeDtypeStruct((rows, n), F32),
        grid=(n // tn,),
        in_specs=[pl.BlockSpec((rows, d), lambda j: (0, 0)),
                  pl.BlockSpec((d, tn), lambda j: (0, j)),
                  pl.BlockSpec((1, tn), lambda j: (0, j))],
        out_specs=pl.BlockSpec((rows, tn), lambda j: (0, j)),
        compiler_params=_cparams(("arbitrary",)),
        name="ada_mod",
    )(cc, w, b.reshape(1, n))


def _mod_table(c, c_ctx, w, b):
    bsz = c.shape[0]
    rows = -(-(bsz + 1) // 8) * 8
    cc = jnp.zeros((rows, D_MODEL), F32).at[:bsz].set(c).at[bsz].set(c_ctx)
    m = ada_mod(cc, w, b)
    mod = m[:bsz].reshape(bsz, 1, N_MOD, D_MODEL)
    mod_c = jnp.broadcast_to(m[bsz].reshape(1, 1, N_MOD, D_MODEL), (bsz, 1, N_MOD, D_MODEL))
    return jnp.concatenate([mod_c, mod], axis=1)


def _norm_mod(x, nw, m, shift_idx):
    y = x * lax.rsqrt(jnp.mean(x * x, axis=-1, keepdims=True) + EPS) * nw
    return y * (1.0 + m[shift_idx + 1:shift_idx + 2]) + m[shift_idx:shift_idx + 1]


def _in_proj_kernel(x_ref, mod_ref, nw_ref, *rest, n_out):
    w_refs, o_refs = rest[:n_out], rest[n_out:]
    h = _norm_mod(x_ref[0], nw_ref[...], mod_ref[0, 0], 0).astype(BF16)
    for w_ref, o_ref in zip(w_refs, o_refs):
        o_ref[0] = _dot(h, w_ref[...])


def in_proj(x, modtab, nw, slabs, ctx_len):
    bsz, lt, d = x.shape
    tm = TOK_TILE
    ctx_tiles = ctx_len // tm
    in_specs = [pl.BlockSpec((1, tm, d), lambda b, i: (b, i, 0)),
                pl.BlockSpec((1, 1, N_MOD, d), lambda b, i: (b, (i >= ctx_tiles).astype(I32), 0, 0)),
                pl.BlockSpec((1, d), lambda b, i: (0, 0))]
    in_specs += [pl.BlockSpec(w.shape, lambda b, i: (0, 0)) for w in slabs]
    return pl.pallas_call(
        functools.partial(_in_proj_kernel, n_out=len(slabs)),
        out_shape=[jax.ShapeDtypeStruct((bsz, lt, w.shape[1]), F32) for w in slabs],
        grid=(bsz, lt // tm),
        in_specs=in_specs,
        out_specs=[pl.BlockSpec((1, tm, w.shape[1]), lambda b, i: (b, i, 0)) for w in slabs],
        compiler_params=_cparams(("parallel", "parallel")),
        name="in_proj",
    )(x, modtab, nw.reshape(1, d), *slabs)


def _halo_specs(tm, width, lt):
    per = tm // HALO
    last = lt // HALO - 1
    return [pl.BlockSpec((1, tm, width), lambda b, i: (b, i, 0)),
            pl.BlockSpec((1, HALO, width), lambda b, i: (b, jnp.maximum(i * per - 1, 0), 0)),
            pl.BlockSpec((1, HALO, width), lambda b, i: (b, jnp.minimum((i + 1) * per, last), 0))]


def _segment_edges(i, tm, ctx_len, lt):
    row0 = i * tm
    has_prev = jnp.logical_and(row0 != 0, row0 != ctx_len)
    has_next = jnp.logical_and(row0 + tm != ctx_len, row0 + tm != lt)
    return has_prev, has_next


def _rwkv_prep_kernel(p_ref, pp_ref, pn_ref, mu_ref, w0_ref, w2_ref, a0_ref, a2_ref, g2_ref, kk_ref, ka_ref,
                      rk_ref, bd_ref, r_o, v_o, kk_o, g_o, bv_o, lw_o, km_o, bb_o, *, tm, ctx_len, lt):
    has_prev, has_next = _segment_edges(pl.program_id(1), tm, ctx_len, lt)
    p = p_ref[0]
    prev_row = jnp.where(has_prev, pp_ref[0, HALO - 1:HALO, :], 0.0)
    next_row = jnp.where(has_next, pn_ref[0, 0:1, :], 0.0)
    rid = lax.broadcasted_iota(I32, (tm, 1), 0)
    up = jnp.where(rid == 0, prev_row, pltpu.roll(p, 1, 0))
    dn = jnp.where(rid == tm - 1, next_row, pltpu.roll(p, tm - 1, 0))
    p = p + (0.5 * (up + dn) - p) * mu_ref[...]
    w = RW_W
    r, k, v = p[:, 0:w], p[:, w:2 * w], p[:, 2 * w:3 * w]
    o = 3 * w
    wd = jnp.tanh(p[:, o:o + 2 * RW_RANK])
    ad = p[:, o + 2 * RW_RANK:o + 4 * RW_RANK]
    gd = p[:, o + 4 * RW_RANK:o + 4 * RW_RANK + RW_GATE_RANK]
    bd = bd_ref[...]
    kk = k * kk_ref[...]
    kk = kk * lax.rsqrt(_dot_into_01(kk * kk, bd) + 1e-12)
    r_o[0] = r
    v_o[0] = v
    kk_o[0] = kk
    g_o[0] = _dot(_sigmoid(gd), g2_ref[...])
    km_sum = jnp.zeros_like(k)
    for z in range(2):
        sl = slice(z * RW_RANK, (z + 1) * RW_RANK)
        w_pre = w0_ref[z:z + 1, :] + _dot(wd[:, sl], w2_ref[z])
        lw_o[z, 0] = -_sigmoid(w_pre) * math.exp(-0.5)
        a = _sigmoid(a0_ref[z:z + 1, :] + _dot(ad[:, sl], a2_ref[z]))
        km = k * (1.0 + (a - 1.0) * ka_ref[...])
        km_o[z, 0] = km
        bb_o[z, 0] = kk * a
        km_sum = km_sum + km
    bv_o[0] = _dot_into_01(r * km_sum * rk_ref[...], bd) * v


def _head_block_ones(width, head):
    i = jnp.arange(width) // head
    return (i[:, None] == i[None, :]).astype(F32)


def rwkv_prep(p, mu, w0, w2, a0, a2, g2, k_k, k_a, r_k, ctx_len):
    bsz, lt, width = p.shape
    tm = TOK_TILE
    w = RW_W
    full = lambda a: pl.BlockSpec(a.shape, lambda b, i: (0,) * a.ndim)
    params = [mu.reshape(1, width), w0, w2, a0, a2, g2, k_k.reshape(1, w), k_a.reshape(1, w), r_k.reshape(1, w),
              _head_block_ones(w, RW_N)]
    one = jax.ShapeDtypeStruct((bsz, lt, w), F32)
    two = jax.ShapeDtypeStruct((2, bsz, lt, w), F32)
    one_spec = pl.BlockSpec((1, tm, w), lambda b, i: (b, i, 0))
    two_spec = pl.BlockSpec((2, 1, tm, w), lambda b, i: (0, b, i, 0))
    return pl.pallas_call(
        functools.partial(_rwkv_prep_kernel, tm=tm, ctx_len=ctx_len, lt=lt),
        out_shape=[one] * 5 + [two] * 3,
        grid=(bsz, lt // tm),
        in_specs=_halo_specs(tm, width, lt) + [full(a) for a in params],
        out_specs=[one_spec] * 5 + [two_spec] * 3,
        compiler_params=_cparams(("parallel", "parallel")),
        name="rwkv_prep",
    )(p, p, p, *params)


def _unit_lower_inverses(n_mats, eye, c):
    xs = [eye - n for n in n_mats]
    ps = [_dot(n, n) for n in n_mats]
    span = 4
    while True:
        xs = [x + _dot(x, p) for x, p in zip(xs, ps)]
        if span >= c:
            return xs
        ps = [_dot(p, p) for p in ps]
        span *= 2


def _rwkv_scan_kernel(r_ref, v_ref, kk_ref, lw_ref, km_ref, bb_ref, o_ref, s_ref, *, tb, c):
    z = pl.program_id(0)
    nsub = tb // c

    @pl.when(pl.program_id(2) == 0)
    def _():
        s_ref[...] = jnp.zeros_like(s_ref)

    incl, strict = _dir_masks(z, c)
    incl_f, strict_f = incl.astype(F32), strict.astype(F32)
    eye = incl_f - strict_f

    def body(j, s_old):
        jj = jnp.where(z == 0, j, nsub - 1 - j)
        rows = pl.ds(pl.multiple_of(jj * c, c), c)
        r, v, kk = r_ref[0, rows, :], v_ref[0, rows, :], kk_ref[0, rows, :]
        lw, km, bb = lw_ref[0, 0, rows, :], km_ref[0, 0, rows, :], bb_ref[0, 0, rows, :]
        cum = _dot_01_into(incl_f, lw)
        tot = jnp.sum(lw, axis=0, keepdims=True)
        e_neg = jnp.exp(-cum)
        e_end = jnp.exp(tot - cum)
        a_t = kk * jnp.exp(cum - lw)
        r_t = r * jnp.exp(cum)
        b_t, k_t = bb * e_neg, km * e_neg
        b_p, k_p = bb * e_end, km * e_end
        e_tot = jnp.exp(tot)
        heads = [slice(h * RW_N, (h + 1) * RW_N) for h in range(RW_H)]
        ar =[jnp.concatenate([a_t[:, hs], r_t[:, hs]], axis=0) for hs in heads]
        bk = [jnp.concatenate([b_t[:, hs], k_t[:, hs]], axis=0) for hs in heads]
        bkp = [jnp.concatenate([b_p[:, hs], k_p[:, hs]], axis=0) for hs in heads]
        vh = [v[:, hs] for hs in heads]
        sc = [_dot_nt(a, b) for a, b in zip(ar, bk)]
        ars = [_dot_nt(a, s) for a, s in zip(ar, s_old)]
        m_ak_v = [_dot(x[:c, c:] * strict_f, vv) for x, vv in zip(sc, vh)]
        inv = _unit_lower_inverses([x[:c, :c] * strict_f for x in sc], eye, c)
        u = [-_dot(i, a[:c] + m) for i, a, m in zip(inv, ars, m_ak_v)]
        uv = [jnp.concatenate([uu, vv], axis=0) for uu, vv in zip(u, vh)]
        n_rbk = [jnp.concatenate([x[c:, :c] * incl_f, x[c:, c:] * incl_f], axis=1) for x in sc]
        outs = [a[c:] + _dot(n, w) for a, n, w in zip(ars, n_rbk, uv)]
        o_ref[0, 0, rows, :] = jnp.concatenate(outs, axis=1)
        return tuple(s * e_tot[:, hs] + _dot_tn(w, b) for s, hs, w, b in zip(s_old, heads, uv, bkp))

    s_fin = lax.fori_loop(0, nsub, body, tuple(s_ref[h] for h in range(RW_H)), unroll=2)
    for h in range(RW_H):
        s_ref[h] = s_fin[h]


def rwkv_scan(r, v, kk, lw, km, bb, ctx_len):
    bsz, lt, w = r.shape
    tb = SCAN_BLOCK
    nb, cb = lt // tb, ctx_len // tb
    tmap = lambda z, b, n: _time_block(z, n, cb, nb)
    one = pl.BlockSpec((1, tb, w), lambda z, b, n: (b, tmap(z, b, n), 0))
    two = pl.BlockSpec((1, 1, tb, w), lambda z, b, n: (z, b, tmap(z, b, n), 0))
    return pl.pallas_call(
        functools.partial(_rwkv_scan_kernel, tb=tb, c=RW_CHUNK),
        out_shape=jax.ShapeDtypeStruct((2, bsz, lt, w), F32),
        grid=(2, bsz, nb),
        in_specs=[one, one, one, two, two, two],
        out_specs=two,
        scratch_shapes=[pltpu.VMEM((RW_H, RW_N, RW_N), F32)],
        compiler_params=_cparams(("parallel", "parallel", "arbitrary")),
        name="rwkv_scan",
    )(r, v, kk, lw, km, bb)


def _gla_kernel(q_ref, f_ref, i_ref, lb_ref, o_ref, s_ref, *, tb, c):
    z = pl.program_id(0)
    nsub = tb // c

    @pl.when(pl.program_id(2) == 0)
    def _():
        s_ref[...] = jnp.zeros_like(s_ref)

    incl, _ = _dir_masks(z, c)
    incl_f = incl.astype(F32)
    lb = lb_ref[...]

    def body(j, s_old):
        jj = jnp.where(z == 0, j, nsub - 1 - j)
        rows = pl.ds(pl.multiple_of(jj * c, c), c)
        q = _silu(q_ref[0, rows, :]) * HG_DK ** -0.5
        f = f_ref[0, rows, :]
        v = i_ref[0, rows, :]
        log_f = jnp.log(lb + (1.0 - lb) * _sigmoid(f))
        k = (1.0 - lb) * _sigmoid(-f)
        cum = _dot_01_into(incl_f, log_f)
        tot = jnp.sum(log_f, axis=0, keepdims=True)
        q_in = q * jnp.exp(cum)
        k_in = k * jnp.exp(-cum)
        k_end = k * jnp.exp(tot - cum)
        e_tot = jnp.exp(tot)
        heads = [slice(h * HG_DK, (h + 1) * HG_DK) for h in range(HG_H)]
        a = [_dot_nt(q_in[:, hs], k_in[:, hs]) * incl_f for hs in heads]
        inter = [_dot_nt(q_in[:, hs], s) for hs, s in zip(heads, s_old)]
        grow = [_dot_tn(v[:, hs], k_end[:, hs]) for hs in heads]
        outs = [_dot(aa, v[:, hs]) + ii for aa, hs, ii in zip(a, heads, inter)]
        o_ref[0, 0, rows, :] = jnp.concatenate(outs, axis=1)
        return tuple(s * e_tot[:, hs] + g for s, hs, g in zip(s_old, heads, grow))

    s_fin = lax.fori_loop(0, nsub, body, tuple(s_ref[h] for h in range(HG_H)), unroll=4)
    for h in range(HG_H):
        s_ref[h] = s_fin[h]


def gla_scan(ph, lb, ctx_len):
    bsz, lt, _ = ph.shape
    w = HG_W
    tb = SCAN_BLOCK
    nb, cb = lt // tb, ctx_len // tb
    tmap = lambda z, n: _time_block(z, n, cb, nb)
    return pl.pallas_call(
        functools.partial(_gla_kernel, tb=tb, c=HG_CHUNK),
        out_shape=jax.ShapeDtypeStruct((2, bsz, lt, w), F32),
        grid=(2, bsz, nb),
        in_specs=[pl.BlockSpec((1, tb, w), lambda z, b, n: (b, tmap(z, n), 0)),
                  pl.BlockSpec((1, tb, w), lambda z, b, n: (b, tmap(z, n), 1 + z)),
                  pl.BlockSpec((1, tb, w), lambda z, b, n: (b, tmap(z, n), 3)),
                  pl.BlockSpec((1, w), lambda z, b, n: (0, 0))],
        out_specs=pl.BlockSpec((1, 1, tb, w), lambda z, b, n: (z, b, tmap(z, n), 0)),
        scratch_shapes=[pltpu.VMEM((HG_H, HG_DV, HG_DK), F32)],
        compiler_params=_cparams(("parallel", "parallel", "arbitrary")),
        name="gla_scan",
    )(ph, ph, ph, lb.reshape(1, w))


def _group_rms(y, groups, gain):
    width = y.shape[-1] // groups
    parts = []
    for g in range(groups):
        yg = y[:, g * width:(g + 1) * width]
        parts.append(yg * lax.rsqrt(jnp.mean(yg * yg, axis=-1, keepdims=True) + EPS))
    return jnp.concatenate(parts, axis=1) * gain


def _even_out_kernel(x_ref, mod_ref, or_ref, bv_ref, g_ref, og_ref, gate_ref, lnw_ref, lnb_ref, nw_ref, bd_ref,
                     w_ref, o_ref):
    bd = bd_ref[...]
    o = or_ref[0, 0] + or_ref[1, 0]
    mu = _dot_into_01(o, bd) * (1.0 / RW_N)
    oc = o - mu
    var = _dot_into_01(oc * oc, bd) * (1.0 / RW_N)
    yr = (oc * lax.rsqrt(var + RW_GN_EPS) * lnw_ref[...] + lnb_ref[...] + bv_ref[0]) * g_ref[0]
    og = og_ref[0, 0] + og_ref[1, 0]
    yh = _group_rms(og, HG_H, nw_ref[...]) * _silu(gate_ref[0])
    y = jnp.concatenate([yr, yh], axis=1).astype(BF16)
    o_ref[0] = x_ref[0] + mod_ref[0, 0][2:3] * _dot(y, w_ref[...])


def even_out(x, modtab, o_r, bv, g, o_g, ph, ln_w, ln_b, norm_w, w_out, ctx_len):
    bsz, lt, d = x.shape
    tm = TOK_TILE
    ctx_tiles = ctx_len // tm
    w = RW_W
    tok = lambda width, col=0: pl.BlockSpec((1, tm, width), lambda b, i: (b, i, col))
    two = pl.BlockSpec((2, 1, tm, w), lambda b, i: (0, b, i, 0))
    row = lambda width: pl.BlockSpec((1, width), lambda b, i: (0, 0))
    return pl.pallas_call(
        _even_out_kernel,
        out_shape=jax.ShapeDtypeStruct((bsz, lt, d), F32),
        grid=(bsz, lt // tm),
        in_specs=[tok(d),
                  pl.BlockSpec((1, 1, N_MOD, d), lambda b, i: (b, (i >= ctx_tiles).astype(I32), 0, 0)),
                  two, tok(w), tok(w), two, tok(HG_W, 4), row(w), row(w), row(HG_W),
                  pl.BlockSpec((w, w), lambda b, i: (0, 0)),
                  pl.BlockSpec(w_out.shape, lambda b, i: (0, 0))],
        out_specs=tok(d),
        compiler_params=_cparams(("parallel", "parallel")),
        name="even_out",
    )(x, modtab, o_r, bv, g, o_g, ph, ln_w.reshape(1, w), ln_b.reshape(1, w), norm_w.reshape(1, HG_W),
      _head_block_ones(w, RW_N), w_out)


def _conv_silu(x_ref, xp_ref, xn_ref, w_ref, b_ref, has_prev, has_next, tm):
    prev = jnp.where(has_prev, xp_ref[0], 0.0)
    nxt = jnp.where(has_next, xn_ref[0], 0.0)
    ext = jnp.concatenate([prev, x_ref[0], nxt], axis=0)
    rows = tm + 2 * HALO
    pad = CONV_K // 2
    acc = b_ref[...] + jnp.zeros((tm, ext.shape[1]), F32)
    for j in range(CONV_K):
        d = j - pad
        sh = ext if d == 0 else pltpu.roll(ext, (-d) % rows, 0)
        acc = acc + sh[HALO:HALO + tm] * w_ref[j:j + 1, :]
    return _silu(acc)


def _odd_prep_kernel(xs_ref, xsp_ref, xsn_ref, xm_ref, xmp_ref, xmn_ref, ws_ref, bs_ref, wm_ref, bm_ref,
                     os_ref, om_ref, *, tm, ctx_len, lt):
    has_prev, has_next = _segment_edges(pl.program_id(1), tm, ctx_len, lt)
    os_ref[0] = _conv_silu(xs_ref, xsp_ref, xsn_ref, ws_ref, bs_ref, has_prev, has_next, tm)
    om_ref[0] = _conv_silu(xm_ref, xmp_ref, xmn_ref, wm_ref, bm_ref, has_prev, has_next, tm)


def odd_prep(xbc, qk, ws, bs, wm, bm, ctx_len):
    bsz, lt, cs = xbc.shape
    cm = wm.shape[1]
    tm = TOK_TILE
    full = lambda a: pl.BlockSpec(a.shape, lambda b, i: (0,) * a.ndim)
    params = [ws, bs.reshape(1, cs), wm, bm.reshape(1, cm)]
    return pl.pallas_call(
        functools.partial(_odd_prep_kernel, tm=tm, ctx_len=ctx_len, lt=lt),
        out_shape=[jax.ShapeDtypeStruct((bsz, lt, cs), F32), jax.ShapeDtypeStruct((bsz, lt, cm), F32)],
        grid=(bsz, lt // tm),
        in_specs=_halo_specs(tm, cs, lt) + _halo_specs(tm, cm, lt) + [full(a) for a in params],
        out_specs=[pl.BlockSpec((1, tm, cs), lambda b, i: (b, i, 0)),
                   pl.BlockSpec((1, tm, cm), lambda b, i: (b, i, 0))],
        compiler_params=_cparams(("parallel", "parallel")),
        name="odd_prep",
    )(xbc, xbc, xbc, qk, qk, qk, *params)


def _ssd_kernel(x_ref, b_ref, c_ref, sm_ref, dtb_ref, alog_ref, ex_ref, o_ref, s_ref, *, tb, c):
    z = pl.program_id(0)
    nsub = tb // c
    hg = SSD_H // SSD_G

    @pl.when(pl.program_id(2) == 0)
    def _():
        s_ref[...] = jnp.zeros_like(s_ref)

    incl, _ = _dir_masks(z, c)
    incl_f = incl.astype(F32)
    ex = ex_ref[...]

    gw = hg * SSD_P

    def body(j, s_old):
        jj = jnp.where(z == 0, j, nsub - 1 - j)
        rows = pl.ds(pl.multiple_of(jj * c, c), c)
        x = x_ref[0, rows, :]
        bm = b_ref[0, rows, :]
        cm = c_ref[0, rows, :]
        dt = _softplus(sm_ref[0, rows, :] + dtb_ref[0])
        la = -dt * jnp.exp(alog_ref[0])
        a_cum = _dot_01_into(incl_f, la)
        a_tot = jnp.sum(la, axis=0, keepdims=True)
        a_cum_t = a_cum.T
        e_tot_col = jnp.exp(jnp.where(z == 0, a_cum_t[:, c - 1:c], a_cum_t[:, 0:1]))
        wide = _dot_into_01(jnp.concatenate([dt, jnp.exp(a_cum), jnp.exp(a_tot - a_cum)], axis=0), ex)
        xdt = x * wide[:c]
        din, dend = wide[c:2 * c], wide[2 * c:]
        xd_end = xdt * dend
        bgs = [bm[:, g * SSD_N:(g + 1) * SSD_N] for g in range(SSD_G)]
        cgs = [cm[:, g * SSD_N:(g + 1) * SSD_N] for g in range(SSD_G)]
        cb = [_dot_nt(cg, bg) for cg, bg in zip(cgs, bgs)]
        inter = [_dot_nt(cg, s) for cg, s in zip(cgs, s_old)]
        grow = [_dot_tn(xd_end[:, g * gw:(g + 1) * gw], bg) for g, bg in enumerate(bgs)]
        intra = []
        for h in range(SSD_H):
            seg = a_cum[:, h:h + 1] - a_cum_t[h:h + 1, :]
            wmat = cb[h // hg] * jnp.where(incl, jnp.exp(seg), 0.0)
            intra.append(_dot(wmat, xdt[:, h * SSD_P:(h + 1) * SSD_P]))
        o_ref[0, 0, rows, :] = jnp.concatenate(intra, axis=1) + din * jnp.concatenate(inter, axis=1)
        new = []
        for g in range(SSD_G):
            decay = jnp.concatenate([jnp.broadcast_to(e_tot_col[h:h + 1, :], (SSD_P, 1))
                                     for h in range(g * hg, (g + 1) * hg)], axis=0)
            new.append(s_old[g] * decay + grow[g])
        return tuple(new)

    s_fin = lax.fori_loop(0, nsub, body, tuple(s_ref[g] for g in range(SSD_G)), unroll=2)
    for g in range(SSD_G):
        s_ref[g] = s_fin[g]


def _head_expand(heads, width, rows):
    h = jnp.arange(rows)[:, None]
    col = jnp.arange(heads * width)[None, :] // width
    return (h == col).astype(F32)


def ssd_scan(xbc, small, dtb, alog, ctx_len):
    bsz, lt, _ = xbc.shape
    tb = SCAN_BLOCK
    nb, cb = lt // tb, ctx_len // tb
    tmap = lambda z, n: _time_block(z, n, cb, nb)
    gw = SSD_G * SSD_N
    return pl.pallas_call(
        functools.partial(_ssd_kernel, tb=tb, c=SSD_CHUNK),
        out_shape=jax.ShapeDtypeStruct((2, bsz, lt, SSD_INNER), F32),
        grid=(2, bsz, nb),
        in_specs=[pl.BlockSpec((1, tb, SSD_INNER), lambda z, b, n: (b, tmap(z, n), 0)),
                  pl.BlockSpec((1, tb, gw), lambda z, b, n: (b, tmap(z, n), SSD_INNER // gw)),
                  pl.BlockSpec((1, tb, gw), lambda z, b, n: (b, tmap(z, n), SSD_INNER // gw + 1)),
                  pl.BlockSpec((1, tb, SMALL_W), lambda z, b, n: (b, tmap(z, n), z)),
                  pl.BlockSpec((1, 1, SMALL_W), lambda z, b, n: (z, 0, 0)),
                  pl.BlockSpec((1, 1, SMALL_W), lambda z, b, n: (z, 0, 0)),
                  pl.BlockSpec((SMALL_W, SSD_INNER), lambda z, b, n: (0, 0))],
        out_specs=pl.BlockSpec((1, 1, tb, SSD_INNER), lambda z, b, n: (z, b, tmap(z, n), 0)),
        scratch_shapes=[pltpu.VMEM((SSD_G, SSD_H // SSD_G * SSD_P, SSD_N), F32)],
        compiler_params=_cparams(("parallel", "parallel", "arbitrary")),
        name="ssd_scan",
    )(xbc, xbc, xbc, small, dtb, alog, _head_expand(SSD_H, SSD_P, SMALL_W))


ML_I_LANE = SSD_H
ML_F_LANE = SSD_H + ML_H


def _mlstm_kernel(q_ref, k_ref, v_ref, sm_ref, gb_ref, o_ref, cs_ref, ns_ref, m_ref, *, tb, c):
    z = pl.program_id(0)
    nsub = tb // c

    @pl.when(pl.program_id(2) == 0)
    def _():
        cs_ref[...] = jnp.zeros_like(cs_ref)
        ns_ref[...] = jnp.zeros_like(ns_ref)
        m_ref[...] = jnp.full(m_ref.shape, ML_M_INIT, F32)

    incl, _ = _dir_masks(z, c)
    incl_f = incl.astype(F32)
    lane = lax.broadcasted_iota(I32, (1, SMALL_W), 1)
    is_f = jnp.logical_and(lane >= ML_F_LANE, lane < ML_F_LANE + ML_H)

    def body(j, carry):
        cs_old, ns_old, m_old = carry
        jj = jnp.where(z == 0, j, nsub - 1 - j)
        rows = pl.ds(pl.multiple_of(jj * c, c), c)
        q = q_ref[0, rows, :]
        k = k_ref[0, rows, :] * ML_DK ** -0.5
        v = v_ref[0, rows, :]
        pre = sm_ref[0, rows, :] + gb_ref[0]
        gates = jnp.where(is_f, -_softplus(-pre), pre)
        f_cum = _dot(incl_f, gates, HI)
        f_tot = jnp.sum(gates, axis=0, keepdims=True)
        f_cum_t = f_cum.T
        gates_t = gates.T
        qs = [q[:, h * ML_DK:(h + 1) * ML_DK] for h in range(ML_H)]
        ks = [k[:, h * ML_DK:(h + 1) * ML_DK] for h in range(ML_H)]
        vs = [v[:, h * ML_DV:(h + 1) * ML_DV] for h in range(ML_H)]
        qk = [_dot_nt(a, b) for a, b in zip(qs, ks)]
        q_cs = [_dot(a, s) for a, s in zip(qs, cs_old)]
        outs, cs_new, ns_new, m_new_all = [], [], [], []
        for h in range(ML_H):
            li, lf = ML_I_LANE + h, ML_F_LANE + h
            f_col = f_cum[:, lf:lf + 1]
            f_last = f_tot[:, lf:lf + 1]
            logw_end = f_last - f_col + gates[:, li:li + 1]
            m_end = jnp.max(logw_end, axis=0, keepdims=True)
            kw = ks[h] * jnp.exp(logw_end - m_end)
            m_prev = m_old[h]
            logw = jnp.where(incl, f_col - f_cum_t[lf:lf + 1, :] + gates_t[li:li + 1, :], -jnp.inf)
            m_t = jnp.maximum(jnp.max(logw, axis=1, keepdims=True), f_col + m_prev)
            w_inter = jnp.exp(f_col + m_prev - m_t)
            scores = qk[h] * jnp.exp(logw - m_t)
            num = _dot(scores, vs[h]) + w_inter * q_cs[h]
            den = (jnp.sum(scores, axis=1, keepdims=True)
                   + w_inter * jnp.sum(qs[h] * ns_old[h], axis=1, keepdims=True))
            outs.append(num / jnp.maximum(jnp.abs(den), jnp.exp(-m_t)))
            m_new = jnp.maximum(f_last + m_prev, m_end)
            s_keep = jnp.exp(f_last + m_prev - m_new)
            s_loc = jnp.exp(m_end - m_new)
            cs_new.append(s_keep * cs_old[h] + s_loc * _dot_tn(kw, vs[h]))
            ns_new.append(s_keep * ns_old[h] + s_loc * jnp.sum(kw, axis=0, keepdims=True))
            m_new_all.append(m_new)
        o_ref[0, 0, rows, :] = jnp.concatenate(outs, axis=1)
        return tuple(cs_new), tuple(ns_new), tuple(m_new_all)

    heads = range(ML_H)
    init = (tuple(cs_ref[h] for h in heads), tuple(ns_ref[h] for h in heads), tuple(m_ref[h] for h in heads))
    cs_fin, ns_fin, m_fin = lax.fori_loop(0, nsub, body, init, unroll=2)
    for h in heads:
        cs_ref[h] = cs_fin[h]
        ns_ref[h] = ns_fin[h]
        m_ref[h] = m_fin[h]


def mlstm_scan(qk, pm, small, gate_bias, ctx_len):
    bsz, lt, _ = qk.shape
    tb = SCAN_BLOCK
    nb, cb = lt // tb, ctx_len // tb
    tmap = lambda z, n: _time_block(z, n, cb, nb)
    return pl.pallas_call(
        functools.partial(_mlstm_kernel, tb=tb, c=ML_CHUNK),
        out_shape=jax.ShapeDtypeStruct((2, bsz, lt, ML_VW), F32),
        grid=(2, bsz, nb),
        in_specs=[pl.BlockSpec((1, tb, ML_QK), lambda z, b, n: (b, tmap(z, n), 0)),
                  pl.BlockSpec((1, tb, ML_QK), lambda z, b, n: (b, tmap(z, n), 1)),
                  pl.BlockSpec((1, tb, ML_VW), lambda z, b, n: (b, tmap(z, n), 1)),
                  pl.BlockSpec((1, tb, SMALL_W), lambda z, b, n: (b, tmap(z, n), z)),
                  pl.BlockSpec((1, 1, SMALL_W), lambda z, b, n: (z, 0, 0))],
        out_specs=pl.BlockSpec((1, 1, tb, ML_VW), lambda z, b, n: (z, b, tmap(z, n), 0)),
        scratch_shapes=[pltpu.VMEM((ML_H, ML_DK, ML_DV), F32), pltpu.VMEM((ML_H, 1, ML_DK), F32),
                        pltpu.VMEM((ML_H, 1, 1), F32)],
        compiler_params=_cparams(("parallel", "parallel", "arbitrary")),
        name="mlstm_scan",
    )(qk, qk, pm, small, gate_bias)


def _odd_out_kernel(x_ref, mod_ref, ys_ref, xc_ref, zg_ref, ym_ref, og_ref, dsk_ref, nws_ref, nwm_ref, w_ref, o_ref):
    y = (ys_ref[0, 0] + ys_ref[1, 0] + dsk_ref[...] * xc_ref[0]) * _silu(zg_ref[0])
    ys = _group_rms(y, SSD_G, nws_ref[...])
    hm = ym_ref[0, 0] + ym_ref[1, 0]
    ym = _group_rms(hm, ML_H, nwm_ref[...]) * _sigmoid(og_ref[0])
    yy = jnp.concatenate([ys, ym], axis=1).astype(BF16)
    o_ref[0] = x_ref[0] + mod_ref[0, 0][2:3] * _dot(yy, w_ref[...])


def odd_out(x, modtab, y_s, xbc, ps, y_m, pm, d_skip, nw_s, nw_m, w_out, ctx_len):
    bsz, lt, d = x.shape
    tm = TOK_TILE
    ct = ctx_len // tm
    seq = lt - ctx_len
    tok = lambda width, col=0: pl.BlockSpec((1, tm, width), lambda b, i: (b, i + ct, col))
    two = lambda width: pl.BlockSpec((2, 1, tm, width), lambda b, i: (0, b, i + ct, 0))
    row = lambda width: pl.BlockSpec((1, width), lambda b, i: (0, 0))
    return pl.pallas_call(
        _odd_out_kernel,
        out_shape=jax.ShapeDtypeStruct((bsz, seq, d), F32),
        grid=(bsz, seq // tm),
        in_specs=[tok(d), pl.BlockSpec((1, 1, N_MOD, d), lambda b, i: (b, 1, 0, 0)),
                  two(SSD_INNER), tok(SSD_INNER), tok(SSD_INNER), two(ML_VW), tok(ML_VW, 2),
                  row(SSD_INNER), row(SSD_INNER), row(ML_VW),
                  pl.BlockSpec(w_out.shape, lambda b, i: (0, 0))],
        out_specs=pl.BlockSpec((1, tm, d), lambda b, i: (b, i, 0)),
        compiler_params=_cparams(("parallel", "parallel")),
        name="odd_out",
    )(x, modtab, y_s, xbc, ps, y_m, pm, d_skip.reshape(1, SSD_INNER), nw_s.reshape(1, SSD_INNER),
      nw_m.reshape(1, ML_VW), w_out)


def _router_kernel(x_ref, mod_ref, nw_ref, rw_ref, rb_ref, h_ref, idx_ref, prob_ref, rank_ref, cnt_ref, carry_ref,
                   *, tm):
    first = jnp.logical_and(pl.program_id(0) == 0, pl.program_id(1) == 0)

    @pl.when(first)
    def _():
        carry_ref[...] = jnp.zeros_like(carry_ref)

    h = _norm_mod(x_ref[0], nw_ref[...], mod_ref[0, 0], 3)
    h_ref[0] = h
    logits = _dot(h, rw_ref[...], HI) + rb_ref[...]
    lane = lax.broadcasted_iota(I32, (tm, N_EXP), 1).astype(F32)
    work = logits
    sel = jnp.zeros((tm, N_EXP), F32)
    vals, idxs, hots = [], [], []
    for _ in range(TOP_K):
        m = jnp.max(work, axis=1, keepdims=True)
        idx = jnp.min(jnp.where(work == m, lane, float(N_EXP)), axis=1, keepdims=True)
        hot = lane == idx
        vals.append(m)
        idxs.append(idx)
        hots.append(hot.astype(F32))
        sel = sel + hots[-1]
        work = jnp.where(hot, -jnp.inf, work)
    ex = [jnp.exp(v - vals[0]) for v in vals]
    tot = ex[0] + ex[1] + ex[2] + ex[3]
    ti = lax.broadcasted_iota(I32, (tm, tm), 0)
    si = lax.broadcasted_iota(I32, (tm, tm), 1)
    before = _dot((si < ti).astype(BF16), sel.astype(BF16)) + carry_ref[...]
    ranks = [jnp.sum(before * hot, axis=1, keepdims=True) for hot in hots]
    idx_ref[0] = jnp.concatenate(idxs, axis=1).astype(I32)
    prob_ref[0] = jnp.concatenate([e / tot for e in ex], axis=1)
    rank_ref[0] = jnp.concatenate(ranks, axis=1).astype(I32)
    carry_ref[...] = carry_ref[...] + jnp.sum(sel, axis=0, keepdims=True)
    cnt_ref[...] = carry_ref[...]


def router(x, modtab, nw, rw, rb, ctx_len):
    bsz, lt, d = x.shape
    tm = TOK_TILE
    ctx_tiles = ctx_len // tm
    tok = lambda width, dt: (jax.ShapeDtypeStruct((bsz, lt, width), dt),
                             pl.BlockSpec((1, tm, width), lambda b, i: (b, i, 0)))
    outs = [tok(d, F32), tok(TOP_K, I32), tok(TOP_K, F32), tok(TOP_K, I32),
            (jax.ShapeDtypeStruct((1, N_EXP), F32), pl.BlockSpec((1, N_EXP), lambda b, i: (0, 0)))]
    return pl.pallas_call(
        functools.partial(_router_kernel, tm=tm),
        out_shape=[o[0] for o in outs],
        grid=(bsz, lt // tm),
        in_specs=[pl.BlockSpec((1, tm, d), lambda b, i: (b, i, 0)),
                  pl.BlockSpec((1, 1, N_MOD, d), lambda b, i: (b, (i >= ctx_tiles).astype(I32), 0, 0)),
                  pl.BlockSpec((1, d), lambda b, i: (0, 0)),
                  pl.BlockSpec((d, N_EXP), lambda b, i: (0, 0)),
                  pl.BlockSpec((1, N_EXP), lambda b, i: (0, 0))],
        out_specs=[o[1] for o in outs],
        scratch_shapes=[pltpu.VMEM((1, N_EXP), F32)],
        compiler_params=_cparams(("arbitrary", "arbitrary")),
        name="router",
    )(x, modtab, nw.reshape(1, d), rw, rb.reshape(1, N_EXP))


def sc_gather_rows(table, idx):
    m = idx.shape[0]
    d = table.shape[1]
    workers = SC_CORES * SC_SUBCORES
    per_w, n_chunks = m // workers, m // workers // SC_GATHER_ROWS
    assert per_w * workers == m and n_chunks * SC_GATHER_ROWS == per_w and n_chunks % 2 == 0
    w = SC_GATHER_ROWS
    mesh = plsc.VectorSubcoreMesh(core_axis_name="c", subcore_axis_name="s")

    @functools.partial(
        pl.kernel, mesh=mesh, out_type=jax.ShapeDtypeStruct((m, d), table.dtype),
        scratch_types=[pltpu.VMEM((per_w,), I32), pltpu.VMEM((2, w, d), table.dtype),
                       pltpu.SemaphoreType.DMA((2,)), pltpu.SemaphoreType.DMA((2,))])
    def gather_kernel(table_hbm, idx_hbm, out_hbm, idx_v, rows_v, gsem, wsem):
        wid = lax.axis_index("s") * SC_CORES + lax.axis_index("c")
        base = pl.multiple_of(wid * per_w, 8)
        pltpu.sync_copy(idx_hbm.at[pl.ds(base, per_w)], idx_v)

        def gather(g, b):
            rows = idx_v.at[pl.ds(pl.multiple_of(g * w, 8), w)]
            return pltpu.make_async_copy(table_hbm.at[rows], rows_v.at[b], gsem.at[b])

        def write(g, b):
            dst = out_hbm.at[pl.ds(pl.multiple_of(base + g * w, 8), w)]
            return pltpu.make_async_copy(rows_v.at[b], dst, wsem.at[b])

        gather(0, 0).start()

        @pl.loop(0, n_chunks, step=2)
        def _(g):
            for b in (0, 1):
                cur = g + b
                gather(cur, b).wait()
                write(cur, b).start()

                @pl.when(cur >= 1)
                def _():
                    write(cur - 1, 1 - b).wait()

                @pl.when(cur + 1 < n_chunks)
                def _():
                    gather(cur + 1, 1 - b).start()

        write(n_chunks - 1, 1).wait()

    return gather_kernel(table, idx)


def _expert_kernel(be_ref, x_ref, wg_ref, bg_ref, wu_ref, bu_ref, wd_ref, bd_ref, o_ref, wg_bf, wu_bf, wd_bf):
    i = pl.program_id(0)

    @pl.when(jnp.logical_or(i == 0, be_ref[i] != be_ref[jnp.maximum(i - 1, 0)]))
    def _():
        wg_bf[...] = wg_ref[0].astype(BF16)
        wu_bf[...] = wu_ref[0].astype(BF16)
        wd_bf[...] = wd_ref[0].astype(BF16)

    xb = x_ref[...].astype(BF16)
    g = _dot(xb, wg_bf[...]) + bg_ref[0]
    u = _dot(xb, wu_bf[...]) + bu_ref[0]
    g = jnp.minimum(g, SWIGLU_LIMIT)
    u = jnp.clip(u, -SWIGLU_LIMIT, SWIGLU_LIMIT)
    act = (g * _sigmoid(SWIGLU_ALPHA * g) * (u + 1.0)).astype(BF16)
    o_ref[...] = _dot(act, wd_bf[...]) + bd_ref[0]


def expert_ffn(block_expert, x_sorted, wg, bg, wu, bu, wd, bd):
    nblk = block_expert.shape[0]
    d, f = wg.shape[1], wg.shape[2]
    wspec = lambda r, c: pl.BlockSpec((1, r, c), lambda i, be: (be[i], 0, 0))
    grid_spec = pltpu.PrefetchScalarGridSpec(
        num_scalar_prefetch=1,
        grid=(nblk,),
        in_specs=[pl.BlockSpec((MOE_BLOCK, d), lambda i, be: (i, 0)),
                  wspec(d, f), wspec(1, f), wspec(d, f), wspec(1, f), wspec(f, d), wspec(1, d)],
        out_specs=pl.BlockSpec((MOE_BLOCK, d), lambda i, be: (i, 0)),
        scratch_shapes=[pltpu.VMEM((d, f), BF16), pltpu.VMEM((d, f), BF16), pltpu.VMEM((f, d), BF16)],
    )
    return pl.pallas_call(
        _expert_kernel,
        out_shape=jax.ShapeDtypeStruct((nblk * MOE_BLOCK, d), F32),
        grid_spec=grid_spec,
        compiler_params=_cparams(("arbitrary",)),
        name="expert_ffn",
    )(block_expert, x_sorted, wg, bg.reshape(N_EXP, 1, f), wu, bu.reshape(N_EXP, 1, f), wd, bd.reshape(N_EXP, 1, d))


def _combine_kernel(x_ref, mod_ref, prob_ref, y_ref, fw_ref, o_ref, *, d, final_norm):
    p = prob_ref[0]
    acc = y_ref[0, :, 0:d] * p[:, 0:1]
    for k in range(1, TOP_K):
        acc = acc + y_ref[0, :, k * d:(k + 1) * d] * p[:, k:k + 1]
    out = x_ref[0] + mod_ref[0, 0][5:6] * acc
    if final_norm:
        out = out * lax.rsqrt(jnp.mean(out * out, axis=-1, keepdims=True) + EPS) * fw_ref[...]
    o_ref[0] = out


def combine(x, modtab, probs, y_tok, final_w, ctx_len, final_norm):
    bsz, lt, d = x.shape
    tm = TOK_TILE
    ctx_tiles = ctx_len // tm
    return pl.pallas_call(
        functools.partial(_combine_kernel, d=d, final_norm=final_norm),
        out_shape=jax.ShapeDtypeStruct((bsz, lt, d), F32),
        grid=(bsz, lt // tm),
        in_specs=[pl.BlockSpec((1, tm, d), lambda b, i: (b, i, 0)),
                  pl.BlockSpec((1, 1, N_MOD, d), lambda b, i: (b, (i >= ctx_tiles).astype(I32), 0, 0)),
                  pl.BlockSpec((1, tm, TOP_K), lambda b, i: (b, i, 0)),
                  pl.BlockSpec((1, tm, TOP_K * d), lambda b, i: (b, i, 0)),
                  pl.BlockSpec((1, d), lambda b, i: (0, 0))],
        out_specs=pl.BlockSpec((1, tm, d), lambda b, i: (b, i, 0)),
        compiler_params=_cparams(("parallel", "parallel")),
        name="combine",
    )(x, modtab, probs, y_tok, final_w.reshape(1, d))


def moe_layer(x, modtab, nw, rw, rb, wg, bg, wu, bu, wd, bd, final_w, ctx_len, final_norm):
    bsz, lt, d = x.shape
    t = bsz * lt
    h, idx, probs, rank, counts = router(x, modtab, nw, rw, rb, ctx_len)
    counts = counts.reshape(N_EXP).astype(I32)
    padded = (counts + MOE_BLOCK - 1) // MOE_BLOCK * MOE_BLOCK
    ends = jnp.cumsum(padded)
    pstart = ends - padded
    idx = idx.reshape(t, TOP_K)
    dest = pstart[idx] + rank.reshape(t, TOP_K)
    nblk = -(-t * TOP_K // MOE_BLOCK) + N_EXP
    block_row0 = jnp.arange(nblk, dtype=I32) * MOE_BLOCK
    block_expert = jnp.minimum(jnp.sum((ends[None, :] <= block_row0[:, None]).astype(I32), axis=1), N_EXP - 1)
    tok_ids = jnp.broadcast_to(jnp.arange(t, dtype=I32)[:, None], (t, TOP_K))
    sorted_tok = jnp.zeros((nblk * MOE_BLOCK,), I32).at[dest.reshape(-1)].set(
        tok_ids.reshape(-1), unique_indices=True, mode="promise_in_bounds")
    x_sorted = sc_gather_rows(h.reshape(t, d), sorted_tok)
    y_sorted = expert_ffn(block_expert, x_sorted, wg, bg, wu, bu, wd, bd)
    y_tok = sc_gather_rows(y_sorted, dest.reshape(t * TOP_K)).reshape(bsz, lt, TOP_K * d)
    return combine(x, modtab, probs, y_tok, final_w, ctx_len, final_norm)


def even_layer(x, modtab, norm1_w, w_in, w_out, mu, w0, w2, a0, a2, g2, k_k, k_a, r_k, ln_w, ln_b, lb, hg_norm_w,
               ctx_len):
    w_in = w_in.astype(BF16)
    pr, ph = in_proj(x, modtab, norm1_w, [w_in[:, :RW_PROJ], w_in[:, RW_PROJ:]], ctx_len)
    r, v, kk, g, bv, lw, km, bb = rwkv_prep(pr, mu, w0, w2, a0, a2, g2, k_k, k_a, r_k, ctx_len)
    o_r = rwkv_scan(r, v, kk, lw, km, bb, ctx_len)
    o_g = gla_scan(ph, lb, ctx_len)
    return even_out(x, modtab, o_r, bv, g, o_g, ph, ln_w, ln_b, hg_norm_w, w_out.astype(BF16), ctx_len)


def _dir_slab(cols, width):
    halves = []
    for z in range(2):
        parts = [c[:, z * (c.shape[1] // 2):(z + 1) * (c.shape[1] // 2)] for c in cols]
        used = sum(p.shape[1] for p in parts)
        parts.append(jnp.zeros((cols[0].shape[0], width - used), cols[0].dtype))
        halves.append(jnp.concatenate(parts, axis=1))
    return jnp.concatenate(halves, axis=1)


def _dir_rows(vals, width):
    rows = jnp.concatenate(vals, axis=1)
    return jnp.pad(rows, ((0, 0), (0, width - rows.shape[1]))).reshape(2, 1, width)


def odd_layer(x, modtab, norm1_w, w_in, w_out, s_conv_w, s_conv_b, dt_bias, a_log, d_skip, s_norm_w, m_conv_w,
              m_conv_b, i_bias, f_bias, m_norm_w, ctx_len):
    o = 0
    cols = {}
    for name, width in (("zg", SSD_INNER), ("xbc", SSD_CONV_CH), ("dt", 2 * SSD_H), ("qk", 2 * ML_QK),
                        ("v", ML_VW), ("og", ML_VW), ("ig", 2 * ML_H), ("fg", 2 * ML_H)):
        cols[name] = w_in[:, o:o + width]
        o += width
    slab_m = jnp.concatenate([cols["qk"], cols["v"], cols["og"]], axis=1).astype(BF16)
    slab_small = _dir_slab([cols["dt"], cols["ig"], cols["fg"]], SMALL_W).astype(BF16)
    pxbc, zg, pm, small = in_proj(x, modtab, norm1_w, [cols["xbc"].astype(BF16), cols["zg"].astype(BF16), slab_m,
                                                       slab_small], ctx_len)
    xbc, qk = odd_prep(pxbc, pm, s_conv_w, s_conv_b, m_conv_w, m_conv_b, ctx_len)
    zeros_g = jnp.zeros((2, 2 * ML_H), F32)
    y_s = ssd_scan(xbc, small, _dir_rows([dt_bias, zeros_g], SMALL_W), _dir_rows([a_log, zeros_g], SMALL_W), ctx_len)
    y_m = mlstm_scan(qk, pm, small, _dir_rows([jnp.zeros((2, SSD_H), F32), i_bias, f_bias], SMALL_W), ctx_len)
    return odd_out(x, modtab, y_s, xbc, zg, y_m, pm, jnp.repeat(d_skip, SSD_P), s_norm_w, m_norm_w,
                   w_out.astype(BF16), ctx_len)


def _to_col_major(t):
    b, s, ch = t.shape
    return t.reshape(b, s // GRID_W, GRID_W, ch).transpose(0, 2, 1, 3).reshape(b, s, ch)


def _to_row_major(t):
    b, s, ch = t.shape
    return t.reshape(b, GRID_W, s // GRID_W, ch).transpose(0, 2, 1, 3).reshape(b, s, ch)


def kernel(x, c, ctx, c_ctx, ada_w, ada_b, norm1_w, norm2_w, even_w_in, even_w_out, rwkv_mu, rwkv_w0, rwkv_w2, rwkv_a0, rwkv_a2, rwkv_g2, rwkv_k_k, rwkv_k_a, rwkv_r_k, rwkv_ln_w, rwkv_ln_b, hgrn_lower_bounds, hgrn_norm_w, odd_w_in, odd_w_out, ssd_conv_w, ssd_conv_b, ssd_dt_bias, ssd_a_log, ssd_d, ssd_norm_w, mlstm_conv_w, mlstm_conv_b, mlstm_i_bias, mlstm_f_bias, mlstm_norm_w, router_w, router_b, exp_w_gate, exp_b_gate, exp_w_up, exp_b_up, exp_w_down, exp_b_down, final_norm_w):
    lc = ctx.shape[1]
    lower_bounds = jnp.cumsum(jax.nn.softmax(hgrn_lower_bounds.astype(F32), axis=0), axis=0)
    moe = lambda l: (norm2_w[l], router_w[l], router_b[l], exp_w_gate[l], exp_b_gate[l], exp_w_up[l], exp_b_up[l],
                     exp_w_down[l], exp_b_down[l], final_norm_w)
    xa = jnp.concatenate([ctx, x], axis=1)
    modtab = _mod_table(c, c_ctx, ada_w[0], ada_b[0])
    xa = even_layer(xa, modtab, norm1_w[0], even_w_in[0], even_w_out[0], rwkv_mu[0], rwkv_w0[0], rwkv_w2[0],
                    rwkv_a0[0], rwkv_a2[0], rwkv_g2[0], rwkv_k_k[0], rwkv_k_a[0], rwkv_r_k[0], rwkv_ln_w[0],
                    rwkv_ln_b[0], lower_bounds[0], hgrn_norm_w[0], lc)
    xa = moe_layer(xa, modtab, *moe(0), lc, False)
    xa = jnp.concatenate([xa[:, :lc], _to_col_major(xa[:, lc:])], axis=1)
    modtab = _mod_table(c, c_ctx, ada_w[1], ada_b[1])
    xl = odd_layer(xa, modtab, norm1_w[1], odd_w_in[0], odd_w_out[0], ssd_conv_w[0], ssd_conv_b[0], ssd_dt_bias[0],
                   ssd_a_log[0], ssd_d[0], ssd_norm_w[0], mlstm_conv_w[0], mlstm_conv_b[0], mlstm_i_bias[0],
                   mlstm_f_bias[0], mlstm_norm_w[0], lc)
    xl = moe_layer(xl, modtab, *moe(1), 0, True)
    return _to_row_major(xl)
```

```python
import functools
import math

import jax
import jax.numpy as jnp
from jax import lax
from jax.experimental import pallas as pl
from jax.experimental.pallas import tpu as pltpu
from jax.experimental.pallas import tpu_sc as plsc

F32 = jnp.float32
BF16 = jnp.bfloat16
I32 = jnp.int32
HI = lax.Precision.HIGHEST

D_MODEL = 1024
GRID_W = 64
N_MOD = 6
EPS = 1e-6
RW_H, RW_N = 8, 64
RW_W = RW_H * RW_N
RW_RANK = 64
RW_GATE_RANK = 128
RW_GN_EPS = 64e-5
RW_PROJ = 3 * RW_W + 4 * RW_RANK + RW_GATE_RANK
RW_CHUNK = 64
HG_H, HG_DK, HG_DV = 4, 128, 128
HG_W = HG_H * HG_DK
HG_CHUNK = 32
HG_PROJ = 5 * HG_W
SSD_H, SSD_P, SSD_G, SSD_N = 16, 64, 2, 128
SSD_INNER = SSD_H * SSD_P
SSD_CONV_CH = SSD_INNER + 2 * SSD_G * SSD_N
SSD_CHUNK = 64
ML_H, ML_DK, ML_DV = 4, 128, 256
ML_QK = ML_H * ML_DK
ML_VW = ML_H * ML_DV
ML_CHUNK = 64
ML_M_INIT = -1e30
CONV_K = 5
N_EXP = 32
TOP_K = 4
D_FF = 1024
SWIGLU_LIMIT = 7.0
SWIGLU_ALPHA = 1.702
MOE_BLOCK = 512

TOK_TILE = 256
SCAN_BLOCK = 256
HALO = 8
SMALL_W = 128
VMEM_LIMIT = 56 * 1024 * 1024
SC_CORES, SC_SUBCORES = 2, 16
SC_GATHER_ROWS = 32


def _cparams(sem):
    return pltpu.CompilerParams(dimension_semantics=sem, vmem_limit_bytes=VMEM_LIMIT)


def _dot(a, b, precision=None):
    return jnp.dot(a, b, preferred_element_type=F32, precision=precision)


def _dot_nt(a, b, precision=None):
    return lax.dot_general(a, b, (((1,), (1,)), ((), ())), preferred_element_type=F32, precision=precision)


def _dot_tn(a, b, precision=None):
    return lax.dot_general(a, b, (((0,), (0,)), ((), ())), preferred_element_type=F32, precision=precision)


def _bf16_parts(x):
    hi = x.astype(BF16)
    rest = x - hi.astype(F32)
    mid = rest.astype(BF16)
    return hi, mid, (rest - mid.astype(F32)).astype(BF16)


def _dot_into_01(x, sel):
    sel = sel.astype(BF16)
    p0, p1, p2 = _bf16_parts(x)
    return _dot(p0, sel) + _dot(p1, sel) + _dot(p2, sel)


def _dot_01_into(sel, x):
    sel = sel.astype(BF16)
    p0, p1, p2 = _bf16_parts(x)
    return _dot(sel, p0) + _dot(sel, p1) + _dot(sel, p2)


def _sigmoid(x):
    return 1.0 / (1.0 + jnp.exp(-x))


def _silu(x):
    return x * _sigmoid(x)


def _softplus(x):
    return jnp.maximum(x, 0.0) + jnp.log(1.0 + jnp.exp(-jnp.abs(x)))


def _dir_masks(z, c):
    ti = lax.broadcasted_iota(I32, (c, c), 0)
    si = lax.broadcasted_iota(I32, (c, c), 1)
    d = (si - ti) * jnp.where(z == 0, 1, -1)
    return d <= 0, d < 0


def _time_block(z, n, ctx_blocks, n_blocks):
    rev = jnp.where(n < ctx_blocks, ctx_blocks - 1 - n, n_blocks - 1 - (n - ctx_blocks))
    return jnp.where(z == 0, n, rev)


def _mod_kernel(c_ref, w_ref, b_ref, o_ref):
    o_ref[...] = _dot(_silu(c_ref[...]), w_ref[...]) + b_ref[...]


def ada_mod(cc, w, b):
    rows, d = cc.shape
    n = w.shape[1]
    tn = 1536
    return pl.pallas_call(
        _mod_kernel,
        out_shape=jax.ShapeDtypeStruct((rows, n), F32),
        grid=(n // tn,),
        in_specs=[pl.BlockSpec((rows, d), lambda j: (0, 0)),
                  pl.BlockSpec((d, tn), lambda j: (0, j)),
                  pl.BlockSpec((1, tn), lambda j: (0, j))],
        out_specs=pl.BlockSpec((rows, tn), lambda j: (0, j)),
        compiler_params=_cparams(("arbitrary",)),
        name="ada_mod",
    )(cc, w, b.reshape(1, n))


def _mod_table(c, c_ctx, w, b):
    bsz = c.shape[0]
    rows = -(-(bsz + 1) // 8) * 8
    cc = jnp.zeros((rows, D_MODEL), F32).at[:bsz].set(c).at[bsz].set(c_ctx)
    m = ada_mod(cc, w, b)
    mod = m[:bsz].reshape(bsz, 1, N_MOD, D_MODEL)
    mod_c = jnp.broadcast_to(m[bsz].reshape(1, 1, N_MOD, D_MODEL), (bsz, 1, N_MOD, D_MODEL))
    return jnp.concatenate([mod_c, mod], axis=1)


def _norm_mod(x, nw, m, shift_idx):
    y = x * lax.rsqrt(jnp.mean(x * x, axis=-1, keepdims=True) + EPS) * nw
    return y * (1.0 + m[shift_idx + 1:shift_idx + 2]) + m[shift_idx:shift_idx + 1]


def _in_proj_kernel(x_ref, mod_ref, nw_ref, *rest, n_out):
    w_refs, o_refs = rest[:n_out], rest[n_out:]
    h = _norm_mod(x_ref[0], nw_ref[...], mod_ref[0, 0], 0).astype(BF16)
    for w_ref, o_ref in zip(w_refs, o_refs):
        o_ref[0] = _dot(h, w_ref[...])


def in_proj(x, modtab, nw, slabs, ctx_len):
    bsz, lt, d = x.shape
    tm = TOK_TILE
    ctx_tiles = ctx_len // tm
    in_specs = [pl.BlockSpec((1, tm, d), lambda b, i: (b, i, 0)),
                pl.BlockSpec((1, 1, N_MOD, d), lambda b, i: (b, (i >= ctx_tiles).astype(I32), 0, 0)),
                pl.BlockSpec((1, d), lambda b, i: (0, 0))]
    in_specs += [pl.BlockSpec(w.shape, lambda b, i: (0, 0)) for w in slabs]
    return pl.pallas_call(
        functools.partial(_in_proj_kernel, n_out=len(slabs)),
        out_shape=[jax.ShapeDtypeStruct((bsz, lt, w.shape[1]), F32) for w in slabs],
        grid=(bsz, lt // tm),
        in_specs=in_specs,
        out_specs=[pl.BlockSpec((1, tm, w.shape[1]), lambda b, i: (b, i, 0)) for w in slabs],
        compiler_params=_cparams(("parallel", "parallel")),
        name="in_proj",
    )(x, modtab, nw.reshape(1, d), *slabs)


def _halo_specs(tm, width, lt):
    per = tm // HALO
    last = lt // HALO - 1
    return [pl.BlockSpec((1, tm, width), lambda b, i: (b, i, 0)),
            pl.BlockSpec((1, HALO, width), lambda b, i: (b, jnp.maximum(i * per - 1, 0), 0)),
            pl.BlockSpec((1, HALO, width), lambda b, i: (b, jnp.minimum((i + 1) * per, last), 0))]


def _segment_edges(i, tm, ctx_len, lt):
    row0 = i * tm
    has_prev = jnp.logical_and(row0 != 0, row0 != ctx_len)
    has_next = jnp.logical_and(row0 + tm != ctx_len, row0 + tm != lt)
    return has_prev, has_next


def _rwkv_prep_kernel(p_ref, pp_ref, pn_ref, mu_ref, w0_ref, w2_ref, a0_ref, a2_ref, g2_ref, kk_ref, ka_ref,
                      rk_ref, bd_ref, r_o, v_o, kk_o, g_o, bv_o, lw_o, km_o, bb_o, *, tm, ctx_len, lt):
    has_prev, has_next = _segment_edges(pl.program_id(1), tm, ctx_len, lt)
    p = p_ref[0]
    prev_row = jnp.where(has_prev, pp_ref[0, HALO - 1:HALO, :], 0.0)
    next_row = jnp.where(has_next, pn_ref[0, 0:1, :], 0.0)
    rid = lax.broadcasted_iota(I32, (tm, 1), 0)
    up = jnp.where(rid == 0, prev_row, pltpu.roll(p, 1, 0))
    dn = jnp.where(rid == tm - 1, next_row, pltpu.roll(p, tm - 1, 0))
    p = p + (0.5 * (up + dn) - p) * mu_ref[...]
    w = RW_W
    r, k, v = p[:, 0:w], p[:, w:2 * w], p[:, 2 * w:3 * w]
    o = 3 * w
    wd = jnp.tanh(p[:, o:o + 2 * RW_RANK])
    ad = p[:, o + 2 * RW_RANK:o + 4 * RW_RANK]
    gd = p[:, o + 4 * RW_RANK:o + 4 * RW_RANK + RW_GATE_RANK]
    bd = bd_ref[...]
    kk = k * kk_ref[...]
    kk = kk * lax.rsqrt(_dot_into_01(kk * kk, bd) + 1e-12)
    r_o[0] = r
    v_o[0] = v
    kk_o[0] = kk
    g_o[0] = _dot(_sigmoid(gd), g2_ref[...])
    km_sum = jnp.zeros_like(k)
    for z in range(2):
        sl = slice(z * RW_RANK, (z + 1) * RW_RANK)
        w_pre = w0_ref[z:z + 1, :] + _dot(wd[:, sl], w2_ref[z])
        lw_o[z, 0] = -_sigmoid(w_pre) * math.exp(-0.5)
        a = _sigmoid(a0_ref[z:z + 1, :] + _dot(ad[:, sl], a2_ref[z]))
        km = k * (1.0 + (a - 1.0) * ka_ref[...])
        km_o[z, 0] = km
        bb_o[z, 0] = kk * a
        km_sum = km_sum + km
    bv_o[0] = _dot_into_01(r * km_sum * rk_ref[...], bd) * v


def _head_block_ones(width, head):
    i = jnp.arange(width) // head
    return (i[:, None] == i[None, :]).astype(F32)


def rwkv_prep(p, mu, w0, w2, a0, a2, g2, k_k, k_a, r_k, ctx_len):
    bsz, lt, width = p.shape
    tm = TOK_TILE
    w = RW_W
    full = lambda a: pl.BlockSpec(a.shape, lambda b, i: (0,) * a.ndim)
    params = [mu.reshape(1, width), w0, w2, a0, a2, g2, k_k.reshape(1, w), k_a.reshape(1, w), r_k.reshape(1, w),
              _head_block_ones(w, RW_N)]
    one = jax.ShapeDtypeStruct((bsz, lt, w), F32)
    two = jax.ShapeDtypeStruct((2, bsz, lt, w), F32)
    one_spec = pl.BlockSpec((1, tm, w), lambda b, i: (b, i, 0))
    two_spec = pl.BlockSpec((2, 1, tm, w), lambda b, i: (0, b, i, 0))
    return pl.pallas_call(
        functools.partial(_rwkv_prep_kernel, tm=tm, ctx_len=ctx_len, lt=lt),
        out_shape=[one] * 5 + [two] * 3,
        grid=(bsz, lt // tm),
        in_specs=_halo_specs(tm, width, lt) + [full(a) for a in params],
        out_specs=[one_spec] * 5 + [two_spec] * 3,
        compiler_params=_cparams(("parallel", "parallel")),
        name="rwkv_prep",
    )(p, p, p, *params)


def _unit_lower_inverses(n_mats, eye, c):
    xs = [eye - n for n in n_mats]
    ps = [_dot(n, n) for n in n_mats]
    span = 4
    while True:
        xs = [x + _dot(x, p) for x, p in zip(xs, ps)]
        if span >= c:
            return xs
        ps = [_dot(p, p) for p in ps]
        span *= 2


def _rwkv_scan_kernel(r_ref, v_ref, kk_ref, lw_ref, km_ref, bb_ref, o_ref, s_ref, *, tb, c):
    z = pl.program_id(0)
    nsub = tb // c

    @pl.when(pl.program_id(2) == 0)
    def _():
        s_ref[...] = jnp.zeros_like(s_ref)

    incl, strict = _dir_masks(z, c)
    incl_f, strict_f = incl.astype(F32), strict.astype(F32)
    eye = incl_f - strict_f

    def body(j, s_old):
        jj = jnp.where(z == 0, j, nsub - 1 - j)
        rows = pl.ds(pl.multiple_of(jj * c, c), c)
        r, v, kk = r_ref[0, rows, :], v_ref[0, rows, :], kk_ref[0, rows, :]
        lw, km, bb = lw_ref[0, 0, rows, :], km_ref[0, 0, rows, :], bb_ref[0, 0, rows, :]
        cum = _dot_01_into(incl_f, lw)
        tot = jnp.sum(lw, axis=0, keepdims=True)
        e_neg = jnp.exp(-cum)
        e_end = jnp.exp(tot - cum)
        a_t = kk * jnp.exp(cum - lw)
        r_t = r * jnp.exp(cum)
        b_t, k_t = bb * e_neg, km * e_neg
        b_p, k_p = bb * e_end, km * e_end
        e_tot = jnp.exp(tot)
        heads = [slice(h * RW_N, (h + 1) * RW_N) for h in range(RW_H)]
        ar =[jnp.concatenate([a_t[:, hs], r_t[:, hs]], axis=0) for hs in heads]
        bk = [jnp.concatenate([b_t[:, hs], k_t[:, hs]], axis=0) for hs in heads]
        bkp = [jnp.concatenate([b_p[:, hs], k_p[:, hs]], axis=0) for hs in heads]
        vh = [v[:, hs] for hs in heads]
        sc = [_dot_nt(a, b) for a, b in zip(ar, bk)]
        ars = [_dot_nt(a, s) for a, s in zip(ar, s_old)]
        m_ak_v = [_dot(x[:c, c:] * strict_f, vv) for x, vv in zip(sc, vh)]
        inv = _unit_lower_inverses([x[:c, :c] * strict_f for x in sc], eye, c)
        u = [-_dot(i, a[:c] + m) for i, a, m in zip(inv, ars, m_ak_v)]
        uv = [jnp.concatenate([uu, vv], axis=0) for uu, vv in zip(u, vh)]
        n_rbk = [jnp.concatenate([x[c:, :c] * incl_f, x[c:, c:] * incl_f], axis=1) for x in sc]
        outs = [a[c:] + _dot(n, w) for a, n, w in zip(ars, n_rbk, uv)]
        o_ref[0, 0, rows, :] = jnp.concatenate(outs, axis=1)
        return tuple(s * e_tot[:, hs] + _dot_tn(w, b) for s, hs, w, b in zip(s_old, heads, uv, bkp))

    s_fin = lax.fori_loop(0, nsub, body, tuple(s_ref[h] for h in range(RW_H)), unroll=2)
    for h in range(RW_H):
        s_ref[h] = s_fin[h]


def rwkv_scan(r, v, kk, lw, km, bb, ctx_len):
    bsz, lt, w = r.shape
    tb = SCAN_BLOCK
    nb, cb = lt // tb, ctx_len // tb
    tmap = lambda z, b, n: _time_block(z, n, cb, nb)
    one = pl.BlockSpec((1, tb, w), lambda z, b, n: (b, tmap(z, b, n), 0))
    two = pl.BlockSpec((1, 1, tb, w), lambda z, b, n: (z, b, tmap(z, b, n), 0))
    return pl.pallas_call(
        functools.partial(_rwkv_scan_kernel, tb=tb, c=RW_CHUNK),
        out_shape=jax.ShapeDtypeStruct((2, bsz, lt, w), F32),
        grid=(2, bsz, nb),
        in_specs=[one, one, one, two, two, two],
        out_specs=two,
        scratch_shapes=[pltpu.VMEM((RW_H, RW_N, RW_N), F32)],
        compiler_params=_cparams(("parallel", "parallel", "arbitrary")),
        name="rwkv_scan",
    )(r, v, kk, lw, km, bb)


def _gla_kernel(q_ref, f_ref, i_ref, lb_ref, o_ref, s_ref, *, tb, c):
    z = pl.program_id(0)
    nsub = tb // c

    @pl.when(pl.program_id(2) == 0)
    def _():
        s_ref[...] = jnp.zeros_like(s_ref)

    incl, _ = _dir_masks(z, c)
    incl_f = incl.astype(F32)
    lb = lb_ref[...]

    def body(j, s_old):
        jj = jnp.where(z == 0, j, nsub - 1 - j)
        rows = pl.ds(pl.multiple_of(jj * c, c), c)
        q = _silu(q_ref[0, rows, :]) * HG_DK ** -0.5
        f = f_ref[0, rows, :]
        v = i_ref[0, rows, :]
        log_f = jnp.log(lb + (1.0 - lb) * _sigmoid(f))
        k = (1.0 - lb) * _sigmoid(-f)
        cum = _dot_01_into(incl_f, log_f)
        tot = jnp.sum(log_f, axis=0, keepdims=True)
        q_in = q * jnp.exp(cum)
        k_in = k * jnp.exp(-cum)
        k_end = k * jnp.exp(tot - cum)
        e_tot = jnp.exp(tot)
        heads = [slice(h * HG_DK, (h + 1) * HG_DK) for h in range(HG_H)]
        a = [_dot_nt(q_in[:, hs], k_in[:, hs]) * incl_f for hs in heads]
        inter = [_dot_nt(q_in[:, hs], s) for hs, s in zip(heads, s_old)]
        grow = [_dot_tn(v[:, hs], k_end[:, hs]) for hs in heads]
        outs = [_dot(aa, v[:, hs]) + ii for aa, hs, ii in zip(a, heads, inter)]
        o_ref[0, 0, rows, :] = jnp.concatenate(outs, axis=1)
        return tuple(s * e_tot[:, hs] + g for s, hs, g in zip(s_old, heads, grow))

    s_fin = lax.fori_loop(0, nsub, body, tuple(s_ref[h] for h in range(HG_H)), unroll=4)
    for h in range(HG_H):
        s_ref[h] = s_fin[h]


def gla_scan(ph, lb, ctx_len):
    bsz, lt, _ = ph.shape
    w = HG_W
    tb = SCAN_BLOCK
    nb, cb = lt // tb, ctx_len // tb
    tmap = lambda z, n: _time_block(z, n, cb, nb)
    return pl.pallas_call(
        functools.partial(_gla_kernel, tb=tb, c=HG_CHUNK),
        out_shape=jax.ShapeDtypeStruct((2, bsz, lt, w), F32),
        grid=(2, bsz, nb),
        in_specs=[pl.BlockSpec((1, tb, w), lambda z, b, n: (b, tmap(z, n), 0)),
                  pl.BlockSpec((1, tb, w), lambda z, b, n: (b, tmap(z, n), 1 + z)),
                  pl.BlockSpec((1, tb, w), lambda z, b, n: (b, tmap(z, n), 3)),
                  pl.BlockSpec((1, w), lambda z, b, n: (0, 0))],
        out_specs=pl.BlockSpec((1, 1, tb, w), lambda z, b, n: (z, b, tmap(z, n), 0)),
        scratch_shapes=[pltpu.VMEM((HG_H, HG_DV, HG_DK), F32)],
        compiler_params=_cparams(("parallel", "parallel", "arbitrary")),
        name="gla_scan",
    )(ph, ph, ph, lb.reshape(1, w))


def _group_rms(y, groups, gain):
    width = y.shape[-1] // groups
    parts = []
    for g in range(groups):
        yg = y[:, g * width:(g + 1) * width]
        parts.append(yg * lax.rsqrt(jnp.mean(yg * yg, axis=-1, keepdims=True) + EPS))
    return jnp.concatenate(parts, axis=1) * gain


def _even_out_kernel(x_ref, mod_ref, or_ref, bv_ref, g_ref, og_ref, gate_ref, lnw_ref, lnb_ref, nw_ref, bd_ref,
                     w_ref, o_ref):
    bd = bd_ref[...]
    o = or_ref[0, 0] + or_ref[1, 0]
    mu = _dot_into_01(o, bd) * (1.0 / RW_N)
    oc = o - mu
    var = _dot_into_01(oc * oc, bd) * (1.0 / RW_N)
    yr = (oc * lax.rsqrt(var + RW_GN_EPS) * lnw_ref[...] + lnb_ref[...] + bv_ref[0]) * g_ref[0]
    og = og_ref[0, 0] + og_ref[1, 0]
    yh = _group_rms(og, HG_H, nw_ref[...]) * _silu(gate_ref[0])
    y = jnp.concatenate([yr, yh], axis=1).astype(BF16)
    o_ref[0] = x_ref[0] + mod_ref[0, 0][2:3] * _dot(y, w_ref[...])


def even_out(x, modtab, o_r, bv, g, o_g, ph, ln_w, ln_b, norm_w, w_out, ctx_len):
    bsz, lt, d = x.shape
    tm = TOK_TILE
    ctx_tiles = ctx_len // tm
    w = RW_W
    tok = lambda width, col=0: pl.BlockSpec((1, tm, width), lambda b, i: (b, i, col))
    two = pl.BlockSpec((2, 1, tm, w), lambda b, i: (0, b, i, 0))
    row = lambda width: pl.BlockSpec((1, width), lambda b, i: (0, 0))
    return pl.pallas_call(
        _even_out_kernel,
        out_shape=jax.ShapeDtypeStruct((bsz, lt, d), F32),
        grid=(bsz, lt // tm),
        in_specs=[tok(d),
                  pl.BlockSpec((1, 1, N_MOD, d), lambda b, i: (b, (i >= ctx_tiles).astype(I32), 0, 0)),
                  two, tok(w), tok(w), two, tok(HG_W, 4), row(w), row(w), row(HG_W),
                  pl.BlockSpec((w, w), lambda b, i: (0, 0)),
                  pl.BlockSpec(w_out.shape, lambda b, i: (0, 0))],
        out_specs=tok(d),
        compiler_params=_cparams(("parallel", "parallel")),
        name="even_out",
    )(x, modtab, o_r, bv, g, o_g, ph, ln_w.reshape(1, w), ln_b.reshape(1, w), norm_w.reshape(1, HG_W),
      _head_block_ones(w, RW_N), w_out)


def _conv_silu(x_ref, xp_ref, xn_ref, w_ref, b_ref, has_prev, has_next, tm):
    prev = jnp.where(has_prev, xp_ref[0], 0.0)
    nxt = jnp.where(has_next, xn_ref[0], 0.0)
    ext = jnp.concatenate([prev, x_ref[0], nxt], axis=0)
    rows = tm + 2 * HALO
    pad = CONV_K // 2
    acc = b_ref[...] + jnp.zeros((tm, ext.shape[1]), F32)
    for j in range(CONV_K):
        d = j - pad
        sh = ext if d == 0 else pltpu.roll(ext, (-d) % rows, 0)
        acc = acc + sh[HALO:HALO + tm] * w_ref[j:j + 1, :]
    return _silu(acc)


def _odd_prep_kernel(xs_ref, xsp_ref, xsn_ref, xm_ref, xmp_ref, xmn_ref, ws_ref, bs_ref, wm_ref, bm_ref,
                     os_ref, om_ref, *, tm, ctx_len, lt):
    has_prev, has_next = _segment_edges(pl.program_id(1), tm, ctx_len, lt)
    os_ref[0] = _conv_silu(xs_ref, xsp_ref, xsn_ref, ws_ref, bs_ref, has_prev, has_next, tm)
    om_ref[0] = _conv_silu(xm_ref, xmp_ref, xmn_ref, wm_ref, bm_ref, has_prev, has_next, tm)


def odd_prep(xbc, qk, ws, bs, wm, bm, ctx_len):
    bsz, lt, cs = xbc.shape
    cm = wm.shape[1]
    tm = TOK_TILE
    full = lambda a: pl.BlockSpec(a.shape, lambda b, i: (0,) * a.ndim)
    params = [ws, bs.reshape(1, cs), wm, bm.reshape(1, cm)]
    return pl.pallas_call(
        functools.partial(_odd_prep_kernel, tm=tm, ctx_len=ctx_len, lt=lt),
        out_shape=[jax.ShapeDtypeStruct((bsz, lt, cs), F32), jax.ShapeDtypeStruct((bsz, lt, cm), F32)],
        grid=(bsz, lt // tm),
        in_specs=_halo_specs(tm, cs, lt) + _halo_specs(tm, cm, lt) + [full(a) for a in params],
        out_specs=[pl.BlockSpec((1, tm, cs), lambda b, i: (b, i, 0)),
                   pl.BlockSpec((1, tm, cm), lambda b, i: (b, i, 0))],
        compiler_params=_cparams(("parallel", "parallel")),
        name="odd_prep",
    )(xbc, xbc, xbc, qk, qk, qk, *params)


def _ssd_kernel(x_ref, b_ref, c_ref, sm_ref, dtb_ref, alog_ref, ex_ref, o_ref, s_ref, *, tb, c):
    z = pl.program_id(0)
    nsub = tb // c
    hg = SSD_H // SSD_G

    @pl.when(pl.program_id(2) == 0)
    def _():
        s_ref[...] = jnp.zeros_like(s_ref)

    incl, _ = _dir_masks(z, c)
    incl_f = incl.astype(F32)
    ex = ex_ref[...]

    gw = hg * SSD_P

    def body(j, s_old):
        jj = jnp.where(z == 0, j, nsub - 1 - j)
        rows = pl.ds(pl.multiple_of(jj * c, c), c)
        x = x_ref[0, rows, :]
        bm = b_ref[0, rows, :]
        cm = c_ref[0, rows, :]
        dt = _softplus(sm_ref[0, rows, :] + dtb_ref[0])
        la = -dt * jnp.exp(alog_ref[0])
        a_cum = _dot_01_into(incl_f, la)
        a_tot = jnp.sum(la, axis=0, keepdims=True)
        a_cum_t = a_cum.T
        e_tot_col = jnp.exp(jnp.where(z == 0, a_cum_t[:, c - 1:c], a_cum_t[:, 0:1]))
        wide = _dot_into_01(jnp.concatenate([dt, jnp.exp(a_cum), jnp.exp(a_tot - a_cum)], axis=0), ex)
        xdt = x * wide[:c]
        din, dend = wide[c:2 * c], wide[2 * c:]
        xd_end = xdt * dend
        bgs = [bm[:, g * SSD_N:(g + 1) * SSD_N] for g in range(SSD_G)]
        cgs = [cm[:, g * SSD_N:(g + 1) * SSD_N] for g in range(SSD_G)]
        cb = [_dot_nt(cg, bg) for cg, bg in zip(cgs, bgs)]
        inter = [_dot_nt(cg, s) for cg, s in zip(cgs, s_old)]
        grow = [_dot_tn(xd_end[:, g * gw:(g + 1) * gw], bg) for g, bg in enumerate(bgs)]
        intra = []
        for h in range(SSD_H):
            seg = a_cum[:, h:h + 1] - a_cum_t[h:h + 1, :]
            wmat = cb[h // hg] * jnp.where(incl, jnp.exp(seg), 0.0)
            intra.append(_dot(wmat, xdt[:, h * SSD_P:(h + 1) * SSD_P]))
        o_ref[0, 0, rows, :] = jnp.concatenate(intra, axis=1) + din * jnp.concatenate(inter, axis=1)
        new = []
        for g in range(SSD_G):
            decay = jnp.concatenate([jnp.broadcast_to(e_tot_col[h:h + 1, :], (SSD_P, 1))
                                     for h in range(g * hg, (g + 1) * hg)], axis=0)
            new.append(s_old[g] * decay + grow[g])
        return tuple(new)

    s_fin = lax.fori_loop(0, nsub, body, tuple(s_ref[g] for g in range(SSD_G)), unroll=2)
    for g in range(SSD_G):
        s_ref[g] = s_fin[g]


def _head_expand(heads, width, rows):
    h = jnp.arange(rows)[:, None]
    col = jnp.arange(heads * width)[None, :] // width
    return (h == col).astype(F32)


def ssd_scan(xbc, small, dtb, alog, ctx_len):
    bsz, lt, _ = xbc.shape
    tb = SCAN_BLOCK
    nb, cb = lt // tb, ctx_len // tb
    tmap = lambda z, n: _time_block(z, n, cb, nb)
    gw = SSD_G * SSD_N
    return pl.pallas_call(
        functools.partial(_ssd_kernel, tb=tb, c=SSD_CHUNK),
        out_shape=jax.ShapeDtypeStruct((2, bsz, lt, SSD_INNER), F32),
        grid=(2, bsz, nb),
        in_specs=[pl.BlockSpec((1, tb, SSD_INNER), lambda z, b, n: (b, tmap(z, n), 0)),
                  pl.BlockSpec((1, tb, gw), lambda z, b, n: (b, tmap(z, n), SSD_INNER // gw)),
                  pl.BlockSpec((1, tb, gw), lambda z, b, n: (b, tmap(z, n), SSD_INNER // gw + 1)),
                  pl.BlockSpec((1, tb, SMALL_W), lambda z, b, n: (b, tmap(z, n), z)),
                  pl.BlockSpec((1, 1, SMALL_W), lambda z, b, n: (z, 0, 0)),
                  pl.BlockSpec((1, 1, SMALL_W), lambda z, b, n: (z, 0, 0)),
                  pl.BlockSpec((SMALL_W, SSD_INNER), lambda z, b, n: (0, 0))],
        out_specs=pl.BlockSpec((1, 1, tb, SSD_INNER), lambda z, b, n: (z, b, tmap(z, n), 0)),
        scratch_shapes=[pltpu.VMEM((SSD_G, SSD_H // SSD_G * SSD_P, SSD_N), F32)],
        compiler_params=_cparams(("parallel", "parallel", "arbitrary")),
        name="ssd_scan",
    )(xbc, xbc, xbc, small, dtb, alog, _head_expand(SSD_H, SSD_P, SMALL_W))


ML_I_LANE = SSD_H
ML_F_LANE = SSD_H + ML_H


def _mlstm_kernel(q_ref, k_ref, v_ref, sm_ref, gb_ref, o_ref, cs_ref, ns_ref, m_ref, *, tb, c):
    z = pl.program_id(0)
    nsub = tb // c

    @pl.when(pl.program_id(2) == 0)
    def _():
        cs_ref[...] = jnp.zeros_like(cs_ref)
        ns_ref[...] = jnp.zeros_like(ns_ref)
        m_ref[...] = jnp.full(m_ref.shape, ML_M_INIT, F32)

    incl, _ = _dir_masks(z, c)
    incl_f = incl.astype(F32)
    lane = lax.broadcasted_iota(I32, (1, SMALL_W), 1)
    is_f = jnp.logical_and(lane >= ML_F_LANE, lane < ML_F_LANE + ML_H)

    def body(j, carry):
        cs_old, ns_old, m_old = carry
        jj = jnp.where(z == 0, j, nsub - 1 - j)
        rows = pl.ds(pl.multiple_of(jj * c, c), c)
        q = q_ref[0, rows, :]
        k = k_ref[0, rows, :] * ML_DK ** -0.5
        v = v_ref[0, rows, :]
        pre = sm_ref[0, rows, :] + gb_ref[0]
        gates = jnp.where(is_f, -_softplus(-pre), pre)
        f_cum = _dot(incl_f, gates, HI)
        f_tot = jnp.sum(gates, axis=0, keepdims=True)
        f_cum_t = f_cum.T
        gates_t = gates.T
        qs = [q[:, h * ML_DK:(h + 1) * ML_DK] for h in range(ML_H)]
        ks = [k[:, h * ML_DK:(h + 1) * ML_DK] for h in range(ML_H)]
        vs = [v[:, h * ML_DV:(h + 1) * ML_DV] for h in range(ML_H)]
        qk = [_dot_nt(a, b) for a, b in zip(qs, ks)]
        q_cs = [_dot(a, s) for a, s in zip(qs, cs_old)]
        outs, cs_new, ns_new, m_new_all = [], [], [], []
        for h in range(ML_H):
            li, lf = ML_I_LANE + h, ML_F_LANE + h
            f_col = f_cum[:, lf:lf + 1]
            f_last = f_tot[:, lf:lf + 1]
            logw_end = f_last - f_col + gates[:, li:li + 1]
            m_end = jnp.max(logw_end, axis=0, keepdims=True)
            kw = ks[h] * jnp.exp(logw_end - m_end)
            m_prev = m_old[h]
            logw = jnp.where(incl, f_col - f_cum_t[lf:lf + 1, :] + gates_t[li:li + 1, :], -jnp.inf)
            m_t = jnp.maximum(jnp.max(logw, axis=1, keepdims=True), f_col + m_prev)
            w_inter = jnp.exp(f_col + m_prev - m_t)
            scores = qk[h] * jnp.exp(logw - m_t)
            num = _dot(scores, vs[h]) + w_inter * q_cs[h]
            den = (jnp.sum(scores, axis=1, keepdims=True)
                   + w_inter * jnp.sum(qs[h] * ns_old[h], axis=1, keepdims=True))
            outs.append(num / jnp.maximum(jnp.abs(den), jnp.exp(-m_t)))
            m_new = jnp.maximum(f_last + m_prev, m_end)
            s_keep = jnp.exp(f_last + m_prev - m_new)
            s_loc = jnp.exp(m_end - m_new)
            cs_new.append(s_keep * cs_old[h] + s_loc * _dot_tn(kw, vs[h]))
            ns_new.append(s_keep * ns_old[h] + s_loc * jnp.sum(kw, axis=0, keepdims=True))
            m_new_all.append(m_new)
        o_ref[0, 0, rows, :] = jnp.concatenate(outs, axis=1)
        return tuple(cs_new), tuple(ns_new), tuple(m_new_all)

    heads = range(ML_H)
    init = (tuple(cs_ref[h] for h in heads), tuple(ns_ref[h] for h in heads), tuple(m_ref[h] for h in heads))
    cs_fin, ns_fin, m_fin = lax.fori_loop(0, nsub, body, init, unroll=2)
    for h in heads:
        cs_ref[h] = cs_fin[h]
        ns_ref[h] = ns_fin[h]
        m_ref[h] = m_fin[h]


def mlstm_scan(qk, pm, small, gate_bias, ctx_len):
    bsz, lt, _ = qk.shape
    tb = SCAN_BLOCK
    nb, cb = lt // tb, ctx_len // tb
    tmap = lambda z, n: _time_block(z, n, cb, nb)
    return pl.pallas_call(
        functools.partial(_mlstm_kernel, tb=tb, c=ML_CHUNK),
        out_shape=jax.ShapeDtypeStruct((2, bsz, lt, ML_VW), F32),
        grid=(2, bsz, nb),
        in_specs=[pl.BlockSpec((1, tb, ML_QK), lambda z, b, n: (b, tmap(z, n), 0)),
                  pl.BlockSpec((1, tb, ML_QK), lambda z, b, n: (b, tmap(z, n), 1)),
                  pl.BlockSpec((1, tb, ML_VW), lambda z, b, n: (b, tmap(z, n), 1)),
                  pl.BlockSpec((1, tb, SMALL_W), lambda z, b, n: (b, tmap(z, n), z)),
                  pl.BlockSpec((1, 1, SMALL_W), lambda z, b, n: (z, 0, 0))],
        out_specs=pl.BlockSpec((1, 1, tb, ML_VW), lambda z, b, n: (z, b, tmap(z, n), 0)),
        scratch_shapes=[pltpu.VMEM((ML_H, ML_DK, ML_DV), F32), pltpu.VMEM((ML_H, 1, ML_DK), F32),
                        pltpu.VMEM((ML_H, 1, 1), F32)],
        compiler_params=_cparams(("parallel", "parallel", "arbitrary")),
        name="mlstm_scan",
    )(qk, qk, pm, small, gate_bias)


def _odd_out_kernel(x_ref, mod_ref, ys_ref, xc_ref, zg_ref, ym_ref, og_ref, dsk_ref, nws_ref, nwm_ref, w_ref, o_ref):
    y = (ys_ref[0, 0] + ys_ref[1, 0] + dsk_ref[...] * xc_ref[0]) * _silu(zg_ref[0])
    ys = _group_rms(y, SSD_G, nws_ref[...])
    hm = ym_ref[0, 0] + ym_ref[1, 0]
    ym = _group_rms(hm, ML_H, nwm_ref[...]) * _sigmoid(og_ref[0])
    yy = jnp.concatenate([ys, ym], axis=1).astype(BF16)
    o_ref[0] = x_ref[0] + mod_ref[0, 0][2:3] * _dot(yy, w_ref[...])


def odd_out(x, modtab, y_s, xbc, ps, y_m, pm, d_skip, nw_s, nw_m, w_out, ctx_len):
    bsz, lt, d = x.shape
    tm = TOK_TILE
    ct = ctx_len // tm
    seq = lt - ctx_len
    tok = lambda width, col=0: pl.BlockSpec((1, tm, width), lambda b, i: (b, i + ct, col))
    two = lambda width: pl.BlockSpec((2, 1, tm, width), lambda b, i: (0, b, i + ct, 0))
    row = lambda width: pl.BlockSpec((1, width), lambda b, i: (0, 0))
    return pl.pallas_call(
        _odd_out_kernel,
        out_shape=jax.ShapeDtypeStruct((bsz, seq, d), F32),
        grid=(bsz, seq // tm),
        in_specs=[tok(d), pl.BlockSpec((1, 1, N_MOD, d), lambda b, i: (b, 1, 0, 0)),
                  two(SSD_INNER), tok(SSD_INNER), tok(SSD_INNER), two(ML_VW), tok(ML_VW, 2),
                  row(SSD_INNER), row(SSD_INNER), row(ML_VW),
                  pl.BlockSpec(w_out.shape, lambda b, i: (0, 0))],
        out_specs=pl.BlockSpec((1, tm, d), lambda b, i: (b, i, 0)),
        compiler_params=_cparams(("parallel", "parallel")),
        name="odd_out",
    )(x, modtab, y_s, xbc, ps, y_m, pm, d_skip.reshape(1, SSD_INNER), nw_s.reshape(1, SSD_INNER),
      nw_m.reshape(1, ML_VW), w_out)


def _router_kernel(x_ref, mod_ref, nw_ref, rw_ref, rb_ref, h_ref, idx_ref, prob_ref, rank_ref, cnt_ref, carry_ref,
                   *, tm):
    first = jnp.logical_and(pl.program_id(0) == 0, pl.program_id(1) == 0)

    @pl.when(first)
    def _():
        carry_ref[...] = jnp.zeros_like(carry_ref)

    h = _norm_mod(x_ref[0], nw_ref[...], mod_ref[0, 0], 3)
    h_ref[0] = h
    logits = _dot(h, rw_ref[...], HI) + rb_ref[...]
    lane = lax.broadcasted_iota(I32, (tm, N_EXP), 1).astype(F32)
    work = logits
    sel = jnp.zeros((tm, N_EXP), F32)
    vals, idxs, hots = [], [], []
    for _ in range(TOP_K):
        m = jnp.max(work, axis=1, keepdims=True)
        idx = jnp.min(jnp.where(work == m, lane, float(N_EXP)), axis=1, keepdims=True)
        hot = lane == idx
        vals.append(m)
        idxs.append(idx)
        hots.append(hot.astype(F32))
        sel = sel + hots[-1]
        work = jnp.where(hot, -jnp.inf, work)
    ex = [jnp.exp(v - vals[0]) for v in vals]
    tot = ex[0] + ex[1] + ex[2] + ex[3]
    ti = lax.broadcasted_iota(I32, (tm, tm), 0)
    si = lax.broadcasted_iota(I32, (tm, tm), 1)
    before = _dot((si < ti).astype(BF16), sel.astype(BF16)) + carry_ref[...]
    ranks = [jnp.sum(before * hot, axis=1, keepdims=True) for hot in hots]
    idx_ref[0] = jnp.concatenate(idxs, axis=1).astype(I32)
    prob_ref[0] = jnp.concatenate([e / tot for e in ex], axis=1)
    rank_ref[0] = jnp.concatenate(ranks, axis=1).astype(I32)
    carry_ref[...] = carry_ref[...] + jnp.sum(sel, axis=0, keepdims=True)
    cnt_ref[...] = carry_ref[...]


def router(x, modtab, nw, rw, rb, ctx_len):
    bsz, lt, d = x.shape
    tm = TOK_TILE
    ctx_tiles = ctx_len // tm
    tok = lambda width, dt: (jax.ShapeDtypeStruct((bsz, lt, width), dt),
                             pl.BlockSpec((1, tm, width), lambda b, i: (b, i, 0)))
    outs = [tok(d, F32), tok(TOP_K, I32), tok(TOP_K, F32), tok(TOP_K, I32),
            (jax.ShapeDtypeStruct((1, N_EXP), F32), pl.BlockSpec((1, N_EXP), lambda b, i: (0, 0)))]
    return pl.pallas_call(
        functools.partial(_router_kernel, tm=tm),
        out_shape=[o[0] for o in outs],
        grid=(bsz, lt // tm),
        in_specs=[pl.BlockSpec((1, tm, d), lambda b, i: (b, i, 0)),
                  pl.BlockSpec((1, 1, N_MOD, d), lambda b, i: (b, (i >= ctx_tiles).astype(I32), 0, 0)),
                  pl.BlockSpec((1, d), lambda b, i: (0, 0)),
                  pl.BlockSpec((d, N_EXP), lambda b, i: (0, 0)),
                  pl.BlockSpec((1, N_EXP), lambda b, i: (0, 0))],
        out_specs=[o[1] for o in outs],
        scratch_shapes=[pltpu.VMEM((1, N_EXP), F32)],
        compiler_params=_cparams(("arbitrary", "arbitrary")),
        name="router",
    )(x, modtab, nw.reshape(1, d), rw, rb.reshape(1, N_EXP))


def sc_gather_rows(table, idx):
    m = idx.shape[0]
    d = table.shape[1]
    workers = SC_CORES * SC_SUBCORES
    per_w, n_chunks = m // workers, m // workers // SC_GATHER_ROWS
    assert per_w * workers == m and n_chunks * SC_GATHER_ROWS == per_w and n_chunks % 2 == 0
    w = SC_GATHER_ROWS
    mesh = plsc.VectorSubcoreMesh(core_axis_name="c", subcore_axis_name="s")

    @functools.partial(
        pl.kernel, mesh=mesh, out_type=jax.ShapeDtypeStruct((m, d), table.dtype),
        scratch_types=[pltpu.VMEM((per_w,), I32), pltpu.VMEM((2, w, d), table.dtype),
                       pltpu.SemaphoreType.DMA((2,)), pltpu.SemaphoreType.DMA((2,))])
    def gather_kernel(table_hbm, idx_hbm, out_hbm, idx_v, rows_v, gsem, wsem):
        wid = lax.axis_index("s") * SC_CORES + lax.axis_index("c")
        base = pl.multiple_of(wid * per_w, 8)
        pltpu.sync_copy(idx_hbm.at[pl.ds(base, per_w)], idx_v)

        def gather(g, b):
            rows = idx_v.at[pl.ds(pl.multiple_of(g * w, 8), w)]
            return pltpu.make_async_copy(table_hbm.at[rows], rows_v.at[b], gsem.at[b])

        def write(g, b):
            dst = out_hbm.at[pl.ds(pl.multiple_of(base + g * w, 8), w)]
            return pltpu.make_async_copy(rows_v.at[b], dst, wsem.at[b])

        gather(0, 0).start()

        @pl.loop(0, n_chunks, step=2)
        def _(g):
            for b in (0, 1):
                cur = g + b
                gather(cur, b).wait()
                write(cur, b).start()

                @pl.when(cur >= 1)
                def _():
                    write(cur - 1, 1 - b).wait()

                @pl.when(cur + 1 < n_chunks)
                def _():
                    gather(cur + 1, 1 - b).start()

        write(n_chunks - 1, 1).wait()

    return gather_kernel(table, idx)


def _expert_kernel(be_ref, x_ref, wg_ref, bg_ref, wu_ref, bu_ref, wd_ref, bd_ref, o_ref, wg_bf, wu_bf, wd_bf):
    i = pl.program_id(0)

    @pl.when(jnp.logical_or(i == 0, be_ref[i] != be_ref[jnp.maximum(i - 1, 0)]))
    def _():
        wg_bf[...] = wg_ref[0, 0].astype(BF16)
        wu_bf[...] = wu_ref[0, 0].astype(BF16)
        wd_bf[...] = wd_ref[0, 0].astype(BF16)

    xb = x_ref[...].astype(BF16)
    g = _dot(xb, wg_bf[...]) + bg_ref[0, 0]
    u = _dot(xb, wu_bf[...]) + bu_ref[0, 0]
    g = jnp.minimum(g, SWIGLU_LIMIT)
    u = jnp.clip(u, -SWIGLU_LIMIT, SWIGLU_LIMIT)
    act = (g * _sigmoid(SWIGLU_ALPHA * g) * (u + 1.0)).astype(BF16)
    o_ref[...] = _dot(act, wd_bf[...]) + bd_ref[0, 0]


def expert_ffn(block_expert, x_sorted, layer, wg, bg, wu, bu, wd, bd):
    nblk = block_expert.shape[0]
    n_layers, _, d, f = wg.shape
    wspec = lambda r, c: pl.BlockSpec((1, 1, r, c), lambda i, be: (layer, be[i], 0, 0))
    grid_spec = pltpu.PrefetchScalarGridSpec(
        num_scalar_prefetch=1,
        grid=(nblk,),
        in_specs=[pl.BlockSpec((MOE_BLOCK, d), lambda i, be: (i, 0)),
                  wspec(d, f), wspec(1, f), wspec(d, f), wspec(1, f), wspec(f, d), wspec(1, d)],
        out_specs=pl.BlockSpec((MOE_BLOCK, d), lambda i, be: (i, 0)),
        scratch_shapes=[pltpu.VMEM((d, f), BF16), pltpu.VMEM((d, f), BF16), pltpu.VMEM((f, d), BF16)],
    )
    return pl.pallas_call(
        _expert_kernel,
        out_shape=jax.ShapeDtypeStruct((nblk * MOE_BLOCK, d), F32),
        grid_spec=grid_spec,
        compiler_params=_cparams(("arbitrary",)),
        name="expert_ffn",
    )(block_expert, x_sorted, wg, bg.reshape(n_layers, N_EXP, 1, f), wu, bu.reshape(n_layers, N_EXP, 1, f), wd,
      bd.reshape(n_layers, N_EXP, 1, d))


def _combine_kernel(x_ref, mod_ref, prob_ref, y_ref, fw_ref, o_ref, *, final_norm):
    p = prob_ref[0]
    acc = y_ref[0, 0] * p[:, 0:1]
    for k in range(1, TOP_K):
        acc = acc + y_ref[k, 0] * p[:, k:k + 1]
    out = x_ref[0] + mod_ref[0, 0][5:6] * acc
    if final_norm:
        out = out * lax.rsqrt(jnp.mean(out * out, axis=-1, keepdims=True) + EPS) * fw_ref[...]
    o_ref[0] = out


def combine(x, modtab, probs, y_tok, final_w, ctx_len, final_norm):
    bsz, lt, d = x.shape
    tm = TOK_TILE
    ctx_tiles = ctx_len // tm
    return pl.pallas_call(
        functools.partial(_combine_kernel, final_norm=final_norm),
        out_shape=jax.ShapeDtypeStruct((bsz, lt, d), F32),
        grid=(bsz, lt // tm),
        in_specs=[pl.BlockSpec((1, tm, d), lambda b, i: (b, i, 0)),
                  pl.BlockSpec((1, 1, N_MOD, d), lambda b, i: (b, (i >= ctx_tiles).astype(I32), 0, 0)),
                  pl.BlockSpec((1, tm, TOP_K), lambda b, i: (b, i, 0)),
                  pl.BlockSpec((TOP_K, 1, tm, d), lambda b, i: (0, b, i, 0)),
                  pl.BlockSpec((1, d), lambda b, i: (0, 0))],
        out_specs=pl.BlockSpec((1, tm, d), lambda b, i: (b, i, 0)),
        compiler_params=_cparams(("parallel", "parallel")),
        name="combine",
    )(x, modtab, probs, y_tok, final_w.reshape(1, d))


def moe_layer(x, modtab, layer, nw, rw, rb, wg, bg, wu, bu, wd, bd, final_w, ctx_len, final_norm):
    bsz, lt, d = x.shape
    t = bsz * lt
    h, idx, probs, rank, counts = router(x, modtab, nw, rw, rb, ctx_len)
    counts = counts.reshape(N_EXP).astype(I32)
    padded = (counts + MOE_BLOCK - 1) // MOE_BLOCK * MOE_BLOCK
    ends = jnp.cumsum(padded)
    pstart = ends - padded
    idx = idx.reshape(t, TOP_K)
    dest = pstart[idx] + rank.reshape(t, TOP_K)
    nblk = -(-t * TOP_K // MOE_BLOCK) + N_EXP
    block_row0 = jnp.arange(nblk, dtype=I32) * MOE_BLOCK
    block_expert = jnp.minimum(jnp.sum((ends[None, :] <= block_row0[:, None]).astype(I32), axis=1), N_EXP - 1)
    tok_ids = jnp.broadcast_to(jnp.arange(t, dtype=I32)[:, None], (t, TOP_K))
    filler = jnp.arange(nblk * MOE_BLOCK, dtype=I32) % t
    sorted_tok = filler.at[dest.reshape(-1)].set(tok_ids.reshape(-1), unique_indices=True, mode="promise_in_bounds")
    x_sorted = sc_gather_rows(h.reshape(t, d), sorted_tok)
    y_sorted = expert_ffn(block_expert, x_sorted, layer, wg, bg, wu, bu, wd, bd)
    y_tok = sc_gather_rows(y_sorted, dest.T.reshape(TOP_K * t)).reshape(TOP_K, bsz, lt, d)
    return combine(x, modtab, probs, y_tok, final_w, ctx_len, final_norm)


def even_layer(x, modtab, norm1_w, w_in, w_out, mu, w0, w2, a0, a2, g2, k_k, k_a, r_k, ln_w, ln_b, lb, hg_norm_w,
               ctx_len):
    w_in = w_in.astype(BF16)
    pr, ph = in_proj(x, modtab, norm1_w, [w_in[:, :RW_PROJ], w_in[:, RW_PROJ:]], ctx_len)
    r, v, kk, g, bv, lw, km, bb = rwkv_prep(pr, mu, w0, w2, a0, a2, g2, k_k, k_a, r_k, ctx_len)
    o_r = rwkv_scan(r, v, kk, lw, km, bb, ctx_len)
    o_g = gla_scan(ph, lb, ctx_len)
    return even_out(x, modtab, o_r, bv, g, o_g, ph, ln_w, ln_b, hg_norm_w, w_out.astype(BF16), ctx_len)


def _dir_slab(cols, width):
    halves = []
    for z in range(2):
        parts = [c[:, z * (c.shape[1] // 2):(z + 1) * (c.shape[1] // 2)] for c in cols]
        used = sum(p.shape[1] for p in parts)
        parts.append(jnp.zeros((cols[0].shape[0], width - used), cols[0].dtype))
        halves.append(jnp.concatenate(parts, axis=1))
    return jnp.concatenate(halves, axis=1)


def _dir_rows(vals, width):
    rows = jnp.concatenate(vals, axis=1)
    return jnp.pad(rows, ((0, 0), (0, width - rows.shape[1]))).reshape(2, 1, width)


def odd_layer(x, modtab, norm1_w, w_in, w_out, s_conv_w, s_conv_b, dt_bias, a_log, d_skip, s_norm_w, m_conv_w,
              m_conv_b, i_bias, f_bias, m_norm_w, ctx_len):
    o = 0
    cols = {}
    for name, width in (("zg", SSD_INNER), ("xbc", SSD_CONV_CH), ("dt", 2 * SSD_H), ("qk", 2 * ML_QK),
                        ("v", ML_VW), ("og", ML_VW), ("ig", 2 * ML_H), ("fg", 2 * ML_H)):
        cols[name] = w_in[:, o:o + width]
        o += width
    slab_m = jnp.concatenate([cols["qk"], cols["v"], cols["og"]], axis=1).astype(BF16)
    slab_small = _dir_slab([cols["dt"], cols["ig"], cols["fg"]], SMALL_W).astype(BF16)
    pxbc, zg, pm, small = in_proj(x, modtab, norm1_w, [cols["xbc"].astype(BF16), cols["zg"].astype(BF16), slab_m,
                                                       slab_small], ctx_len)
    xbc, qk = odd_prep(pxbc, pm, s_conv_w, s_conv_b, m_conv_w, m_conv_b, ctx_len)
    zeros_g = jnp.zeros((2, 2 * ML_H), F32)
    y_s = ssd_scan(xbc, small, _dir_rows([dt_bias, zeros_g], SMALL_W), _dir_rows([a_log, zeros_g], SMALL_W), ctx_len)
    y_m = mlstm_scan(qk, pm, small, _dir_rows([jnp.zeros((2, SSD_H), F32), i_bias, f_bias], SMALL_W), ctx_len)
    return odd_out(x, modtab, y_s, xbc, zg, y_m, pm, jnp.repeat(d_skip, SSD_P), s_norm_w, m_norm_w,
                   w_out.astype(BF16), ctx_len)


def _to_col_major(t):
    b, s, ch = t.shape
    return t.reshape(b, s // GRID_W, GRID_W, ch).transpose(0, 2, 1, 3).reshape(b, s, ch)


def _to_row_major(t):
    b, s, ch = t.shape
    return t.reshape(b, GRID_W, s // GRID_W, ch).transpose(0, 2, 1, 3).reshape(b, s, ch)


def kernel(x, c, ctx, c_ctx, ada_w, ada_b, norm1_w, norm2_w, even_w_in, even_w_out, rwkv_mu, rwkv_w0, rwkv_w2, rwkv_a0, rwkv_a2, rwkv_g2, rwkv_k_k, rwkv_k_a, rwkv_r_k, rwkv_ln_w, rwkv_ln_b, hgrn_lower_bounds, hgrn_norm_w, odd_w_in, odd_w_out, ssd_conv_w, ssd_conv_b, ssd_dt_bias, ssd_a_log, ssd_d, ssd_norm_w, mlstm_conv_w, mlstm_conv_b, mlstm_i_bias, mlstm_f_bias, mlstm_norm_w, router_w, router_b, exp_w_gate, exp_b_gate, exp_w_up, exp_b_up, exp_w_down, exp_b_down, final_norm_w):
    lc = ctx.shape[1]
    lower_bounds = jnp.cumsum(jax.nn.softmax(hgrn_lower_bounds.astype(F32), axis=0), axis=0)
    moe = lambda l: (l, norm2_w[l], router_w[l], router_b[l], exp_w_gate, exp_b_gate, exp_w_up, exp_b_up, exp_w_down,
                     exp_b_down, final_norm_w)
    xa = jnp.concatenate([ctx, x], axis=1)
    modtab = _mod_table(c, c_ctx, ada_w[0], ada_b[0])
    xa = even_layer(xa, modtab, norm1_w[0], even_w_in[0], even_w_out[0], rwkv_mu[0], rwkv_w0[0], rwkv_w2[0],
                    rwkv_a0[0], rwkv_a2[0], rwkv_g2[0], rwkv_k_k[0], rwkv_k_a[0], rwkv_r_k[0], rwkv_ln_w[0],
                    rwkv_ln_b[0], lower_bounds[0], hgrn_norm_w[0], lc)
    xa = moe_layer(xa, modtab, *moe(0), lc, False)
    xa = jnp.concatenate([xa[:, :lc], _to_col_major(xa[:, lc:])], axis=1)
    modtab = _mod_table(c, c_ctx, ada_w[1], ada_b[1])
    xl = odd_layer(xa, modtab, norm1_w[1], odd_w_in[0], odd_w_out[0], ssd_conv_w[0], ssd_conv_b[0], ssd_dt_bias[0],
                   ssd_a_log[0], ssd_d[0], ssd_norm_w[0], mlstm_conv_w[0], mlstm_conv_b[0], mlstm_i_bias[0],
                   mlstm_f_bias[0], mlstm_norm_w[0], lc)
    xl = moe_layer(xl, modtab, *moe(1), 0, True)
    return _to_row_major(xl)
```

```python
import functools
import math

import jax
import jax.numpy as jnp
from jax import lax
from jax.experimental import pallas as pl
from jax.experimental.pallas import tpu as pltpu
from jax.experimental.pallas import tpu_sc as plsc

F32 = jnp.float32
BF16 = jnp.bfloat16
I32 = jnp.int32
HI = lax.Precision.HIGHEST

D_MODEL = 1024
GRID_W = 64
N_MOD = 6
EPS = 1e-6
RW_H, RW_N = 8, 64
RW_W = RW_H * RW_N
RW_RANK = 64
RW_GATE_RANK = 128
RW_GN_EPS = 64e-5
RW_PROJ = 3 * RW_W + 4 * RW_RANK + RW_GATE_RANK
RW_CHUNK = 64
HG_H, HG_DK, HG_DV = 4, 128, 128
HG_W = HG_H * HG_DK
HG_CHUNK = 32
HG_PROJ = 5 * HG_W
SSD_H, SSD_P, SSD_G, SSD_N = 16, 64, 2, 128
SSD_INNER = SSD_H * SSD_P
SSD_CONV_CH = SSD_INNER + 2 * SSD_G * SSD_N
SSD_CHUNK = 64
ML_H, ML_DK, ML_DV = 4, 128, 256
ML_QK = ML_H * ML_DK
ML_VW = ML_H * ML_DV
ML_CHUNK = 64
ML_M_INIT = -1e30
CONV_K = 5
N_EXP = 32
TOP_K = 4
D_FF = 1024
SWIGLU_LIMIT = 7.0
SWIGLU_ALPHA = 1.702
MOE_BLOCK = 512

TOK_TILE = 256
SCAN_BLOCK = 256
HALO = 8
SMALL_W = 128
VMEM_LIMIT = 56 * 1024 * 1024
SC_CORES, SC_SUBCORES = 2, 16
SC_GATHER_ROWS = 32


def _cparams(sem):
    return pltpu.CompilerParams(dimension_semantics=sem, vmem_limit_bytes=VMEM_LIMIT)


def _dot(a, b, precision=None):
    return jnp.dot(a, b, preferred_element_type=F32, precision=precision)


def _dot_nt(a, b, precision=None):
    return lax.dot_general(a, b, (((1,), (1,)), ((), ())), preferred_element_type=F32, precision=precision)


def _dot_tn(a, b, precision=None):
    return lax.dot_general(a, b, (((0,), (0,)), ((), ())), preferred_element_type=F32, precision=precision)


def _bf16_parts(x):
    hi = x.astype(BF16)
    rest = x - hi.astype(F32)
    mid = rest.astype(BF16)
    return hi, mid, (rest - mid.astype(F32)).astype(BF16)


def _dot_into_01(x, sel):
    sel = sel.astype(BF16)
    p0, p1, p2 = _bf16_parts(x)
    return _dot(p0, sel) + _dot(p1, sel) + _dot(p2, sel)


def _dot_01_into(sel, x):
    sel = sel.astype(BF16)
    p0, p1, p2 = _bf16_parts(x)
    return _dot(sel, p0) + _dot(sel, p1) + _dot(sel, p2)


def _sigmoid(x):
    return 1.0 / (1.0 + jnp.exp(-x))


def _silu(x):
    return x * _sigmoid(x)


def _softplus(x):
    return jnp.maximum(x, 0.0) + jnp.log(1.0 + jnp.exp(-jnp.abs(x)))


def _dir_masks(z, c):
    ti = lax.broadcasted_iota(I32, (c, c), 0)
    si = lax.broadcasted_iota(I32, (c, c), 1)
    d = (si - ti) * jnp.where(z == 0, 1, -1)
    return d <= 0, d < 0


def _time_block(z, n, ctx_blocks, n_blocks):
    rev = jnp.where(n < ctx_blocks, ctx_blocks - 1 - n, n_blocks - 1 - (n - ctx_blocks))
    return jnp.where(z == 0, n, rev)


def _mod_kernel(c_ref, w_ref, b_ref, o_ref):
    o_ref[...] = _dot(_silu(c_ref[...]), w_ref[...]) + b_ref[...]


def ada_mod(cc, w, b):
    rows, d = cc.shape
    n = w.shape[1]
    tn = 1536
    return pl.pallas_call(
        _mod_kernel,
        out_shape=jax.ShapeDtypeStruct((rows, n), F32),
        grid=(n // tn,),
        in_specs=[pl.BlockSpec((rows, d), lambda j: (0, 0)),
                  pl.BlockSpec((d, tn), lambda j: (0, j)),
                  pl.BlockSpec((1, tn), lambda j: (0, j))],
        out_specs=pl.BlockSpec((rows, tn), lambda j: (0, j)),
        compiler_params=_cparams(("arbitrary",)),
        name="ada_mod",
    )(cc, w, b.reshape(1, n))


def _mod_table(c, c_ctx, w, b):
    bsz = c.shape[0]
    rows = -(-(bsz + 1) // 8) * 8
    cc = jnp.zeros((rows, D_MODEL), F32).at[:bsz].set(c).at[bsz].set(c_ctx)
    m = ada_mod(cc, w, b)
    mod = m[:bsz].reshape(bsz, 1, N_MOD, D_MODEL)
    mod_c = jnp.broadcast_to(m[bsz].reshape(1, 1, N_MOD, D_MODEL), (bsz, 1, N_MOD, D_MODEL))
    return jnp.concatenate([mod_c, mod], axis=1)


def _norm_mod(x, nw, m, shift_idx):
    y = x * lax.rsqrt(jnp.mean(x * x, axis=-1, keepdims=True) + EPS) * nw
    return y * (1.0 + m[shift_idx + 1:shift_idx + 2]) + m[shift_idx:shift_idx + 1]


def _in_proj_kernel(x_ref, mod_ref, nw_ref, *rest, n_out):
    w_refs, o_refs = rest[:n_out], rest[n_out:]
    h = _norm_mod(x_ref[0], nw_ref[...], mod_ref[0, 0], 0).astype(BF16)
    for w_ref, o_ref in zip(w_refs, o_refs):
        o_ref[0] = _dot(h, w_ref[...])


def in_proj(x, modtab, nw, slabs, ctx_len):
    bsz, lt, d = x.shape
    tm = TOK_TILE
    ctx_tiles = ctx_len // tm
    in_specs = [pl.BlockSpec((1, tm, d), lambda b, i: (b, i, 0)),
                pl.BlockSpec((1, 1, N_MOD, d), lambda b, i: (b, (i >= ctx_tiles).astype(I32), 0, 0)),
                pl.BlockSpec((1, d), lambda b, i: (0, 0))]
    in_specs += [pl.BlockSpec(w.shape, lambda b, i: (0, 0)) for w in slabs]
    return pl.pallas_call(
        functools.partial(_in_proj_kernel, n_out=len(slabs)),
        out_shape=[jax.ShapeDtypeStruct((bsz, lt, w.shape[1]), F32) for w in slabs],
        grid=(bsz, lt // tm),
        in_specs=in_specs,
        out_specs=[pl.BlockSpec((1, tm, w.shape[1]), lambda b, i: (b, i, 0)) for w in slabs],
        compiler_params=_cparams(("parallel", "parallel")),
        name="in_proj",
    )(x, modtab, nw.reshape(1, d), *slabs)


def _halo_specs(tm, width, lt):
    per = tm // HALO
    last = lt // HALO - 1
    return [pl.BlockSpec((1, tm, width), lambda b, i: (b, i, 0)),
            pl.BlockSpec((1, HALO, width), lambda b, i: (b, jnp.maximum(i * per - 1, 0), 0)),
            pl.BlockSpec((1, HALO, width), lambda b, i: (b, jnp.minimum((i + 1) * per, last), 0))]


def _segment_edges(i, tm, ctx_len, lt):
    row0 = i * tm
    has_prev = jnp.logical_and(row0 != 0, row0 != ctx_len)
    has_next = jnp.logical_and(row0 + tm != ctx_len, row0 + tm != lt)
    return has_prev, has_next


def _rwkv_prep_kernel(p_ref, pp_ref, pn_ref, mu_ref, w0_ref, w2_ref, a0_ref, a2_ref, g2_ref, kk_ref, ka_ref,
                      rk_ref, bd_ref, r_o, v_o, kk_o, g_o, bv_o, lw_o, km_o, bb_o, *, tm, ctx_len, lt):
    has_prev, has_next = _segment_edges(pl.program_id(1), tm, ctx_len, lt)
    p = p_ref[0]
    prev_row = jnp.where(has_prev, pp_ref[0, HALO - 1:HALO, :], 0.0)
    next_row = jnp.where(has_next, pn_ref[0, 0:1, :], 0.0)
    rid = lax.broadcasted_iota(I32, (tm, 1), 0)
    up = jnp.where(rid == 0, prev_row, pltpu.roll(p, 1, 0))
    dn = jnp.where(rid == tm - 1, next_row, pltpu.roll(p, tm - 1, 0))
    p = p + (0.5 * (up + dn) - p) * mu_ref[...]
    w = RW_W
    r, k, v = p[:, 0:w], p[:, w:2 * w], p[:, 2 * w:3 * w]
    o = 3 * w
    wd = jnp.tanh(p[:, o:o + 2 * RW_RANK])
    ad = p[:, o + 2 * RW_RANK:o + 4 * RW_RANK]
    gd = p[:, o + 4 * RW_RANK:o + 4 * RW_RANK + RW_GATE_RANK]
    bd = bd_ref[...]
    kk = k * kk_ref[...]
    kk = kk * lax.rsqrt(_dot_into_01(kk * kk, bd) + 1e-12)
    r_o[0] = r
    v_o[0] = v
    kk_o[0] = kk
    g_o[0] = _dot(_sigmoid(gd), g2_ref[...])
    km_sum = jnp.zeros_like(k)
    for z in range(2):
        sl = slice(z * RW_RANK, (z + 1) * RW_RANK)
        w_pre = w0_ref[z:z + 1, :] + _dot(wd[:, sl], w2_ref[z])
        lw_o[z, 0] = -_sigmoid(w_pre) * math.exp(-0.5)
        a = _sigmoid(a0_ref[z:z + 1, :] + _dot(ad[:, sl], a2_ref[z]))
        km = k * (1.0 + (a - 1.0) * ka_ref[...])
        km_o[z, 0] = km
        bb_o[z, 0] = kk * a
        km_sum = km_sum + km
    bv_o[0] = _dot_into_01(r * km_sum * rk_ref[...], bd) * v


def _head_block_ones(width, head):
    i = jnp.arange(width) // head
    return (i[:, None] == i[None, :]).astype(F32)


def rwkv_prep(p, mu, w0, w2, a0, a2, g2, k_k, k_a, r_k, ctx_len):
    bsz, lt, width = p.shape
    tm = TOK_TILE
    w = RW_W
    full = lambda a: pl.BlockSpec(a.shape, lambda b, i: (0,) * a.ndim)
    params = [mu.reshape(1, width), w0, w2, a0, a2, g2, k_k.reshape(1, w), k_a.reshape(1, w), r_k.reshape(1, w),
              _head_block_ones(w, RW_N)]
    one = jax.ShapeDtypeStruct((bsz, lt, w), F32)
    two = jax.ShapeDtypeStruct((2, bsz, lt, w), F32)
    one_spec = pl.BlockSpec((1, tm, w), lambda b, i: (b, i, 0))
    two_spec = pl.BlockSpec((2, 1, tm, w), lambda b, i: (0, b, i, 0))
    return pl.pallas_call(
        functools.partial(_rwkv_prep_kernel, tm=tm, ctx_len=ctx_len, lt=lt),
        out_shape=[one] * 5 + [two] * 3,
        grid=(bsz, lt // tm),
        in_specs=_halo_specs(tm, width, lt) + [full(a) for a in params],
        out_specs=[one_spec] * 5 + [two_spec] * 3,
        compiler_params=_cparams(("parallel", "parallel")),
        name="rwkv_prep",
    )(p, p, p, *params)


def _unit_lower_inverses(n_mats, eye, c):
    xs = [eye - n for n in n_mats]
    ps = [_dot(n, n) for n in n_mats]
    span = 4
    while True:
        xs = [x + _dot(x, p) for x, p in zip(xs, ps)]
        if span >= c:
            return xs
        ps = [_dot(p, p) for p in ps]
        span *= 2


def _rwkv_scan_kernel(r_ref, v_ref, kk_ref, lw_ref, km_ref, bb_ref, o_ref, s_ref, *, tb, c):
    z = pl.program_id(0)
    nsub = tb // c

    @pl.when(pl.program_id(2) == 0)
    def _():
        s_ref[...] = jnp.zeros_like(s_ref)

    incl, strict = _dir_masks(z, c)
    incl_f, strict_f = incl.astype(F32), strict.astype(F32)
    eye = incl_f - strict_f

    def body(j, s_old):
        jj = jnp.where(z == 0, j, nsub - 1 - j)
        rows = pl.ds(pl.multiple_of(jj * c, c), c)
        r, v, kk = r_ref[0, rows, :], v_ref[0, rows, :], kk_ref[0, rows, :]
        lw, km, bb = lw_ref[0, 0, rows, :], km_ref[0, 0, rows, :], bb_ref[0, 0, rows, :]
        cum = _dot_01_into(incl_f, lw)
        tot = jnp.sum(lw, axis=0, keepdims=True)
        e_neg = jnp.exp(-cum)
        e_end = jnp.exp(tot - cum)
        a_t = kk * jnp.exp(cum - lw)
        r_t = r * jnp.exp(cum)
        b_t, k_t = bb * e_neg, km * e_neg
        b_p, k_p = bb * e_end, km * e_end
        e_tot = jnp.exp(tot)
        heads = [slice(h * RW_N, (h + 1) * RW_N) for h in range(RW_H)]
        ar =[jnp.concatenate([a_t[:, hs], r_t[:, hs]], axis=0) for hs in heads]
        bk = [jnp.concatenate([b_t[:, hs], k_t[:, hs]], axis=0) for hs in heads]
        bkp = [jnp.concatenate([b_p[:, hs], k_p[:, hs]], axis=0) for hs in heads]
        vh = [v[:, hs] for hs in heads]
        sc = [_dot_nt(a, b) for a, b in zip(ar, bk)]
        ars = [_dot_nt(a, s) for a, s in zip(ar, s_old)]
        m_ak_v = [_dot(x[:c, c:] * strict_f, vv) for x, vv in zip(sc, vh)]
        inv = _unit_lower_inverses([x[:c, :c] * strict_f for x in sc], eye, c)
        u = [-_dot(i, a[:c] + m) for i, a, m in zip(inv, ars, m_ak_v)]
        uv = [jnp.concatenate([uu, vv], axis=0) for uu, vv in zip(u, vh)]
        n_rbk = [jnp.concatenate([x[c:, :c] * incl_f, x[c:, c:] * incl_f], axis=1) for x in sc]
        outs = [a[c:] + _dot(n, w) for a, n, w in zip(ars, n_rbk, uv)]
        o_ref[0, 0, rows, :] = jnp.concatenate(outs, axis=1)
        return tuple(s * e_tot[:, hs] + _dot_tn(w, b) for s, hs, w, b in zip(s_old, heads, uv, bkp))

    s_fin = lax.fori_loop(0, nsub, body, tuple(s_ref[h] for h in range(RW_H)), unroll=2)
    for h in range(RW_H):
        s_ref[h] = s_fin[h]


def rwkv_scan(r, v, kk, lw, km, bb, ctx_len):
    bsz, lt, w = r.shape
    tb = SCAN_BLOCK
    nb, cb = lt // tb, ctx_len // tb
    tmap = lambda z, b, n: _time_block(z, n, cb, nb)
    one = pl.BlockSpec((1, tb, w), lambda z, b, n: (b, tmap(z, b, n), 0))
    two = pl.BlockSpec((1, 1, tb, w), lambda z, b, n: (z, b, tmap(z, b, n), 0))
    return pl.pallas_call(
        functools.partial(_rwkv_scan_kernel, tb=tb, c=RW_CHUNK),
        out_shape=jax.ShapeDtypeStruct((2, bsz, lt, w), F32),
        grid=(2, bsz, nb),
        in_specs=[one, one, one, two, two, two],
        out_specs=two,
        scratch_shapes=[pltpu.VMEM((RW_H, RW_N, RW_N), F32)],
        compiler_params=_cparams(("parallel", "parallel", "arbitrary")),
        name="rwkv_scan",
    )(r, v, kk, lw, km, bb)


def _gla_kernel(q_ref, f_ref, i_ref, lb_ref, o_ref, s_ref, *, tb, c):
    z = pl.program_id(0)
    nsub = tb // c

    @pl.when(pl.program_id(2) == 0)
    def _():
        s_ref[...] = jnp.zeros_like(s_ref)

    incl, _ = _dir_masks(z, c)
    incl_f = incl.astype(F32)
    lb = lb_ref[...]

    def body(j, s_old):
        jj = jnp.where(z == 0, j, nsub - 1 - j)
        rows = pl.ds(pl.multiple_of(jj * c, c), c)
        q = _silu(q_ref[0, rows, :]) * HG_DK ** -0.5
        f = f_ref[0, rows, :]
        v = i_ref[0, rows, :]
        log_f = jnp.log(lb + (1.0 - lb) * _sigmoid(f))
        k = (1.0 - lb) * _sigmoid(-f)
        cum = _dot_01_into(incl_f, log_f)
        tot = jnp.sum(log_f, axis=0, keepdims=True)
        q_in = q * jnp.exp(cum)
        k_in = k * jnp.exp(-cum)
        k_end = k * jnp.exp(tot - cum)
        e_tot = jnp.exp(tot)
        heads = [slice(h * HG_DK, (h + 1) * HG_DK) for h in range(HG_H)]
        a = [_dot_nt(q_in[:, hs], k_in[:, hs]) * incl_f for hs in heads]
        inter = [_dot_nt(q_in[:, hs], s) for hs, s in zip(heads, s_old)]
        grow = [_dot_tn(v[:, hs], k_end[:, hs]) for hs in heads]
        outs = [_dot(aa, v[:, hs]) + ii for aa, hs, ii in zip(a, heads, inter)]
        o_ref[0, 0, rows, :] = jnp.concatenate(outs, axis=1)
        return tuple(s * e_tot[:, hs] + g for s, hs, g in zip(s_old, heads, grow))

    s_fin = lax.fori_loop(0, nsub, body, tuple(s_ref[h] for h in range(HG_H)), unroll=4)
    for h in range(HG_H):
        s_ref[h] = s_fin[h]


def gla_scan(ph, lb, ctx_len):
    bsz, lt, _ = ph.shape
    w = HG_W
    tb = SCAN_BLOCK
    nb, cb = lt // tb, ctx_len // tb
    tmap = lambda z, n: _time_block(z, n, cb, nb)
    return pl.pallas_call(
        functools.partial(_gla_kernel, tb=tb, c=HG_CHUNK),
        out_shape=jax.ShapeDtypeStruct((2, bsz, lt, w), F32),
        grid=(2, bsz, nb),
        in_specs=[pl.BlockSpec((1, tb, w), lambda z, b, n: (b, tmap(z, n), 0)),
                  pl.BlockSpec((1, tb, w), lambda z, b, n: (b, tmap(z, n), 1 + z)),
                  pl.BlockSpec((1, tb, w), lambda z, b, n: (b, tmap(z, n), 3)),
                  pl.BlockSpec((1, w), lambda z, b, n: (0, 0))],
        out_specs=pl.BlockSpec((1, 1, tb, w), lambda z, b, n: (z, b, tmap(z, n), 0)),
        scratch_shapes=[pltpu.VMEM((HG_H, HG_DV, HG_DK), F32)],
        compiler_params=_cparams(("parallel", "parallel", "arbitrary")),
        name="gla_scan",
    )(ph, ph, ph, lb.reshape(1, w))


def _group_rms(y, groups, gain):
    width = y.shape[-1] // groups
    parts = []
    for g in range(groups):
        yg = y[:, g * width:(g + 1) * width]
        parts.append(yg * lax.rsqrt(jnp.mean(yg * yg, axis=-1, keepdims=True) + EPS))
    return jnp.concatenate(parts, axis=1) * gain


def _even_out_kernel(x_ref, mod_ref, or_ref, bv_ref, g_ref, og_ref, gate_ref, lnw_ref, lnb_ref, nw_ref, bd_ref,
                     w_ref, o_ref):
    bd = bd_ref[...]
    o = or_ref[0, 0] + or_ref[1, 0]
    mu = _dot_into_01(o, bd) * (1.0 / RW_N)
    oc = o - mu
    var = _dot_into_01(oc * oc, bd) * (1.0 / RW_N)
    yr = (oc * lax.rsqrt(var + RW_GN_EPS) * lnw_ref[...] + lnb_ref[...] + bv_ref[0]) * g_ref[0]
    og = og_ref[0, 0] + og_ref[1, 0]
    yh = _group_rms(og, HG_H, nw_ref[...]) * _silu(gate_ref[0])
    y = jnp.concatenate([yr, yh], axis=1).astype(BF16)
    o_ref[0] = x_ref[0] + mod_ref[0, 0][2:3] * _dot(y, w_ref[...])


def even_out(x, modtab, o_r, bv, g, o_g, ph, ln_w, ln_b, norm_w, w_out, ctx_len):
    bsz, lt, d = x.shape
    tm = TOK_TILE
    ctx_tiles = ctx_len // tm
    w = RW_W
    tok = lambda width, col=0: pl.BlockSpec((1, tm, width), lambda b, i: (b, i, col))
    two = pl.BlockSpec((2, 1, tm, w), lambda b, i: (0, b, i, 0))
    row = lambda width: pl.BlockSpec((1, width), lambda b, i: (0, 0))
    return pl.pallas_call(
        _even_out_kernel,
        out_shape=jax.ShapeDtypeStruct((bsz, lt, d), F32),
        grid=(bsz, lt // tm),
        in_specs=[tok(d),
                  pl.BlockSpec((1, 1, N_MOD, d), lambda b, i: (b, (i >= ctx_tiles).astype(I32), 0, 0)),
                  two, tok(w), tok(w), two, tok(HG_W, 4), row(w), row(w), row(HG_W),
                  pl.BlockSpec((w, w), lambda b, i: (0, 0)),
                  pl.BlockSpec(w_out.shape, lambda b, i: (0, 0))],
        out_specs=tok(d),
        compiler_params=_cparams(("parallel", "parallel")),
        name="even_out",
    )(x, modtab, o_r, bv, g, o_g, ph, ln_w.reshape(1, w), ln_b.reshape(1, w), norm_w.reshape(1, HG_W),
      _head_block_ones(w, RW_N), w_out)


def _conv_silu(x_ref, xp_ref, xn_ref, w_ref, b_ref, has_prev, has_next, tm):
    prev = jnp.where(has_prev, xp_ref[0], 0.0)
    nxt = jnp.where(has_next, xn_ref[0], 0.0)
    ext = jnp.concatenate([prev, x_ref[0], nxt], axis=0)
    rows = tm + 2 * HALO
    pad = CONV_K // 2
    acc = b_ref[...] + jnp.zeros((tm, ext.shape[1]), F32)
    for j in range(CONV_K):
        d = j - pad
        sh = ext if d == 0 else pltpu.roll(ext, (-d) % rows, 0)
        acc = acc + sh[HALO:HALO + tm] * w_ref[j:j + 1, :]
    return _silu(acc)


def _odd_prep_kernel(xs_ref, xsp_ref, xsn_ref, xm_ref, xmp_ref, xmn_ref, ws_ref, bs_ref, wm_ref, bm_ref,
                     os_ref, om_ref, *, tm, ctx_len, lt):
    has_prev, has_next = _segment_edges(pl.program_id(1), tm, ctx_len, lt)
    os_ref[0] = _conv_silu(xs_ref, xsp_ref, xsn_ref, ws_ref, bs_ref, has_prev, has_next, tm)
    om_ref[0] = _conv_silu(xm_ref, xmp_ref, xmn_ref, wm_ref, bm_ref, has_prev, has_next, tm)


def odd_prep(xbc, qk, ws, bs, wm, bm, ctx_len):
    bsz, lt, cs = xbc.shape
    cm = wm.shape[1]
    tm = TOK_TILE
    full = lambda a: pl.BlockSpec(a.shape, lambda b, i: (0,) * a.ndim)
    params = [ws, bs.reshape(1, cs), wm, bm.reshape(1, cm)]
    return pl.pallas_call(
        functools.partial(_odd_prep_kernel, tm=tm, ctx_len=ctx_len, lt=lt),
        out_shape=[jax.ShapeDtypeStruct((bsz, lt, cs), F32), jax.ShapeDtypeStruct((bsz, lt, cm), F32)],
        grid=(bsz, lt // tm),
        in_specs=_halo_specs(tm, cs, lt) + _halo_specs(tm, cm, lt) + [full(a) for a in params],
        out_specs=[pl.BlockSpec((1, tm, cs), lambda b, i: (b, i, 0)),
                   pl.BlockSpec((1, tm, cm), lambda b, i: (b, i, 0))],
        compiler_params=_cparams(("parallel", "parallel")),
        name="odd_prep",
    )(xbc, xbc, xbc, qk, qk, qk, *params)


def _ssd_kernel(x_ref, b_ref, c_ref, sm_ref, dtb_ref, alog_ref, ex_ref, o_ref, s_ref, *, tb, c):
    z = pl.program_id(0)
    nsub = tb // c
    hg = SSD_H // SSD_G

    @pl.when(pl.program_id(2) == 0)
    def _():
        s_ref[...] = jnp.zeros_like(s_ref)

    incl, _ = _dir_masks(z, c)
    incl_f = incl.astype(F32)
    ex = ex_ref[...]

    gw = hg * SSD_P

    def body(j, s_old):
        jj = jnp.where(z == 0, j, nsub - 1 - j)
        rows = pl.ds(pl.multiple_of(jj * c, c), c)
        x = x_ref[0, rows, :]
        bm = b_ref[0, rows, :]
        cm = c_ref[0, rows, :]
        dt = _softplus(sm_ref[0, rows, :] + dtb_ref[0])
        la = -dt * jnp.exp(alog_ref[0])
        a_cum = _dot_01_into(incl_f, la)
        a_tot = jnp.sum(la, axis=0, keepdims=True)
        a_cum_t = a_cum.T
        e_tot_col = jnp.exp(jnp.where(z == 0, a_cum_t[:, c - 1:c], a_cum_t[:, 0:1]))
        wide = _dot_into_01(jnp.concatenate([dt, jnp.exp(a_cum), jnp.exp(a_tot - a_cum)], axis=0), ex)
        xdt = x * wide[:c]
        din, dend = wide[c:2 * c], wide[2 * c:]
        xd_end = xdt * dend
        bgs = [bm[:, g * SSD_N:(g + 1) * SSD_N] for g in range(SSD_G)]
        cgs = [cm[:, g * SSD_N:(g + 1) * SSD_N] for g in range(SSD_G)]
        cb = [_dot_nt(cg, bg) for cg, bg in zip(cgs, bgs)]
        inter = [_dot_nt(cg, s) for cg, s in zip(cgs, s_old)]
        grow = [_dot_tn(xd_end[:, g * gw:(g + 1) * gw], bg) for g, bg in enumerate(bgs)]
        intra = []
        for h in range(SSD_H):
            seg = a_cum[:, h:h + 1] - a_cum_t[h:h + 1, :]
            wmat = cb[h // hg] * jnp.where(incl, jnp.exp(seg), 0.0)
            intra.append(_dot(wmat, xdt[:, h * SSD_P:(h + 1) * SSD_P]))
        o_ref[0, 0, rows, :] = jnp.concatenate(intra, axis=1) + din * jnp.concatenate(inter, axis=1)
        new = []
        for g in range(SSD_G):
            decay = jnp.concatenate([jnp.broadcast_to(e_tot_col[h:h + 1, :], (SSD_P, 1))
                                     for h in range(g * hg, (g + 1) * hg)], axis=0)
            new.append(s_old[g] * decay + grow[g])
        return tuple(new)

    s_fin = lax.fori_loop(0, nsub, body, tuple(s_ref[g] for g in range(SSD_G)), unroll=2)
    for g in range(SSD_G):
        s_ref[g] = s_fin[g]


def _head_expand(heads, width, rows):
    h = jnp.arange(rows)[:, None]
    col = jnp.arange(heads * width)[None, :] // width
    return (h == col).astype(F32)


def ssd_scan(xbc, small, dtb, alog, ctx_len):
    bsz, lt, _ = xbc.shape
    tb = SCAN_BLOCK
    nb, cb = lt // tb, ctx_len // tb
    tmap = lambda z, n: _time_block(z, n, cb, nb)
    gw = SSD_G * SSD_N
    return pl.pallas_call(
        functools.partial(_ssd_kernel, tb=tb, c=SSD_CHUNK),
        out_shape=jax.ShapeDtypeStruct((2, bsz, lt, SSD_INNER), F32),
        grid=(2, bsz, nb),
        in_specs=[pl.BlockSpec((1, tb, SSD_INNER), lambda z, b, n: (b, tmap(z, n), 0)),
                  pl.BlockSpec((1, tb, gw), lambda z, b, n: (b, tmap(z, n), SSD_INNER // gw)),
                  pl.BlockSpec((1, tb, gw), lambda z, b, n: (b, tmap(z, n), SSD_INNER // gw + 1)),
                  pl.BlockSpec((1, tb, SMALL_W), lambda z, b, n: (b, tmap(z, n), z)),
                  pl.BlockSpec((1, 1, SMALL_W), lambda z, b, n: (z, 0, 0)),
                  pl.BlockSpec((1, 1, SMALL_W), lambda z, b, n: (z, 0, 0)),
                  pl.BlockSpec((SMALL_W, SSD_INNER), lambda z, b, n: (0, 0))],
        out_specs=pl.BlockSpec((1, 1, tb, SSD_INNER), lambda z, b, n: (z, b, tmap(z, n), 0)),
        scratch_shapes=[pltpu.VMEM((SSD_G, SSD_H // SSD_G * SSD_P, SSD_N), F32)],
        compiler_params=_cparams(("parallel", "parallel", "arbitrary")),
        name="ssd_scan",
    )(xbc, xbc, xbc, small, dtb, alog, _head_expand(SSD_H, SSD_P, SMALL_W))


ML_I_LANE = SSD_H
ML_F_LANE = SSD_H + ML_H


def _mlstm_kernel(q_ref, k_ref, v_ref, sm_ref, gb_ref, o_ref, cs_ref, ns_ref, m_ref, *, tb, c):
    z = pl.program_id(0)
    nsub = tb // c

    @pl.when(pl.program_id(2) == 0)
    def _():
        cs_ref[...] = jnp.zeros_like(cs_ref)
        ns_ref[...] = jnp.zeros_like(ns_ref)
        m_ref[...] = jnp.full(m_ref.shape, ML_M_INIT, F32)

    incl, _ = _dir_masks(z, c)
    incl_f = incl.astype(F32)
    lane = lax.broadcasted_iota(I32, (1, SMALL_W), 1)
    is_f = jnp.logical_and(lane >= ML_F_LANE, lane < ML_F_LANE + ML_H)

    def body(j, carry):
        cs_old, ns_old, m_old = carry
        jj = jnp.where(z == 0, j, nsub - 1 - j)
        rows = pl.ds(pl.multiple_of(jj * c, c), c)
        q = q_ref[0, rows, :]
        k = k_ref[0, rows, :] * ML_DK ** -0.5
        v = v_ref[0, rows, :]
        pre = sm_ref[0, rows, :] + gb_ref[0]
        gates = jnp.where(is_f, -_softplus(-pre), pre)
        f_cum = _dot(incl_f, gates, HI)
        f_tot = jnp.sum(gates, axis=0, keepdims=True)
        f_cum_t = f_cum.T
        gates_t = gates.T
        qs = [q[:, h * ML_DK:(h + 1) * ML_DK] for h in range(ML_H)]
        ks = [k[:, h * ML_DK:(h + 1) * ML_DK] for h in range(ML_H)]
        vs = [v[:, h * ML_DV:(h + 1) * ML_DV] for h in range(ML_H)]
        qk = [_dot_nt(a, b) for a, b in zip(qs, ks)]
        q_cs = [_dot(a, s) for a, s in zip(qs, cs_old)]
        outs, cs_new, ns_new, m_new_all = [], [], [], []
        for h in range(ML_H):
            li, lf = ML_I_LANE + h, ML_F_LANE + h
            f_col = f_cum[:, lf:lf + 1]
            f_last = f_tot[:, lf:lf + 1]
            logw_end = f_last - f_col + gates[:, li:li + 1]
            m_end = jnp.max(logw_end, axis=0, keepdims=True)
            kw = ks[h] * jnp.exp(logw_end - m_end)
            m_prev = m_old[h]
            logw = jnp.where(incl, f_col - f_cum_t[lf:lf + 1, :] + gates_t[li:li + 1, :], -jnp.inf)
            m_t = jnp.maximum(jnp.max(logw, axis=1, keepdims=True), f_col + m_prev)
            w_inter = jnp.exp(f_col + m_prev - m_t)
            scores = qk[h] * jnp.exp(logw - m_t)
            num = _dot(scores, vs[h]) + w_inter * q_cs[h]
            den = (jnp.sum(scores, axis=1, keepdims=True)
                   + w_inter * jnp.sum(qs[h] * ns_old[h], axis=1, keepdims=True))
            outs.append(num / jnp.maximum(jnp.abs(den), jnp.exp(-m_t)))
            m_new = jnp.maximum(f_last + m_prev, m_end)
            s_keep = jnp.exp(f_last + m_prev - m_new)
            s_loc = jnp.exp(m_end - m_new)
            cs_new.append(s_keep * cs_old[h] + s_loc * _dot_tn(kw, vs[h]))
            ns_new.append(s_keep * ns_old[h] + s_loc * jnp.sum(kw, axis=0, keepdims=True))
            m_new_all.append(m_new)
        o_ref[0, 0, rows, :] = jnp.concatenate(outs, axis=1)
        return tuple(cs_new), tuple(ns_new), tuple(m_new_all)

    heads = range(ML_H)
    init = (tuple(cs_ref[h] for h in heads), tuple(ns_ref[h] for h in heads), tuple(m_ref[h] for h in heads))
    cs_fin, ns_fin, m_fin = lax.fori_loop(0, nsub, body, init, unroll=2)
    for h in heads:
        cs_ref[h] = cs_fin[h]
        ns_ref[h] = ns_fin[h]
        m_ref[h] = m_fin[h]


def mlstm_scan(qk, pm, small, gate_bias, ctx_len):
    bsz, lt, _ = qk.shape
    tb = SCAN_BLOCK
    nb, cb = lt // tb, ctx_len // tb
    tmap = lambda z, n: _time_block(z, n, cb, nb)
    return pl.pallas_call(
        functools.partial(_mlstm_kernel, tb=tb, c=ML_CHUNK),
        out_shape=jax.ShapeDtypeStruct((2, bsz, lt, ML_VW), F32),
        grid=(2, bsz, nb),
        in_specs=[pl.BlockSpec((1, tb, ML_QK), lambda z, b, n: (b, tmap(z, n), 0)),
                  pl.BlockSpec((1, tb, ML_QK), lambda z, b, n: (b, tmap(z, n), 1)),
                  pl.BlockSpec((1, tb, ML_VW), lambda z, b, n: (b, tmap(z, n), 1)),
                  pl.BlockSpec((1, tb, SMALL_W), lambda z, b, n: (b, tmap(z, n), z)),
                  pl.BlockSpec((1, 1, SMALL_W), lambda z, b, n: (z, 0, 0))],
        out_specs=pl.BlockSpec((1, 1, tb, ML_VW), lambda z, b, n: (z, b, tmap(z, n), 0)),
        scratch_shapes=[pltpu.VMEM((ML_H, ML_DK, ML_DV), F32), pltpu.VMEM((ML_H, 1, ML_DK), F32),
                        pltpu.VMEM((ML_H, 1, 1), F32)],
        compiler_params=_cparams(("parallel", "parallel", "arbitrary")),
        name="mlstm_scan",
    )(qk, qk, pm, small, gate_bias)


def _odd_out_kernel(x_ref, mod_ref, ys_ref, xc_ref, zg_ref, ym_ref, og_ref, dsk_ref, nws_ref, nwm_ref, w_ref, o_ref):
    y = (ys_ref[0, 0] + ys_ref[1, 0] + dsk_ref[...] * xc_ref[0]) * _silu(zg_ref[0])
    ys = _group_rms(y, SSD_G, nws_ref[...])
    hm = ym_ref[0, 0] + ym_ref[1, 0]
    ym = _group_rms(hm, ML_H, nwm_ref[...]) * _sigmoid(og_ref[0])
    yy = jnp.concatenate([ys, ym], axis=1).astype(BF16)
    o_ref[0] = x_ref[0] + mod_ref[0, 0][2:3] * _dot(yy, w_ref[...])


def odd_out(x, modtab, y_s, xbc, ps, y_m, pm, d_skip, nw_s, nw_m, w_out, ctx_len):
    bsz, lt, d = x.shape
    tm = TOK_TILE
    ct = ctx_len // tm
    seq = lt - ctx_len
    tok = lambda width, col=0: pl.BlockSpec((1, tm, width), lambda b, i: (b, i + ct, col))
    two = lambda width: pl.BlockSpec((2, 1, tm, width), lambda b, i: (0, b, i + ct, 0))
    row = lambda width: pl.BlockSpec((1, width), lambda b, i: (0, 0))
    return pl.pallas_call(
        _odd_out_kernel,
        out_shape=jax.ShapeDtypeStruct((bsz, seq, d), F32),
        grid=(bsz, seq // tm),
        in_specs=[tok(d), pl.BlockSpec((1, 1, N_MOD, d), lambda b, i: (b, 1, 0, 0)),
                  two(SSD_INNER), tok(SSD_INNER), tok(SSD_INNER), two(ML_VW), tok(ML_VW, 2),
                  row(SSD_INNER), row(SSD_INNER), row(ML_VW),
                  pl.BlockSpec(w_out.shape, lambda b, i: (0, 0))],
        out_specs=pl.BlockSpec((1, tm, d), lambda b, i: (b, i, 0)),
        compiler_params=_cparams(("parallel", "parallel")),
        name="odd_out",
    )(x, modtab, y_s, xbc, ps, y_m, pm, d_skip.reshape(1, SSD_INNER), nw_s.reshape(1, SSD_INNER),
      nw_m.reshape(1, ML_VW), w_out)


def _router_kernel(x_ref, mod_ref, nw_ref, rw_ref, rb_ref, h_ref, idx_ref, prob_ref, rank_ref, cnt_ref, carry_ref,
                   *, tm):
    first = jnp.logical_and(pl.program_id(0) == 0, pl.program_id(1) == 0)

    @pl.when(first)
    def _():
        carry_ref[...] = jnp.zeros_like(carry_ref)

    h = _norm_mod(x_ref[0], nw_ref[...], mod_ref[0, 0], 3)
    h_ref[0] = h
    logits = _dot(h, rw_ref[...], HI) + rb_ref[...]
    lane = lax.broadcasted_iota(I32, (tm, N_EXP), 1).astype(F32)
    work = logits
    sel = jnp.zeros((tm, N_EXP), F32)
    vals, idxs, hots = [], [], []
    for _ in range(TOP_K):
        m = jnp.max(work, axis=1, keepdims=True)
        idx = jnp.min(jnp.where(work == m, lane, float(N_EXP)), axis=1, keepdims=True)
        hot = lane == idx
        vals.append(m)
        idxs.append(idx)
        hots.append(hot.astype(F32))
        sel = sel + hots[-1]
        work = jnp.where(hot, -jnp.inf, work)
    ex = [jnp.exp(v - vals[0]) for v in vals]
    tot = ex[0] + ex[1] + ex[2] + ex[3]
    ti = lax.broadcasted_iota(I32, (tm, tm), 0)
    si = lax.broadcasted_iota(I32, (tm, tm), 1)
    before = _dot((si < ti).astype(BF16), sel.astype(BF16)) + carry_ref[...]
    ranks = [jnp.sum(before * hot, axis=1, keepdims=True) for hot in hots]
    idx_ref[0] = jnp.concatenate(idxs, axis=1).astype(I32)
    prob_ref[0] = jnp.concatenate([e / tot for e in ex], axis=1)
    rank_ref[0] = jnp.concatenate(ranks, axis=1).astype(I32)
    carry_ref[...] = carry_ref[...] + jnp.sum(sel, axis=0, keepdims=True)
    cnt_ref[...] = carry_ref[...]


def router(x, modtab, nw, rw, rb, ctx_len):
    bsz, lt, d = x.shape
    tm = TOK_TILE
    ctx_tiles = ctx_len // tm
    tok = lambda width, dt: (jax.ShapeDtypeStruct((bsz, lt, width), dt),
                             pl.BlockSpec((1, tm, width), lambda b, i: (b, i, 0)))
    outs = [tok(d, F32), tok(TOP_K, I32), tok(TOP_K, F32), tok(TOP_K, I32),
            (jax.ShapeDtypeStruct((1, N_EXP), F32), pl.BlockSpec((1, N_EXP), lambda b, i: (0, 0)))]
    return pl.pallas_call(
        functools.partial(_router_kernel, tm=tm),
        out_shape=[o[0] for o in outs],
        grid=(bsz, lt // tm),
        in_specs=[pl.BlockSpec((1, tm, d), lambda b, i: (b, i, 0)),
                  pl.BlockSpec((1, 1, N_MOD, d), lambda b, i: (b, (i >= ctx_tiles).astype(I32), 0, 0)),
                  pl.BlockSpec((1, d), lambda b, i: (0, 0)),
                  pl.BlockSpec((d, N_EXP), lambda b, i: (0, 0)),
                  pl.BlockSpec((1, N_EXP), lambda b, i: (0, 0))],
        out_specs=[o[1] for o in outs],
        scratch_shapes=[pltpu.VMEM((1, N_EXP), F32)],
        compiler_params=_cparams(("arbitrary", "arbitrary")),
        name="router",
    )(x, modtab, nw.reshape(1, d), rw, rb.reshape(1, N_EXP))


def sc_gather_rows(table, idx):
    m = idx.shape[0]
    d = table.shape[1]
    workers = SC_CORES * SC_SUBCORES
    per_w, n_chunks = m // workers, m // workers // SC_GATHER_ROWS
    assert per_w * workers == m and n_chunks * SC_GATHER_ROWS == per_w and n_chunks % 2 == 0
    w = SC_GATHER_ROWS
    mesh = plsc.VectorSubcoreMesh(core_axis_name="c", subcore_axis_name="s")

    @functools.partial(
        pl.kernel, mesh=mesh, out_type=jax.ShapeDtypeStruct((m, d), table.dtype),
        scratch_types=[pltpu.VMEM((per_w,), I32), pltpu.VMEM((2, w, d), table.dtype),
                       pltpu.SemaphoreType.DMA((2,)), pltpu.SemaphoreType.DMA((2,))])
    def gather_kernel(table_hbm, idx_hbm, out_hbm, idx_v, rows_v, gsem, wsem):
        wid = lax.axis_index("s") * SC_CORES + lax.axis_index("c")
        base = pl.multiple_of(wid * per_w, 8)
        pltpu.sync_copy(idx_hbm.at[pl.ds(base, per_w)], idx_v)

        def gather(g, b):
            rows = idx_v.at[pl.ds(pl.multiple_of(g * w, 8), w)]
            return pltpu.make_async_copy(table_hbm.at[rows], rows_v.at[b], gsem.at[b])

        def write(g, b):
            dst = out_hbm.at[pl.ds(pl.multiple_of(base + g * w, 8), w)]
            return pltpu.make_async_copy(rows_v.at[b], dst, wsem.at[b])

        gather(0, 0).start()

        @pl.loop(0, n_chunks, step=2)
        def _(g):
            for b in (0, 1):
                cur = g + b
                gather(cur, b).wait()
                write(cur, b).start()

                @pl.when(cur >= 1)
                def _():
                    write(cur - 1, 1 - b).wait()

                @pl.when(cur + 1 < n_chunks)
                def _():
                    gather(cur + 1, 1 - b).start()

        write(n_chunks - 1, 1).wait()

    return gather_kernel(table, idx)


def sc_scatter_rows(src, idx, n_out):
    k_dup, t = idx.shape
    d = src.shape[1]
    workers = SC_CORES * SC_SUBCORES
    w = SC_GATHER_ROWS
    per_w, n_chunks = t // workers, t // workers // w
    assert per_w * workers == t and n_chunks * w == per_w and n_chunks % 2 == 0
    mesh = plsc.VectorSubcoreMesh(core_axis_name="c", subcore_axis_name="s")

    @functools.partial(
        pl.kernel, mesh=mesh, out_type=jax.ShapeDtypeStruct((n_out, d), src.dtype),
        scratch_types=[pltpu.VMEM((k_dup, n_chunks, w), I32), pltpu.VMEM((2, w, d), src.dtype),
                       pltpu.SemaphoreType.DMA((2,)), pltpu.SemaphoreType.DMA((2,))])
    def scatter_kernel(src_hbm, idx_hbm, out_hbm, idx_v, rows_v, rsem, wsem):
        wid = lax.axis_index("s") * SC_CORES + lax.axis_index("c")
        base = pl.multiple_of(wid * per_w, 8)
        for k in range(k_dup):
            pltpu.sync_copy(idx_hbm.at[k, wid], idx_v.at[k])

        def read(g, b):
            rows = src_hbm.at[pl.ds(pl.multiple_of(base + g * w, 8), w)]
            return pltpu.make_async_copy(rows, rows_v.at[b], rsem.at[b])

        def write(g, b, k):
            return pltpu.make_async_copy(rows_v.at[b], out_hbm.at[idx_v.at[k, g]], wsem.at[b])

        read(0, 0).start()

        @pl.loop(0, n_chunks, step=2)
        def _(g):
            for b in (0, 1):
                cur = g + b
                read(cur, b).wait()
                for k in range(k_dup):
                    write(cur, b, k).start()

                @pl.when(cur >= 1)
                def _():
                    for k in range(k_dup):
                        write(cur - 1, 1 - b, k).wait()

                @pl.when(cur + 1 < n_chunks)
                def _():
                    read(cur + 1, 1 - b).start()

        for k in range(k_dup):
            write(n_chunks - 1, 1, k).wait()

    return scatter_kernel(src, idx.reshape(k_dup, workers, n_chunks, w))


def _expert_kernel(be_ref, x_ref, wg_ref, bg_ref, wu_ref, bu_ref, wd_ref, bd_ref, o_ref, wg_bf, wu_bf, wd_bf):
    i = pl.program_id(0)

    @pl.when(jnp.logical_or(i == 0, be_ref[i] != be_ref[jnp.maximum(i - 1, 0)]))
    def _():
        wg_bf[...] = wg_ref[0, 0].astype(BF16)
        wu_bf[...] = wu_ref[0, 0].astype(BF16)
        wd_bf[...] = wd_ref[0, 0].astype(BF16)

    xb = x_ref[...].astype(BF16)
    g = _dot(xb, wg_bf[...]) + bg_ref[0, 0]
    u = _dot(xb, wu_bf[...]) + bu_ref[0, 0]
    g = jnp.minimum(g, SWIGLU_LIMIT)
    u = jnp.clip(u, -SWIGLU_LIMIT, SWIGLU_LIMIT)
    act = (g * _sigmoid(SWIGLU_ALPHA * g) * (u + 1.0)).astype(BF16)
    o_ref[...] = _dot(act, wd_bf[...]) + bd_ref[0, 0]


def expert_ffn(block_expert, x_sorted, layer, wg, bg, wu, bu, wd, bd):
    nblk = block_expert.shape[0]
    n_layers, _, d, f = wg.shape
    wspec = lambda r, c: pl.BlockSpec((1, 1, r, c), lambda i, be: (layer, be[i], 0, 0))
    grid_spec = pltpu.PrefetchScalarGridSpec(
        num_scalar_prefetch=1,
        grid=(nblk,),
        in_specs=[pl.BlockSpec((MOE_BLOCK, d), lambda i, be: (i, 0)),
                  wspec(d, f), wspec(1, f), wspec(d, f), wspec(1, f), wspec(f, d), wspec(1, d)],
        out_specs=pl.BlockSpec((MOE_BLOCK, d), lambda i, be: (i, 0)),
        scratch_shapes=[pltpu.VMEM((d, f), BF16), pltpu.VMEM((d, f), BF16), pltpu.VMEM((f, d), BF16)],
    )
    return pl.pallas_call(
        _expert_kernel,
        out_shape=jax.ShapeDtypeStruct((nblk * MOE_BLOCK, d), F32),
        grid_spec=grid_spec,
        compiler_params=_cparams(("arbitrary",)),
        name="expert_ffn",
    )(block_expert, x_sorted, wg, bg.reshape(n_layers, N_EXP, 1, f), wu, bu.reshape(n_layers, N_EXP, 1, f), wd,
      bd.reshape(n_layers, N_EXP, 1, d))


def _combine_kernel(x_ref, mod_ref, prob_ref, y_ref, fw_ref, o_ref, *, final_norm):
    p = prob_ref[0]
    acc = y_ref[0, 0] * p[:, 0:1]
    for k in range(1, TOP_K):
        acc = acc + y_ref[k, 0] * p[:, k:k + 1]
    out = x_ref[0] + mod_ref[0, 0][5:6] * acc
    if final_norm:
        out = out * lax.rsqrt(jnp.mean(out * out, axis=-1, keepdims=True) + EPS) * fw_ref[...]
    o_ref[0] = out


def combine(x, modtab, probs, y_tok, final_w, ctx_len, final_norm):
    bsz, lt, d = x.shape
    tm = TOK_TILE
    ctx_tiles = ctx_len // tm
    return pl.pallas_call(
        functools.partial(_combine_kernel, final_norm=final_norm),
        out_shape=jax.ShapeDtypeStruct((bsz, lt, d), F32),
        grid=(bsz, lt // tm),
        in_specs=[pl.BlockSpec((1, tm, d), lambda b, i: (b, i, 0)),
                  pl.BlockSpec((1, 1, N_MOD, d), lambda b, i: (b, (i >= ctx_tiles).astype(I32), 0, 0)),
                  pl.BlockSpec((1, tm, TOP_K), lambda b, i: (b, i, 0)),
                  pl.BlockSpec((TOP_K, 1, tm, d), lambda b, i: (0, b, i, 0)),
                  pl.BlockSpec((1, d), lambda b, i: (0, 0))],
        out_specs=pl.BlockSpec((1, tm, d), lambda b, i: (b, i, 0)),
        compiler_params=_cparams(("parallel", "parallel")),
        name="combine",
    )(x, modtab, probs, y_tok, final_w.reshape(1, d))


def moe_layer(x, modtab, layer, nw, rw, rb, wg, bg, wu, bu, wd, bd, final_w, ctx_len, final_norm):
    bsz, lt, d = x.shape
    t = bsz * lt
    h, idx, probs, rank, counts = router(x, modtab, nw, rw, rb, ctx_len)
    counts = counts.reshape(N_EXP).astype(I32)
    padded = (counts + MOE_BLOCK - 1) // MOE_BLOCK * MOE_BLOCK
    ends = jnp.cumsum(padded)
    pstart = ends - padded
    idx = idx.reshape(t, TOP_K)
    dest = pstart[idx] + rank.reshape(t, TOP_K)
    nblk = -(-t * TOP_K // MOE_BLOCK) + N_EXP
    block_row0 = jnp.arange(nblk, dtype=I32) * MOE_BLOCK
    block_expert = jnp.minimum(jnp.sum((ends[None, :] <= block_row0[:, None]).astype(I32), axis=1), N_EXP - 1)
    dest_k = dest.T
    x_sorted = sc_scatter_rows(h.reshape(t, d), dest_k, nblk * MOE_BLOCK)
    y_sorted = expert_ffn(block_expert, x_sorted, layer, wg, bg, wu, bu, wd, bd)
    y_tok = sc_gather_rows(y_sorted, dest_k.reshape(TOP_K * t)).reshape(TOP_K, bsz, lt, d)
    return combine(x, modtab, probs, y_tok, final_w, ctx_len, final_norm)


def even_layer(x, modtab, norm1_w, w_in, w_out, mu, w0, w2, a0, a2, g2, k_k, k_a, r_k, ln_w, ln_b, lb, hg_norm_w,
               ctx_len):
    w_in = w_in.astype(BF16)
    pr, ph = in_proj(x, modtab, norm1_w, [w_in[:, :RW_PROJ], w_in[:, RW_PROJ:]], ctx_len)
    r, v, kk, g, bv, lw, km, bb = rwkv_prep(pr, mu, w0, w2, a0, a2, g2, k_k, k_a, r_k, ctx_len)
    o_r = rwkv_scan(r, v, kk, lw, km, bb, ctx_len)
    o_g = gla_scan(ph, lb, ctx_len)
    return even_out(x, modtab, o_r, bv, g, o_g, ph, ln_w, ln_b, hg_norm_w, w_out.astype(BF16), ctx_len)


def _dir_slab(cols, width):
    halves = []
    for z in range(2):
        parts = [c[:, z * (c.shape[1] // 2):(z + 1) * (c.shape[1] // 2)] for c in cols]
        used = sum(p.shape[1] for p in parts)
        parts.append(jnp.zeros((cols[0].shape[0], width - used), cols[0].dtype))
        halves.append(jnp.concatenate(parts, axis=1))
    return jnp.concatenate(halves, axis=1)


def _dir_rows(vals, width):
    rows = jnp.concatenate(vals, axis=1)
    return jnp.pad(rows, ((0, 0), (0, width - rows.shape[1]))).reshape(2, 1, width)


def odd_layer(x, modtab, norm1_w, w_in, w_out, s_conv_w, s_conv_b, dt_bias, a_log, d_skip, s_norm_w, m_conv_w,
              m_conv_b, i_bias, f_bias, m_norm_w, ctx_len):
    o = 0
    cols = {}
    for name, width in (("zg", SSD_INNER), ("xbc", SSD_CONV_CH), ("dt", 2 * SSD_H), ("qk", 2 * ML_QK),
                        ("v", ML_VW), ("og", ML_VW), ("ig", 2 * ML_H), ("fg", 2 * ML_H)):
        cols[name] = w_in[:, o:o + width]
        o += width
    slab_m = jnp.concatenate([cols["qk"], cols["v"], cols["og"]], axis=1).astype(BF16)
    slab_small = _dir_slab([cols["dt"], cols["ig"], cols["fg"]], SMALL_W).astype(BF16)
    pxbc, zg, pm, small = in_proj(x, modtab, norm1_w, [cols["xbc"].astype(BF16), cols["zg"].astype(BF16), slab_m,
                                                       slab_small], ctx_len)
    xbc, qk = odd_prep(pxbc, pm, s_conv_w, s_conv_b, m_conv_w, m_conv_b, ctx_len)
    zeros_g = jnp.zeros((2, 2 * ML_H), F32)
    y_s = ssd_scan(xbc, small, _dir_rows([dt_bias, zeros_g], SMALL_W), _dir_rows([a_log, zeros_g], SMALL_W), ctx_len)
    y_m = mlstm_scan(qk, pm, small, _dir_rows([jnp.zeros((2, SSD_H), F32), i_bias, f_bias], SMALL_W), ctx_len)
    return odd_out(x, modtab, y_s, xbc, zg, y_m, pm, jnp.repeat(d_skip, SSD_P), s_norm_w, m_norm_w,
                   w_out.astype(BF16), ctx_len)


def _to_col_major(t):
    b, s, ch = t.shape
    return t.reshape(b, s // GRID_W, GRID_W, ch).transpose(0, 2, 1, 3).reshape(b, s, ch)


def _to_row_major(t):
    b, s, ch = t.shape
    return t.reshape(b, GRID_W, s // GRID_W, ch).transpose(0, 2, 1, 3).reshape(b, s, ch)


def kernel(x, c, ctx, c_ctx, ada_w, ada_b, norm1_w, norm2_w, even_w_in, even_w_out, rwkv_mu, rwkv_w0, rwkv_w2, rwkv_a0, rwkv_a2, rwkv_g2, rwkv_k_k, rwkv_k_a, rwkv_r_k, rwkv_ln_w, rwkv_ln_b, hgrn_lower_bounds, hgrn_norm_w, odd_w_in, odd_w_out, ssd_conv_w, ssd_conv_b, ssd_dt_bias, ssd_a_log, ssd_d, ssd_norm_w, mlstm_conv_w, mlstm_conv_b, mlstm_i_bias, mlstm_f_bias, mlstm_norm_w, router_w, router_b, exp_w_gate, exp_b_gate, exp_w_up, exp_b_up, exp_w_down, exp_b_down, final_norm_w):
    lc = ctx.shape[1]
    lower_bounds = jnp.cumsum(jax.nn.softmax(hgrn_lower_bounds.astype(F32), axis=0), axis=0)
    moe = lambda l: (l, norm2_w[l], router_w[l], router_b[l], exp_w_gate, exp_b_gate, exp_w_up, exp_b_up, exp_w_down,
                     exp_b_down, final_norm_w)
    xa = jnp.concatenate([ctx, x], axis=1)
    modtab = _mod_table(c, c_ctx, ada_w[0], ada_b[0])
    xa = even_layer(xa, modtab, norm1_w[0], even_w_in[0], even_w_out[0], rwkv_mu[0], rwkv_w0[0], rwkv_w2[0],
                    rwkv_a0[0], rwkv_a2[0], rwkv_g2[0], rwkv_k_k[0], rwkv_k_a[0], rwkv_r_k[0], rwkv_ln_w[0],
                    rwkv_ln_b[0], lower_bounds[0], hgrn_norm_w[0], lc)
    xa = moe_layer(xa, modtab, *moe(0), lc, False)
    xa = jnp.concatenate([xa[:, :lc], _to_col_major(xa[:, lc:])], axis=1)
    modtab = _mod_table(c, c_ctx, ada_w[1], ada_b[1])
    xl = odd_layer(xa, modtab, norm1_w[1], odd_w_in[0], odd_w_out[0], ssd_conv_w[0], ssd_conv_b[0], ssd_dt_bias[0],
                   ssd_a_log[0], ssd_d[0], ssd_norm_w[0], mlstm_conv_w[0], mlstm_conv_b[0], mlstm_i_bias[0],
                   mlstm_f_bias[0], mlstm_norm_w[0], lc)
    xl = moe_layer(xl, modtab, *moe(1), 0, True)
    return _to_row_major(xl)
```

```python
import functools
import math

import jax
import jax.numpy as jnp
from jax import lax
from jax.experimental import pallas as pl
from jax.experimental.pallas import tpu as pltpu
from jax.experimental.pallas import tpu_sc as plsc

F32 = jnp.float32
BF16 = jnp.bfloat16
I32 = jnp.int32
U32 = jnp.uint32
HI = lax.Precision.HIGHEST

D_MODEL = 1024
GRID_W = 64
N_MOD = 6
EPS = 1e-6
RW_H, RW_N = 8, 64
RW_W = RW_H * RW_N
RW_RANK = 64
RW_GATE_RANK = 128
RW_GN_EPS = 64e-5
RW_PROJ = 3 * RW_W + 4 * RW_RANK + RW_GATE_RANK
RW_CHUNK = 64
RW_GROUP = 4
HG_H, HG_DK, HG_DV = 4, 128, 128
HG_W = HG_H * HG_DK
HG_CHUNK = 32
HG_PROJ = 5 * HG_W
SSD_H, SSD_P, SSD_G, SSD_N = 16, 64, 2, 128
SSD_INNER = SSD_H * SSD_P
SSD_CONV_CH = SSD_INNER + 2 * SSD_G * SSD_N
SSD_CHUNK = 64
ML_H, ML_DK, ML_DV = 4, 128, 256
ML_QK = ML_H * ML_DK
ML_VW = ML_H * ML_DV
ML_CHUNK = 64
ML_M_INIT = -1e30
CONV_K = 5
N_EXP = 32
TOP_K = 4
D_FF = 1024
SWIGLU_LIMIT = 7.0
SWIGLU_ALPHA = 1.702
MOE_BLOCK = 512

TOK_TILE = 256
SCAN_BLOCK = 256
HALO = 8
SMALL_W = 128
VMEM_LIMIT = 56 * 1024 * 1024
SC_CORES, SC_SUBCORES = 2, 16
SC_GATHER_ROWS = 32


def _cparams(sem):
    return pltpu.CompilerParams(dimension_semantics=sem, vmem_limit_bytes=VMEM_LIMIT)


def _dot(a, b, precision=None):
    return jnp.dot(a, b, preferred_element_type=F32, precision=precision)


def _dot_nt(a, b, precision=None):
    return lax.dot_general(a, b, (((1,), (1,)), ((), ())), preferred_element_type=F32, precision=precision)


def _dot_tn(a, b, precision=None):
    return lax.dot_general(a, b, (((0,), (0,)), ((), ())), preferred_element_type=F32, precision=precision)


def _bf16_parts(x):
    hi = x.astype(BF16)
    rest = x - hi.astype(F32)
    mid = rest.astype(BF16)
    return hi, mid, (rest - mid.astype(F32)).astype(BF16)


def _dot_into_01(x, sel):
    sel = sel.astype(BF16)
    p0, p1, p2 = _bf16_parts(x)
    return _dot(p0, sel) + _dot(p1, sel) + _dot(p2, sel)


def _dot_01_into(sel, x):
    sel = sel.astype(BF16)
    p0, p1, p2 = _bf16_parts(x)
    return _dot(sel, p0) + _dot(sel, p1) + _dot(sel, p2)


def _pack_bf16_halves(x):
    n = x.shape[1] // 2
    bits = lax.bitcast_convert_type(x.astype(BF16).astype(F32), U32)
    return bits[:, :n] | (bits[:, n:] >> 16)


def _unpack_bf16_halves(w):
    hi = lax.bitcast_convert_type(w & jnp.uint32(0xFFFF0000), F32)
    lo = lax.bitcast_convert_type(w << 16, F32)
    return jnp.concatenate([hi, lo], axis=1)


def _sigmoid(x):
    return 1.0 / (1.0 + jnp.exp(-x))


def _silu(x):
    return x * _sigmoid(x)


def _softplus(x):
    return jnp.maximum(x, 0.0) + jnp.log(1.0 + jnp.exp(-jnp.abs(x)))


def _dir_masks(z, c):
    ti = lax.broadcasted_iota(I32, (c, c), 0)
    si = lax.broadcasted_iota(I32, (c, c), 1)
    d = (si - ti) * jnp.where(z == 0, 1, -1)
    return d <= 0, d < 0


def _time_block(z, n, ctx_blocks, n_blocks):
    rev = jnp.where(n < ctx_blocks, ctx_blocks - 1 - n, n_blocks - 1 - (n - ctx_blocks))
    return jnp.where(z == 0, n, rev)


def _mod_kernel(c_ref, w_ref, b_ref, o_ref):
    o_ref[...] = _dot(_silu(c_ref[...]), w_ref[...]) + b_ref[...]


def ada_mod(cc, w, b):
    rows, d = cc.shape
    n = w.shape[1]
    tn = 1536
    return pl.pallas_call(
        _mod_kernel,
        out_shape=jax.ShapeDtypeStruct((rows, n), F32),
        grid=(n // tn,),
        in_specs=[pl.BlockSpec((rows, d), lambda j: (0, 0)),
                  pl.BlockSpec((d, tn), lambda j: (0, j)),
                  pl.BlockSpec((1, tn), lambda j: (0, j))],
        out_specs=pl.BlockSpec((rows, tn), lambda j: (0, j)),
        compiler_params=_cparams(("arbitrary",)),
        name="ada_mod",
    )(cc, w, b.reshape(1, n))


def _mod_table(c, c_ctx, w, b):
    bsz = c.shape[0]
    rows = -(-(bsz + 1) // 8) * 8
    cc = jnp.zeros((rows, D_MODEL), F32).at[:bsz].set(c).at[bsz].set(c_ctx)
    m = ada_mod(cc, w, b)
    mod = m[:bsz].reshape(bsz, 1, N_MOD, D_MODEL)
    mod_c = jnp.broadcast_to(m[bsz].reshape(1, 1, N_MOD, D_MODEL), (bsz, 1, N_MOD, D_MODEL))
    return jnp.concatenate([mod_c, mod], axis=1)


def _norm_mod(x, nw, m, shift_idx):
    y = x * lax.rsqrt(jnp.mean(x * x, axis=-1, keepdims=True) + EPS) * nw
    return y * (1.0 + m[shift_idx + 1:shift_idx + 2]) + m[shift_idx:shift_idx + 1]


def _in_proj_kernel(x_ref, mod_ref, nw_ref, *rest, n_out):
    w_refs, o_refs = rest[:n_out], rest[n_out:]
    h = _norm_mod(x_ref[0], nw_ref[...], mod_ref[0, 0], 0).astype(BF16)
    for w_ref, o_ref in zip(w_refs, o_refs):
        o_ref[0] = _dot(h, w_ref[...])


def in_proj(x, modtab, nw, slabs, ctx_len):
    bsz, lt, d = x.shape
    tm = TOK_TILE
    ctx_tiles = ctx_len // tm
    in_specs = [pl.BlockSpec((1, tm, d), lambda b, i: (b, i, 0)),
                pl.BlockSpec((1, 1, N_MOD, d), lambda b, i: (b, (i >= ctx_tiles).astype(I32), 0, 0)),
                pl.BlockSpec((1, d), lambda b, i: (0, 0))]
    in_specs += [pl.BlockSpec(w.shape, lambda b, i: (0, 0)) for w in slabs]
    return pl.pallas_call(
        functools.partial(_in_proj_kernel, n_out=len(slabs)),
        out_shape=[jax.ShapeDtypeStruct((bsz, lt, w.shape[1]), F32) for w in slabs],
        grid=(bsz, lt // tm),
        in_specs=in_specs,
        out_specs=[pl.BlockSpec((1, tm, w.shape[1]), lambda b, i: (b, i, 0)) for w in slabs],
        compiler_params=_cparams(("parallel", "parallel")),
        name="in_proj",
    )(x, modtab, nw.reshape(1, d), *slabs)


def _halo_specs(tm, width, lt):
    per = tm // HALO
    last = lt // HALO - 1
    return [pl.BlockSpec((1, tm, width), lambda b, i: (b, i, 0)),
            pl.BlockSpec((1, HALO, width), lambda b, i: (b, jnp.maximum(i * per - 1, 0), 0)),
            pl.BlockSpec((1, HALO, width), lambda b, i: (b, jnp.minimum((i + 1) * per, last), 0))]


def _segment_edges(i, tm, ctx_len, lt):
    row0 = i * tm
    has_prev = jnp.logical_and(row0 != 0, row0 != ctx_len)
    has_next = jnp.logical_and(row0 + tm != ctx_len, row0 + tm != lt)
    return has_prev, has_next


def _rwkv_prep_kernel(p_ref, pp_ref, pn_ref, mu_ref, w0_ref, w2_ref, a0_ref, a2_ref, g2_ref, kk_ref, ka_ref,
                      rk_ref, bd_ref, r_o, v_o, kk_o, g_o, bv_o, lw_o, km_o, bb_o, *, tm, ctx_len, lt):
    has_prev, has_next = _segment_edges(pl.program_id(1), tm, ctx_len, lt)
    p = p_ref[0]
    prev_row = jnp.where(has_prev, pp_ref[0, HALO - 1:HALO, :], 0.0)
    next_row = jnp.where(has_next, pn_ref[0, 0:1, :], 0.0)
    rid = lax.broadcasted_iota(I32, (tm, 1), 0)
    up = jnp.where(rid == 0, prev_row, pltpu.roll(p, 1, 0))
    dn = jnp.where(rid == tm - 1, next_row, pltpu.roll(p, tm - 1, 0))
    p = p + (0.5 * (up + dn) - p) * mu_ref[...]
    w = RW_W
    r, k, v = p[:, 0:w], p[:, w:2 * w], p[:, 2 * w:3 * w]
    o = 3 * w
    wd = jnp.tanh(p[:, o:o + 2 * RW_RANK])
    ad = p[:, o + 2 * RW_RANK:o + 4 * RW_RANK]
    gd = p[:, o + 4 * RW_RANK:o + 4 * RW_RANK + RW_GATE_RANK]
    bd = bd_ref[...]
    kk = k * kk_ref[...]
    kk = kk * lax.rsqrt(_dot_into_01(kk * kk, bd) + 1e-12)
    r_o[0] = r
    v_o[0] = v
    kk_o[0] = kk
    g_o[0] = _dot(_sigmoid(gd), g2_ref[...])
    km_sum = jnp.zeros_like(k)
    for z in range(2):
        sl = slice(z * RW_RANK, (z + 1) * RW_RANK)
        w_pre = w0_ref[z:z + 1, :] + _dot(wd[:, sl], w2_ref[z])
        lw_o[z, 0] = -_sigmoid(w_pre) * math.exp(-0.5)
        a = _sigmoid(a0_ref[z:z + 1, :] + _dot(ad[:, sl], a2_ref[z]))
        km = k * (1.0 + (a - 1.0) * ka_ref[...])
        km_o[z, 0] = km
        bb_o[z, 0] = kk * a
        km_sum = km_sum + km
    bv_o[0] = _dot_into_01(r * km_sum * rk_ref[...], bd) * v


def _head_block_ones(width, head):
    i = jnp.arange(width) // head
    return (i[:, None] == i[None, :]).astype(F32)


def rwkv_prep(p, mu, w0, w2, a0, a2, g2, k_k, k_a, r_k, ctx_len):
    bsz, lt, width = p.shape
    tm = TOK_TILE
    w = RW_W
    full = lambda a: pl.BlockSpec(a.shape, lambda b, i: (0,) * a.ndim)
    params = [mu.reshape(1, width), w0, w2, a0, a2, g2, k_k.reshape(1, w), k_a.reshape(1, w), r_k.reshape(1, w),
              _head_block_ones(w, RW_N)]
    one = jax.ShapeDtypeStruct((bsz, lt, w), F32)
    two = jax.ShapeDtypeStruct((2, bsz, lt, w), F32)
    one_spec = pl.BlockSpec((1, tm, w), lambda b, i: (b, i, 0))
    two_spec = pl.BlockSpec((2, 1, tm, w), lambda b, i: (0, b, i, 0))
    return pl.pallas_call(
        functools.partial(_rwkv_prep_kernel, tm=tm, ctx_len=ctx_len, lt=lt),
        out_shape=[one] * 5 + [two] * 3,
        grid=(bsz, lt // tm),
        in_specs=_halo_specs(tm, width, lt) + [full(a) for a in params],
        out_specs=[one_spec] * 5 + [two_spec] * 3,
        compiler_params=_cparams(("parallel", "parallel")),
        name="rwkv_prep",
    )(p, p, p, *params)


def _unit_lower_inverses(n_mats, eye, c):
    xs = [eye - n for n in n_mats]
    ps = [_dot(n, n) for n in n_mats]
    span = 4
    while True:
        xs = [x + _dot(x, p) for x, p in zip(xs, ps)]
        if span >= c:
            return xs
        ps = [_dot(p, p) for p in ps]
        span *= 2


def _rwkv_scan_kernel(r_ref, v_ref, kk_ref, lw_ref, km_ref, bb_ref, o_ref, s_ref, *, tb, c):
    z = pl.program_id(0)
    nsub = tb // c

    @pl.when(pl.program_id(2) == 0)
    def _():
        s_ref[...] = jnp.zeros_like(s_ref)

    incl, strict = _dir_masks(z, c)
    incl_f, strict_f = incl.astype(F32), strict.astype(F32)
    eye = incl_f - strict_f

    heads = [slice(h * RW_N, (h + 1) * RW_N) for h in range(RW_H)]

    def group(g, s_old):
        subs = []
        for i in range(RW_GROUP):
            j = g * RW_GROUP + i
            rows = pl.ds(pl.multiple_of(jnp.where(z == 0, j, nsub - 1 - j) * c, c), c)
            r, v, kk = r_ref[0, rows, :], v_ref[0, rows, :], kk_ref[0, rows, :]
            lw, km, bb = lw_ref[0, 0, rows, :], km_ref[0, 0, rows, :], bb_ref[0, 0, rows, :]
            cum = _dot_01_into(incl_f, lw)
            tot = jnp.sum(lw, axis=0, keepdims=True)
            e_neg = jnp.exp(-cum)
            e_end = jnp.exp(tot - cum)
            a_t = kk * jnp.exp(cum - lw)
            r_t = r * jnp.exp(cum)
            b_t, k_t = bb * e_neg, km * e_neg
            b_p, k_p = bb * e_end, km * e_end
            subs.append(dict(
                rows=rows, e_tot=jnp.exp(tot), a=[a_t[:, hs] for hs in heads], r=[r_t[:, hs] for hs in heads],
                ar=[jnp.concatenate([a_t[:, hs], r_t[:, hs]], axis=0) for hs in heads],
                bk=[jnp.concatenate([b_t[:, hs], k_t[:, hs]], axis=0) for hs in heads],
                bkp=[jnp.concatenate([b_p[:, hs], k_p[:, hs]], axis=0) for hs in heads],
                v=[v[:, hs] for hs in heads]))
        flat = [(i, h) for i in range(RW_GROUP) for h in range(RW_H)]
        sc = [_dot_nt(subs[i]["ar"][h], subs[i]["bk"][h]) for i, h in flat]
        m_ak_v = [_dot(x[:c, c:] * strict_f, subs[i]["v"][h]) for x, (i, h) in zip(sc, flat)]
        inv = _unit_lower_inverses([x[:c, :c] * strict_f for x in sc], eye, c)
        w1r = [jnp.concatenate([_dot(n, subs[i]["a"][h]), subs[i]["r"][h]], axis=0) for n, (i, h) in zip(inv, flat)]
        w2 = [_dot(n, m) for n, m in zip(inv, m_ak_v)]
        n_rbk = [jnp.concatenate([x[c:, :c] * incl_f, x[c:, c:] * incl_f], axis=1) for x in sc]
        state = list(s_old)
        for i in range(RW_GROUP):
            sub = subs[i]
            base = i * RW_H
            ws = [_dot_nt(w1r[base + h], state[h]) for h in range(RW_H)]
            uv = [jnp.concatenate([-(ws[h][:c] + w2[base + h]), sub["v"][h]], axis=0) for h in range(RW_H)]
            outs = [ws[h][c:] + _dot(n_rbk[base + h], uv[h]) for h in range(RW_H)]
            o_ref[0, 0, sub["rows"], :] = jnp.concatenate(outs, axis=1)
            state = [state[h] * sub["e_tot"][:, heads[h]] + _dot_tn(uv[h], sub["bkp"][h]) for h in range(RW_H)]
        return tuple(state)

    s_fin = lax.fori_loop(0, nsub // RW_GROUP, group, tuple(s_ref[h] for h in range(RW_H)))
    for h in range(RW_H):
        s_ref[h] = s_fin[h]


def rwkv_scan(r, v, kk, lw, km, bb, ctx_len):
    bsz, lt, w = r.shape
    tb = SCAN_BLOCK
    nb, cb = lt // tb, ctx_len // tb
    tmap = lambda z, b, n: _time_block(z, n, cb, nb)
    one = pl.BlockSpec((1, tb, w), lambda z, b, n: (b, tmap(z, b, n), 0))
    two = pl.BlockSpec((1, 1, tb, w), lambda z, b, n: (z, b, tmap(z, b, n), 0))
    return pl.pallas_call(
        functools.partial(_rwkv_scan_kernel, tb=tb, c=RW_CHUNK),
        out_shape=jax.ShapeDtypeStruct((2, bsz, lt, w), F32),
        grid=(2, bsz, nb),
        in_specs=[one, one, one, two, two, two],
        out_specs=two,
        scratch_shapes=[pltpu.VMEM((RW_H, RW_N, RW_N), F32)],
        compiler_params=_cparams(("parallel", "parallel", "arbitrary")),
        name="rwkv_scan",
    )(r, v, kk, lw, km, bb)


def _gla_kernel(q_ref, f_ref, i_ref, lb_ref, o_ref, s_ref, *, tb, c):
    z = pl.program_id(0)
    nsub = tb // c

    @pl.when(pl.program_id(2) == 0)
    def _():
        s_ref[...] = jnp.zeros_like(s_ref)

    incl, _ = _dir_masks(z, c)
    incl_f = incl.astype(F32)
    lb = lb_ref[...]

    def body(j, s_old):
        jj = jnp.where(z == 0, j, nsub - 1 - j)
        rows = pl.ds(pl.multiple_of(jj * c, c), c)
        q = _silu(q_ref[0, rows, :]) * HG_DK ** -0.5
        f = f_ref[0, rows, :]
        v = i_ref[0, rows, :]
        sig_f = _sigmoid(f)
        log_f = jnp.log(lb + (1.0 - lb) * sig_f)
        k = (1.0 - lb) * (1.0 - sig_f)
        cum = _dot_01_into(incl_f, log_f)
        tot = jnp.sum(log_f, axis=0, keepdims=True)
        q_in = q * jnp.exp(cum)
        k_in = k * jnp.exp(-cum)
        k_end = k * jnp.exp(tot - cum)
        e_tot = jnp.exp(tot)
        heads = [slice(h * HG_DK, (h + 1) * HG_DK) for h in range(HG_H)]
        a = [_dot_nt(q_in[:, hs], k_in[:, hs]) * incl_f for hs in heads]
        inter = [_dot_nt(q_in[:, hs], s) for hs, s in zip(heads, s_old)]
        grow = [_dot_tn(v[:, hs], k_end[:, hs]) for hs in heads]
        outs = [_dot(aa, v[:, hs]) + ii for aa, hs, ii in zip(a, heads, inter)]
        o_ref[0, 0, rows, :] = jnp.concatenate(outs, axis=1)
        return tuple(s * e_tot[:, hs] + g for s, hs, g in zip(s_old, heads, grow))

    s_fin = lax.fori_loop(0, nsub, body, tuple(s_ref[h] for h in range(HG_H)), unroll=4)
    for h in range(HG_H):
        s_ref[h] = s_fin[h]


def gla_scan(ph, lb, ctx_len):
    bsz, lt, _ = ph.shape
    w = HG_W
    tb = SCAN_BLOCK
    nb, cb = lt // tb, ctx_len // tb
    tmap = lambda z, n: _time_block(z, n, cb, nb)
    return pl.pallas_call(
        functools.partial(_gla_kernel, tb=tb, c=HG_CHUNK),
        out_shape=jax.ShapeDtypeStruct((2, bsz, lt, w), F32),
        grid=(2, bsz, nb),
        in_specs=[pl.BlockSpec((1, tb, w), lambda z, b, n: (b, tmap(z, n), 0)),
                  pl.BlockSpec((1, tb, w), lambda z, b, n: (b, tmap(z, n), 1 + z)),
                  pl.BlockSpec((1, tb, w), lambda z, b, n: (b, tmap(z, n), 3)),
                  pl.BlockSpec((1, w), lambda z, b, n: (0, 0))],
        out_specs=pl.BlockSpec((1, 1, tb, w), lambda z, b, n: (z, b, tmap(z, n), 0)),
        scratch_shapes=[pltpu.VMEM((HG_H, HG_DV, HG_DK), F32)],
        compiler_params=_cparams(("parallel", "parallel", "arbitrary")),
        name="gla_scan",
    )(ph, ph, ph, lb.reshape(1, w))


def _group_rms(y, groups, gain):
    width = y.shape[-1] // groups
    parts = []
    for g in range(groups):
        yg = y[:, g * width:(g + 1) * width]
        parts.append(yg * lax.rsqrt(jnp.mean(yg * yg, axis=-1, keepdims=True) + EPS))
    return jnp.concatenate(parts, axis=1) * gain


def _even_out_kernel(x_ref, mod_ref, or_ref, bv_ref, g_ref, og_ref, gate_ref, lnw_ref, lnb_ref, nw_ref, bd_ref,
                     w_ref, o_ref):
    bd = bd_ref[...]
    o = or_ref[0, 0] + or_ref[1, 0]
    mu = _dot_into_01(o, bd) * (1.0 / RW_N)
    oc = o - mu
    var = _dot_into_01(oc * oc, bd) * (1.0 / RW_N)
    yr = (oc * lax.rsqrt(var + RW_GN_EPS) * lnw_ref[...] + lnb_ref[...] + bv_ref[0]) * g_ref[0]
    og = og_ref[0, 0] + og_ref[1, 0]
    yh = _group_rms(og, HG_H, nw_ref[...]) * _silu(gate_ref[0])
    y = jnp.concatenate([yr, yh], axis=1).astype(BF16)
    o_ref[0] = x_ref[0] + mod_ref[0, 0][2:3] * _dot(y, w_ref[...])


def even_out(x, modtab, o_r, bv, g, o_g, ph, ln_w, ln_b, norm_w, w_out, ctx_len):
    bsz, lt, d = x.shape
    tm = TOK_TILE
    ctx_tiles = ctx_len // tm
    w = RW_W
    tok = lambda width, col=0: pl.BlockSpec((1, tm, width), lambda b, i: (b, i, col))
    two = pl.BlockSpec((2, 1, tm, w), lambda b, i: (0, b, i, 0))
    row = lambda width: pl.BlockSpec((1, width), lambda b, i: (0, 0))
    return pl.pallas_call(
        _even_out_kernel,
        out_shape=jax.ShapeDtypeStruct((bsz, lt, d), F32),
        grid=(bsz, lt // tm),
        in_specs=[tok(d),
                  pl.BlockSpec((1, 1, N_MOD, d), lambda b, i: (b, (i >= ctx_tiles).astype(I32), 0, 0)),
                  two, tok(w), tok(w), two, tok(HG_W, 4), row(w), row(w), row(HG_W),
                  pl.BlockSpec((w, w), lambda b, i: (0, 0)),
                  pl.BlockSpec(w_out.shape, lambda b, i: (0, 0))],
        out_specs=tok(d),
        compiler_params=_cparams(("parallel", "parallel")),
        name="even_out",
    )(x, modtab, o_r, bv, g, o_g, ph, ln_w.reshape(1, w), ln_b.reshape(1, w), norm_w.reshape(1, HG_W),
      _head_block_ones(w, RW_N), w_out)


def _conv_silu(x_ref, xp_ref, xn_ref, w_ref, b_ref, has_prev, has_next, tm):
    prev = jnp.where(has_prev, xp_ref[0], 0.0)
    nxt = jnp.where(has_next, xn_ref[0], 0.0)
    ext = jnp.concatenate([prev, x_ref[0], nxt], axis=0)
    rows = tm + 2 * HALO
    pad = CONV_K // 2
    acc = b_ref[...] + jnp.zeros((tm, ext.shape[1]), F32)
    for j in range(CONV_K):
        d = j - pad
        sh = ext if d == 0 else pltpu.roll(ext, (-d) % rows, 0)
        acc = acc + sh[HALO:HALO + tm] * w_ref[j:j + 1, :]
    return _silu(acc)


def _odd_prep_kernel(xs_ref, xsp_ref, xsn_ref, xm_ref, xmp_ref, xmn_ref, ws_ref, bs_ref, wm_ref, bm_ref,
                     os_ref, om_ref, *, tm, ctx_len, lt):
    has_prev, has_next = _segment_edges(pl.program_id(1), tm, ctx_len, lt)
    os_ref[0] = _conv_silu(xs_ref, xsp_ref, xsn_ref, ws_ref, bs_ref, has_prev, has_next, tm)
    om_ref[0] = _conv_silu(xm_ref, xmp_ref, xmn_ref, wm_ref, bm_ref, has_prev, has_next, tm)


def odd_prep(xbc, qk, ws, bs, wm, bm, ctx_len):
    bsz, lt, cs = xbc.shape
    cm = wm.shape[1]
    tm = TOK_TILE
    full = lambda a: pl.BlockSpec(a.shape, lambda b, i: (0,) * a.ndim)
    params = [ws, bs.reshape(1, cs), wm, bm.reshape(1, cm)]
    return pl.pallas_call(
        functools.partial(_odd_prep_kernel, tm=tm, ctx_len=ctx_len, lt=lt),
        out_shape=[jax.ShapeDtypeStruct((bsz, lt, cs), F32), jax.ShapeDtypeStruct((bsz, lt, cm), F32)],
        grid=(bsz, lt // tm),
        in_specs=_halo_specs(tm, cs, lt) + _halo_specs(tm, cm, lt) + [full(a) for a in params],
        out_specs=[pl.BlockSpec((1, tm, cs), lambda b, i: (b, i, 0)),
                   pl.BlockSpec((1, tm, cm), lambda b, i: (b, i, 0))],
        compiler_params=_cparams(("parallel", "parallel")),
        name="odd_prep",
    )(xbc, xbc, xbc, qk, qk, qk, *params)


def _ssd_kernel(x_ref, b_ref, c_ref, sm_ref, dtb_ref, alog_ref, ex_ref, o_ref, s_ref, *, tb, c):
    z = pl.program_id(0)
    nsub = tb // c
    hg = SSD_H // SSD_G

    @pl.when(pl.program_id(2) == 0)
    def _():
        s_ref[...] = jnp.zeros_like(s_ref)

    incl, _ = _dir_masks(z, c)
    incl_f = incl.astype(F32)
    ex = ex_ref[...]

    gw = hg * SSD_P

    def body(j, s_old):
        jj = jnp.where(z == 0, j, nsub - 1 - j)
        rows = pl.ds(pl.multiple_of(jj * c, c), c)
        x = x_ref[0, rows, :]
        bm = b_ref[0, rows, :]
        cm = c_ref[0, rows, :]
        dt = _softplus(sm_ref[0, rows, :] + dtb_ref[0])
        la = -dt * jnp.exp(alog_ref[0])
        a_cum = _dot_01_into(incl_f, la)
        a_tot = jnp.sum(la, axis=0, keepdims=True)
        a_cum_t = a_cum.T
        e_tot_col = jnp.exp(jnp.where(z == 0, a_cum_t[:, c - 1:c], a_cum_t[:, 0:1]))
        wide = _dot(jnp.concatenate([dt, jnp.exp(a_cum), jnp.exp(a_tot - a_cum)], axis=0).astype(BF16),
                    ex.astype(BF16))
        xdt = x * wide[:c]
        din, dend = wide[c:2 * c], wide[2 * c:]
        xd_end = xdt * dend
        bgs = [bm[:, g * SSD_N:(g + 1) * SSD_N] for g in range(SSD_G)]
        cgs = [cm[:, g * SSD_N:(g + 1) * SSD_N] for g in range(SSD_G)]
        cb = [_dot_nt(cg, bg) for cg, bg in zip(cgs, bgs)]
        inter = [_dot_nt(cg, s) for cg, s in zip(cgs, s_old)]
        grow = [_dot_tn(xd_end[:, g * gw:(g + 1) * gw], bg) for g, bg in enumerate(bgs)]
        intra = []
        for h in range(SSD_H):
            seg = a_cum[:, h:h + 1] - a_cum_t[h:h + 1, :]
            wmat = cb[h // hg] * jnp.where(incl, jnp.exp(seg), 0.0)
            intra.append(_dot(wmat, xdt[:, h * SSD_P:(h + 1) * SSD_P]))
        o_ref[0, 0, rows, :] = jnp.concatenate(intra, axis=1) + din * jnp.concatenate(inter, axis=1)
        new = []
        for g in range(SSD_G):
            decay = jnp.concatenate([jnp.broadcast_to(e_tot_col[h:h + 1, :], (SSD_P, 1))
                                     for h in range(g * hg, (g + 1) * hg)], axis=0)
            new.append(s_old[g] * decay + grow[g])
        return tuple(new)

    s_fin = lax.fori_loop(0, nsub, body, tuple(s_ref[g] for g in range(SSD_G)), unroll=2)
    for g in range(SSD_G):
        s_ref[g] = s_fin[g]


def _head_expand(heads, width, rows):
    h = jnp.arange(rows)[:, None]
    col = jnp.arange(heads * width)[None, :] // width
    return (h == col).astype(F32)


def ssd_scan(xbc, small, dtb, alog, ctx_len):
    bsz, lt, _ = xbc.shape
    tb = SCAN_BLOCK
    nb, cb = lt // tb, ctx_len // tb
    tmap = lambda z, n: _time_block(z, n, cb, nb)
    gw = SSD_G * SSD_N
    return pl.pallas_call(
        functools.partial(_ssd_kernel, tb=tb, c=SSD_CHUNK),
        out_shape=jax.ShapeDtypeStruct((2, bsz, lt, SSD_INNER), F32),
        grid=(2, bsz, nb),
        in_specs=[pl.BlockSpec((1, tb, SSD_INNER), lambda z, b, n: (b, tmap(z, n), 0)),
                  pl.BlockSpec((1, tb, gw), lambda z, b, n: (b, tmap(z, n), SSD_INNER // gw)),
                  pl.BlockSpec((1, tb, gw), lambda z, b, n: (b, tmap(z, n), SSD_INNER // gw + 1)),
                  pl.BlockSpec((1, tb, SMALL_W), lambda z, b, n: (b, tmap(z, n), z)),
                  pl.BlockSpec((1, 1, SMALL_W), lambda z, b, n: (z, 0, 0)),
                  pl.BlockSpec((1, 1, SMALL_W), lambda z, b, n: (z, 0, 0)),
                  pl.BlockSpec((SMALL_W, SSD_INNER), lambda z, b, n: (0, 0))],
        out_specs=pl.BlockSpec((1, 1, tb, SSD_INNER), lambda z, b, n: (z, b, tmap(z, n), 0)),
        scratch_shapes=[pltpu.VMEM((SSD_G, SSD_H // SSD_G * SSD_P, SSD_N), F32)],
        compiler_params=_cparams(("parallel", "parallel", "arbitrary")),
        name="ssd_scan",
    )(xbc, xbc, xbc, small, dtb, alog, _head_expand(SSD_H, SSD_P, SMALL_W))


ML_I_LANE = SSD_H
ML_F_LANE = SSD_H + ML_H


def _mlstm_kernel(q_ref, k_ref, v_ref, sm_ref, gb_ref, o_ref, cs_ref, ns_ref, m_ref, *, tb, c):
    z = pl.program_id(0)
    nsub = tb // c

    @pl.when(pl.program_id(2) == 0)
    def _():
        cs_ref[...] = jnp.zeros_like(cs_ref)
        ns_ref[...] = jnp.zeros_like(ns_ref)
        m_ref[...] = jnp.full(m_ref.shape, ML_M_INIT, F32)

    incl, _ = _dir_masks(z, c)
    incl_f = incl.astype(F32)
    lane = lax.broadcasted_iota(I32, (1, SMALL_W), 1)
    is_f = jnp.logical_and(lane >= ML_F_LANE, lane < ML_F_LANE + ML_H)

    def local(j):
        rows = pl.ds(pl.multiple_of(jnp.where(z == 0, j, nsub - 1 - j) * c, c), c)
        q = q_ref[0, rows, :]
        k = k_ref[0, rows, :] * ML_DK ** -0.5
        v = v_ref[0, rows, :]
        pre = sm_ref[0, rows, :] + gb_ref[0]
        gates = jnp.where(is_f, -_softplus(-pre), pre)
        f_cum = _dot(incl_f, gates, HI)
        f_tot = jnp.sum(gates, axis=0, keepdims=True)
        f_cum_t = f_cum.T
        gates_t = gates.T
        per_head = []
        for h in range(ML_H):
            li, lf = ML_I_LANE + h, ML_F_LANE + h
            qh = q[:, h * ML_DK:(h + 1) * ML_DK]
            kh = k[:, h * ML_DK:(h + 1) * ML_DK]
            vh = v[:, h * ML_DV:(h + 1) * ML_DV]
            f_col = f_cum[:, lf:lf + 1]
            f_last = f_tot[:, lf:lf + 1]
            logw_end = f_last - f_col + gates[:, li:li + 1]
            m_end = jnp.max(logw_end, axis=0, keepdims=True)
            kw = kh * jnp.exp(logw_end - m_end)
            logw = jnp.where(incl, f_col - f_cum_t[lf:lf + 1, :] + gates_t[li:li + 1, :], -jnp.inf)
            m_loc = jnp.max(logw, axis=1, keepdims=True)
            scores = _dot_nt(qh, kh) * jnp.exp(logw - m_loc)
            per_head.append(dict(
                q=qh, f_col=f_col, f_last=f_last, m_end=m_end, m_loc=m_loc, num=_dot(scores, vh),
                den=jnp.sum(scores, axis=1, keepdims=True), kv=_dot_tn(kw, vh),
                ksum=jnp.sum(kw, axis=0, keepdims=True)))
        return rows, per_head

    subs = [local(j) for j in range(nsub)]
    cs = [cs_ref[h] for h in range(ML_H)]
    ns = [ns_ref[h] for h in range(ML_H)]
    m = [m_ref[h] for h in range(ML_H)]
    for rows, per_head in subs:
        outs = []
        for h, p in enumerate(per_head):
            m_t = jnp.maximum(p["m_loc"], p["f_col"] + m[h])
            w_loc = jnp.exp(p["m_loc"] - m_t)
            w_inter = jnp.exp(p["f_col"] + m[h] - m_t)
            num = p["num"] * w_loc + w_inter * _dot(p["q"], cs[h])
            den = p["den"] * w_loc + w_inter * jnp.sum(p["q"] * ns[h], axis=1, keepdims=True)
            outs.append(num / jnp.maximum(jnp.abs(den), jnp.exp(-m_t)))
            m_new = jnp.maximum(p["f_last"] + m[h], p["m_end"])
            s_keep = jnp.exp(p["f_last"] + m[h] - m_new)
            s_loc = jnp.exp(p["m_end"] - m_new)
            cs[h] = s_keep * cs[h] + s_loc * p["kv"]
            ns[h] = s_keep * ns[h] + s_loc * p["ksum"]
            m[h] = m_new
        o_ref[0, 0, rows, :] = jnp.concatenate(outs, axis=1)
    for h in range(ML_H):
        cs_ref[h] = cs[h]
        ns_ref[h] = ns[h]
        m_ref[h] = m[h]


def mlstm_scan(qk, pm, small, gate_bias, ctx_len):
    bsz, lt, _ = qk.shape
    tb = SCAN_BLOCK
    nb, cb = lt // tb, ctx_len // tb
    tmap = lambda z, n: _time_block(z, n, cb, nb)
    return pl.pallas_call(
        functools.partial(_mlstm_kernel, tb=tb, c=ML_CHUNK),
        out_shape=jax.ShapeDtypeStruct((2, bsz, lt, ML_VW), F32),
        grid=(2, bsz, nb),
        in_specs=[pl.BlockSpec((1, tb, ML_QK), lambda z, b, n: (b, tmap(z, n), 0)),
                  pl.BlockSpec((1, tb, ML_QK), lambda z, b, n: (b, tmap(z, n), 1)),
                  pl.BlockSpec((1, tb, ML_VW), lambda z, b, n: (b, tmap(z, n), 1)),
                  pl.BlockSpec((1, tb, SMALL_W), lambda z, b, n: (b, tmap(z, n), z)),
                  pl.BlockSpec((1, 1, SMALL_W), lambda z, b, n: (z, 0, 0))],
        out_specs=pl.BlockSpec((1, 1, tb, ML_VW), lambda z, b, n: (z, b, tmap(z, n), 0)),
        scratch_shapes=[pltpu.VMEM((ML_H, ML_DK, ML_DV), F32), pltpu.VMEM((ML_H, 1, ML_DK), F32),
                        pltpu.VMEM((ML_H, 1, 1), F32)],
        compiler_params=_cparams(("parallel", "parallel", "arbitrary")),
        name="mlstm_scan",
    )(qk, qk, pm, small, gate_bias)


def _odd_out_kernel(x_ref, mod_ref, ys_ref, xc_ref, zg_ref, ym_ref, og_ref, dsk_ref, nws_ref, nwm_ref, w_ref, o_ref):
    y = (ys_ref[0, 0] + ys_ref[1, 0] + dsk_ref[...] * xc_ref[0]) * _silu(zg_ref[0])
    ys = _group_rms(y, SSD_G, nws_ref[...])
    hm = ym_ref[0, 0] + ym_ref[1, 0]
    ym = _group_rms(hm, ML_H, nwm_ref[...]) * _sigmoid(og_ref[0])
    yy = jnp.concatenate([ys, ym], axis=1).astype(BF16)
    o_ref[0] = x_ref[0] + mod_ref[0, 0][2:3] * _dot(yy, w_ref[...])


def odd_out(x, modtab, y_s, xbc, ps, y_m, pm, d_skip, nw_s, nw_m, w_out, ctx_len):
    bsz, lt, d = x.shape
    tm = TOK_TILE
    ct = ctx_len // tm
    seq = lt - ctx_len
    tok = lambda width, col=0: pl.BlockSpec((1, tm, width), lambda b, i: (b, i + ct, col))
    two = lambda width: pl.BlockSpec((2, 1, tm, width), lambda b, i: (0, b, i + ct, 0))
    row = lambda width: pl.BlockSpec((1, width), lambda b, i: (0, 0))
    return pl.pallas_call(
        _odd_out_kernel,
        out_shape=jax.ShapeDtypeStruct((bsz, seq, d), F32),
        grid=(bsz, seq // tm),
        in_specs=[tok(d), pl.BlockSpec((1, 1, N_MOD, d), lambda b, i: (b, 1, 0, 0)),
                  two(SSD_INNER), tok(SSD_INNER), tok(SSD_INNER), two(ML_VW), tok(ML_VW, 2),
                  row(SSD_INNER), row(SSD_INNER), row(ML_VW),
                  pl.BlockSpec(w_out.shape, lambda b, i: (0, 0))],
        out_specs=pl.BlockSpec((1, tm, d), lambda b, i: (b, i, 0)),
        compiler_params=_cparams(("parallel", "parallel")),
        name="odd_out",
    )(x, modtab, y_s, xbc, ps, y_m, pm, d_skip.reshape(1, SSD_INNER), nw_s.reshape(1, SSD_INNER),
      nw_m.reshape(1, ML_VW), w_out)


def _router_kernel(x_ref, mod_ref, nw_ref, rw_ref, rb_ref, h_ref, idx_ref, prob_ref, rank_ref, cnt_ref, carry_ref,
                   *, tm):
    first = jnp.logical_and(pl.program_id(0) == 0, pl.program_id(1) == 0)

    @pl.when(first)
    def _():
        carry_ref[...] = jnp.zeros_like(carry_ref)

    h = _norm_mod(x_ref[0], nw_ref[...], mod_ref[0, 0], 3)
    h_ref[0] = _pack_bf16_halves(h)
    logits = _dot(h, rw_ref[...], HI) + rb_ref[...]
    lane = lax.broadcasted_iota(I32, (tm, N_EXP), 1).astype(F32)
    work = logits
    sel = jnp.zeros((tm, N_EXP), F32)
    vals, idxs, hots = [], [], []
    for _ in range(TOP_K):
        m = jnp.max(work, axis=1, keepdims=True)
        idx = jnp.min(jnp.where(work == m, lane, float(N_EXP)), axis=1, keepdims=True)
        hot = lane == idx
        vals.append(m)
        idxs.append(idx)
        hots.append(hot.astype(F32))
        sel = sel + hots[-1]
        work = jnp.where(hot, -jnp.inf, work)
    ex = [jnp.exp(v - vals[0]) for v in vals]
    tot = ex[0] + ex[1] + ex[2] + ex[3]
    ti = lax.broadcasted_iota(I32, (tm, tm), 0)
    si = lax.broadcasted_iota(I32, (tm, tm), 1)
    before = _dot((si < ti).astype(BF16), sel.astype(BF16)) + carry_ref[...]
    ranks = [jnp.sum(before * hot, axis=1, keepdims=True) for hot in hots]
    idx_ref[0] = jnp.concatenate(idxs, axis=1).astype(I32)
    prob_ref[0] = jnp.concatenate([e / tot for e in ex], axis=1)
    rank_ref[0] = jnp.concatenate(ranks, axis=1).astype(I32)
    carry_ref[...] = carry_ref[...] + jnp.sum(sel, axis=0, keepdims=True)
    cnt_ref[...] = carry_ref[...]


def router(x, modtab, nw, rw, rb, ctx_len):
    bsz, lt, d = x.shape
    tm = TOK_TILE
    ctx_tiles = ctx_len // tm
    tok = lambda width, dt: (jax.ShapeDtypeStruct((bsz, lt, width), dt),
                             pl.BlockSpec((1, tm, width), lambda b, i: (b, i, 0)))
    outs = [tok(d // 2, U32), tok(TOP_K, I32), tok(TOP_K, F32), tok(TOP_K, I32),
            (jax.ShapeDtypeStruct((1, N_EXP), F32), pl.BlockSpec((1, N_EXP), lambda b, i: (0, 0)))]
    return pl.pallas_call(
        functools.partial(_router_kernel, tm=tm),
        out_shape=[o[0] for o in outs],
        grid=(bsz, lt // tm),
        in_specs=[pl.BlockSpec((1, tm, d), lambda b, i: (b, i, 0)),
                  pl.BlockSpec((1, 1, N_MOD, d), lambda b, i: (b, (i >= ctx_tiles).astype(I32), 0, 0)),
                  pl.BlockSpec((1, d), lambda b, i: (0, 0)),
                  pl.BlockSpec((d, N_EXP), lambda b, i: (0, 0)),
                  pl.BlockSpec((1, N_EXP), lambda b, i: (0, 0))],
        out_specs=[o[1] for o in outs],
        scratch_shapes=[pltpu.VMEM((1, N_EXP), F32)],
        compiler_params=_cparams(("arbitrary", "arbitrary")),
        name="router",
    )(x, modtab, nw.reshape(1, d), rw, rb.reshape(1, N_EXP))


def sc_gather_rows(table, idx):
    m = idx.shape[0]
    d = table.shape[1]
    workers = SC_CORES * SC_SUBCORES
    per_w, n_chunks = m // workers, m // workers // SC_GATHER_ROWS
    assert per_w * workers == m and n_chunks * SC_GATHER_ROWS == per_w and n_chunks % 2 == 0
    w = SC_GATHER_ROWS
    mesh = plsc.VectorSubcoreMesh(core_axis_name="c", subcore_axis_name="s")

    @functools.partial(
        pl.kernel, mesh=mesh, out_type=jax.ShapeDtypeStruct((m, d), table.dtype),
        scratch_types=[pltpu.VMEM((per_w,), I32), pltpu.VMEM((2, w, d), table.dtype),
                       pltpu.SemaphoreType.DMA((2,)), pltpu.SemaphoreType.DMA((2,))])
    def gather_kernel(table_hbm, idx_hbm, out_hbm, idx_v, rows_v, gsem, wsem):
        wid = lax.axis_index("s") * SC_CORES + lax.axis_index("c")
        base = pl.multiple_of(wid * per_w, 8)
        pltpu.sync_copy(idx_hbm.at[pl.ds(base, per_w)], idx_v)

        def gather(g, b):
            rows = idx_v.at[pl.ds(pl.multiple_of(g * w, 8), w)]
            return pltpu.make_async_copy(table_hbm.at[rows], rows_v.at[b], gsem.at[b])

        def write(g, b):
            dst = out_hbm.at[pl.ds(pl.multiple_of(base + g * w, 8), w)]
            return pltpu.make_async_copy(rows_v.at[b], dst, wsem.at[b])

        gather(0, 0).start()

        @pl.loop(0, n_chunks, step=2)
        def _(g):
            for b in (0, 1):
                cur = g + b
                gather(cur, b).wait()
                write(cur, b).start()

                @pl.when(cur >= 1)
                def _():
                    write(cur - 1, 1 - b).wait()

                @pl.when(cur + 1 < n_chunks)
                def _():
                    gather(cur + 1, 1 - b).start()

        write(n_chunks - 1, 1).wait()

    return gather_kernel(table, idx)


def sc_scatter_rows(src, idx, n_out):
    k_dup, t = idx.shape
    d = src.shape[1]
    workers = SC_CORES * SC_SUBCORES
    w = SC_GATHER_ROWS
    per_w, n_chunks = t // workers, t // workers // w
    assert per_w * workers == t and n_chunks * w == per_w and n_chunks % 2 == 0
    mesh = plsc.VectorSubcoreMesh(core_axis_name="c", subcore_axis_name="s")

    @functools.partial(
        pl.kernel, mesh=mesh, out_type=jax.ShapeDtypeStruct((n_out, d), src.dtype),
        scratch_types=[pltpu.VMEM((k_dup, n_chunks, w), I32), pltpu.VMEM((2, w, d), src.dtype),
                       pltpu.SemaphoreType.DMA((2,)), pltpu.SemaphoreType.DMA((2,))])
    def scatter_kernel(src_hbm, idx_hbm, out_hbm, idx_v, rows_v, rsem, wsem):
        wid = lax.axis_index("s") * SC_CORES + lax.axis_index("c")
        base = pl.multiple_of(wid * per_w, 8)
        for k in range(k_dup):
            pltpu.sync_copy(idx_hbm.at[k, wid], idx_v.at[k])

        def read(g, b):
            rows = src_hbm.at[pl.ds(pl.multiple_of(base + g * w, 8), w)]
            return pltpu.make_async_copy(rows, rows_v.at[b], rsem.at[b])

        def write(g, b, k):
            return pltpu.make_async_copy(rows_v.at[b], out_hbm.at[idx_v.at[k, g]], wsem.at[b])

        read(0, 0).start()

        @pl.loop(0, n_chunks, step=2)
        def _(g):
            for b in (0, 1):
                cur = g + b
                read(cur, b).wait()
                for k in range(k_dup):
                    write(cur, b, k).start()

                @pl.when(cur >= 1)
                def _():
                    for k in range(k_dup):
                        write(cur - 1, 1 - b, k).wait()

                @pl.when(cur + 1 < n_chunks)
                def _():
                    read(cur + 1, 1 - b).start()

        for k in range(k_dup):
            write(n_chunks - 1, 1, k).wait()

    return scatter_kernel(src, idx.reshape(k_dup, workers, n_chunks, w))


def _expert_kernel(be_ref, x_ref, wg_ref, bg_ref, wu_ref, bu_ref, wd_ref, bd_ref, o_ref, wg_bf, wu_bf, wd_bf):
    i = pl.program_id(0)

    @pl.when(jnp.logical_or(i == 0, be_ref[i] != be_ref[jnp.maximum(i - 1, 0)]))
    def _():
        wg_bf[...] = wg_ref[0, 0].astype(BF16)
        wu_bf[...] = wu_ref[0, 0].astype(BF16)
        wd_bf[...] = wd_ref[0, 0].astype(BF16)

    xb = _unpack_bf16_halves(x_ref[...]).astype(BF16)
    g = _dot(xb, wg_bf[...]) + bg_ref[0, 0]
    u = _dot(xb, wu_bf[...]) + bu_ref[0, 0]
    g = jnp.minimum(g, SWIGLU_LIMIT)
    u = jnp.clip(u, -SWIGLU_LIMIT, SWIGLU_LIMIT)
    act = (g * _sigmoid(SWIGLU_ALPHA * g) * (u + 1.0)).astype(BF16)
    o_ref[...] = _pack_bf16_halves(_dot(act, wd_bf[...]) + bd_ref[0, 0])


def expert_ffn(block_expert, x_sorted, layer, wg, bg, wu, bu, wd, bd):
    nblk = block_expert.shape[0]
    n_layers, _, d, f = wg.shape
    wspec = lambda r, c: pl.BlockSpec((1, 1, r, c), lambda i, be: (layer, be[i], 0, 0))
    grid_spec = pltpu.PrefetchScalarGridSpec(
        num_scalar_prefetch=1,
        grid=(nblk,),
        in_specs=[pl.BlockSpec((MOE_BLOCK, d // 2), lambda i, be: (i, 0)),
                  wspec(d, f), wspec(1, f), wspec(d, f), wspec(1, f), wspec(f, d), wspec(1, d)],
        out_specs=pl.BlockSpec((MOE_BLOCK, d // 2), lambda i, be: (i, 0)),
        scratch_shapes=[pltpu.VMEM((d, f), BF16), pltpu.VMEM((d, f), BF16), pltpu.VMEM((f, d), BF16)],
    )
    return pl.pallas_call(
        _expert_kernel,
        out_shape=jax.ShapeDtypeStruct((nblk * MOE_BLOCK, d // 2), U32),
        grid_spec=grid_spec,
        compiler_params=_cparams(("arbitrary",)),
        name="expert_ffn",
    )(block_expert, x_sorted, wg, bg.reshape(n_layers, N_EXP, 1, f), wu, bu.reshape(n_layers, N_EXP, 1, f), wd,
      bd.reshape(n_layers, N_EXP, 1, d))


def _combine_kernel(x_ref, mod_ref, prob_ref, y_ref, fw_ref, o_ref, *, final_norm):
    p = prob_ref[0]
    acc = _unpack_bf16_halves(y_ref[0, 0]) * p[:, 0:1]
    for k in range(1, TOP_K):
        acc = acc + _unpack_bf16_halves(y_ref[k, 0]) * p[:, k:k + 1]
    out = x_ref[0] + mod_ref[0, 0][5:6] * acc
    if final_norm:
        out = out * lax.rsqrt(jnp.mean(out * out, axis=-1, keepdims=True) + EPS) * fw_ref[...]
    o_ref[0] = out


def combine(x, modtab, probs, y_tok, final_w, ctx_len, final_norm):
    bsz, lt, d = x.shape
    tm = TOK_TILE
    ctx_tiles = ctx_len // tm
    return pl.pallas_call(
        functools.partial(_combine_kernel, final_norm=final_norm),
        out_shape=jax.ShapeDtypeStruct((bsz, lt, d), F32),
        grid=(bsz, lt // tm),
        in_specs=[pl.BlockSpec((1, tm, d), lambda b, i: (b, i, 0)),
                  pl.BlockSpec((1, 1, N_MOD, d), lambda b, i: (b, (i >= ctx_tiles).astype(I32), 0, 0)),
                  pl.BlockSpec((1, tm, TOP_K), lambda b, i: (b, i, 0)),
                  pl.BlockSpec((TOP_K, 1, tm, d // 2), lambda b, i: (0, b, i, 0)),
                  pl.BlockSpec((1, d), lambda b, i: (0, 0))],
        out_specs=pl.BlockSpec((1, tm, d), lambda b, i: (b, i, 0)),
        compiler_params=_cparams(("parallel", "parallel")),
        name="combine",
    )(x, modtab, probs, y_tok, final_w.reshape(1, d))


def moe_layer(x, modtab, layer, nw, rw, rb, wg, bg, wu, bu, wd, bd, final_w, ctx_len, final_norm):
    bsz, lt, d = x.shape
    t = bsz * lt
    h, idx, probs, rank, counts = router(x, modtab, nw, rw, rb, ctx_len)
    counts = counts.reshape(N_EXP).astype(I32)
    padded = (counts + MOE_BLOCK - 1) // MOE_BLOCK * MOE_BLOCK
    ends = jnp.cumsum(padded)
    pstart = ends - padded
    idx = idx.reshape(t, TOP_K)
    dest = pstart[idx] + rank.reshape(t, TOP_K)
    nblk = -(-t * TOP_K // MOE_BLOCK) + N_EXP
    block_row0 = jnp.arange(nblk, dtype=I32) * MOE_BLOCK
    block_expert = jnp.minimum(jnp.sum((ends[None, :] <= block_row0[:, None]).astype(I32), axis=1), N_EXP - 1)
    dest_k = dest.T
    x_sorted = sc_scatter_rows(h.reshape(t, d // 2), dest_k, nblk * MOE_BLOCK)
    y_sorted = expert_ffn(block_expert, x_sorted, layer, wg, bg, wu, bu, wd, bd)
    y_tok = sc_gather_rows(y_sorted, dest_k.reshape(TOP_K * t)).reshape(TOP_K, bsz, lt, d // 2)
    return combine(x, modtab, probs, y_tok, final_w, ctx_len, final_norm)


def even_layer(x, modtab, norm1_w, w_in, w_out, mu, w0, w2, a0, a2, g2, k_k, k_a, r_k, ln_w, ln_b, lb, hg_norm_w,
               ctx_len):
    w_in = w_in.astype(BF16)
    pr, ph = in_proj(x, modtab, norm1_w, [w_in[:, :RW_PROJ], w_in[:, RW_PROJ:]], ctx_len)
    r, v, kk, g, bv, lw, km, bb = rwkv_prep(pr, mu, w0, w2, a0, a2, g2, k_k, k_a, r_k, ctx_len)
    o_r = rwkv_scan(r, v, kk, lw, km, bb, ctx_len)
    o_g = gla_scan(ph, lb, ctx_len)
    return even_out(x, modtab, o_r, bv, g, o_g, ph, ln_w, ln_b, hg_norm_w, w_out.astype(BF16), ctx_len)


def _dir_slab(cols, width):
    halves = []
    for z in range(2):
        parts = [c[:, z * (c.shape[1] // 2):(z + 1) * (c.shape[1] // 2)] for c in cols]
        used = sum(p.shape[1] for p in parts)
        parts.append(jnp.zeros((cols[0].shape[0], width - used), cols[0].dtype))
        halves.append(jnp.concatenate(parts, axis=1))
    return jnp.concatenate(halves, axis=1)


def _dir_rows(vals, width):
    rows = jnp.concatenate(vals, axis=1)
    return jnp.pad(rows, ((0, 0), (0, width - rows.shape[1]))).reshape(2, 1, width)


def odd_layer(x, modtab, norm1_w, w_in, w_out, s_conv_w, s_conv_b, dt_bias, a_log, d_skip, s_norm_w, m_conv_w,
              m_conv_b, i_bias, f_bias, m_norm_w, ctx_len):
    o = 0
    cols = {}
    for name, width in (("zg", SSD_INNER), ("xbc", SSD_CONV_CH), ("dt", 2 * SSD_H), ("qk", 2 * ML_QK),
                        ("v", ML_VW), ("og", ML_VW), ("ig", 2 * ML_H), ("fg", 2 * ML_H)):
        cols[name] = w_in[:, o:o + width]
        o += width
    slab_m = jnp.concatenate([cols["qk"], cols["v"], cols["og"]], axis=1).astype(BF16)
    slab_small = _dir_slab([cols["dt"], cols["ig"], cols["fg"]], SMALL_W).astype(BF16)
    pxbc, zg, pm, small = in_proj(x, modtab, norm1_w, [cols["xbc"].astype(BF16), cols["zg"].astype(BF16), slab_m,
                                                       slab_small], ctx_len)
    xbc, qk = odd_prep(pxbc, pm, s_conv_w, s_conv_b, m_conv_w, m_conv_b, ctx_len)
    zeros_g = jnp.zeros((2, 2 * ML_H), F32)
    y_s = ssd_scan(xbc, small, _dir_rows([dt_bias, zeros_g], SMALL_W), _dir_rows([a_log, zeros_g], SMALL_W), ctx_len)
    y_m = mlstm_scan(qk, pm, small, _dir_rows([jnp.zeros((2, SSD_H), F32), i_bias, f_bias], SMALL_W), ctx_len)
    return odd_out(x, modtab, y_s, xbc, zg, y_m, pm, jnp.repeat(d_skip, SSD_P), s_norm_w, m_norm_w,
                   w_out.astype(BF16), ctx_len)


def _to_col_major(t):
    b, s, ch = t.shape
    return t.reshape(b, s // GRID_W, GRID_W, ch).transpose(0, 2, 1, 3).reshape(b, s, ch)


def _to_row_major(t):
    b, s, ch = t.shape
    return t.reshape(b, GRID_W, s // GRID_W, ch).transpose(0, 2, 1, 3).reshape(b, s, ch)


def kernel(x, c, ctx, c_ctx, ada_w, ada_b, norm1_w, norm2_w, even_w_in, even_w_out, rwkv_mu, rwkv_w0, rwkv_w2, rwkv_a0, rwkv_a2, rwkv_g2, rwkv_k_k, rwkv_k_a, rwkv_r_k, rwkv_ln_w, rwkv_ln_b, hgrn_lower_bounds, hgrn_norm_w, odd_w_in, odd_w_out, ssd_conv_w, ssd_conv_b, ssd_dt_bias, ssd_a_log, ssd_d, ssd_norm_w, mlstm_conv_w, mlstm_conv_b, mlstm_i_bias, mlstm_f_bias, mlstm_norm_w, router_w, router_b, exp_w_gate, exp_b_gate, exp_w_up, exp_b_up, exp_w_down, exp_b_down, final_norm_w):
    lc = ctx.shape[1]
    lower_bounds = jnp.cumsum(jax.nn.softmax(hgrn_lower_bounds.astype(F32), axis=0), axis=0)
    moe = lambda l: (l, norm2_w[l], router_w[l], router_b[l], exp_w_gate, exp_b_gate, exp_w_up, exp_b_up, exp_w_down,
                     exp_b_down, final_norm_w)
    xa = jnp.concatenate([ctx, x], axis=1)
    modtab = _mod_table(c, c_ctx, ada_w[0], ada_b[0])
    xa = even_layer(xa, modtab, norm1_w[0], even_w_in[0], even_w_out[0], rwkv_mu[0], rwkv_w0[0], rwkv_w2[0],
                    rwkv_a0[0], rwkv_a2[0], rwkv_g2[0], rwkv_k_k[0], rwkv_k_a[0], rwkv_r_k[0], rwkv_ln_w[0],
                    rwkv_ln_b[0], lower_bounds[0], hgrn_norm_w[0], lc)
    xa = moe_layer(xa, modtab, *moe(0), lc, False)
    xa = jnp.concatenate([xa[:, :lc], _to_col_major(xa[:, lc:])], axis=1)
    modtab = _mod_table(c, c_ctx, ada_w[1], ada_b[1])
    xl = odd_layer(xa, modtab, norm1_w[1], odd_w_in[0], odd_w_out[0], ssd_conv_w[0], ssd_conv_b[0], ssd_dt_bias[0],
                   ssd_a_log[0], ssd_d[0], ssd_norm_w[0], mlstm_conv_w[0], mlstm_conv_b[0], mlstm_i_bias[0],
                   mlstm_f_bias[0], mlstm_norm_w[0], lc)
    xl = moe_layer(xl, modtab, *moe(1), 0, True)
    return _to_row_major(xl)
```

```python
import functools
import math

import jax
import jax.numpy as jnp
from jax import lax
from jax.experimental import pallas as pl
from jax.experimental.pallas import tpu as pltpu
from jax.experimental.pallas import tpu_sc as plsc

F32 = jnp.float32
BF16 = jnp.bfloat16
I32 = jnp.int32
U32 = jnp.uint32
HI = lax.Precision.HIGHEST

D_MODEL = 1024
GRID_W = 64
N_MOD = 6
EPS = 1e-6
RW_H, RW_N = 8, 64
RW_W = RW_H * RW_N
RW_RANK = 64
RW_GATE_RANK = 128
RW_GN_EPS = 64e-5
RW_PROJ = 3 * RW_W + 4 * RW_RANK + RW_GATE_RANK
RW_CHUNK = 64
RW_GROUP = 4
HG_H, HG_DK, HG_DV = 4, 128, 128
HG_W = HG_H * HG_DK
HG_CHUNK = 32
HG_PROJ = 5 * HG_W
SSD_H, SSD_P, SSD_G, SSD_N = 16, 64, 2, 128
SSD_INNER = SSD_H * SSD_P
SSD_CONV_CH = SSD_INNER + 2 * SSD_G * SSD_N
SSD_CHUNK = 64
ML_H, ML_DK, ML_DV = 4, 128, 256
ML_QK = ML_H * ML_DK
ML_VW = ML_H * ML_DV
ML_CHUNK = 64
ML_M_INIT = -1e30
CONV_K = 5
N_EXP = 32
TOP_K = 4
D_FF = 1024
SWIGLU_LIMIT = 7.0
SWIGLU_ALPHA = 1.702
MOE_BLOCK = 512

TOK_TILE = 256
SCAN_BLOCK = 256
HALO = 8
SMALL_W = 128
VMEM_LIMIT = 56 * 1024 * 1024
SC_CORES, SC_SUBCORES = 2, 16
SC_GATHER_ROWS = 32


def _cparams(sem):
    return pltpu.CompilerParams(dimension_semantics=sem, vmem_limit_bytes=VMEM_LIMIT)


def _dot(a, b, precision=None):
    return jnp.dot(a, b, preferred_element_type=F32, precision=precision)


def _dot_nt(a, b, precision=None):
    return lax.dot_general(a, b, (((1,), (1,)), ((), ())), preferred_element_type=F32, precision=precision)


def _dot_tn(a, b, precision=None):
    return lax.dot_general(a, b, (((0,), (0,)), ((), ())), preferred_element_type=F32, precision=precision)


def _bf16_parts(x):
    hi = x.astype(BF16)
    rest = x - hi.astype(F32)
    mid = rest.astype(BF16)
    return hi, mid, (rest - mid.astype(F32)).astype(BF16)


def _dot_into_01(x, sel):
    sel = sel.astype(BF16)
    p0, p1, p2 = _bf16_parts(x)
    return _dot(p0, sel) + _dot(p1, sel) + _dot(p2, sel)


def _dot_01_into(sel, x):
    sel = sel.astype(BF16)
    p0, p1, p2 = _bf16_parts(x)
    return _dot(sel, p0) + _dot(sel, p1) + _dot(sel, p2)


def _pack_bf16_halves(x):
    n = x.shape[1] // 2
    bits = lax.bitcast_convert_type(x.astype(BF16).astype(F32), U32)
    return bits[:, :n] | (bits[:, n:] >> 16)


def _unpack_bf16_halves(w):
    hi = lax.bitcast_convert_type(w & jnp.uint32(0xFFFF0000), F32)
    lo = lax.bitcast_convert_type(w << 16, F32)
    return jnp.concatenate([hi, lo], axis=1)


def _sigmoid(x):
    return 1.0 / (1.0 + jnp.exp(-x))


def _silu(x):
    return x * _sigmoid(x)


def _softplus(x):
    return jnp.maximum(x, 0.0) + jnp.log(1.0 + jnp.exp(-jnp.abs(x)))


def _dir_masks(z, c):
    ti = lax.broadcasted_iota(I32, (c, c), 0)
    si = lax.broadcasted_iota(I32, (c, c), 1)
    d = (si - ti) * jnp.where(z == 0, 1, -1)
    return d <= 0, d < 0


def _time_block(z, n, ctx_blocks, n_blocks):
    rev = jnp.where(n < ctx_blocks, ctx_blocks - 1 - n, n_blocks - 1 - (n - ctx_blocks))
    return jnp.where(z == 0, n, rev)


def _mod_kernel(c_ref, w_ref, b_ref, o_ref):
    o_ref[...] = _dot(_silu(c_ref[...]), w_ref[...]) + b_ref[...]


def ada_mod(cc, w, b):
    rows, d = cc.shape
    n = w.shape[1]
    tn = 1536
    return pl.pallas_call(
        _mod_kernel,
        out_shape=jax.ShapeDtypeStruct((rows, n), F32),
        grid=(n // tn,),
        in_specs=[pl.BlockSpec((rows, d), lambda j: (0, 0)),
                  pl.BlockSpec((d, tn), lambda j: (0, j)),
                  pl.BlockSpec((1, tn), lambda j: (0, j))],
        out_specs=pl.BlockSpec((rows, tn), lambda j: (0, j)),
        compiler_params=_cparams(("arbitrary",)),
        name="ada_mod",
    )(cc, w, b.reshape(1, n))


def _mod_table(c, c_ctx, w, b):
    bsz = c.shape[0]
    rows = -(-(bsz + 1) // 8) * 8
    cc = jnp.zeros((rows, D_MODEL), F32).at[:bsz].set(c).at[bsz].set(c_ctx)
    m = ada_mod(cc, w, b)
    mod = m[:bsz].reshape(bsz, 1, N_MOD, D_MODEL)
    mod_c = jnp.broadcast_to(m[bsz].reshape(1, 1, N_MOD, D_MODEL), (bsz, 1, N_MOD, D_MODEL))
    return jnp.concatenate([mod_c, mod], axis=1)


def _norm_mod(x, nw, m, shift_idx):
    y = x * lax.rsqrt(jnp.mean(x * x, axis=-1, keepdims=True) + EPS) * nw
    return y * (1.0 + m[shift_idx + 1:shift_idx + 2]) + m[shift_idx:shift_idx + 1]


def _in_proj_kernel(x_ref, mod_ref, nw_ref, *rest, n_out):
    w_refs, o_refs = rest[:n_out], rest[n_out:]
    h = _norm_mod(x_ref[0], nw_ref[...], mod_ref[0, 0], 0).astype(BF16)
    for w_ref, o_ref in zip(w_refs, o_refs):
        o_ref[0] = _dot(h, w_ref[...])


def in_proj(x, modtab, nw, slabs, ctx_len):
    bsz, lt, d = x.shape
    tm = TOK_TILE
    ctx_tiles = ctx_len // tm
    in_specs = [pl.BlockSpec((1, tm, d), lambda b, i: (b, i, 0)),
                pl.BlockSpec((1, 1, N_MOD, d), lambda b, i: (b, (i >= ctx_tiles).astype(I32), 0, 0)),
                pl.BlockSpec((1, d), lambda b, i: (0, 0))]
    in_specs += [pl.BlockSpec(w.shape, lambda b, i: (0, 0)) for w in slabs]
    return pl.pallas_call(
        functools.partial(_in_proj_kernel, n_out=len(slabs)),
        out_shape=[jax.ShapeDtypeStruct((bsz, lt, w.shape[1]), F32) for w in slabs],
        grid=(bsz, lt // tm),
        in_specs=in_specs,
        out_specs=[pl.BlockSpec((1, tm, w.shape[1]), lambda b, i: (b, i, 0)) for w in slabs],
        compiler_params=_cparams(("parallel", "parallel")),
        name="in_proj",
    )(x, modtab, nw.reshape(1, d), *slabs)


def _halo_specs(tm, width, lt):
    per = tm // HALO
    last = lt // HALO - 1
    return [pl.BlockSpec((1, tm, width), lambda b, i: (b, i, 0)),
            pl.BlockSpec((1, HALO, width), lambda b, i: (b, jnp.maximum(i * per - 1, 0), 0)),
            pl.BlockSpec((1, HALO, width), lambda b, i: (b, jnp.minimum((i + 1) * per, last), 0))]


def _segment_edges(i, tm, ctx_len, lt):
    row0 = i * tm
    has_prev = jnp.logical_and(row0 != 0, row0 != ctx_len)
    has_next = jnp.logical_and(row0 + tm != ctx_len, row0 + tm != lt)
    return has_prev, has_next


def _rwkv_prep_kernel(p_ref, pp_ref, pn_ref, mu_ref, w0_ref, w2_ref, a0_ref, a2_ref, g2_ref, kk_ref, ka_ref,
                      rk_ref, bd_ref, r_o, v_o, kk_o, g_o, bv_o, lw_o, km_o, bb_o, *, tm, ctx_len, lt):
    has_prev, has_next = _segment_edges(pl.program_id(1), tm, ctx_len, lt)
    p = p_ref[0]
    prev_row = jnp.where(has_prev, pp_ref[0, HALO - 1:HALO, :], 0.0)
    next_row = jnp.where(has_next, pn_ref[0, 0:1, :], 0.0)
    rid = lax.broadcasted_iota(I32, (tm, 1), 0)
    up = jnp.where(rid == 0, prev_row, pltpu.roll(p, 1, 0))
    dn = jnp.where(rid == tm - 1, next_row, pltpu.roll(p, tm - 1, 0))
    p = p + (0.5 * (up + dn) - p) * mu_ref[...]
    w = RW_W
    r, k, v = p[:, 0:w], p[:, w:2 * w], p[:, 2 * w:3 * w]
    o = 3 * w
    wd = jnp.tanh(p[:, o:o + 2 * RW_RANK])
    ad = p[:, o + 2 * RW_RANK:o + 4 * RW_RANK]
    gd = p[:, o + 4 * RW_RANK:o + 4 * RW_RANK + RW_GATE_RANK]
    bd = bd_ref[...]
    kk = k * kk_ref[...]
    kk = kk * lax.rsqrt(_dot_into_01(kk * kk, bd) + 1e-12)
    r_o[0] = r
    v_o[0] = v
    kk_o[0] = kk
    g_o[0] = _dot(_sigmoid(gd), g2_ref[...])
    km_sum = jnp.zeros_like(k)
    for z in range(2):
        sl = slice(z * RW_RANK, (z + 1) * RW_RANK)
        w_pre = w0_ref[z:z + 1, :] + _dot(wd[:, sl], w2_ref[z])
        lw_o[z, 0] = -_sigmoid(w_pre) * math.exp(-0.5)
        a = _sigmoid(a0_ref[z:z + 1, :] + _dot(ad[:, sl], a2_ref[z]))
        km = k * (1.0 + (a - 1.0) * ka_ref[...])
        km_o[z, 0] = km
        bb_o[z, 0] = kk * a
        km_sum = km_sum + km
    bv_o[0] = _dot_into_01(r * km_sum * rk_ref[...], bd) * v


def _head_block_ones(width, head):
    i = jnp.arange(width) // head
    return (i[:, None] == i[None, :]).astype(F32)


def rwkv_prep(p, mu, w0, w2, a0, a2, g2, k_k, k_a, r_k, ctx_len):
    bsz, lt, width = p.shape
    tm = TOK_TILE
    w = RW_W
    full = lambda a: pl.BlockSpec(a.shape, lambda b, i: (0,) * a.ndim)
    params = [mu.reshape(1, width), w0, w2, a0, a2, g2, k_k.reshape(1, w), k_a.reshape(1, w), r_k.reshape(1, w),
              _head_block_ones(w, RW_N)]
    one = jax.ShapeDtypeStruct((bsz, lt, w), F32)
    two = jax.ShapeDtypeStruct((2, bsz, lt, w), F32)
    one_spec = pl.BlockSpec((1, tm, w), lambda b, i: (b, i, 0))
    two_spec = pl.BlockSpec((2, 1, tm, w), lambda b, i: (0, b, i, 0))
    return pl.pallas_call(
        functools.partial(_rwkv_prep_kernel, tm=tm, ctx_len=ctx_len, lt=lt),
        out_shape=[one] * 5 + [two] * 3,
        grid=(bsz, lt // tm),
        in_specs=_halo_specs(tm, width, lt) + [full(a) for a in params],
        out_specs=[one_spec] * 5 + [two_spec] * 3,
        compiler_params=_cparams(("parallel", "parallel")),
        name="rwkv_prep",
    )(p, p, p, *params)


def _unit_lower_inverses(n_mats, eye, c):
    xs = [eye - n for n in n_mats]
    ps = [_dot(n, n) for n in n_mats]
    span = 4
    while True:
        xs = [x + _dot(x, p) for x, p in zip(xs, ps)]
        if span >= c:
            return xs
        ps = [_dot(p, p) for p in ps]
        span *= 2


def _rwkv_scan_kernel(r_ref, v_ref, kk_ref, lw_ref, km_ref, bb_ref, o_ref, s_ref, *, tb, c):
    z = pl.program_id(0)
    nsub = tb // c

    @pl.when(pl.program_id(2) == 0)
    def _():
        s_ref[...] = jnp.zeros_like(s_ref)

    incl, strict = _dir_masks(z, c)
    incl_f, strict_f = incl.astype(F32), strict.astype(F32)
    eye = incl_f - strict_f

    heads = [slice(h * RW_N, (h + 1) * RW_N) for h in range(RW_H)]

    def group(g, s_old):
        subs = []
        for i in range(RW_GROUP):
            j = g * RW_GROUP + i
            rows = pl.ds(pl.multiple_of(jnp.where(z == 0, j, nsub - 1 - j) * c, c), c)
            r, v, kk = r_ref[0, rows, :], v_ref[0, rows, :], kk_ref[0, rows, :]
            lw, km, bb = lw_ref[0, 0, rows, :], km_ref[0, 0, rows, :], bb_ref[0, 0, rows, :]
            cum = _dot_01_into(incl_f, lw)
            tot = jnp.sum(lw, axis=0, keepdims=True)
            e_neg = jnp.exp(-cum)
            e_end = jnp.exp(tot - cum)
            a_t = kk * jnp.exp(cum - lw)
            r_t = r * jnp.exp(cum)
            b_t, k_t = bb * e_neg, km * e_neg
            b_p, k_p = bb * e_end, km * e_end
            subs.append(dict(
                rows=rows, e_tot=jnp.exp(tot), a=[a_t[:, hs] for hs in heads], r=[r_t[:, hs] for hs in heads],
                ar=[jnp.concatenate([a_t[:, hs], r_t[:, hs]], axis=0) for hs in heads],
                bk=[jnp.concatenate([b_t[:, hs], k_t[:, hs]], axis=0) for hs in heads],
                bkp=[jnp.concatenate([b_p[:, hs], k_p[:, hs]], axis=0) for hs in heads],
                v=[v[:, hs] for hs in heads]))
        flat = [(i, h) for i in range(RW_GROUP) for h in range(RW_H)]
        sc = [_dot_nt(subs[i]["ar"][h], subs[i]["bk"][h]) for i, h in flat]
        m_ak_v = [_dot(x[:c, c:] * strict_f, subs[i]["v"][h]) for x, (i, h) in zip(sc, flat)]
        inv = _unit_lower_inverses([x[:c, :c] * strict_f for x in sc], eye, c)
        w1r = [jnp.concatenate([_dot(n, subs[i]["a"][h]), subs[i]["r"][h]], axis=0) for n, (i, h) in zip(inv, flat)]
        w2 = [_dot(n, m) for n, m in zip(inv, m_ak_v)]
        n_rbk = [jnp.concatenate([x[c:, :c] * incl_f, x[c:, c:] * incl_f], axis=1) for x in sc]
        state = list(s_old)
        for i in range(RW_GROUP):
            sub = subs[i]
            base = i * RW_H
            ws = [_dot_nt(w1r[base + h], state[h]) for h in range(RW_H)]
            uv = [jnp.concatenate([-(ws[h][:c] + w2[base + h]), sub["v"][h]], axis=0) for h in range(RW_H)]
            outs = [ws[h][c:] + _dot(n_rbk[base + h], uv[h]) for h in range(RW_H)]
            o_ref[0, 0, sub["rows"], :] = jnp.concatenate(outs, axis=1)
            state = [state[h] * sub["e_tot"][:, heads[h]] + _dot_tn(uv[h], sub["bkp"][h]) for h in range(RW_H)]
        return tuple(state)

    s_fin = lax.fori_loop(0, nsub // RW_GROUP, group, tuple(s_ref[h] for h in range(RW_H)))
    for h in range(RW_H):
        s_ref[h] = s_fin[h]


def rwkv_scan(r, v, kk, lw, km, bb, ctx_len):
    bsz, lt, w = r.shape
    tb = SCAN_BLOCK
    nb, cb = lt // tb, ctx_len // tb
    tmap = lambda z, b, n: _time_block(z, n, cb, nb)
    one = pl.BlockSpec((1, tb, w), lambda z, b, n: (b, tmap(z, b, n), 0))
    two = pl.BlockSpec((1, 1, tb, w), lambda z, b, n: (z, b, tmap(z, b, n), 0))
    return pl.pallas_call(
        functools.partial(_rwkv_scan_kernel, tb=tb, c=RW_CHUNK),
        out_shape=jax.ShapeDtypeStruct((2, bsz, lt, w), F32),
        grid=(2, bsz, nb),
        in_specs=[one, one, one, two, two, two],
        out_specs=two,
        scratch_shapes=[pltpu.VMEM((RW_H, RW_N, RW_N), F32)],
        compiler_params=_cparams(("parallel", "parallel", "arbitrary")),
        name="rwkv_scan",
    )(r, v, kk, lw, km, bb)


def _gla_kernel(q_ref, f_ref, i_ref, lb_ref, o_ref, s_ref, *, tb, c):
    z = pl.program_id(0)
    nsub = tb // c

    @pl.when(pl.program_id(2) == 0)
    def _():
        s_ref[...] = jnp.zeros_like(s_ref)

    lb = lb_ref[...]
    heads = [slice(h * HG_DK, (h + 1) * HG_DK) for h in range(HG_H)]
    ti = lax.broadcasted_iota(I32, (tb, tb), 0)
    si = lax.broadcasted_iota(I32, (tb, tb), 1)
    same_chunk = (ti // c) == (si // c)

    def run(zs):
        incl = jnp.logical_and(same_chunk, (si <= ti) if zs == 0 else (si >= ti))
        q = _silu(q_ref[0]) * HG_DK ** -0.5
        f = f_ref[0]
        v = i_ref[0]
        sig_f = _sigmoid(f)
        log_f = jnp.log(lb + (1.0 - lb) * sig_f)
        k = (1.0 - lb) * (1.0 - sig_f)
        cum = _dot_01_into(incl.astype(F32), log_f)
        last = c - 1 if zs == 0 else 0
        tot = jnp.concatenate([jnp.broadcast_to(cum[j * c + last:j * c + last + 1, :], (c, cum.shape[1]))
                               for j in range(nsub)], axis=0)
        q_in = q * jnp.exp(cum)
        k_in = k * jnp.exp(-cum)
        k_end = k * jnp.exp(tot - cum)
        e_tot = jnp.exp(tot)
        intra = [_dot(jnp.where(incl, _dot_nt(q_in[:, hs], k_in[:, hs]), 0.0), v[:, hs]) for hs in heads]
        order = range(nsub) if zs == 0 else range(nsub - 1, -1, -1)
        grow = {(j, h): _dot_tn(v[j * c:(j + 1) * c, hs], k_end[j * c:(j + 1) * c, hs])
                for j in order for h, hs in enumerate(heads)}
        state = [s_ref[h] for h in range(HG_H)]
        inter = {}
        for j in order:
            rows = slice(j * c, (j + 1) * c)
            for h, hs in enumerate(heads):
                inter[(j, h)] = _dot_nt(q_in[rows, hs], state[h])
                state[h] = state[h] * e_tot[j * c:j * c + 1, hs] + grow[(j, h)]
        for h, hs in enumerate(heads):
            o_ref[0, 0, :, hs] = intra[h] + jnp.concatenate([inter[(j, h)] for j in range(nsub)], axis=0)
            s_ref[h] = state[h]

    for zs in range(2):
        pl.when(z == zs)(functools.partial(run, zs))


def gla_scan(ph, lb, ctx_len):
    bsz, lt, _ = ph.shape
    w = HG_W
    tb = SCAN_BLOCK
    nb, cb = lt // tb, ctx_len // tb
    tmap = lambda z, n: _time_block(z, n, cb, nb)
    return pl.pallas_call(
        functools.partial(_gla_kernel, tb=tb, c=HG_CHUNK),
        out_shape=jax.ShapeDtypeStruct((2, bsz, lt, w), F32),
        grid=(2, bsz, nb),
        in_specs=[pl.BlockSpec((1, tb, w), lambda z, b, n: (b, tmap(z, n), 0)),
                  pl.BlockSpec((1, tb, w), lambda z, b, n: (b, tmap(z, n), 1 + z)),
                  pl.BlockSpec((1, tb, w), lambda z, b, n: (b, tmap(z, n), 3)),
                  pl.BlockSpec((1, w), lambda z, b, n: (0, 0))],
        out_specs=pl.BlockSpec((1, 1, tb, w), lambda z, b, n: (z, b, tmap(z, n), 0)),
        scratch_shapes=[pltpu.VMEM((HG_H, HG_DV, HG_DK), F32)],
        compiler_params=_cparams(("parallel", "parallel", "arbitrary")),
        name="gla_scan",
    )(ph, ph, ph, lb.reshape(1, w))


def _group_rms(y, groups, gain):
    width = y.shape[-1] // groups
    parts = []
    for g in range(groups):
        yg = y[:, g * width:(g + 1) * width]
        parts.append(yg * lax.rsqrt(jnp.mean(yg * yg, axis=-1, keepdims=True) + EPS))
    return jnp.concatenate(parts, axis=1) * gain


def _even_out_kernel(x_ref, mod_ref, or_ref, bv_ref, g_ref, og_ref, gate_ref, lnw_ref, lnb_ref, nw_ref, bd_ref,
                     w_ref, o_ref):
    bd = bd_ref[...]
    o = or_ref[0, 0] + or_ref[1, 0]
    mu = _dot_into_01(o, bd) * (1.0 / RW_N)
    oc = o - mu
    var = _dot_into_01(oc * oc, bd) * (1.0 / RW_N)
    yr = (oc * lax.rsqrt(var + RW_GN_EPS) * lnw_ref[...] + lnb_ref[...] + bv_ref[0]) * g_ref[0]
    og = og_ref[0, 0] + og_ref[1, 0]
    yh = _group_rms(og, HG_H, nw_ref[...]) * _silu(gate_ref[0])
    y = jnp.concatenate([yr, yh], axis=1).astype(BF16)
    o_ref[0] = x_ref[0] + mod_ref[0, 0][2:3] * _dot(y, w_ref[...])


def even_out(x, modtab, o_r, bv, g, o_g, ph, ln_w, ln_b, norm_w, w_out, ctx_len):
    bsz, lt, d = x.shape
    tm = TOK_TILE
    ctx_tiles = ctx_len // tm
    w = RW_W
    tok = lambda width, col=0: pl.BlockSpec((1, tm, width), lambda b, i: (b, i, col))
    two = pl.BlockSpec((2, 1, tm, w), lambda b, i: (0, b, i, 0))
    row = lambda width: pl.BlockSpec((1, width), lambda b, i: (0, 0))
    return pl.pallas_call(
        _even_out_kernel,
        out_shape=jax.ShapeDtypeStruct((bsz, lt, d), F32),
        grid=(bsz, lt // tm),
        in_specs=[tok(d),
                  pl.BlockSpec((1, 1, N_MOD, d), lambda b, i: (b, (i >= ctx_tiles).astype(I32), 0, 0)),
                  two, tok(w), tok(w), two, tok(HG_W, 4), row(w), row(w), row(HG_W),
                  pl.BlockSpec((w, w), lambda b, i: (0, 0)),
                  pl.BlockSpec(w_out.shape, lambda b, i: (0, 0))],
        out_specs=tok(d),
        compiler_params=_cparams(("parallel", "parallel")),
        name="even_out",
    )(x, modtab, o_r, bv, g, o_g, ph, ln_w.reshape(1, w), ln_b.reshape(1, w), norm_w.reshape(1, HG_W),
      _head_block_ones(w, RW_N), w_out)


def _conv_silu(x_ref, xp_ref, xn_ref, w_ref, b_ref, has_prev, has_next, tm):
    prev = jnp.where(has_prev, xp_ref[0], 0.0)
    nxt = jnp.where(has_next, xn_ref[0], 0.0)
    ext = jnp.concatenate([prev, x_ref[0], nxt], axis=0)
    rows = tm + 2 * HALO
    pad = CONV_K // 2
    acc = b_ref[...] + jnp.zeros((tm, ext.shape[1]), F32)
    for j in range(CONV_K):
        d = j - pad
        sh = ext if d == 0 else pltpu.roll(ext, (-d) % rows, 0)
        acc = acc + sh[HALO:HALO + tm] * w_ref[j:j + 1, :]
    return _silu(acc)


def _odd_prep_kernel(xs_ref, xsp_ref, xsn_ref, xm_ref, xmp_ref, xmn_ref, ws_ref, bs_ref, wm_ref, bm_ref,
                     os_ref, om_ref, *, tm, ctx_len, lt):
    has_prev, has_next = _segment_edges(pl.program_id(1), tm, ctx_len, lt)
    os_ref[0] = _conv_silu(xs_ref, xsp_ref, xsn_ref, ws_ref, bs_ref, has_prev, has_next, tm)
    om_ref[0] = _conv_silu(xm_ref, xmp_ref, xmn_ref, wm_ref, bm_ref, has_prev, has_next, tm)


def odd_prep(xbc, qk, ws, bs, wm, bm, ctx_len):
    bsz, lt, cs = xbc.shape
    cm = wm.shape[1]
    tm = TOK_TILE
    full = lambda a: pl.BlockSpec(a.shape, lambda b, i: (0,) * a.ndim)
    params = [ws, bs.reshape(1, cs), wm, bm.reshape(1, cm)]
    return pl.pallas_call(
        functools.partial(_odd_prep_kernel, tm=tm, ctx_len=ctx_len, lt=lt),
        out_shape=[jax.ShapeDtypeStruct((bsz, lt, cs), F32), jax.ShapeDtypeStruct((bsz, lt, cm), F32)],
        grid=(bsz, lt // tm),
        in_specs=_halo_specs(tm, cs, lt) + _halo_specs(tm, cm, lt) + [full(a) for a in params],
        out_specs=[pl.BlockSpec((1, tm, cs), lambda b, i: (b, i, 0)),
                   pl.BlockSpec((1, tm, cm), lambda b, i: (b, i, 0))],
        compiler_params=_cparams(("parallel", "parallel")),
        name="odd_prep",
    )(xbc, xbc, xbc, qk, qk, qk, *params)


def _ssd_kernel(x_ref, b_ref, c_ref, sm_ref, dtb_ref, alog_ref, ex_ref, o_ref, s_ref, *, tb, c):
    z = pl.program_id(0)
    nsub = tb // c
    hg = SSD_H // SSD_G

    @pl.when(pl.program_id(2) == 0)
    def _():
        s_ref[...] = jnp.zeros_like(s_ref)

    incl, _ = _dir_masks(z, c)
    incl_f = incl.astype(F32)
    ex = ex_ref[...]

    gw = hg * SSD_P

    def body(j, s_old):
        jj = jnp.where(z == 0, j, nsub - 1 - j)
        rows = pl.ds(pl.multiple_of(jj * c, c), c)
        x = x_ref[0, rows, :]
        bm = b_ref[0, rows, :]
        cm = c_ref[0, rows, :]
        dt = _softplus(sm_ref[0, rows, :] + dtb_ref[0])
        la = -dt * jnp.exp(alog_ref[0])
        a_cum = _dot_01_into(incl_f, la)
        a_tot = jnp.sum(la, axis=0, keepdims=True)
        a_cum_t = a_cum.T
        e_tot_col = jnp.exp(jnp.where(z == 0, a_cum_t[:, c - 1:c], a_cum_t[:, 0:1]))
        wide = _dot(jnp.concatenate([dt, jnp.exp(a_cum), jnp.exp(a_tot - a_cum)], axis=0).astype(BF16),
                    ex.astype(BF16))
        xdt = x * wide[:c]
        din, dend = wide[c:2 * c], wide[2 * c:]
        xd_end = xdt * dend
        bgs = [bm[:, g * SSD_N:(g + 1) * SSD_N] for g in range(SSD_G)]
        cgs = [cm[:, g * SSD_N:(g + 1) * SSD_N] for g in range(SSD_G)]
        cb = [_dot_nt(cg, bg) for cg, bg in zip(cgs, bgs)]
        inter = [_dot_nt(cg, s) for cg, s in zip(cgs, s_old)]
        grow = [_dot_tn(xd_end[:, g * gw:(g + 1) * gw], bg) for g, bg in enumerate(bgs)]
        intra = []
        for h in range(SSD_H):
            seg = a_cum[:, h:h + 1] - a_cum_t[h:h + 1, :]
            wmat = cb[h // hg] * jnp.where(incl, jnp.exp(seg), 0.0)
            intra.append(_dot(wmat, xdt[:, h * SSD_P:(h + 1) * SSD_P]))
        o_ref[0, 0, rows, :] = jnp.concatenate(intra, axis=1) + din * jnp.concatenate(inter, axis=1)
        new = []
        for g in range(SSD_G):
            decay = jnp.concatenate([jnp.broadcast_to(e_tot_col[h:h + 1, :], (SSD_P, 1))
                                     for h in range(g * hg, (g + 1) * hg)], axis=0)
            new.append(s_old[g] * decay + grow[g])
        return tuple(new)

    s_fin = lax.fori_loop(0, nsub, body, tuple(s_ref[g] for g in range(SSD_G)), unroll=2)
    for g in range(SSD_G):
        s_ref[g] = s_fin[g]


def _head_expand(heads, width, rows):
    h = jnp.arange(rows)[:, None]
    col = jnp.arange(heads * width)[None, :] // width
    return (h == col).astype(F32)


def ssd_scan(xbc, small, dtb, alog, ctx_len):
    bsz, lt, _ = xbc.shape
    tb = SCAN_BLOCK
    nb, cb = lt // tb, ctx_len // tb
    tmap = lambda z, n: _time_block(z, n, cb, nb)
    gw = SSD_G * SSD_N
    return pl.pallas_call(
        functools.partial(_ssd_kernel, tb=tb, c=SSD_CHUNK),
        out_shape=jax.ShapeDtypeStruct((2, bsz, lt, SSD_INNER), F32),
        grid=(2, bsz, nb),
        in_specs=[pl.BlockSpec((1, tb, SSD_INNER), lambda z, b, n: (b, tmap(z, n), 0)),
                  pl.BlockSpec((1, tb, gw), lambda z, b, n: (b, tmap(z, n), SSD_INNER // gw)),
                  pl.BlockSpec((1, tb, gw), lambda z, b, n: (b, tmap(z, n), SSD_INNER // gw + 1)),
                  pl.BlockSpec((1, tb, SMALL_W), lambda z, b, n: (b, tmap(z, n), z)),
                  pl.BlockSpec((1, 1, SMALL_W), lambda z, b, n: (z, 0, 0)),
                  pl.BlockSpec((1, 1, SMALL_W), lambda z, b, n: (z, 0, 0)),
                  pl.BlockSpec((SMALL_W, SSD_INNER), lambda z, b, n: (0, 0))],
        out_specs=pl.BlockSpec((1, 1, tb, SSD_INNER), lambda z, b, n: (z, b, tmap(z, n), 0)),
        scratch_shapes=[pltpu.VMEM((SSD_G, SSD_H // SSD_G * SSD_P, SSD_N), F32)],
        compiler_params=_cparams(("parallel", "parallel", "arbitrary")),
        name="ssd_scan",
    )(xbc, xbc, xbc, small, dtb, alog, _head_expand(SSD_H, SSD_P, SMALL_W))


ML_I_LANE = SSD_H
ML_F_LANE = SSD_H + ML_H


def _mlstm_kernel(q_ref, k_ref, v_ref, sm_ref, gb_ref, o_ref, cs_ref, ns_ref, m_ref, *, tb, c):
    z = pl.program_id(0)
    nsub = tb // c

    @pl.when(pl.program_id(2) == 0)
    def _():
        cs_ref[...] = jnp.zeros_like(cs_ref)
        ns_ref[...] = jnp.zeros_like(ns_ref)
        m_ref[...] = jnp.full(m_ref.shape, ML_M_INIT, F32)

    incl, _ = _dir_masks(z, c)
    incl_f = incl.astype(F32)
    lane = lax.broadcasted_iota(I32, (1, SMALL_W), 1)
    is_f = jnp.logical_and(lane >= ML_F_LANE, lane < ML_F_LANE + ML_H)

    def local(j):
        rows = pl.ds(pl.multiple_of(jnp.where(z == 0, j, nsub - 1 - j) * c, c), c)
        q = q_ref[0, rows, :]
        k = k_ref[0, rows, :] * ML_DK ** -0.5
        v = v_ref[0, rows, :]
        pre = sm_ref[0, rows, :] + gb_ref[0]
        gates = jnp.where(is_f, -_softplus(-pre), pre)
        f_cum = _dot(incl_f, gates, HI)
        f_tot = jnp.sum(gates, axis=0, keepdims=True)
        f_cum_t = f_cum.T
        gates_t = gates.T
        per_head = []
        for h in range(ML_H):
            li, lf = ML_I_LANE + h, ML_F_LANE + h
            qh = q[:, h * ML_DK:(h + 1) * ML_DK]
            kh = k[:, h * ML_DK:(h + 1) * ML_DK]
            vh = v[:, h * ML_DV:(h + 1) * ML_DV]
            f_col = f_cum[:, lf:lf + 1]
            f_last = f_tot[:, lf:lf + 1]
            logw_end = f_last - f_col + gates[:, li:li + 1]
            m_end = jnp.max(logw_end, axis=0, keepdims=True)
            kw = kh * jnp.exp(logw_end - m_end)
            logw = jnp.where(incl, f_col - f_cum_t[lf:lf + 1, :] + gates_t[li:li + 1, :], -jnp.inf)
            m_loc = jnp.max(logw, axis=1, keepdims=True)
            scores = _dot_nt(qh, kh) * jnp.exp(logw - m_loc)
            per_head.append(dict(
                q=qh, f_col=f_col, f_last=f_last, m_end=m_end, m_loc=m_loc, num=_dot(scores, vh),
                den=jnp.sum(scores, axis=1, keepdims=True), kv=_dot_tn(kw, vh),
                ksum=jnp.sum(kw, axis=0, keepdims=True)))
        return rows, per_head

    subs = [local(j) for j in range(nsub)]
    cs = [cs_ref[h] for h in range(ML_H)]
    ns = [ns_ref[h] for h in range(ML_H)]
    m = [m_ref[h] for h in range(ML_H)]
    for rows, per_head in subs:
        outs = []
        for h, p in enumerate(per_head):
            m_t = jnp.maximum(p["m_loc"], p["f_col"] + m[h])
            w_loc = jnp.exp(p["m_loc"] - m_t)
            w_inter = jnp.exp(p["f_col"] + m[h] - m_t)
            num = p["num"] * w_loc + w_inter * _dot(p["q"], cs[h])
            den = p["den"] * w_loc + w_inter * jnp.sum(p["q"] * ns[h], axis=1, keepdims=True)
            outs.append(num / jnp.maximum(jnp.abs(den), jnp.exp(-m_t)))
            m_new = jnp.maximum(p["f_last"] + m[h], p["m_end"])
            s_keep = jnp.exp(p["f_last"] + m[h] - m_new)
            s_loc = jnp.exp(p["m_end"] - m_new)
            cs[h] = s_keep * cs[h] + s_loc * p["kv"]
            ns[h] = s_keep * ns[h] + s_loc * p["ksum"]
            m[h] = m_new
        o_ref[0, 0, rows, :] = jnp.concatenate(outs, axis=1)
    for h in range(ML_H):
        cs_ref[h] = cs[h]
        ns_ref[h] = ns[h]
        m_ref[h] = m[h]


def mlstm_scan(qk, pm, small, gate_bias, ctx_len):
    bsz, lt, _ = qk.shape
    tb = SCAN_BLOCK
    nb, cb = lt // tb, ctx_len // tb
    tmap = lambda z, n: _time_block(z, n, cb, nb)
    return pl.pallas_call(
        functools.partial(_mlstm_kernel, tb=tb, c=ML_CHUNK),
        out_shape=jax.ShapeDtypeStruct((2, bsz, lt, ML_VW), F32),
        grid=(2, bsz, nb),
        in_specs=[pl.BlockSpec((1, tb, ML_QK), lambda z, b, n: (b, tmap(z, n), 0)),
                  pl.BlockSpec((1, tb, ML_QK), lambda z, b, n: (b, tmap(z, n), 1)),
                  pl.BlockSpec((1, tb, ML_VW), lambda z, b, n: (b, tmap(z, n), 1)),
                  pl.BlockSpec((1, tb, SMALL_W), lambda z, b, n: (b, tmap(z, n), z)),
                  pl.BlockSpec((1, 1, SMALL_W), lambda z, b, n: (z, 0, 0))],
        out_specs=pl.BlockSpec((1, 1, tb, ML_VW), lambda z, b, n: (z, b, tmap(z, n), 0)),
        scratch_shapes=[pltpu.VMEM((ML_H, ML_DK, ML_DV), F32), pltpu.VMEM((ML_H, 1, ML_DK), F32),
                        pltpu.VMEM((ML_H, 1, 1), F32)],
        compiler_params=_cparams(("parallel", "parallel", "arbitrary")),
        name="mlstm_scan",
    )(qk, qk, pm, small, gate_bias)


def _odd_out_kernel(x_ref, mod_ref, ys_ref, xc_ref, zg_ref, ym_ref, og_ref, dsk_ref, nws_ref, nwm_ref, w_ref, o_ref):
    y = (ys_ref[0, 0] + ys_ref[1, 0] + dsk_ref[...] * xc_ref[0]) * _silu(zg_ref[0])
    ys = _group_rms(y, SSD_G, nws_ref[...])
    hm = ym_ref[0, 0] + ym_ref[1, 0]
    ym = _group_rms(hm, ML_H, nwm_ref[...]) * _sigmoid(og_ref[0])
    yy = jnp.concatenate([ys, ym], axis=1).astype(BF16)
    o_ref[0] = x_ref[0] + mod_ref[0, 0][2:3] * _dot(yy, w_ref[...])


def odd_out(x, modtab, y_s, xbc, ps, y_m, pm, d_skip, nw_s, nw_m, w_out, ctx_len):
    bsz, lt, d = x.shape
    tm = TOK_TILE
    ct = ctx_len // tm
    seq = lt - ctx_len
    tok = lambda width, col=0: pl.BlockSpec((1, tm, width), lambda b, i: (b, i + ct, col))
    two = lambda width: pl.BlockSpec((2, 1, tm, width), lambda b, i: (0, b, i + ct, 0))
    row = lambda width: pl.BlockSpec((1, width), lambda b, i: (0, 0))
    return pl.pallas_call(
        _odd_out_kernel,
        out_shape=jax.ShapeDtypeStruct((bsz, seq, d), F32),
        grid=(bsz, seq // tm),
        in_specs=[tok(d), pl.BlockSpec((1, 1, N_MOD, d), lambda b, i: (b, 1, 0, 0)),
                  two(SSD_INNER), tok(SSD_INNER), tok(SSD_INNER), two(ML_VW), tok(ML_VW, 2),
                  row(SSD_INNER), row(SSD_INNER), row(ML_VW),
                  pl.BlockSpec(w_out.shape, lambda b, i: (0, 0))],
        out_specs=pl.BlockSpec((1, tm, d), lambda b, i: (b, i, 0)),
        compiler_params=_cparams(("parallel", "parallel")),
        name="odd_out",
    )(x, modtab, y_s, xbc, ps, y_m, pm, d_skip.reshape(1, SSD_INNER), nw_s.reshape(1, SSD_INNER),
      nw_m.reshape(1, ML_VW), w_out)


def _router_kernel(x_ref, mod_ref, nw_ref, rw_ref, rb_ref, h_ref, idx_ref, prob_ref, rank_ref, cnt_ref, carry_ref,
                   *, tm):
    first = jnp.logical_and(pl.program_id(0) == 0, pl.program_id(1) == 0)

    @pl.when(first)
    def _():
        carry_ref[...] = jnp.zeros_like(carry_ref)

    h = _norm_mod(x_ref[0], nw_ref[...], mod_ref[0, 0], 3)
    h_ref[0] = _pack_bf16_halves(h)
    logits = _dot(h, rw_ref[...], HI) + rb_ref[...]
    lane = lax.broadcasted_iota(I32, (tm, N_EXP), 1).astype(F32)
    work = logits
    sel = jnp.zeros((tm, N_EXP), F32)
    vals, idxs, hots = [], [], []
    for _ in range(TOP_K):
        m = jnp.max(work, axis=1, keepdims=True)
        idx = jnp.min(jnp.where(work == m, lane, float(N_EXP)), axis=1, keepdims=True)
        hot = lane == idx
        vals.append(m)
        idxs.append(idx)
        hots.append(hot.astype(F32))
        sel = sel + hots[-1]
        work = jnp.where(hot, -jnp.inf, work)
    ex = [jnp.exp(v - vals[0]) for v in vals]
    tot = ex[0] + ex[1] + ex[2] + ex[3]
    ti = lax.broadcasted_iota(I32, (tm, tm), 0)
    si = lax.broadcasted_iota(I32, (tm, tm), 1)
    before = _dot((si < ti).astype(BF16), sel.astype(BF16)) + carry_ref[...]
    ranks = [jnp.sum(before * hot, axis=1, keepdims=True) for hot in hots]
    idx_ref[0] = jnp.concatenate(idxs, axis=1).astype(I32)
    prob_ref[0] = jnp.concatenate([e / tot for e in ex], axis=1)
    rank_ref[0] = jnp.concatenate(ranks, axis=1).astype(I32)
    carry_ref[...] = carry_ref[...] + jnp.sum(sel, axis=0, keepdims=True)
    cnt_ref[...] = carry_ref[...]


def router(x, modtab, nw, rw, rb, ctx_len):
    bsz, lt, d = x.shape
    tm = TOK_TILE
    ctx_tiles = ctx_len // tm
    tok = lambda width, dt: (jax.ShapeDtypeStruct((bsz, lt, width), dt),
                             pl.BlockSpec((1, tm, width), lambda b, i: (b, i, 0)))
    outs = [tok(d // 2, U32), tok(TOP_K, I32), tok(TOP_K, F32), tok(TOP_K, I32),
            (jax.ShapeDtypeStruct((1, N_EXP), F32), pl.BlockSpec((1, N_EXP), lambda b, i: (0, 0)))]
    return pl.pallas_call(
        functools.partial(_router_kernel, tm=tm),
        out_shape=[o[0] for o in outs],
        grid=(bsz, lt // tm),
        in_specs=[pl.BlockSpec((1, tm, d), lambda b, i: (b, i, 0)),
                  pl.BlockSpec((1, 1, N_MOD, d), lambda b, i: (b, (i >= ctx_tiles).astype(I32), 0, 0)),
                  pl.BlockSpec((1, d), lambda b, i: (0, 0)),
                  pl.BlockSpec((d, N_EXP), lambda b, i: (0, 0)),
                  pl.BlockSpec((1, N_EXP), lambda b, i: (0, 0))],
        out_specs=[o[1] for o in outs],
        scratch_shapes=[pltpu.VMEM((1, N_EXP), F32)],
        compiler_params=_cparams(("arbitrary", "arbitrary")),
        name="router",
    )(x, modtab, nw.reshape(1, d), rw, rb.reshape(1, N_EXP))


def sc_gather_rows(table, idx):
    m = idx.shape[0]
    d = table.shape[1]
    workers = SC_CORES * SC_SUBCORES
    per_w, n_chunks = m // workers, m // workers // SC_GATHER_ROWS
    assert per_w * workers == m and n_chunks * SC_GATHER_ROWS == per_w and n_chunks % 2 == 0
    w = SC_GATHER_ROWS
    mesh = plsc.VectorSubcoreMesh(core_axis_name="c", subcore_axis_name="s")

    @functools.partial(
        pl.kernel, mesh=mesh, out_type=jax.ShapeDtypeStruct((m, d), table.dtype),
        scratch_types=[pltpu.VMEM((per_w,), I32), pltpu.VMEM((2, w, d), table.dtype),
                       pltpu.SemaphoreType.DMA((2,)), pltpu.SemaphoreType.DMA((2,))])
    def gather_kernel(table_hbm, idx_hbm, out_hbm, idx_v, rows_v, gsem, wsem):
        wid = lax.axis_index("s") * SC_CORES + lax.axis_index("c")
        base = pl.multiple_of(wid * per_w, 8)
        pltpu.sync_copy(idx_hbm.at[pl.ds(base, per_w)], idx_v)

        def gather(g, b):
            rows = idx_v.at[pl.ds(pl.multiple_of(g * w, 8), w)]
            return pltpu.make_async_copy(table_hbm.at[rows], rows_v.at[b], gsem.at[b])

        def write(g, b):
            dst = out_hbm.at[pl.ds(pl.multiple_of(base + g * w, 8), w)]
            return pltpu.make_async_copy(rows_v.at[b], dst, wsem.at[b])

        gather(0, 0).start()

        @pl.loop(0, n_chunks, step=2)
        def _(g):
            for b in (0, 1):
                cur = g + b
                gather(cur, b).wait()
                write(cur, b).start()

                @pl.when(cur >= 1)
                def _():
                    write(cur - 1, 1 - b).wait()

                @pl.when(cur + 1 < n_chunks)
                def _():
                    gather(cur + 1, 1 - b).start()

        write(n_chunks - 1, 1).wait()

    return gather_kernel(table, idx)


def sc_scatter_rows(src, idx, n_out):
    k_dup, t = idx.shape
    d = src.shape[1]
    workers = SC_CORES * SC_SUBCORES
    w = SC_GATHER_ROWS
    per_w, n_chunks = t // workers, t // workers // w
    assert per_w * workers == t and n_chunks * w == per_w and n_chunks % 2 == 0
    mesh = plsc.VectorSubcoreMesh(core_axis_name="c", subcore_axis_name="s")

    @functools.partial(
        pl.kernel, mesh=mesh, out_type=jax.ShapeDtypeStruct((n_out, d), src.dtype),
        scratch_types=[pltpu.VMEM((k_dup, n_chunks, w), I32), pltpu.VMEM((2, w, d), src.dtype),
                       pltpu.SemaphoreType.DMA((2,)), pltpu.SemaphoreType.DMA((2,))])
    def scatter_kernel(src_hbm, idx_hbm, out_hbm, idx_v, rows_v, rsem, wsem):
        wid = lax.axis_index("s") * SC_CORES + lax.axis_index("c")
        base = pl.multiple_of(wid * per_w, 8)
        for k in range(k_dup):
            pltpu.sync_copy(idx_hbm.at[k, wid], idx_v.at[k])

        def read(g, b):
            rows = src_hbm.at[pl.ds(pl.multiple_of(base + g * w, 8), w)]
            return pltpu.make_async_copy(rows, rows_v.at[b], rsem.at[b])

        def write(g, b, k):
            return pltpu.make_async_copy(rows_v.at[b], out_hbm.at[idx_v.at[k, g]], wsem.at[b])

        read(0, 0).start()

        @pl.loop(0, n_chunks, step=2)
        def _(g):
            for b in (0, 1):
                cur = g + b
                read(cur, b).wait()
                for k in range(k_dup):
                    write(cur, b, k).start()

                @pl.when(cur >= 1)
                def _():
                    for k in range(k_dup):
                        write(cur - 1, 1 - b, k).wait()

                @pl.when(cur + 1 < n_chunks)
                def _():
                    read(cur + 1, 1 - b).start()

        for k in range(k_dup):
            write(n_chunks - 1, 1, k).wait()

    return scatter_kernel(src, idx.reshape(k_dup, workers, n_chunks, w))


def _expert_kernel(be_ref, used_ref, x_ref, wg_ref, bg_ref, wu_ref, bu_ref, wd_ref, bd_ref, o_ref, wg_bf, wu_bf,
                   wd_bf):
    i = pl.program_id(0)

    @pl.when(jnp.logical_or(i == 0, be_ref[i] != be_ref[jnp.maximum(i - 1, 0)]))
    def _():
        wg_bf[...] = wg_ref[0, 0].astype(BF16)
        wu_bf[...] = wu_ref[0, 0].astype(BF16)
        wd_bf[...] = wd_ref[0, 0].astype(BF16)

    @pl.when(i < used_ref[0])
    def _():
        xb = _unpack_bf16_halves(x_ref[...]).astype(BF16)
        g = _dot(xb, wg_bf[...]) + bg_ref[0, 0]
        u = _dot(xb, wu_bf[...]) + bu_ref[0, 0]
        g = jnp.minimum(g, SWIGLU_LIMIT)
        u = jnp.clip(u, -SWIGLU_LIMIT, SWIGLU_LIMIT)
        act = (g * _sigmoid(SWIGLU_ALPHA * g) * (u + 1.0)).astype(BF16)
        o_ref[...] = _pack_bf16_halves(_dot(act, wd_bf[...]) + bd_ref[0, 0])


def expert_ffn(block_expert, blocks_used, x_sorted, layer, wg, bg, wu, bu, wd, bd):
    nblk = block_expert.shape[0]
    n_layers, _, d, f = wg.shape
    wspec = lambda r, c: pl.BlockSpec((1, 1, r, c), lambda i, be, used: (layer, be[i], 0, 0))
    grid_spec = pltpu.PrefetchScalarGridSpec(
        num_scalar_prefetch=2,
        grid=(nblk,),
        in_specs=[pl.BlockSpec((MOE_BLOCK, d // 2), lambda i, be, used: (i, 0)),
                  wspec(d, f), wspec(1, f), wspec(d, f), wspec(1, f), wspec(f, d), wspec(1, d)],
        out_specs=pl.BlockSpec((MOE_BLOCK, d // 2), lambda i, be, used: (i, 0)),
        scratch_shapes=[pltpu.VMEM((d, f), BF16), pltpu.VMEM((d, f), BF16), pltpu.VMEM((f, d), BF16)],
    )
    return pl.pallas_call(
        _expert_kernel,
        out_shape=jax.ShapeDtypeStruct((nblk * MOE_BLOCK, d // 2), U32),
        grid_spec=grid_spec,
        compiler_params=_cparams(("arbitrary",)),
        name="expert_ffn",
    )(block_expert, blocks_used, x_sorted, wg, bg.reshape(n_layers, N_EXP, 1, f), wu,
      bu.reshape(n_layers, N_EXP, 1, f), wd, bd.reshape(n_layers, N_EXP, 1, d))


def _combine_kernel(x_ref, mod_ref, prob_ref, y_ref, fw_ref, o_ref, *, final_norm, grid_rows):
    p = prob_ref[0]
    acc = _unpack_bf16_halves(y_ref[0, 0]) * p[:, 0:1]
    for k in range(1, TOP_K):
        acc = acc + _unpack_bf16_halves(y_ref[k, 0]) * p[:, k:k + 1]
    out = x_ref[0] + mod_ref[0, 0][5:6] * acc
    if final_norm:
        out = out * lax.rsqrt(jnp.mean(out * out, axis=-1, keepdims=True) + EPS) * fw_ref[...]
    if grid_rows:
        out = jnp.concatenate([out[c * grid_rows:(c + 1) * grid_rows] for c in range(out.shape[0] // grid_rows)],
                              axis=1)
    o_ref[0] = out


def combine(x, modtab, probs, y_tok, final_w, ctx_len, final_norm):
    bsz, lt, d = x.shape
    tm = TOK_TILE
    ctx_tiles = ctx_len // tm
    grid_rows = lt // GRID_W if final_norm else 0
    if grid_rows:
        assert tm % grid_rows == 0
        cols = tm // grid_rows
        out_shape = jax.ShapeDtypeStruct((bsz, grid_rows, GRID_W * d), F32)
        out_spec = pl.BlockSpec((1, grid_rows, cols * d), lambda b, i: (b, 0, i))
    else:
        out_shape = jax.ShapeDtypeStruct((bsz, lt, d), F32)
        out_spec = pl.BlockSpec((1, tm, d), lambda b, i: (b, i, 0))
    out = pl.pallas_call(
        functools.partial(_combine_kernel, final_norm=final_norm, grid_rows=grid_rows),
        out_shape=out_shape,
        grid=(bsz, lt // tm),
        in_specs=[pl.BlockSpec((1, tm, d), lambda b, i: (b, i, 0)),
                  pl.BlockSpec((1, 1, N_MOD, d), lambda b, i: (b, (i >= ctx_tiles).astype(I32), 0, 0)),
                  pl.BlockSpec((1, tm, TOP_K), lambda b, i: (b, i, 0)),
                  pl.BlockSpec((TOP_K, 1, tm, d // 2), lambda b, i: (0, b, i, 0)),
                  pl.BlockSpec((1, d), lambda b, i: (0, 0))],
        out_specs=out_spec,
        compiler_params=_cparams(("parallel", "parallel")),
        name="combine",
    )(x, modtab, probs, y_tok, final_w.reshape(1, d))
    return out.reshape(bsz, lt, d)


def moe_layer(x, modtab, layer, nw, rw, rb, wg, bg, wu, bu, wd, bd, final_w, ctx_len, final_norm):
    bsz, lt, d = x.shape
    t = bsz * lt
    h, idx, probs, rank, counts = router(x, modtab, nw, rw, rb, ctx_len)
    counts = counts.reshape(N_EXP).astype(I32)
    padded = (counts + MOE_BLOCK - 1) // MOE_BLOCK * MOE_BLOCK
    ends = jnp.cumsum(padded)
    pstart = ends - padded
    idx = idx.reshape(t, TOP_K)
    dest = pstart[idx] + rank.reshape(t, TOP_K)
    nblk = -(-t * TOP_K // MOE_BLOCK) + N_EXP
    block_row0 = jnp.arange(nblk, dtype=I32) * MOE_BLOCK
    block_expert = jnp.minimum(jnp.sum((ends[None, :] <= block_row0[:, None]).astype(I32), axis=1), N_EXP - 1)
    dest_k = dest.T
    x_sorted = sc_scatter_rows(h.reshape(t, d // 2), dest_k, nblk * MOE_BLOCK)
    blocks_used = (ends[-1:] // MOE_BLOCK).astype(I32)
    y_sorted = expert_ffn(block_expert, blocks_used, x_sorted, layer, wg, bg, wu, bu, wd, bd)
    y_tok = sc_gather_rows(y_sorted, dest_k.reshape(TOP_K * t)).reshape(TOP_K, bsz, lt, d // 2)
    return combine(x, modtab, probs, y_tok, final_w, ctx_len, final_norm)


def even_layer(x, modtab, norm1_w, w_in, w_out, mu, w0, w2, a0, a2, g2, k_k, k_a, r_k, ln_w, ln_b, lb, hg_norm_w,
               ctx_len):
    w_in = w_in.astype(BF16)
    pr, ph = in_proj(x, modtab, norm1_w, [w_in[:, :RW_PROJ], w_in[:, RW_PROJ:]], ctx_len)
    r, v, kk, g, bv, lw, km, bb = rwkv_prep(pr, mu, w0, w2, a0, a2, g2, k_k, k_a, r_k, ctx_len)
    o_r = rwkv_scan(r, v, kk, lw, km, bb, ctx_len)
    o_g = gla_scan(ph, lb, ctx_len)
    return even_out(x, modtab, o_r, bv, g, o_g, ph, ln_w, ln_b, hg_norm_w, w_out.astype(BF16), ctx_len)


def _dir_slab(cols, width):
    halves = []
    for z in range(2):
        parts = [c[:, z * (c.shape[1] // 2):(z + 1) * (c.shape[1] // 2)] for c in cols]
        used = sum(p.shape[1] for p in parts)
        parts.append(jnp.zeros((cols[0].shape[0], width - used), cols[0].dtype))
        halves.append(jnp.concatenate(parts, axis=1))
    return jnp.concatenate(halves, axis=1)


def _dir_rows(vals, width):
    rows = jnp.concatenate(vals, axis=1)
    return jnp.pad(rows, ((0, 0), (0, width - rows.shape[1]))).reshape(2, 1, width)


def odd_layer(x, modtab, norm1_w, w_in, w_out, s_conv_w, s_conv_b, dt_bias, a_log, d_skip, s_norm_w, m_conv_w,
              m_conv_b, i_bias, f_bias, m_norm_w, ctx_len):
    o = 0
    cols = {}
    for name, width in (("zg", SSD_INNER), ("xbc", SSD_CONV_CH), ("dt", 2 * SSD_H), ("qk", 2 * ML_QK),
                        ("v", ML_VW), ("og", ML_VW), ("ig", 2 * ML_H), ("fg", 2 * ML_H)):
        cols[name] = w_in[:, o:o + width]
        o += width
    slab_m = jnp.concatenate([cols["qk"], cols["v"], cols["og"]], axis=1).astype(BF16)
    slab_small = _dir_slab([cols["dt"], cols["ig"], cols["fg"]], SMALL_W).astype(BF16)
    pxbc, zg, pm, small = in_proj(x, modtab, norm1_w, [cols["xbc"].astype(BF16), cols["zg"].astype(BF16), slab_m,
                                                       slab_small], ctx_len)
    xbc, qk = odd_prep(pxbc, pm, s_conv_w, s_conv_b, m_conv_w, m_conv_b, ctx_len)
    zeros_g = jnp.zeros((2, 2 * ML_H), F32)
    y_s = ssd_scan(xbc, small, _dir_rows([dt_bias, zeros_g], SMALL_W), _dir_rows([a_log, zeros_g], SMALL_W), ctx_len)
    y_m = mlstm_scan(qk, pm, small, _dir_rows([jnp.zeros((2, SSD_H), F32), i_bias, f_bias], SMALL_W), ctx_len)
    return odd_out(x, modtab, y_s, xbc, zg, y_m, pm, jnp.repeat(d_skip, SSD_P), s_norm_w, m_norm_w,
                   w_out.astype(BF16), ctx_len)


def _to_col_major(t):
    b, s, ch = t.shape
    return t.reshape(b, s // GRID_W, GRID_W, ch).transpose(0, 2, 1, 3).reshape(b, s, ch)


def kernel(x, c, ctx, c_ctx, ada_w, ada_b, norm1_w, norm2_w, even_w_in, even_w_out, rwkv_mu, rwkv_w0, rwkv_w2, rwkv_a0, rwkv_a2, rwkv_g2, rwkv_k_k, rwkv_k_a, rwkv_r_k, rwkv_ln_w, rwkv_ln_b, hgrn_lower_bounds, hgrn_norm_w, odd_w_in, odd_w_out, ssd_conv_w, ssd_conv_b, ssd_dt_bias, ssd_a_log, ssd_d, ssd_norm_w, mlstm_conv_w, mlstm_conv_b, mlstm_i_bias, mlstm_f_bias, mlstm_norm_w, router_w, router_b, exp_w_gate, exp_b_gate, exp_w_up, exp_b_up, exp_w_down, exp_b_down, final_norm_w):
    lc = ctx.shape[1]
    lower_bounds = jnp.cumsum(jax.nn.softmax(hgrn_lower_bounds.astype(F32), axis=0), axis=0)
    moe = lambda l: (l, norm2_w[l], router_w[l], router_b[l], exp_w_gate, exp_b_gate, exp_w_up, exp_b_up, exp_w_down,
                     exp_b_down, final_norm_w)
    xa = jnp.concatenate([ctx, x], axis=1)
    modtab = _mod_table(c, c_ctx, ada_w[0], ada_b[0])
    xa = even_layer(xa, modtab, norm1_w[0], even_w_in[0], even_w_out[0], rwkv_mu[0], rwkv_w0[0], rwkv_w2[0],
                    rwkv_a0[0], rwkv_a2[0], rwkv_g2[0], rwkv_k_k[0], rwkv_k_a[0], rwkv_r_k[0], rwkv_ln_w[0],
                    rwkv_ln_b[0], lower_bounds[0], hgrn_norm_w[0], lc)
    xa = moe_layer(xa, modtab, *moe(0), lc, False)
    xa = jnp.concatenate([xa[:, :lc], _to_col_major(xa[:, lc:])], axis=1)
    modtab = _mod_table(c, c_ctx, ada_w[1], ada_b[1])
    xl = odd_layer(xa, modtab, norm1_w[1], odd_w_in[0], odd_w_out[0], ssd_conv_w[0], ssd_conv_b[0], ssd_dt_bias[0],
                   ssd_a_log[0], ssd_d[0], ssd_norm_w[0], mlstm_conv_w[0], mlstm_conv_b[0], mlstm_i_bias[0],
                   mlstm_f_bias[0], mlstm_norm_w[0], lc)
    return moe_layer(xl, modtab, *moe(1), 0, True)
```

```python
import functools
import math

import jax
import jax.numpy as jnp
from jax import lax
from jax.experimental import pallas as pl
from jax.experimental.pallas import tpu as pltpu
from jax.experimental.pallas import tpu_sc as plsc

F32 = jnp.float32
BF16 = jnp.bfloat16
I32 = jnp.int32
U32 = jnp.uint32
HI = lax.Precision.HIGHEST

D_MODEL = 1024
GRID_W = 64
N_MOD = 6
EPS = 1e-6
RW_H, RW_N = 8, 64
RW_W = RW_H * RW_N
RW_RANK = 64
RW_GATE_RANK = 128
RW_GN_EPS = 64e-5
RW_PROJ = 3 * RW_W + 4 * RW_RANK + RW_GATE_RANK
RW_CHUNK = 64
RW_GROUP = 4
HG_H, HG_DK, HG_DV = 4, 128, 128
HG_W = HG_H * HG_DK
HG_CHUNK = 32
HG_PROJ = 5 * HG_W
SSD_H, SSD_P, SSD_G, SSD_N = 16, 64, 2, 128
SSD_INNER = SSD_H * SSD_P
SSD_CONV_CH = SSD_INNER + 2 * SSD_G * SSD_N
SSD_CHUNK = 64
ML_H, ML_DK, ML_DV = 4, 128, 256
ML_QK = ML_H * ML_DK
ML_VW = ML_H * ML_DV
ML_CHUNK = 64
ML_M_INIT = -1e30
CONV_K = 5
N_EXP = 32
TOP_K = 4
D_FF = 1024
SWIGLU_LIMIT = 7.0
SWIGLU_ALPHA = 1.702
MOE_BLOCK = 512

TOK_TILE = 256
SCAN_BLOCK = 256
HALO = 8
SMALL_W = 128
VMEM_LIMIT = 56 * 1024 * 1024
SC_CORES, SC_SUBCORES = 2, 16
SC_GATHER_ROWS = 32


def _cparams(sem):
    return pltpu.CompilerParams(dimension_semantics=sem, vmem_limit_bytes=VMEM_LIMIT)


def _dot(a, b, precision=None):
    return jnp.dot(a, b, preferred_element_type=F32, precision=precision)


def _dot_nt(a, b, precision=None):
    return lax.dot_general(a, b, (((1,), (1,)), ((), ())), preferred_element_type=F32, precision=precision)


def _dot_tn(a, b, precision=None):
    return lax.dot_general(a, b, (((0,), (0,)), ((), ())), preferred_element_type=F32, precision=precision)


def _bf16_parts(x):
    hi = x.astype(BF16)
    rest = x - hi.astype(F32)
    mid = rest.astype(BF16)
    return hi, mid, (rest - mid.astype(F32)).astype(BF16)


def _dot_into_01(x, sel):
    sel = sel.astype(BF16)
    p0, p1, p2 = _bf16_parts(x)
    return _dot(p0, sel) + _dot(p1, sel) + _dot(p2, sel)


def _dot_01_into(sel, x):
    sel = sel.astype(BF16)
    p0, p1, p2 = _bf16_parts(x)
    return _dot(sel, p0) + _dot(sel, p1) + _dot(sel, p2)


def _pack_bf16_halves(x):
    n = x.shape[1] // 2
    bits = lax.bitcast_convert_type(x.astype(BF16).astype(F32), U32)
    return bits[:, :n] | (bits[:, n:] >> 16)


def _unpack_bf16_halves(w):
    hi = lax.bitcast_convert_type(w & jnp.uint32(0xFFFF0000), F32)
    lo = lax.bitcast_convert_type(w << 16, F32)
    return jnp.concatenate([hi, lo], axis=1)


def _sigmoid(x):
    return 1.0 / (1.0 + jnp.exp(-x))


def _silu(x):
    return x * _sigmoid(x)


def _softplus(x):
    return jnp.maximum(x, 0.0) + jnp.log(1.0 + jnp.exp(-jnp.abs(x)))


def _dir_masks(z, c):
    ti = lax.broadcasted_iota(I32, (c, c), 0)
    si = lax.broadcasted_iota(I32, (c, c), 1)
    d = (si - ti) * jnp.where(z == 0, 1, -1)
    return d <= 0, d < 0


def _time_block(z, n, ctx_blocks, n_blocks):
    rev = jnp.where(n < ctx_blocks, ctx_blocks - 1 - n, n_blocks - 1 - (n - ctx_blocks))
    return jnp.where(z == 0, n, rev)


def _mod_kernel(c_ref, w_ref, b_ref, o_ref):
    o_ref[...] = _dot(_silu(c_ref[...]), w_ref[...]) + b_ref[...]


def ada_mod(cc, w, b):
    rows, d = cc.shape
    n = w.shape[1]
    tn = 1536
    return pl.pallas_call(
        _mod_kernel,
        out_shape=jax.ShapeDtypeStruct((rows, n), F32),
        grid=(n // tn,),
        in_specs=[pl.BlockSpec((rows, d), lambda j: (0, 0)),
                  pl.BlockSpec((d, tn), lambda j: (0, j)),
                  pl.BlockSpec((1, tn), lambda j: (0, j))],
        out_specs=pl.BlockSpec((rows, tn), lambda j: (0, j)),
        compiler_params=_cparams(("arbitrary",)),
        name="ada_mod",
    )(cc, w, b.reshape(1, n))


def _mod_table(c, c_ctx, w, b):
    bsz = c.shape[0]
    rows = -(-(bsz + 1) // 8) * 8
    cc = jnp.zeros((rows, D_MODEL), F32).at[:bsz].set(c).at[bsz].set(c_ctx)
    m = ada_mod(cc, w, b)
    mod = m[:bsz].reshape(bsz, 1, N_MOD, D_MODEL)
    mod_c = jnp.broadcast_to(m[bsz].reshape(1, 1, N_MOD, D_MODEL), (bsz, 1, N_MOD, D_MODEL))
    return jnp.concatenate([mod_c, mod], axis=1)


def _norm_mod(x, nw, m, shift_idx):
    y = x * lax.rsqrt(jnp.mean(x * x, axis=-1, keepdims=True) + EPS) * nw
    return y * (1.0 + m[shift_idx + 1:shift_idx + 2]) + m[shift_idx:shift_idx + 1]


def _in_proj_kernel(x_ref, mod_ref, nw_ref, *rest, n_out):
    w_refs, o_refs = rest[:n_out], rest[n_out:]
    h = _norm_mod(x_ref[0], nw_ref[...], mod_ref[0, 0], 0).astype(BF16)
    for w_ref, o_ref in zip(w_refs, o_refs):
        o_ref[0] = _dot(h, w_ref[...])


def in_proj(x, modtab, nw, slabs, ctx_len):
    bsz, lt, d = x.shape
    tm = TOK_TILE
    ctx_tiles = ctx_len // tm
    in_specs = [pl.BlockSpec((1, tm, d), lambda b, i: (b, i, 0)),
                pl.BlockSpec((1, 1, N_MOD, d), lambda b, i: (b, (i >= ctx_tiles).astype(I32), 0, 0)),
                pl.BlockSpec((1, d), lambda b, i: (0, 0))]
    in_specs += [pl.BlockSpec(w.shape, lambda b, i: (0, 0)) for w in slabs]
    return pl.pallas_call(
        functools.partial(_in_proj_kernel, n_out=len(slabs)),
        out_shape=[jax.ShapeDtypeStruct((bsz, lt, w.shape[1]), F32) for w in slabs],
        grid=(bsz, lt // tm),
        in_specs=in_specs,
        out_specs=[pl.BlockSpec((1, tm, w.shape[1]), lambda b, i: (b, i, 0)) for w in slabs],
        compiler_params=_cparams(("parallel", "parallel")),
        name="in_proj",
    )(x, modtab, nw.reshape(1, d), *slabs)


def _halo_specs(tm, width, lt):
    per = tm // HALO
    last = lt // HALO - 1
    return [pl.BlockSpec((1, tm, width), lambda b, i: (b, i, 0)),
            pl.BlockSpec((1, HALO, width), lambda b, i: (b, jnp.maximum(i * per - 1, 0), 0)),
            pl.BlockSpec((1, HALO, width), lambda b, i: (b, jnp.minimum((i + 1) * per, last), 0))]


def _segment_edges(i, tm, ctx_len, lt):
    row0 = i * tm
    has_prev = jnp.logical_and(row0 != 0, row0 != ctx_len)
    has_next = jnp.logical_and(row0 + tm != ctx_len, row0 + tm != lt)
    return has_prev, has_next


def _rwkv_prep_kernel(p_ref, pp_ref, pn_ref, mu_ref, w0_ref, w2_ref, a0_ref, a2_ref, g2_ref, kk_ref, ka_ref,
                      rk_ref, bd_ref, r_o, v_o, kk_o, g_o, bv_o, lw_o, km_o, bb_o, *, tm, ctx_len, lt):
    has_prev, has_next = _segment_edges(pl.program_id(1), tm, ctx_len, lt)
    p = p_ref[0]
    prev_row = jnp.where(has_prev, pp_ref[0, HALO - 1:HALO, :], 0.0)
    next_row = jnp.where(has_next, pn_ref[0, 0:1, :], 0.0)
    rid = lax.broadcasted_iota(I32, (tm, 1), 0)
    up = jnp.where(rid == 0, prev_row, pltpu.roll(p, 1, 0))
    dn = jnp.where(rid == tm - 1, next_row, pltpu.roll(p, tm - 1, 0))
    p = p + (0.5 * (up + dn) - p) * mu_ref[...]
    w = RW_W
    r, k, v = p[:, 0:w], p[:, w:2 * w], p[:, 2 * w:3 * w]
    o = 3 * w
    wd = jnp.tanh(p[:, o:o + 2 * RW_RANK])
    ad = p[:, o + 2 * RW_RANK:o + 4 * RW_RANK]
    gd = p[:, o + 4 * RW_RANK:o + 4 * RW_RANK + RW_GATE_RANK]
    bd = bd_ref[...]
    kk = k * kk_ref[...]
    kk = kk * lax.rsqrt(_dot_into_01(kk * kk, bd) + 1e-12)
    r_o[0] = r
    v_o[0] = v
    kk_o[0] = kk
    g_o[0] = _dot(_sigmoid(gd), g2_ref[...])
    km_sum = jnp.zeros_like(k)
    for z in range(2):
        sl = slice(z * RW_RANK, (z + 1) * RW_RANK)
        w_pre = w0_ref[z:z + 1, :] + _dot(wd[:, sl], w2_ref[z])
        lw_o[z, 0] = -_sigmoid(w_pre) * math.exp(-0.5)
        a = _sigmoid(a0_ref[z:z + 1, :] + _dot(ad[:, sl], a2_ref[z]))
        km = k * (1.0 + (a - 1.0) * ka_ref[...])
        km_o[z, 0] = km
        bb_o[z, 0] = kk * a
        km_sum = km_sum + km
    bv_o[0] = _dot_into_01(r * km_sum * rk_ref[...], bd) * v


def _head_block_ones(width, head):
    i = jnp.arange(width) // head
    return (i[:, None] == i[None, :]).astype(F32)


def rwkv_prep(p, mu, w0, w2, a0, a2, g2, k_k, k_a, r_k, ctx_len):
    bsz, lt, width = p.shape
    tm = TOK_TILE
    w = RW_W
    full = lambda a: pl.BlockSpec(a.shape, lambda b, i: (0,) * a.ndim)
    params = [mu.reshape(1, width), w0, w2, a0, a2, g2, k_k.reshape(1, w), k_a.reshape(1, w), r_k.reshape(1, w),
              _head_block_ones(w, RW_N)]
    one = jax.ShapeDtypeStruct((bsz, lt, w), F32)
    two = jax.ShapeDtypeStruct((2, bsz, lt, w), F32)
    one_spec = pl.BlockSpec((1, tm, w), lambda b, i: (b, i, 0))
    two_spec = pl.BlockSpec((2, 1, tm, w), lambda b, i: (0, b, i, 0))
    return pl.pallas_call(
        functools.partial(_rwkv_prep_kernel, tm=tm, ctx_len=ctx_len, lt=lt),
        out_shape=[one] * 5 + [two] * 3,
        grid=(bsz, lt // tm),
        in_specs=_halo_specs(tm, width, lt) + [full(a) for a in params],
        out_specs=[one_spec] * 5 + [two_spec] * 3,
        compiler_params=_cparams(("parallel", "parallel")),
        name="rwkv_prep",
    )(p, p, p, *params)


def _unit_lower_inverses(n_mats, eye, c):
    xs = [eye - n for n in n_mats]
    ps = [_dot(n, n) for n in n_mats]
    span = 4
    while True:
        xs = [x + _dot(x, p) for x, p in zip(xs, ps)]
        if span >= c:
            return xs
        ps = [_dot(p, p) for p in ps]
        span *= 2


def _rwkv_scan_kernel(r_ref, v_ref, kk_ref, lw_ref, km_ref, bb_ref, o_ref, s_ref, *, tb, c):
    z = pl.program_id(0)
    nsub = tb // c

    @pl.when(pl.program_id(2) == 0)
    def _():
        s_ref[...] = jnp.zeros_like(s_ref)

    incl, strict = _dir_masks(z, c)
    incl_f, strict_f = incl.astype(F32), strict.astype(F32)
    eye = incl_f - strict_f

    heads = [slice(h * RW_N, (h + 1) * RW_N) for h in range(RW_H)]

    def group(g, s_old):
        subs = []
        for i in range(RW_GROUP):
            j = g * RW_GROUP + i
            rows = pl.ds(pl.multiple_of(jnp.where(z == 0, j, nsub - 1 - j) * c, c), c)
            r, v, kk = r_ref[0, rows, :], v_ref[0, rows, :], kk_ref[0, rows, :]
            lw, km, bb = lw_ref[0, 0, rows, :], km_ref[0, 0, rows, :], bb_ref[0, 0, rows, :]
            cum = _dot_01_into(incl_f, lw)
            tot = jnp.sum(lw, axis=0, keepdims=True)
            e_neg = jnp.exp(-cum)
            e_end = jnp.exp(tot - cum)
            a_t = kk * jnp.exp(cum - lw)
            r_t = r * jnp.exp(cum)
            b_t, k_t = bb * e_neg, km * e_neg
            b_p, k_p = bb * e_end, km * e_end
            subs.append(dict(
                rows=rows, e_tot=jnp.exp(tot), a=[a_t[:, hs] for hs in heads], r=[r_t[:, hs] for hs in heads],
                ar=[jnp.concatenate([a_t[:, hs], r_t[:, hs]], axis=0) for hs in heads],
                bk=[jnp.concatenate([b_t[:, hs], k_t[:, hs]], axis=0) for hs in heads],
                bkp=[jnp.concatenate([b_p[:, hs], k_p[:, hs]], axis=0) for hs in heads],
                v=[v[:, hs] for hs in heads]))
        flat = [(i, h) for i in range(RW_GROUP) for h in range(RW_H)]
        sc = [_dot_nt(subs[i]["ar"][h], subs[i]["bk"][h]) for i, h in flat]
        m_ak_v = [_dot(x[:c, c:] * strict_f, subs[i]["v"][h]) for x, (i, h) in zip(sc, flat)]
        inv = _unit_lower_inverses([x[:c, :c] * strict_f for x in sc], eye, c)
        w1r = [jnp.concatenate([_dot(n, subs[i]["a"][h]), subs[i]["r"][h]], axis=0) for n, (i, h) in zip(inv, flat)]
        w2 = [_dot(n, m) for n, m in zip(inv, m_ak_v)]
        n_rbk = [jnp.concatenate([x[c:, :c] * incl_f, x[c:, c:] * incl_f], axis=1) for x in sc]
        state = list(s_old)
        for i in range(RW_GROUP):
            sub = subs[i]
            base = i * RW_H
            ws = [_dot_nt(w1r[base + h], state[h]) for h in range(RW_H)]
            uv = [jnp.concatenate([-(ws[h][:c] + w2[base + h]), sub["v"][h]], axis=0) for h in range(RW_H)]
            outs = [ws[h][c:] + _dot(n_rbk[base + h], uv[h]) for h in range(RW_H)]
            o_ref[0, 0, sub["rows"], :] = jnp.concatenate(outs, axis=1)
            state = [state[h] * sub["e_tot"][:, heads[h]] + _dot_tn(uv[h], sub["bkp"][h]) for h in range(RW_H)]
        return tuple(state)

    s_fin = lax.fori_loop(0, nsub // RW_GROUP, group, tuple(s_ref[h] for h in range(RW_H)))
    for h in range(RW_H):
        s_ref[h] = s_fin[h]


def rwkv_scan(r, v, kk, lw, km, bb, ctx_len):
    bsz, lt, w = r.shape
    tb = SCAN_BLOCK
    nb, cb = lt // tb, ctx_len // tb
    tmap = lambda z, b, n: _time_block(z, n, cb, nb)
    one = pl.BlockSpec((1, tb, w), lambda z, b, n: (b, tmap(z, b, n), 0))
    two = pl.BlockSpec((1, 1, tb, w), lambda z, b, n: (z, b, tmap(z, b, n), 0))
    return pl.pallas_call(
        functools.partial(_rwkv_scan_kernel, tb=tb, c=RW_CHUNK),
        out_shape=jax.ShapeDtypeStruct((2, bsz, lt, w), F32),
        grid=(2, bsz, nb),
        in_specs=[one, one, one, two, two, two],
        out_specs=two,
        scratch_shapes=[pltpu.VMEM((RW_H, RW_N, RW_N), F32)],
        compiler_params=_cparams(("parallel", "parallel", "arbitrary")),
        name="rwkv_scan",
    )(r, v, kk, lw, km, bb)


def _gla_kernel(q_ref, f_ref, i_ref, lb_ref, o_ref, s_ref, *, tb, c):
    z = pl.program_id(0)
    nsub = tb // c

    @pl.when(pl.program_id(2) == 0)
    def _():
        s_ref[...] = jnp.zeros_like(s_ref)

    lb = lb_ref[...]
    heads = [slice(h * HG_DK, (h + 1) * HG_DK) for h in range(HG_H)]
    ti = lax.broadcasted_iota(I32, (tb, tb), 0)
    si = lax.broadcasted_iota(I32, (tb, tb), 1)
    same_chunk = (ti // c) == (si // c)

    def run(zs):
        incl = jnp.logical_and(same_chunk, (si <= ti) if zs == 0 else (si >= ti))
        q = _silu(q_ref[0]) * HG_DK ** -0.5
        f = f_ref[0]
        v = i_ref[0]
        sig_f = _sigmoid(f)
        log_f = jnp.log(lb + (1.0 - lb) * sig_f)
        k = (1.0 - lb) * (1.0 - sig_f)
        cum = _dot_01_into(incl.astype(F32), log_f)
        last = c - 1 if zs == 0 else 0
        tot = jnp.concatenate([jnp.broadcast_to(cum[j * c + last:j * c + last + 1, :], (c, cum.shape[1]))
                               for j in range(nsub)], axis=0)
        q_in = q * jnp.exp(cum)
        k_in = k * jnp.exp(-cum)
        k_end = k * jnp.exp(tot - cum)
        e_tot = jnp.exp(tot)
        intra = [_dot(jnp.where(incl, _dot_nt(q_in[:, hs], k_in[:, hs]), 0.0), v[:, hs]) for hs in heads]
        order = range(nsub) if zs == 0 else range(nsub - 1, -1, -1)
        grow = {(j, h): _dot_tn(v[j * c:(j + 1) * c, hs], k_end[j * c:(j + 1) * c, hs])
                for j in order for h, hs in enumerate(heads)}
        state = [s_ref[h] for h in range(HG_H)]
        inter = {}
        for j in order:
            rows = slice(j * c, (j + 1) * c)
            for h, hs in enumerate(heads):
                inter[(j, h)] = _dot_nt(q_in[rows, hs], state[h])
                state[h] = state[h] * e_tot[j * c:j * c + 1, hs] + grow[(j, h)]
        for h, hs in enumerate(heads):
            o_ref[0, 0, :, hs] = intra[h] + jnp.concatenate([inter[(j, h)] for j in range(nsub)], axis=0)
            s_ref[h] = state[h]

    for zs in range(2):
        pl.when(z == zs)(functools.partial(run, zs))


def gla_scan(ph, lb, ctx_len):
    bsz, lt, _ = ph.shape
    w = HG_W
    tb = SCAN_BLOCK
    nb, cb = lt // tb, ctx_len // tb
    tmap = lambda z, n: _time_block(z, n, cb, nb)
    return pl.pallas_call(
        functools.partial(_gla_kernel, tb=tb, c=HG_CHUNK),
        out_shape=jax.ShapeDtypeStruct((2, bsz, lt, w), F32),
        grid=(2, bsz, nb),
        in_specs=[pl.BlockSpec((1, tb, w), lambda z, b, n: (b, tmap(z, n), 0)),
                  pl.BlockSpec((1, tb, w), lambda z, b, n: (b, tmap(z, n), 1 + z)),
                  pl.BlockSpec((1, tb, w), lambda z, b, n: (b, tmap(z, n), 3)),
                  pl.BlockSpec((1, w), lambda z, b, n: (0, 0))],
        out_specs=pl.BlockSpec((1, 1, tb, w), lambda z, b, n: (z, b, tmap(z, n), 0)),
        scratch_shapes=[pltpu.VMEM((HG_H, HG_DV, HG_DK), F32)],
        compiler_params=_cparams(("parallel", "parallel", "arbitrary")),
        name="gla_scan",
    )(ph, ph, ph, lb.reshape(1, w))


def _group_rms(y, groups, gain):
    width = y.shape[-1] // groups
    parts = []
    for g in range(groups):
        yg = y[:, g * width:(g + 1) * width]
        parts.append(yg * lax.rsqrt(jnp.mean(yg * yg, axis=-1, keepdims=True) + EPS))
    return jnp.concatenate(parts, axis=1) * gain


def _even_out_kernel(x_ref, mod_ref, or_ref, bv_ref, g_ref, og_ref, gate_ref, lnw_ref, lnb_ref, nw_ref, bd_ref,
                     w_ref, n2_ref, rw_ref, rb_ref, o_ref, *route_refs):
    bd = bd_ref[...]
    o = or_ref[0, 0] + or_ref[1, 0]
    mu = _dot_into_01(o, bd) * (1.0 / RW_N)
    oc = o - mu
    var = _dot_into_01(oc * oc, bd) * (1.0 / RW_N)
    yr = (oc * lax.rsqrt(var + RW_GN_EPS) * lnw_ref[...] + lnb_ref[...] + bv_ref[0]) * g_ref[0]
    og = og_ref[0, 0] + og_ref[1, 0]
    yh = _group_rms(og, HG_H, nw_ref[...]) * _silu(gate_ref[0])
    y = jnp.concatenate([yr, yh], axis=1).astype(BF16)
    x_new = x_ref[0] + mod_ref[0, 0][2:3] * _dot(y, w_ref[...])
    o_ref[0] = x_new
    _route_tile(x_new, mod_ref[0, 0], n2_ref, rw_ref, rb_ref, *route_refs)


def even_out(x, modtab, o_r, bv, g, o_g, ph, ln_w, ln_b, norm_w, w_out, norm2_w, router_w, router_b, ctx_len):
    bsz, lt, d = x.shape
    tm = TOK_TILE
    ctx_tiles = ctx_len // tm
    w = RW_W
    tok = lambda width, col=0: pl.BlockSpec((1, tm, width), lambda b, i: (b, i, col))
    two = pl.BlockSpec((2, 1, tm, w), lambda b, i: (0, b, i, 0))
    row = lambda width: pl.BlockSpec((1, width), lambda b, i: (0, 0))
    r_ins, r_shapes, r_specs, r_scratch = _route_specs(bsz, lt, tm, d)
    x_new, *routing = pl.pallas_call(
        _even_out_kernel,
        out_shape=[jax.ShapeDtypeStruct((bsz, lt, d), F32)] + r_shapes,
        grid=(bsz, lt // tm),
        in_specs=[tok(d),
                  pl.BlockSpec((1, 1, N_MOD, d), lambda b, i: (b, (i >= ctx_tiles).astype(I32), 0, 0)),
                  two, tok(w), tok(w), two, tok(HG_W, 4), row(w), row(w), row(HG_W),
                  pl.BlockSpec((w, w), lambda b, i: (0, 0)),
                  pl.BlockSpec(w_out.shape, lambda b, i: (0, 0))] + r_ins,
        out_specs=[tok(d)] + r_specs,
        scratch_shapes=r_scratch,
        compiler_params=_cparams(("arbitrary", "arbitrary")),
        name="even_out",
    )(x, modtab, o_r, bv, g, o_g, ph, ln_w.reshape(1, w), ln_b.reshape(1, w), norm_w.reshape(1, HG_W),
      _head_block_ones(w, RW_N), w_out, norm2_w.reshape(1, d), router_w, router_b.reshape(1, N_EXP))
    return x_new, routing


def _conv_silu(x_ref, xp_ref, xn_ref, w_ref, b_ref, has_prev, has_next, tm):
    prev = jnp.where(has_prev, xp_ref[0], 0.0)
    nxt = jnp.where(has_next, xn_ref[0], 0.0)
    ext = jnp.concatenate([prev, x_ref[0], nxt], axis=0)
    rows = tm + 2 * HALO
    pad = CONV_K // 2
    acc = b_ref[...] + jnp.zeros((tm, ext.shape[1]), F32)
    for j in range(CONV_K):
        d = j - pad
        sh = ext if d == 0 else pltpu.roll(ext, (-d) % rows, 0)
        acc = acc + sh[HALO:HALO + tm] * w_ref[j:j + 1, :]
    return _silu(acc)


def _odd_prep_kernel(xs_ref, xsp_ref, xsn_ref, xm_ref, xmp_ref, xmn_ref, ws_ref, bs_ref, wm_ref, bm_ref,
                     os_ref, om_ref, *, tm, ctx_len, lt):
    has_prev, has_next = _segment_edges(pl.program_id(1), tm, ctx_len, lt)
    os_ref[0] = _conv_silu(xs_ref, xsp_ref, xsn_ref, ws_ref, bs_ref, has_prev, has_next, tm)
    om_ref[0] = _conv_silu(xm_ref, xmp_ref, xmn_ref, wm_ref, bm_ref, has_prev, has_next, tm)


def odd_prep(xbc, qk, ws, bs, wm, bm, ctx_len):
    bsz, lt, cs = xbc.shape
    cm = wm.shape[1]
    tm = TOK_TILE
    full = lambda a: pl.BlockSpec(a.shape, lambda b, i: (0,) * a.ndim)
    params = [ws, bs.reshape(1, cs), wm, bm.reshape(1, cm)]
    return pl.pallas_call(
        functools.partial(_odd_prep_kernel, tm=tm, ctx_len=ctx_len, lt=lt),
        out_shape=[jax.ShapeDtypeStruct((bsz, lt, cs), F32), jax.ShapeDtypeStruct((bsz, lt, cm), F32)],
        grid=(bsz, lt // tm),
        in_specs=_halo_specs(tm, cs, lt) + _halo_specs(tm, cm, lt) + [full(a) for a in params],
        out_specs=[pl.BlockSpec((1, tm, cs), lambda b, i: (b, i, 0)),
                   pl.BlockSpec((1, tm, cm), lambda b, i: (b, i, 0))],
        compiler_params=_cparams(("parallel", "parallel")),
        name="odd_prep",
    )(xbc, xbc, xbc, qk, qk, qk, *params)


def _ssd_kernel(x_ref, b_ref, c_ref, sm_ref, dtb_ref, alog_ref, ex_ref, o_ref, s_ref, *, tb, c):
    z = pl.program_id(0)
    nsub = tb // c
    hg = SSD_H // SSD_G

    @pl.when(pl.program_id(2) == 0)
    def _():
        s_ref[...] = jnp.zeros_like(s_ref)

    incl, _ = _dir_masks(z, c)
    incl_f = incl.astype(F32)
    ex = ex_ref[...]

    gw = hg * SSD_P

    def local(j):
        jj = jnp.where(z == 0, j, nsub - 1 - j)
        rows = pl.ds(pl.multiple_of(jj * c, c), c)
        x = x_ref[0, rows, :]
        bm = b_ref[0, rows, :]
        cm = c_ref[0, rows, :]
        dt = _softplus(sm_ref[0, rows, :] + dtb_ref[0])
        la = -dt * jnp.exp(alog_ref[0])
        a_cum = _dot_01_into(incl_f, la)
        a_tot = jnp.sum(la, axis=0, keepdims=True)
        a_cum_t = a_cum.T
        e_tot_col = jnp.exp(jnp.where(z == 0, a_cum_t[:, c - 1:c], a_cum_t[:, 0:1]))
        wide = _dot(jnp.concatenate([dt, jnp.exp(a_cum), jnp.exp(a_tot - a_cum)], axis=0).astype(BF16),
                    ex.astype(BF16))
        xdt = x * wide[:c]
        din, dend = wide[c:2 * c], wide[2 * c:]
        xd_end = xdt * dend
        bgs = [bm[:, g * SSD_N:(g + 1) * SSD_N] for g in range(SSD_G)]
        cgs = [cm[:, g * SSD_N:(g + 1) * SSD_N] for g in range(SSD_G)]
        cb = [_dot_nt(cg, bg) for cg, bg in zip(cgs, bgs)]
        grow = [_dot_tn(xd_end[:, g * gw:(g + 1) * gw], bg) for g, bg in enumerate(bgs)]
        intra = []
        for h in range(SSD_H):
            seg = a_cum[:, h:h + 1] - a_cum_t[h:h + 1, :]
            wmat = cb[h // hg] * jnp.where(incl, jnp.exp(seg), 0.0)
            intra.append(_dot(wmat, xdt[:, h * SSD_P:(h + 1) * SSD_P]))
        decay = [jnp.concatenate([jnp.broadcast_to(e_tot_col[h:h + 1, :], (SSD_P, 1))
                                  for h in range(g * hg, (g + 1) * hg)], axis=0) for g in range(SSD_G)]
        return dict(rows=rows, cgs=cgs, din=din, intra=jnp.concatenate(intra, axis=1), grow=grow, decay=decay)

    subs = [local(j) for j in range(nsub)]
    state = [s_ref[g] for g in range(SSD_G)]
    for sub in subs:
        inter = [_dot_nt(cg, s) for cg, s in zip(sub["cgs"], state)]
        o_ref[0, 0, sub["rows"], :] = sub["intra"] + sub["din"] * jnp.concatenate(inter, axis=1)
        state = [s * d + g for s, d, g in zip(state, sub["decay"], sub["grow"])]
    for g in range(SSD_G):
        s_ref[g] = state[g]


def _head_expand(heads, width, rows):
    h = jnp.arange(rows)[:, None]
    col = jnp.arange(heads * width)[None, :] // width
    return (h == col).astype(F32)


def ssd_scan(xbc, small, dtb, alog, ctx_len):
    bsz, lt, _ = xbc.shape
    tb = SCAN_BLOCK
    nb, cb = lt // tb, ctx_len // tb
    tmap = lambda z, n: _time_block(z, n, cb, nb)
    gw = SSD_G * SSD_N
    return pl.pallas_call(
        functools.partial(_ssd_kernel, tb=tb, c=SSD_CHUNK),
        out_shape=jax.ShapeDtypeStruct((2, bsz, lt, SSD_INNER), F32),
        grid=(2, bsz, nb),
        in_specs=[pl.BlockSpec((1, tb, SSD_INNER), lambda z, b, n: (b, tmap(z, n), 0)),
                  pl.BlockSpec((1, tb, gw), lambda z, b, n: (b, tmap(z, n), SSD_INNER // gw)),
                  pl.BlockSpec((1, tb, gw), lambda z, b, n: (b, tmap(z, n), SSD_INNER // gw + 1)),
                  pl.BlockSpec((1, tb, SMALL_W), lambda z, b, n: (b, tmap(z, n), z)),
                  pl.BlockSpec((1, 1, SMALL_W), lambda z, b, n: (z, 0, 0)),
                  pl.BlockSpec((1, 1, SMALL_W), lambda z, b, n: (z, 0, 0)),
                  pl.BlockSpec((SMALL_W, SSD_INNER), lambda z, b, n: (0, 0))],
        out_specs=pl.BlockSpec((1, 1, tb, SSD_INNER), lambda z, b, n: (z, b, tmap(z, n), 0)),
        scratch_shapes=[pltpu.VMEM((SSD_G, SSD_H // SSD_G * SSD_P, SSD_N), F32)],
        compiler_params=_cparams(("parallel", "parallel", "arbitrary")),
        name="ssd_scan",
    )(xbc, xbc, xbc, small, dtb, alog, _head_expand(SSD_H, SSD_P, SMALL_W))


ML_I_LANE = SSD_H
ML_F_LANE = SSD_H + ML_H


def _mlstm_kernel(q_ref, k_ref, v_ref, sm_ref, gb_ref, o_ref, cs_ref, ns_ref, m_ref, *, tb, c):
    z = pl.program_id(0)
    nsub = tb // c

    @pl.when(pl.program_id(2) == 0)
    def _():
        cs_ref[...] = jnp.zeros_like(cs_ref)
        ns_ref[...] = jnp.zeros_like(ns_ref)
        m_ref[...] = jnp.full(m_ref.shape, ML_M_INIT, F32)

    incl, _ = _dir_masks(z, c)
    incl_f = incl.astype(F32)
    lane = lax.broadcasted_iota(I32, (1, SMALL_W), 1)
    is_f = jnp.logical_and(lane >= ML_F_LANE, lane < ML_F_LANE + ML_H)

    def local(j):
        rows = pl.ds(pl.multiple_of(jnp.where(z == 0, j, nsub - 1 - j) * c, c), c)
        q = q_ref[0, rows, :]
        k = k_ref[0, rows, :] * ML_DK ** -0.5
        v = v_ref[0, rows, :]
        pre = sm_ref[0, rows, :] + gb_ref[0]
        gates = jnp.where(is_f, -_softplus(-pre), pre)
        f_cum = _dot(incl_f, gates, HI)
        f_tot = jnp.sum(gates, axis=0, keepdims=True)
        f_cum_t = f_cum.T
        gates_t = gates.T
        per_head = []
        for h in range(ML_H):
            li, lf = ML_I_LANE + h, ML_F_LANE + h
            qh = q[:, h * ML_DK:(h + 1) * ML_DK]
            kh = k[:, h * ML_DK:(h + 1) * ML_DK]
            vh = v[:, h * ML_DV:(h + 1) * ML_DV]
            f_col = f_cum[:, lf:lf + 1]
            f_last = f_tot[:, lf:lf + 1]
            logw_end = f_last - f_col + gates[:, li:li + 1]
            m_end = jnp.max(logw_end, axis=0, keepdims=True)
            kw = kh * jnp.exp(logw_end - m_end)
            logw = jnp.where(incl, f_col - f_cum_t[lf:lf + 1, :] + gates_t[li:li + 1, :], -jnp.inf)
            m_loc = jnp.max(logw, axis=1, keepdims=True)
            scores = _dot_nt(qh, kh) * jnp.exp(logw - m_loc)
            per_head.append(dict(
                q=qh, f_col=f_col, f_last=f_last, m_end=m_end, m_loc=m_loc, num=_dot(scores, vh),
                den=jnp.sum(scores, axis=1, keepdims=True), kv=_dot_tn(kw, vh),
                ksum=jnp.sum(kw, axis=0, keepdims=True)))
        return rows, per_head

    subs = [local(j) for j in range(nsub)]
    cs = [cs_ref[h] for h in range(ML_H)]
    ns = [ns_ref[h] for h in range(ML_H)]
    m = [m_ref[h] for h in range(ML_H)]
    for rows, per_head in subs:
        outs = []
        for h, p in enumerate(per_head):
            m_t = jnp.maximum(p["m_loc"], p["f_col"] + m[h])
            w_loc = jnp.exp(p["m_loc"] - m_t)
            w_inter = jnp.exp(p["f_col"] + m[h] - m_t)
            num = p["num"] * w_loc + w_inter * _dot(p["q"], cs[h])
            den = p["den"] * w_loc + w_inter * jnp.sum(p["q"] * ns[h], axis=1, keepdims=True)
            outs.append(num / jnp.maximum(jnp.abs(den), jnp.exp(-m_t)))
            m_new = jnp.maximum(p["f_last"] + m[h], p["m_end"])
            s_keep = jnp.exp(p["f_last"] + m[h] - m_new)
            s_loc = jnp.exp(p["m_end"] - m_new)
            cs[h] = s_keep * cs[h] + s_loc * p["kv"]
            ns[h] = s_keep * ns[h] + s_loc * p["ksum"]
            m[h] = m_new
        o_ref[0, 0, rows, :] = jnp.concatenate(outs, axis=1)
    for h in range(ML_H):
        cs_ref[h] = cs[h]
        ns_ref[h] = ns[h]
        m_ref[h] = m[h]


def mlstm_scan(qk, pm, small, gate_bias, ctx_len):
    bsz, lt, _ = qk.shape
    tb = SCAN_BLOCK
    nb, cb = lt // tb, ctx_len // tb
    tmap = lambda z, n: _time_block(z, n, cb, nb)
    return pl.pallas_call(
        functools.partial(_mlstm_kernel, tb=tb, c=ML_CHUNK),
        out_shape=jax.ShapeDtypeStruct((2, bsz, lt, ML_VW), F32),
        grid=(2, bsz, nb),
        in_specs=[pl.BlockSpec((1, tb, ML_QK), lambda z, b, n: (b, tmap(z, n), 0)),
                  pl.BlockSpec((1, tb, ML_QK), lambda z, b, n: (b, tmap(z, n), 1)),
                  pl.BlockSpec((1, tb, ML_VW), lambda z, b, n: (b, tmap(z, n), 1)),
                  pl.BlockSpec((1, tb, SMALL_W), lambda z, b, n: (b, tmap(z, n), z)),
                  pl.BlockSpec((1, 1, SMALL_W), lambda z, b, n: (z, 0, 0))],
        out_specs=pl.BlockSpec((1, 1, tb, ML_VW), lambda z, b, n: (z, b, tmap(z, n), 0)),
        scratch_shapes=[pltpu.VMEM((ML_H, ML_DK, ML_DV), F32), pltpu.VMEM((ML_H, 1, ML_DK), F32),
                        pltpu.VMEM((ML_H, 1, 1), F32)],
        compiler_params=_cparams(("parallel", "parallel", "arbitrary")),
        name="mlstm_scan",
    )(qk, qk, pm, small, gate_bias)


def _odd_out_kernel(x_ref, mod_ref, ys_ref, xc_ref, zg_ref, ym_ref, og_ref, dsk_ref, nws_ref, nwm_ref, w_ref,
                    n2_ref, rw_ref, rb_ref, o_ref, *route_refs):
    y = (ys_ref[0, 0] + ys_ref[1, 0] + dsk_ref[...] * xc_ref[0]) * _silu(zg_ref[0])
    ys = _group_rms(y, SSD_G, nws_ref[...])
    hm = ym_ref[0, 0] + ym_ref[1, 0]
    ym = _group_rms(hm, ML_H, nwm_ref[...]) * _sigmoid(og_ref[0])
    yy = jnp.concatenate([ys, ym], axis=1).astype(BF16)
    x_new = x_ref[0] + mod_ref[0, 0][2:3] * _dot(yy, w_ref[...])
    o_ref[0] = x_new
    _route_tile(x_new, mod_ref[0, 0], n2_ref, rw_ref, rb_ref, *route_refs)


def odd_out(x, modtab, y_s, xbc, ps, y_m, pm, d_skip, nw_s, nw_m, w_out, norm2_w, router_w, router_b, ctx_len):
    bsz, lt, d = x.shape
    tm = TOK_TILE
    ct = ctx_len // tm
    seq = lt - ctx_len
    tok = lambda width, col=0: pl.BlockSpec((1, tm, width), lambda b, i: (b, i + ct, col))
    two = lambda width: pl.BlockSpec((2, 1, tm, width), lambda b, i: (0, b, i + ct, 0))
    row = lambda width: pl.BlockSpec((1, width), lambda b, i: (0, 0))
    r_ins, r_shapes, r_specs, r_scratch = _route_specs(bsz, seq, tm, d)
    x_new, *routing = pl.pallas_call(
        _odd_out_kernel,
        out_shape=[jax.ShapeDtypeStruct((bsz, seq, d), F32)] + r_shapes,
        grid=(bsz, seq // tm),
        in_specs=[tok(d), pl.BlockSpec((1, 1, N_MOD, d), lambda b, i: (b, 1, 0, 0)),
                  two(SSD_INNER), tok(SSD_INNER), tok(SSD_INNER), two(ML_VW), tok(ML_VW, 2),
                  row(SSD_INNER), row(SSD_INNER), row(ML_VW),
                  pl.BlockSpec(w_out.shape, lambda b, i: (0, 0))] + r_ins,
        out_specs=[pl.BlockSpec((1, tm, d), lambda b, i: (b, i, 0))] + r_specs,
        scratch_shapes=r_scratch,
        compiler_params=_cparams(("arbitrary", "arbitrary")),
        name="odd_out",
    )(x, modtab, y_s, xbc, ps, y_m, pm, d_skip.reshape(1, SSD_INNER), nw_s.reshape(1, SSD_INNER),
      nw_m.reshape(1, ML_VW), w_out, norm2_w.reshape(1, d), router_w, router_b.reshape(1, N_EXP))
    return x_new, routing


def _route_tile(x, mod, nw_ref, rw_ref, rb_ref, h_ref, idx_ref, prob_ref, rank_ref, cnt_ref, carry_ref):
    tm = x.shape[0]
    first = jnp.logical_and(pl.program_id(0) == 0, pl.program_id(1) == 0)

    @pl.when(first)
    def _():
        carry_ref[...] = jnp.zeros_like(carry_ref)

    h = _norm_mod(x, nw_ref[...], mod, 3)
    h_ref[0] = _pack_bf16_halves(h)
    logits = _dot(h, rw_ref[...], HI) + rb_ref[...]
    lane = lax.broadcasted_iota(I32, (tm, N_EXP), 1).astype(F32)
    work = logits
    sel = jnp.zeros((tm, N_EXP), F32)
    vals, idxs, hots = [], [], []
    for _ in range(TOP_K):
        m = jnp.max(work, axis=1, keepdims=True)
        idx = jnp.min(jnp.where(work == m, lane, float(N_EXP)), axis=1, keepdims=True)
        hot = lane == idx
        vals.append(m)
        idxs.append(idx)
        hots.append(hot.astype(F32))
        sel = sel + hots[-1]
        work = jnp.where(hot, -jnp.inf, work)
    ex = [jnp.exp(v - vals[0]) for v in vals]
    tot = ex[0] + ex[1] + ex[2] + ex[3]
    ti = lax.broadcasted_iota(I32, (tm, tm), 0)
    si = lax.broadcasted_iota(I32, (tm, tm), 1)
    before = _dot((si < ti).astype(BF16), sel.astype(BF16)) + carry_ref[...]
    ranks = [jnp.sum(before * hot, axis=1, keepdims=True) for hot in hots]
    idx_ref[0] = jnp.concatenate(idxs, axis=1).astype(I32)
    prob_ref[0] = jnp.concatenate([e / tot for e in ex], axis=1)
    rank_ref[0] = jnp.concatenate(ranks, axis=1).astype(I32)
    carry_ref[...] = carry_ref[...] + jnp.sum(sel, axis=0, keepdims=True)
    cnt_ref[...] = carry_ref[...]


def _route_specs(bsz, lt, tm, d):
    tok = lambda width, dt: (jax.ShapeDtypeStruct((bsz, lt, width), dt),
                             pl.BlockSpec((1, tm, width), lambda b, i: (b, i, 0)))
    outs = [tok(d // 2, U32), tok(TOP_K, I32), tok(TOP_K, F32), tok(TOP_K, I32),
            (jax.ShapeDtypeStruct((1, N_EXP), F32), pl.BlockSpec((1, N_EXP), lambda b, i: (0, 0)))]
    ins = [pl.BlockSpec((1, d), lambda b, i: (0, 0)), pl.BlockSpec((d, N_EXP), lambda b, i: (0, 0)),
           pl.BlockSpec((1, N_EXP), lambda b, i: (0, 0))]
    return ins, [o[0] for o in outs], [o[1] for o in outs], [pltpu.VMEM((1, N_EXP), F32)]


def sc_gather_rows(table, idx):
    m = idx.shape[0]
    d = table.shape[1]
    workers = SC_CORES * SC_SUBCORES
    per_w, n_chunks = m // workers, m // workers // SC_GATHER_ROWS
    assert per_w * workers == m and n_chunks * SC_GATHER_ROWS == per_w and n_chunks % 2 == 0
    w = SC_GATHER_ROWS
    mesh = plsc.VectorSubcoreMesh(core_axis_name="c", subcore_axis_name="s")

    @functools.partial(
        pl.kernel, mesh=mesh, out_type=jax.ShapeDtypeStruct((m, d), table.dtype),
        scratch_types=[pltpu.VMEM((per_w,), I32), pltpu.VMEM((2, w, d), table.dtype),
                       pltpu.SemaphoreType.DMA((2,)), pltpu.SemaphoreType.DMA((2,))])
    def gather_kernel(table_hbm, idx_hbm, out_hbm, idx_v, rows_v, gsem, wsem):
        wid = lax.axis_index("s") * SC_CORES + lax.axis_index("c")
        base = pl.multiple_of(wid * per_w, 8)
        pltpu.sync_copy(idx_hbm.at[pl.ds(base, per_w)], idx_v)

        def gather(g, b):
            rows = idx_v.at[pl.ds(pl.multiple_of(g * w, 8), w)]
            return pltpu.make_async_copy(table_hbm.at[rows], rows_v.at[b], gsem.at[b])

        def write(g, b):
            dst = out_hbm.at[pl.ds(pl.multiple_of(base + g * w, 8), w)]
            return pltpu.make_async_copy(rows_v.at[b], dst, wsem.at[b])

        gather(0, 0).start()

        @pl.loop(0, n_chunks, step=2)
        def _(g):
            for b in (0, 1):
                cur = g + b
                gather(cur, b).wait()
                write(cur, b).start()

                @pl.when(cur >= 1)
                def _():
                    write(cur - 1, 1 - b).wait()

                @pl.when(cur + 1 < n_chunks)
                def _():
                    gather(cur + 1, 1 - b).start()

        write(n_chunks - 1, 1).wait()

    return gather_kernel(table, idx)


def sc_scatter_rows(src, idx, n_out):
    k_dup, t = idx.shape
    d = src.shape[1]
    workers = SC_CORES * SC_SUBCORES
    w = SC_GATHER_ROWS
    per_w, n_chunks = t // workers, t // workers // w
    assert per_w * workers == t and n_chunks * w == per_w and n_chunks % 2 == 0
    mesh = plsc.VectorSubcoreMesh(core_axis_name="c", subcore_axis_name="s")

    @functools.partial(
        pl.kernel, mesh=mesh, out_type=jax.ShapeDtypeStruct((n_out, d), src.dtype),
        scratch_types=[pltpu.VMEM((k_dup, n_chunks, w), I32), pltpu.VMEM((2, w, d), src.dtype),
                       pltpu.SemaphoreType.DMA((2,)), pltpu.SemaphoreType.DMA((2,))])
    def scatter_kernel(src_hbm, idx_hbm, out_hbm, idx_v, rows_v, rsem, wsem):
        wid = lax.axis_index("s") * SC_CORES + lax.axis_index("c")
        base = pl.multiple_of(wid * per_w, 8)
        for k in range(k_dup):
            pltpu.sync_copy(idx_hbm.at[k, wid], idx_v.at[k])

        def read(g, b):
            rows = src_hbm.at[pl.ds(pl.multiple_of(base + g * w, 8), w)]
            return pltpu.make_async_copy(rows, rows_v.at[b], rsem.at[b])

        def write(g, b, k):
            return pltpu.make_async_copy(rows_v.at[b], out_hbm.at[idx_v.at[k, g]], wsem.at[b])

        read(0, 0).start()

        @pl.loop(0, n_chunks, step=2)
        def _(g):
            for b in (0, 1):
                cur = g + b
                read(cur, b).wait()
                for k in range(k_dup):
                    write(cur, b, k).start()

                @pl.when(cur >= 1)
                def _():
                    for k in range(k_dup):
                        write(cur - 1, 1 - b, k).wait()

                @pl.when(cur + 1 < n_chunks)
                def _():
                    read(cur + 1, 1 - b).start()

        for k in range(k_dup):
            write(n_chunks - 1, 1, k).wait()

    return scatter_kernel(src, idx.reshape(k_dup, workers, n_chunks, w))


def _expert_kernel(be_ref, used_ref, x_ref, wg_ref, bg_ref, wu_ref, bu_ref, wd_ref, bd_ref, o_ref, wg_bf, wu_bf,
                   wd_bf):
    i = pl.program_id(0)

    @pl.when(jnp.logical_or(i == 0, be_ref[i] != be_ref[jnp.maximum(i - 1, 0)]))
    def _():
        wg_bf[...] = wg_ref[0, 0].astype(BF16)
        wu_bf[...] = wu_ref[0, 0].astype(BF16)
        wd_bf[...] = wd_ref[0, 0].astype(BF16)

    @pl.when(i < used_ref[0])
    def _():
        xb = _unpack_bf16_halves(x_ref[...]).astype(BF16)
        g = _dot(xb, wg_bf[...]) + bg_ref[0, 0]
        u = _dot(xb, wu_bf[...]) + bu_ref[0, 0]
        g = jnp.minimum(g, SWIGLU_LIMIT)
        u = jnp.clip(u, -SWIGLU_LIMIT, SWIGLU_LIMIT)
        act = (g * _sigmoid(SWIGLU_ALPHA * g) * (u + 1.0)).astype(BF16)
        o_ref[...] = _pack_bf16_halves(_dot(act, wd_bf[...]) + bd_ref[0, 0])


def expert_ffn(block_expert, blocks_used, x_sorted, layer, wg, bg, wu, bu, wd, bd):
    nblk = block_expert.shape[0]
    n_layers, _, d, f = wg.shape
    wspec = lambda r, c: pl.BlockSpec((1, 1, r, c), lambda i, be, used: (layer, be[i], 0, 0))
    grid_spec = pltpu.PrefetchScalarGridSpec(
        num_scalar_prefetch=2,
        grid=(nblk,),
        in_specs=[pl.BlockSpec((MOE_BLOCK, d // 2), lambda i, be, used: (i, 0)),
                  wspec(d, f), wspec(1, f), wspec(d, f), wspec(1, f), wspec(f, d), wspec(1, d)],
        out_specs=pl.BlockSpec((MOE_BLOCK, d // 2), lambda i, be, used: (i, 0)),
        scratch_shapes=[pltpu.VMEM((d, f), BF16), pltpu.VMEM((d, f), BF16), pltpu.VMEM((f, d), BF16)],
    )
    return pl.pallas_call(
        _expert_kernel,
        out_shape=jax.ShapeDtypeStruct((nblk * MOE_BLOCK, d // 2), U32),
        grid_spec=grid_spec,
        compiler_params=_cparams(("arbitrary",)),
        name="expert_ffn",
    )(block_expert, blocks_used, x_sorted, wg, bg.reshape(n_layers, N_EXP, 1, f), wu,
      bu.reshape(n_layers, N_EXP, 1, f), wd, bd.reshape(n_layers, N_EXP, 1, d))


def _combine_kernel(x_ref, mod_ref, prob_ref, y_ref, fw_ref, o_ref, *, final_norm):
    p = prob_ref[0]
    acc = _unpack_bf16_halves(y_ref[0, 0]) * p[:, 0:1]
    for k in range(1, TOP_K):
        acc = acc + _unpack_bf16_halves(y_ref[k, 0]) * p[:, k:k + 1]
    out = x_ref[0] + mod_ref[0, 0][5:6] * acc
    if final_norm:
        out = out * lax.rsqrt(jnp.mean(out * out, axis=-1, keepdims=True) + EPS) * fw_ref[...]
    o_ref[0] = out


def combine(x, modtab, probs, y_tok, final_w, ctx_len, final_norm):
    bsz, lt, d = x.shape
    tm = TOK_TILE
    ctx_tiles = ctx_len // tm
    return pl.pallas_call(
        functools.partial(_combine_kernel, final_norm=final_norm),
        out_shape=jax.ShapeDtypeStruct((bsz, lt, d), F32),
        grid=(bsz, lt // tm),
        in_specs=[pl.BlockSpec((1, tm, d), lambda b, i: (b, i, 0)),
                  pl.BlockSpec((1, 1, N_MOD, d), lambda b, i: (b, (i >= ctx_tiles).astype(I32), 0, 0)),
                  pl.BlockSpec((1, tm, TOP_K), lambda b, i: (b, i, 0)),
                  pl.BlockSpec((TOP_K, 1, tm, d // 2), lambda b, i: (0, b, i, 0)),
                  pl.BlockSpec((1, d), lambda b, i: (0, 0))],
        out_specs=pl.BlockSpec((1, tm, d), lambda b, i: (b, i, 0)),
        compiler_params=_cparams(("parallel", "parallel")),
        name="combine",
    )(x, modtab, probs, y_tok, final_w.reshape(1, d))


def moe_layer(x, routing, modtab, layer, wg, bg, wu, bu, wd, bd, final_w, ctx_len, final_norm):
    bsz, lt, d = x.shape
    t = bsz * lt
    h, idx, probs, rank, counts = routing
    counts = counts.reshape(N_EXP).astype(I32)
    padded = (counts + MOE_BLOCK - 1) // MOE_BLOCK * MOE_BLOCK
    ends = jnp.cumsum(padded)
    pstart = ends - padded
    idx = idx.reshape(t, TOP_K)
    dest = pstart[idx] + rank.reshape(t, TOP_K)
    nblk = -(-t * TOP_K // MOE_BLOCK) + N_EXP
    block_row0 = jnp.arange(nblk, dtype=I32) * MOE_BLOCK
    block_expert = jnp.minimum(jnp.sum((ends[None, :] <= block_row0[:, None]).astype(I32), axis=1), N_EXP - 1)
    dest_k = dest.T
    x_sorted = sc_scatter_rows(h.reshape(t, d // 2), dest_k, nblk * MOE_BLOCK)
    blocks_used = (ends[-1:] // MOE_BLOCK).astype(I32)
    y_sorted = expert_ffn(block_expert, blocks_used, x_sorted, layer, wg, bg, wu, bu, wd, bd)
    y_tok = sc_gather_rows(y_sorted, dest_k.reshape(TOP_K * t)).reshape(TOP_K, bsz, lt, d // 2)
    return combine(x, modtab, probs, y_tok, final_w, ctx_len, final_norm)


def even_layer(x, modtab, norm1_w, w_in, w_out, mu, w0, w2, a0, a2, g2, k_k, k_a, r_k, ln_w, ln_b, lb, hg_norm_w,
               norm2_w, router_w, router_b, ctx_len):
    w_in = w_in.astype(BF16)
    pr, ph = in_proj(x, modtab, norm1_w, [w_in[:, :RW_PROJ], w_in[:, RW_PROJ:]], ctx_len)
    r, v, kk, g, bv, lw, km, bb = rwkv_prep(pr, mu, w0, w2, a0, a2, g2, k_k, k_a, r_k, ctx_len)
    o_r = rwkv_scan(r, v, kk, lw, km, bb, ctx_len)
    o_g = gla_scan(ph, lb, ctx_len)
    return even_out(x, modtab, o_r, bv, g, o_g, ph, ln_w, ln_b, hg_norm_w, w_out.astype(BF16), norm2_w, router_w,
                    router_b, ctx_len)


def _dir_slab(cols, width):
    halves = []
    for z in range(2):
        parts = [c[:, z * (c.shape[1] // 2):(z + 1) * (c.shape[1] // 2)] for c in cols]
        used = sum(p.shape[1] for p in parts)
        parts.append(jnp.zeros((cols[0].shape[0], width - used), cols[0].dtype))
        halves.append(jnp.concatenate(parts, axis=1))
    return jnp.concatenate(halves, axis=1)


def _dir_rows(vals, width):
    rows = jnp.concatenate(vals, axis=1)
    return jnp.pad(rows, ((0, 0), (0, width - rows.shape[1]))).reshape(2, 1, width)


def odd_layer(x, modtab, norm1_w, w_in, w_out, s_conv_w, s_conv_b, dt_bias, a_log, d_skip, s_norm_w, m_conv_w,
              m_conv_b, i_bias, f_bias, m_norm_w, norm2_w, router_w, router_b, ctx_len):
    o = 0
    cols = {}
    for name, width in (("zg", SSD_INNER), ("xbc", SSD_CONV_CH), ("dt", 2 * SSD_H), ("qk", 2 * ML_QK),
                        ("v", ML_VW), ("og", ML_VW), ("ig", 2 * ML_H), ("fg", 2 * ML_H)):
        cols[name] = w_in[:, o:o + width]
        o += width
    slab_m = jnp.concatenate([cols["qk"], cols["v"], cols["og"]], axis=1).astype(BF16)
    slab_small = _dir_slab([cols["dt"], cols["ig"], cols["fg"]], SMALL_W).astype(BF16)
    pxbc, zg, pm, small = in_proj(x, modtab, norm1_w, [cols["xbc"].astype(BF16), cols["zg"].astype(BF16), slab_m,
                                                       slab_small], ctx_len)
    xbc, qk = odd_prep(pxbc, pm, s_conv_w, s_conv_b, m_conv_w, m_conv_b, ctx_len)
    zeros_g = jnp.zeros((2, 2 * ML_H), F32)
    y_s = ssd_scan(xbc, small, _dir_rows([dt_bias, zeros_g], SMALL_W), _dir_rows([a_log, zeros_g], SMALL_W), ctx_len)
    y_m = mlstm_scan(qk, pm, small, _dir_rows([jnp.zeros((2, SSD_H), F32), i_bias, f_bias], SMALL_W), ctx_len)
    return odd_out(x, modtab, y_s, xbc, zg, y_m, pm, jnp.repeat(d_skip, SSD_P), s_norm_w, m_norm_w,
                   w_out.astype(BF16), norm2_w, router_w, router_b, ctx_len)


def _to_col_major(t):
    b, s, ch = t.shape
    return t.reshape(b, s // GRID_W, GRID_W, ch).transpose(0, 2, 1, 3).reshape(b, s, ch)


def _to_row_major(t):
    b, s, ch = t.shape
    return t.reshape(b, GRID_W, s // GRID_W, ch).transpose(0, 2, 1, 3).reshape(b, s, ch)


def kernel(x, c, ctx, c_ctx, ada_w, ada_b, norm1_w, norm2_w, even_w_in, even_w_out, rwkv_mu, rwkv_w0, rwkv_w2, rwkv_a0, rwkv_a2, rwkv_g2, rwkv_k_k, rwkv_k_a, rwkv_r_k, rwkv_ln_w, rwkv_ln_b, hgrn_lower_bounds, hgrn_norm_w, odd_w_in, odd_w_out, ssd_conv_w, ssd_conv_b, ssd_dt_bias, ssd_a_log, ssd_d, ssd_norm_w, mlstm_conv_w, mlstm_conv_b, mlstm_i_bias, mlstm_f_bias, mlstm_norm_w, router_w, router_b, exp_w_gate, exp_b_gate, exp_w_up, exp_b_up, exp_w_down, exp_b_down, final_norm_w):
    lc = ctx.shape[1]
    lower_bounds = jnp.cumsum(jax.nn.softmax(hgrn_lower_bounds.astype(F32), axis=0), axis=0)
    experts = (exp_w_gate, exp_b_gate, exp_w_up, exp_b_up, exp_w_down, exp_b_down, final_norm_w)
    xa = jnp.concatenate([ctx, x], axis=1)
    modtab = _mod_table(c, c_ctx, ada_w[0], ada_b[0])
    xa, routing = even_layer(xa, modtab, norm1_w[0], even_w_in[0], even_w_out[0], rwkv_mu[0], rwkv_w0[0], rwkv_w2[0],
                             rwkv_a0[0], rwkv_a2[0], rwkv_g2[0], rwkv_k_k[0], rwkv_k_a[0], rwkv_r_k[0],
                             rwkv_ln_w[0], rwkv_ln_b[0], lower_bounds[0], hgrn_norm_w[0], norm2_w[0], router_w[0],
                             router_b[0], lc)
    xa = moe_layer(xa, routing, modtab, 0, *experts, lc, False)
    xa = jnp.concatenate([xa[:, :lc], _to_col_major(xa[:, lc:])], axis=1)
    modtab = _mod_table(c, c_ctx, ada_w[1], ada_b[1])
    xl, routing = odd_layer(xa, modtab, norm1_w[1], odd_w_in[0], odd_w_out[0], ssd_conv_w[0], ssd_conv_b[0],
                            ssd_dt_bias[0], ssd_a_log[0], ssd_d[0], ssd_norm_w[0], mlstm_conv_w[0], mlstm_conv_b[0],
                            mlstm_i_bias[0], mlstm_f_bias[0], mlstm_norm_w[0], norm2_w[1], router_w[1], router_b[1],
                            lc)
    xl = moe_layer(xl, routing, modtab, 1, *experts, 0, True)
    return _to_row_major(xl)
```

```python
import functools
import math

import jax
import jax.numpy as jnp
from jax import lax
from jax.experimental import pallas as pl
from jax.experimental.pallas import tpu as pltpu
from jax.experimental.pallas import tpu_sc as plsc

F32 = jnp.float32
BF16 = jnp.bfloat16
I32 = jnp.int32
U32 = jnp.uint32
HI = lax.Precision.HIGHEST

D_MODEL = 1024
GRID_W = 64
N_MOD = 6
EPS = 1e-6
RW_H, RW_N = 8, 64
RW_W = RW_H * RW_N
RW_RANK = 64
RW_GATE_RANK = 128
RW_GN_EPS = 64e-5
RW_PROJ = 3 * RW_W + 4 * RW_RANK + RW_GATE_RANK
RW_CHUNK = 64
RW_GROUP = 4
HG_H, HG_DK, HG_DV = 4, 128, 128
HG_W = HG_H * HG_DK
HG_CHUNK = 32
HG_PROJ = 5 * HG_W
SSD_H, SSD_P, SSD_G, SSD_N = 16, 64, 2, 128
SSD_INNER = SSD_H * SSD_P
SSD_CONV_CH = SSD_INNER + 2 * SSD_G * SSD_N
SSD_CHUNK = 64
ML_H, ML_DK, ML_DV = 4, 128, 256
ML_QK = ML_H * ML_DK
ML_VW = ML_H * ML_DV
ML_CHUNK = 64
ML_M_INIT = -1e30
CONV_K = 5
N_EXP = 32
TOP_K = 4
D_FF = 1024
SWIGLU_LIMIT = 7.0
SWIGLU_ALPHA = 1.702
MOE_BLOCK = 512

TOK_TILE = 256
SCAN_BLOCK = 256
HALO = 8
SMALL_W = 128
VMEM_LIMIT = 56 * 1024 * 1024
SC_CORES, SC_SUBCORES = 2, 16
SC_GATHER_ROWS = 32


def _cparams(sem):
    return pltpu.CompilerParams(dimension_semantics=sem, vmem_limit_bytes=VMEM_LIMIT)


def _dot(a, b, precision=None):
    return jnp.dot(a, b, preferred_element_type=F32, precision=precision)


def _dot_nt(a, b, precision=None):
    return lax.dot_general(a, b, (((1,), (1,)), ((), ())), preferred_element_type=F32, precision=precision)


def _dot_tn(a, b, precision=None):
    return lax.dot_general(a, b, (((0,), (0,)), ((), ())), preferred_element_type=F32, precision=precision)


def _bf16_parts(x):
    hi = x.astype(BF16)
    rest = x - hi.astype(F32)
    mid = rest.astype(BF16)
    return hi, mid, (rest - mid.astype(F32)).astype(BF16)


def _dot_into_01(x, sel):
    sel = sel.astype(BF16)
    p0, p1, p2 = _bf16_parts(x)
    return _dot(p0, sel) + _dot(p1, sel) + _dot(p2, sel)


def _dot_01_into(sel, x):
    sel = sel.astype(BF16)
    p0, p1, p2 = _bf16_parts(x)
    return _dot(sel, p0) + _dot(sel, p1) + _dot(sel, p2)


def _pack_bf16_halves(x):
    n = x.shape[1] // 2
    bits = lax.bitcast_convert_type(x.astype(BF16).astype(F32), U32)
    return bits[:, :n] | (bits[:, n:] >> 16)


def _unpack_bf16_halves(w):
    hi = lax.bitcast_convert_type(w & jnp.uint32(0xFFFF0000), F32)
    lo = lax.bitcast_convert_type(w << 16, F32)
    return jnp.concatenate([hi, lo], axis=1)


def _sigmoid(x):
    return 1.0 / (1.0 + jnp.exp(-x))


def _silu(x):
    return x * _sigmoid(x)


def _softplus(x):
    return jnp.maximum(x, 0.0) + jnp.log(1.0 + jnp.exp(-jnp.abs(x)))


def _dir_masks(z, c):
    ti = lax.broadcasted_iota(I32, (c, c), 0)
    si = lax.broadcasted_iota(I32, (c, c), 1)
    d = (si - ti) * jnp.where(z == 0, 1, -1)
    return d <= 0, d < 0


def _time_block(z, n, ctx_blocks, n_blocks):
    rev = jnp.where(n < ctx_blocks, ctx_blocks - 1 - n, n_blocks - 1 - (n - ctx_blocks))
    return jnp.where(z == 0, n, rev)


def _mod_kernel(c_ref, w_ref, b_ref, o_ref):
    o_ref[...] = _dot(_silu(c_ref[...]), w_ref[...]) + b_ref[...]


def ada_mod(cc, w, b):
    rows, d = cc.shape
    n = w.shape[1]
    tn = 1536
    return pl.pallas_call(
        _mod_kernel,
        out_shape=jax.ShapeDtypeStruct((rows, n), F32),
        grid=(n // tn,),
        in_specs=[pl.BlockSpec((rows, d), lambda j: (0, 0)),
                  pl.BlockSpec((d, tn), lambda j: (0, j)),
                  pl.BlockSpec((1, tn), lambda j: (0, j))],
        out_specs=pl.BlockSpec((rows, tn), lambda j: (0, j)),
        compiler_params=_cparams(("arbitrary",)),
        name="ada_mod",
    )(cc, w, b.reshape(1, n))


def _mod_table(c, c_ctx, w, b):
    bsz = c.shape[0]
    rows = -(-(bsz + 1) // 8) * 8
    cc = jnp.zeros((rows, D_MODEL), F32).at[:bsz].set(c).at[bsz].set(c_ctx)
    m = ada_mod(cc, w, b)
    mod = m[:bsz].reshape(bsz, 1, N_MOD, D_MODEL)
    mod_c = jnp.broadcast_to(m[bsz].reshape(1, 1, N_MOD, D_MODEL), (bsz, 1, N_MOD, D_MODEL))
    return jnp.concatenate([mod_c, mod], axis=1)


def _norm_mod(x, nw, m, shift_idx):
    y = x * lax.rsqrt(jnp.mean(x * x, axis=-1, keepdims=True) + EPS) * nw
    return y * (1.0 + m[shift_idx + 1:shift_idx + 2]) + m[shift_idx:shift_idx + 1]


def _in_proj_kernel(x_ref, mod_ref, nw_ref, *rest, n_out):
    w_refs, o_refs = rest[:n_out], rest[n_out:]
    h = _norm_mod(x_ref[0], nw_ref[...], mod_ref[0, 0], 0).astype(BF16)
    for w_ref, o_ref in zip(w_refs, o_refs):
        o_ref[0] = _dot(h, w_ref[...])


def in_proj(x, modtab, nw, slabs, ctx_len):
    bsz, lt, d = x.shape
    tm = TOK_TILE
    ctx_tiles = ctx_len // tm
    in_specs = [pl.BlockSpec((1, tm, d), lambda b, i: (b, i, 0)),
                pl.BlockSpec((1, 1, N_MOD, d), lambda b, i: (b, (i >= ctx_tiles).astype(I32), 0, 0)),
                pl.BlockSpec((1, d), lambda b, i: (0, 0))]
    in_specs += [pl.BlockSpec(w.shape, lambda b, i: (0, 0)) for w in slabs]
    return pl.pallas_call(
        functools.partial(_in_proj_kernel, n_out=len(slabs)),
        out_shape=[jax.ShapeDtypeStruct((bsz, lt, w.shape[1]), F32) for w in slabs],
        grid=(bsz, lt // tm),
        in_specs=in_specs,
        out_specs=[pl.BlockSpec((1, tm, w.shape[1]), lambda b, i: (b, i, 0)) for w in slabs],
        compiler_params=_cparams(("parallel", "parallel")),
        name="in_proj",
    )(x, modtab, nw.reshape(1, d), *slabs)


def _halo_specs(tm, width, lt):
    per = tm // HALO
    last = lt // HALO - 1
    return [pl.BlockSpec((1, tm, width), lambda b, i: (b, i, 0)),
            pl.BlockSpec((1, HALO, width), lambda b, i: (b, jnp.maximum(i * per - 1, 0), 0)),
            pl.BlockSpec((1, HALO, width), lambda b, i: (b, jnp.minimum((i + 1) * per, last), 0))]


def _segment_edges(i, tm, ctx_len, lt):
    row0 = i * tm
    has_prev = jnp.logical_and(row0 != 0, row0 != ctx_len)
    has_next = jnp.logical_and(row0 + tm != ctx_len, row0 + tm != lt)
    return has_prev, has_next


def _rwkv_prep_kernel(p_ref, pp_ref, pn_ref, mu_ref, w0_ref, w2_ref, a0_ref, a2_ref, g2_ref, kk_ref, ka_ref,
                      rk_ref, bd_ref, r_o, v_o, kk_o, g_o, bv_o, lw_o, km_o, bb_o, *, tm, ctx_len, lt):
    has_prev, has_next = _segment_edges(pl.program_id(1), tm, ctx_len, lt)
    p = p_ref[0]
    prev_row = jnp.where(has_prev, pp_ref[0, HALO - 1:HALO, :], 0.0)
    next_row = jnp.where(has_next, pn_ref[0, 0:1, :], 0.0)
    rid = lax.broadcasted_iota(I32, (tm, 1), 0)
    up = jnp.where(rid == 0, prev_row, pltpu.roll(p, 1, 0))
    dn = jnp.where(rid == tm - 1, next_row, pltpu.roll(p, tm - 1, 0))
    p = p + (0.5 * (up + dn) - p) * mu_ref[...]
    w = RW_W
    r, k, v = p[:, 0:w], p[:, w:2 * w], p[:, 2 * w:3 * w]
    o = 3 * w
    wd = jnp.tanh(p[:, o:o + 2 * RW_RANK])
    ad = p[:, o + 2 * RW_RANK:o + 4 * RW_RANK]
    gd = p[:, o + 4 * RW_RANK:o + 4 * RW_RANK + RW_GATE_RANK]
    bd = bd_ref[...]
    kk = k * kk_ref[...]
    kk = kk * lax.rsqrt(_dot_into_01(kk * kk, bd) + 1e-12)
    r_o[0] = r
    v_o[0] = v
    kk_o[0] = kk
    g_o[0] = _dot(_sigmoid(gd), g2_ref[...])
    km_sum = jnp.zeros_like(k)
    for z in range(2):
        sl = slice(z * RW_RANK, (z + 1) * RW_RANK)
        w_pre = w0_ref[z:z + 1, :] + _dot(wd[:, sl], w2_ref[z])
        lw_o[z, 0] = -_sigmoid(w_pre) * math.exp(-0.5)
        a = _sigmoid(a0_ref[z:z + 1, :] + _dot(ad[:, sl], a2_ref[z]))
        km = k * (1.0 + (a - 1.0) * ka_ref[...])
        km_o[z, 0] = km
        bb_o[z, 0] = kk * a
        km_sum = km_sum + km
    bv_o[0] = _dot_into_01(r * km_sum * rk_ref[...], bd) * v


def _head_block_ones(width, head):
    i = jnp.arange(width) // head
    return (i[:, None] == i[None, :]).astype(F32)


def rwkv_prep(p, mu, w0, w2, a0, a2, g2, k_k, k_a, r_k, ctx_len):
    bsz, lt, width = p.shape
    tm = TOK_TILE
    w = RW_W
    full = lambda a: pl.BlockSpec(a.shape, lambda b, i: (0,) * a.ndim)
    params = [mu.reshape(1, width), w0, w2, a0, a2, g2, k_k.reshape(1, w), k_a.reshape(1, w), r_k.reshape(1, w),
              _head_block_ones(w, RW_N)]
    one = jax.ShapeDtypeStruct((bsz, lt, w), F32)
    two = jax.ShapeDtypeStruct((2, bsz, lt, w), F32)
    one_spec = pl.BlockSpec((1, tm, w), lambda b, i: (b, i, 0))
    two_spec = pl.BlockSpec((2, 1, tm, w), lambda b, i: (0, b, i, 0))
    return pl.pallas_call(
        functools.partial(_rwkv_prep_kernel, tm=tm, ctx_len=ctx_len, lt=lt),
        out_shape=[one] * 5 + [two] * 3,
        grid=(bsz, lt // tm),
        in_specs=_halo_specs(tm, width, lt) + [full(a) for a in params],
        out_specs=[one_spec] * 5 + [two_spec] * 3,
        compiler_params=_cparams(("parallel", "parallel")),
        name="rwkv_prep",
    )(p, p, p, *params)


def _unit_lower_inverses(n_mats, eye, c):
    xs = [eye - n for n in n_mats]
    ps = [_dot(n, n) for n in n_mats]
    span = 4
    while True:
        xs = [x + _dot(x, p) for x, p in zip(xs, ps)]
        if span >= c:
            return xs
        ps = [_dot(p, p) for p in ps]
        span *= 2


def _rwkv_scan_kernel(r_ref, v_ref, kk_ref, lw_ref, km_ref, bb_ref, o_ref, s_ref, *, tb, c):
    z = pl.program_id(0)
    nsub = tb // c

    @pl.when(pl.program_id(2) == 0)
    def _():
        s_ref[...] = jnp.zeros_like(s_ref)

    incl, strict = _dir_masks(z, c)
    incl_f, strict_f = incl.astype(F32), strict.astype(F32)
    eye = incl_f - strict_f

    heads = [slice(h * RW_N, (h + 1) * RW_N) for h in range(RW_H)]

    def group(g, s_old):
        subs = []
        for i in range(RW_GROUP):
            j = g * RW_GROUP + i
            rows = pl.ds(pl.multiple_of(jnp.where(z == 0, j, nsub - 1 - j) * c, c), c)
            r, v, kk = r_ref[0, rows, :], v_ref[0, rows, :], kk_ref[0, rows, :]
            lw, km, bb = lw_ref[0, 0, rows, :], km_ref[0, 0, rows, :], bb_ref[0, 0, rows, :]
            cum = _dot_01_into(incl_f, lw)
            tot = jnp.sum(lw, axis=0, keepdims=True)
            e_neg = jnp.exp(-cum)
            e_end = jnp.exp(tot - cum)
            a_t = kk * jnp.exp(cum - lw)
            r_t = r * jnp.exp(cum)
            b_t, k_t = bb * e_neg, km * e_neg
            b_p, k_p = bb * e_end, km * e_end
            subs.append(dict(
                rows=rows, e_tot=jnp.exp(tot), a=[a_t[:, hs] for hs in heads], r=[r_t[:, hs] for hs in heads],
                ar=[jnp.concatenate([a_t[:, hs], r_t[:, hs]], axis=0) for hs in heads],
                bk=[jnp.concatenate([b_t[:, hs], k_t[:, hs]], axis=0) for hs in heads],
                bkp=[jnp.concatenate([b_p[:, hs], k_p[:, hs]], axis=0) for hs in heads],
                v=[v[:, hs] for hs in heads]))
        flat = [(i, h) for i in range(RW_GROUP) for h in range(RW_H)]
        sc = [_dot_nt(subs[i]["ar"][h], subs[i]["bk"][h]) for i, h in flat]
        m_ak_v = [_dot(x[:c, c:] * strict_f, subs[i]["v"][h]) for x, (i, h) in zip(sc, flat)]
        inv = _unit_lower_inverses([x[:c, :c] * strict_f for x in sc], eye, c)
        w1r = [jnp.concatenate([_dot(n, subs[i]["a"][h]), subs[i]["r"][h]], axis=0) for n, (i, h) in zip(inv, flat)]
        w2 = [_dot(n, m) for n, m in zip(inv, m_ak_v)]
        n_rbk = [jnp.concatenate([x[c:, :c] * incl_f, x[c:, c:] * incl_f], axis=1) for x in sc]
        state = list(s_old)
        for i in range(RW_GROUP):
            sub = subs[i]
            base = i * RW_H
            ws = [_dot_nt(w1r[base + h], state[h]) for h in range(RW_H)]
            uv = [jnp.concatenate([-(ws[h][:c] + w2[base + h]), sub["v"][h]], axis=0) for h in range(RW_H)]
            outs = [ws[h][c:] + _dot(n_rbk[base + h], uv[h]) for h in range(RW_H)]
            o_ref[0, 0, sub["rows"], :] = jnp.concatenate(outs, axis=1)
            state = [state[h] * sub["e_tot"][:, heads[h]] + _dot_tn(uv[h], sub["bkp"][h]) for h in range(RW_H)]
        return tuple(state)

    s_fin = lax.fori_loop(0, nsub // RW_GROUP, group, tuple(s_ref[h] for h in range(RW_H)))
    for h in range(RW_H):
        s_ref[h] = s_fin[h]


def rwkv_scan(r, v, kk, lw, km, bb, ctx_len):
    bsz, lt, w = r.shape
    tb = SCAN_BLOCK
    nb, cb = lt // tb, ctx_len // tb
    tmap = lambda z, b, n: _time_block(z, n, cb, nb)
    one = pl.BlockSpec((1, tb, w), lambda z, b, n: (b, tmap(z, b, n), 0))
    two = pl.BlockSpec((1, 1, tb, w), lambda z, b, n: (z, b, tmap(z, b, n), 0))
    return pl.pallas_call(
        functools.partial(_rwkv_scan_kernel, tb=tb, c=RW_CHUNK),
        out_shape=jax.ShapeDtypeStruct((2, bsz, lt, w), F32),
        grid=(2, bsz, nb),
        in_specs=[one, one, one, two, two, two],
        out_specs=two,
        scratch_shapes=[pltpu.VMEM((RW_H, RW_N, RW_N), F32)],
        compiler_params=_cparams(("parallel", "parallel", "arbitrary")),
        name="rwkv_scan",
    )(r, v, kk, lw, km, bb)


def _gla_kernel(q_ref, f_ref, i_ref, lb_ref, o_ref, s_ref, *, tb, c):
    z = pl.program_id(0)
    nsub = tb // c

    @pl.when(pl.program_id(2) == 0)
    def _():
        s_ref[...] = jnp.zeros_like(s_ref)

    lb = lb_ref[...]
    heads = [slice(h * HG_DK, (h + 1) * HG_DK) for h in range(HG_H)]
    ti = lax.broadcasted_iota(I32, (tb, tb), 0)
    si = lax.broadcasted_iota(I32, (tb, tb), 1)
    same_chunk = (ti // c) == (si // c)

    def run(zs):
        incl = jnp.logical_and(same_chunk, (si <= ti) if zs == 0 else (si >= ti))
        q = _silu(q_ref[0]) * HG_DK ** -0.5
        f = f_ref[0]
        v = i_ref[0]
        sig_f = _sigmoid(f)
        log_f = jnp.log(lb + (1.0 - lb) * sig_f)
        k = (1.0 - lb) * (1.0 - sig_f)
        cum = _dot_01_into(incl.astype(F32), log_f)
        last = c - 1 if zs == 0 else 0
        tot = jnp.concatenate([jnp.broadcast_to(cum[j * c + last:j * c + last + 1, :], (c, cum.shape[1]))
                               for j in range(nsub)], axis=0)
        q_in = q * jnp.exp(cum)
        k_in = k * jnp.exp(-cum)
        k_end = k * jnp.exp(tot - cum)
        e_tot = jnp.exp(tot)
        intra = [_dot(jnp.where(incl, _dot_nt(q_in[:, hs], k_in[:, hs]), 0.0), v[:, hs]) for hs in heads]
        order = range(nsub) if zs == 0 else range(nsub - 1, -1, -1)
        grow = {(j, h): _dot_tn(v[j * c:(j + 1) * c, hs], k_end[j * c:(j + 1) * c, hs])
                for j in order for h, hs in enumerate(heads)}
        state = [s_ref[h] for h in range(HG_H)]
        inter = {}
        for j in order:
            rows = slice(j * c, (j + 1) * c)
            for h, hs in enumerate(heads):
                inter[(j, h)] = _dot_nt(q_in[rows, hs], state[h])
                state[h] = state[h] * e_tot[j * c:j * c + 1, hs] + grow[(j, h)]
        for h, hs in enumerate(heads):
            o_ref[0, 0, :, hs] = intra[h] + jnp.concatenate([inter[(j, h)] for j in range(nsub)], axis=0)
            s_ref[h] = state[h]

    for zs in range(2):
        pl.when(z == zs)(functools.partial(run, zs))


def gla_scan(ph, lb, ctx_len):
    bsz, lt, _ = ph.shape
    w = HG_W
    tb = SCAN_BLOCK
    nb, cb = lt // tb, ctx_len // tb
    tmap = lambda z, n: _time_block(z, n, cb, nb)
    return pl.pallas_call(
        functools.partial(_gla_kernel, tb=tb, c=HG_CHUNK),
        out_shape=jax.ShapeDtypeStruct((2, bsz, lt, w), F32),
        grid=(2, bsz, nb),
        in_specs=[pl.BlockSpec((1, tb, w), lambda z, b, n: (b, tmap(z, n), 0)),
                  pl.BlockSpec((1, tb, w), lambda z, b, n: (b, tmap(z, n), 1 + z)),
                  pl.BlockSpec((1, tb, w), lambda z, b, n: (b, tmap(z, n), 3)),
                  pl.BlockSpec((1, w), lambda z, b, n: (0, 0))],
        out_specs=pl.BlockSpec((1, 1, tb, w), lambda z, b, n: (z, b, tmap(z, n), 0)),
        scratch_shapes=[pltpu.VMEM((HG_H, HG_DV, HG_DK), F32)],
        compiler_params=_cparams(("parallel", "parallel", "arbitrary")),
        name="gla_scan",
    )(ph, ph, ph, lb.reshape(1, w))


def _group_rms(y, groups, gain):
    width = y.shape[-1] // groups
    parts = []
    for g in range(groups):
        yg = y[:, g * width:(g + 1) * width]
        parts.append(yg * lax.rsqrt(jnp.mean(yg * yg, axis=-1, keepdims=True) + EPS))
    return jnp.concatenate(parts, axis=1) * gain


def _even_out_kernel(x_ref, mod_ref, or_ref, bv_ref, g_ref, og_ref, gate_ref, lnw_ref, lnb_ref, nw_ref, bd_ref,
                     w_ref, o_ref):
    bd = bd_ref[...]
    o = or_ref[0, 0] + or_ref[1, 0]
    mu = _dot_into_01(o, bd) * (1.0 / RW_N)
    oc = o - mu
    var = _dot_into_01(oc * oc, bd) * (1.0 / RW_N)
    yr = (oc * lax.rsqrt(var + RW_GN_EPS) * lnw_ref[...] + lnb_ref[...] + bv_ref[0]) * g_ref[0]
    og = og_ref[0, 0] + og_ref[1, 0]
    yh = _group_rms(og, HG_H, nw_ref[...]) * _silu(gate_ref[0])
    y = jnp.concatenate([yr, yh], axis=1).astype(BF16)
    o_ref[0] = x_ref[0] + mod_ref[0, 0][2:3] * _dot(y, w_ref[...])


def even_out(x, modtab, o_r, bv, g, o_g, ph, ln_w, ln_b, norm_w, w_out, ctx_len):
    bsz, lt, d = x.shape
    tm = TOK_TILE
    ctx_tiles = ctx_len // tm
    w = RW_W
    tok = lambda width, col=0: pl.BlockSpec((1, tm, width), lambda b, i: (b, i, col))
    two = pl.BlockSpec((2, 1, tm, w), lambda b, i: (0, b, i, 0))
    row = lambda width: pl.BlockSpec((1, width), lambda b, i: (0, 0))
    return pl.pallas_call(
        _even_out_kernel,
        out_shape=jax.ShapeDtypeStruct((bsz, lt, d), F32),
        grid=(bsz, lt // tm),
        in_specs=[tok(d),
                  pl.BlockSpec((1, 1, N_MOD, d), lambda b, i: (b, (i >= ctx_tiles).astype(I32), 0, 0)),
                  two, tok(w), tok(w), two, tok(HG_W, 4), row(w), row(w), row(HG_W),
                  pl.BlockSpec((w, w), lambda b, i: (0, 0)),
                  pl.BlockSpec(w_out.shape, lambda b, i: (0, 0))],
        out_specs=tok(d),
        compiler_params=_cparams(("parallel", "parallel")),
        name="even_out",
    )(x, modtab, o_r, bv, g, o_g, ph, ln_w.reshape(1, w), ln_b.reshape(1, w), norm_w.reshape(1, HG_W),
      _head_block_ones(w, RW_N), w_out)


def _conv_silu(x_ref, xp_ref, xn_ref, w_ref, b_ref, has_prev, has_next, tm):
    prev = jnp.where(has_prev, xp_ref[0], 0.0)
    nxt = jnp.where(has_next, xn_ref[0], 0.0)
    ext = jnp.concatenate([prev, x_ref[0], nxt], axis=0)
    rows = tm + 2 * HALO
    pad = CONV_K // 2
    acc = b_ref[...] + jnp.zeros((tm, ext.shape[1]), F32)
    for j in range(CONV_K):
        d = j - pad
        sh = ext if d == 0 else pltpu.roll(ext, (-d) % rows, 0)
        acc = acc + sh[HALO:HALO + tm] * w_ref[j:j + 1, :]
    return _silu(acc)


def _odd_prep_kernel(xs_ref, xsp_ref, xsn_ref, xm_ref, xmp_ref, xmn_ref, ws_ref, bs_ref, wm_ref, bm_ref,
                     os_ref, om_ref, *, tm, ctx_len, lt):
    has_prev, has_next = _segment_edges(pl.program_id(1), tm, ctx_len, lt)
    os_ref[0] = _conv_silu(xs_ref, xsp_ref, xsn_ref, ws_ref, bs_ref, has_prev, has_next, tm)
    om_ref[0] = _conv_silu(xm_ref, xmp_ref, xmn_ref, wm_ref, bm_ref, has_prev, has_next, tm)


def odd_prep(xbc, qk, ws, bs, wm, bm, ctx_len):
    bsz, lt, cs = xbc.shape
    cm = wm.shape[1]
    tm = TOK_TILE
    full = lambda a: pl.BlockSpec(a.shape, lambda b, i: (0,) * a.ndim)
    params = [ws, bs.reshape(1, cs), wm, bm.reshape(1, cm)]
    return pl.pallas_call(
        functools.partial(_odd_prep_kernel, tm=tm, ctx_len=ctx_len, lt=lt),
        out_shape=[jax.ShapeDtypeStruct((bsz, lt, cs), F32), jax.ShapeDtypeStruct((bsz, lt, cm), F32)],
        grid=(bsz, lt // tm),
        in_specs=_halo_specs(tm, cs, lt) + _halo_specs(tm, cm, lt) + [full(a) for a in params],
        out_specs=[pl.BlockSpec((1, tm, cs), lambda b, i: (b, i, 0)),
                   pl.BlockSpec((1, tm, cm), lambda b, i: (b, i, 0))],
        compiler_params=_cparams(("parallel", "parallel")),
        name="odd_prep",
    )(xbc, xbc, xbc, qk, qk, qk, *params)


def _ssd_kernel(x_ref, b_ref, c_ref, sm_ref, dtb_ref, alog_ref, ex_ref, o_ref, s_ref, *, tb, c):
    z = pl.program_id(0)
    nsub = tb // c
    hg = SSD_H // SSD_G

    @pl.when(pl.program_id(2) == 0)
    def _():
        s_ref[...] = jnp.zeros_like(s_ref)

    incl, _ = _dir_masks(z, c)
    incl_f = incl.astype(F32)
    ex = ex_ref[...]

    gw = hg * SSD_P

    def local(j):
        jj = jnp.where(z == 0, j, nsub - 1 - j)
        rows = pl.ds(pl.multiple_of(jj * c, c), c)
        x = x_ref[0, rows, :]
        bm = b_ref[0, rows, :]
        cm = c_ref[0, rows, :]
        dt = _softplus(sm_ref[0, rows, :] + dtb_ref[0])
        la = -dt * jnp.exp(alog_ref[0])
        a_cum = _dot_01_into(incl_f, la)
        a_tot = jnp.sum(la, axis=0, keepdims=True)
        a_cum_t = a_cum.T
        e_tot_col = jnp.exp(jnp.where(z == 0, a_cum_t[:, c - 1:c], a_cum_t[:, 0:1]))
        wide = _dot(jnp.concatenate([dt, jnp.exp(a_cum), jnp.exp(a_tot - a_cum)], axis=0).astype(BF16),
                    ex.astype(BF16))
        xdt = x * wide[:c]
        din, dend = wide[c:2 * c], wide[2 * c:]
        xd_end = xdt * dend
        bgs = [bm[:, g * SSD_N:(g + 1) * SSD_N] for g in range(SSD_G)]
        cgs = [cm[:, g * SSD_N:(g + 1) * SSD_N] for g in range(SSD_G)]
        cb = [_dot_nt(cg, bg) for cg, bg in zip(cgs, bgs)]
        grow = [_dot_tn(xd_end[:, g * gw:(g + 1) * gw], bg) for g, bg in enumerate(bgs)]
        intra = []
        for h in range(SSD_H):
            seg = a_cum[:, h:h + 1] - a_cum_t[h:h + 1, :]
            wmat = cb[h // hg] * jnp.where(incl, jnp.exp(seg), 0.0)
            intra.append(_dot(wmat, xdt[:, h * SSD_P:(h + 1) * SSD_P]))
        decay = [jnp.concatenate([jnp.broadcast_to(e_tot_col[h:h + 1, :], (SSD_P, 1))
                                  for h in range(g * hg, (g + 1) * hg)], axis=0) for g in range(SSD_G)]
        return dict(rows=rows, cgs=cgs, din=din, intra=jnp.concatenate(intra, axis=1), grow=grow, decay=decay)

    subs = [local(j) for j in range(nsub)]
    state = [s_ref[g] for g in range(SSD_G)]
    for sub in subs:
        inter = [_dot_nt(cg, s) for cg, s in zip(sub["cgs"], state)]
        o_ref[0, 0, sub["rows"], :] = sub["intra"] + sub["din"] * jnp.concatenate(inter, axis=1)
        state = [s * d + g for s, d, g in zip(state, sub["decay"], sub["grow"])]
    for g in range(SSD_G):
        s_ref[g] = state[g]


def _head_expand(heads, width, rows):
    h = jnp.arange(rows)[:, None]
    col = jnp.arange(heads * width)[None, :] // width
    return (h == col).astype(F32)


def ssd_scan(xbc, small, dtb, alog, ctx_len):
    bsz, lt, _ = xbc.shape
    tb = SCAN_BLOCK
    nb, cb = lt // tb, ctx_len // tb
    tmap = lambda z, n: _time_block(z, n, cb, nb)
    gw = SSD_G * SSD_N
    return pl.pallas_call(
        functools.partial(_ssd_kernel, tb=tb, c=SSD_CHUNK),
        out_shape=jax.ShapeDtypeStruct((2, bsz, lt, SSD_INNER), F32),
        grid=(2, bsz, nb),
        in_specs=[pl.BlockSpec((1, tb, SSD_INNER), lambda z, b, n: (b, tmap(z, n), 0)),
                  pl.BlockSpec((1, tb, gw), lambda z, b, n: (b, tmap(z, n), SSD_INNER // gw)),
                  pl.BlockSpec((1, tb, gw), lambda z, b, n: (b, tmap(z, n), SSD_INNER // gw + 1)),
                  pl.BlockSpec((1, tb, SMALL_W), lambda z, b, n: (b, tmap(z, n), z)),
                  pl.BlockSpec((1, 1, SMALL_W), lambda z, b, n: (z, 0, 0)),
                  pl.BlockSpec((1, 1, SMALL_W), lambda z, b, n: (z, 0, 0)),
                  pl.BlockSpec((SMALL_W, SSD_INNER), lambda z, b, n: (0, 0))],
        out_specs=pl.BlockSpec((1, 1, tb, SSD_INNER), lambda z, b, n: (z, b, tmap(z, n), 0)),
        scratch_shapes=[pltpu.VMEM((SSD_G, SSD_H // SSD_G * SSD_P, SSD_N), F32)],
        compiler_params=_cparams(("parallel", "parallel", "arbitrary")),
        name="ssd_scan",
    )(xbc, xbc, xbc, small, dtb, alog, _head_expand(SSD_H, SSD_P, SMALL_W))


ML_I_LANE = SSD_H
ML_F_LANE = SSD_H + ML_H


def _mlstm_kernel(q_ref, k_ref, v_ref, sm_ref, gb_ref, o_ref, cs_ref, ns_ref, m_ref, *, tb, c):
    z = pl.program_id(0)
    nsub = tb // c

    @pl.when(pl.program_id(2) == 0)
    def _():
        cs_ref[...] = jnp.zeros_like(cs_ref)
        ns_ref[...] = jnp.zeros_like(ns_ref)
        m_ref[...] = jnp.full(m_ref.shape, ML_M_INIT, F32)

    incl, _ = _dir_masks(z, c)
    incl_f = incl.astype(F32)
    lane = lax.broadcasted_iota(I32, (1, SMALL_W), 1)
    is_f = jnp.logical_and(lane >= ML_F_LANE, lane < ML_F_LANE + ML_H)

    def local(j):
        rows = pl.ds(pl.multiple_of(jnp.where(z == 0, j, nsub - 1 - j) * c, c), c)
        q = q_ref[0, rows, :]
        k = k_ref[0, rows, :] * ML_DK ** -0.5
        v = v_ref[0, rows, :]
        pre = sm_ref[0, rows, :] + gb_ref[0]
        gates = jnp.where(is_f, -_softplus(-pre), pre)
        f_cum = _dot(incl_f, gates, HI)
        f_tot = jnp.sum(gates, axis=0, keepdims=True)
        f_cum_t = f_cum.T
        gates_t = gates.T
        per_head = []
        for h in range(ML_H):
            li, lf = ML_I_LANE + h, ML_F_LANE + h
            qh = q[:, h * ML_DK:(h + 1) * ML_DK]
            kh = k[:, h * ML_DK:(h + 1) * ML_DK]
            vh = v[:, h * ML_DV:(h + 1) * ML_DV]
            f_col = f_cum[:, lf:lf + 1]
            f_last = f_tot[:, lf:lf + 1]
            logw_end = f_last - f_col + gates[:, li:li + 1]
            m_end = jnp.max(logw_end, axis=0, keepdims=True)
            kw = kh * jnp.exp(logw_end - m_end)
            logw = jnp.where(incl, f_col - f_cum_t[lf:lf + 1, :] + gates_t[li:li + 1, :], -jnp.inf)
            m_loc = jnp.max(logw, axis=1, keepdims=True)
            scores = _dot_nt(qh, kh) * jnp.exp(logw - m_loc)
            per_head.append(dict(
                q=qh, f_col=f_col, f_last=f_last, m_end=m_end, m_loc=m_loc, num=_dot(scores, vh),
                den=jnp.sum(scores, axis=1, keepdims=True), kv=_dot_tn(kw, vh),
                ksum=jnp.sum(kw, axis=0, keepdims=True)))
        return rows, per_head

    subs = [local(j) for j in range(nsub)]
    cs = [cs_ref[h] for h in range(ML_H)]
    ns = [ns_ref[h] for h in range(ML_H)]
    m = [m_ref[h] for h in range(ML_H)]
    for rows, per_head in subs:
        outs = []
        for h, p in enumerate(per_head):
            m_t = jnp.maximum(p["m_loc"], p["f_col"] + m[h])
            w_loc = jnp.exp(p["m_loc"] - m_t)
            w_inter = jnp.exp(p["f_col"] + m[h] - m_t)
            num = p["num"] * w_loc + w_inter * _dot(p["q"], cs[h])
            den = p["den"] * w_loc + w_inter * jnp.sum(p["q"] * ns[h], axis=1, keepdims=True)
            outs.append(num / jnp.maximum(jnp.abs(den), jnp.exp(-m_t)))
            m_new = jnp.maximum(p["f_last"] + m[h], p["m_end"])
            s_keep = jnp.exp(p["f_last"] + m[h] - m_new)
            s_loc = jnp.exp(p["m_end"] - m_new)
            cs[h] = s_keep * cs[h] + s_loc * p["kv"]
            ns[h] = s_keep * ns[h] + s_loc * p["ksum"]
            m[h] = m_new
        o_ref[0, 0, rows, :] = jnp.concatenate(outs, axis=1)
    for h in range(ML_H):
        cs_ref[h] = cs[h]
        ns_ref[h] = ns[h]
        m_ref[h] = m[h]


def mlstm_scan(qk, pm, small, gate_bias, ctx_len):
    bsz, lt, _ = qk.shape
    tb = SCAN_BLOCK
    nb, cb = lt // tb, ctx_len // tb
    tmap = lambda z, n: _time_block(z, n, cb, nb)
    return pl.pallas_call(
        functools.partial(_mlstm_kernel, tb=tb, c=ML_CHUNK),
        out_shape=jax.ShapeDtypeStruct((2, bsz, lt, ML_VW), F32),
        grid=(2, bsz, nb),
        in_specs=[pl.BlockSpec((1, tb, ML_QK), lambda z, b, n: (b, tmap(z, n), 0)),
                  pl.BlockSpec((1, tb, ML_QK), lambda z, b, n: (b, tmap(z, n), 1)),
                  pl.BlockSpec((1, tb, ML_VW), lambda z, b, n: (b, tmap(z, n), 1)),
                  pl.BlockSpec((1, tb, SMALL_W), lambda z, b, n: (b, tmap(z, n), z)),
                  pl.BlockSpec((1, 1, SMALL_W), lambda z, b, n: (z, 0, 0))],
        out_specs=pl.BlockSpec((1, 1, tb, ML_VW), lambda z, b, n: (z, b, tmap(z, n), 0)),
        scratch_shapes=[pltpu.VMEM((ML_H, ML_DK, ML_DV), F32), pltpu.VMEM((ML_H, 1, ML_DK), F32),
                        pltpu.VMEM((ML_H, 1, 1), F32)],
        compiler_params=_cparams(("parallel", "parallel", "arbitrary")),
        name="mlstm_scan",
    )(qk, qk, pm, small, gate_bias)


def _odd_out_kernel(x_ref, mod_ref, ys_ref, xc_ref, zg_ref, ym_ref, og_ref, dsk_ref, nws_ref, nwm_ref, w_ref, o_ref):
    y = (ys_ref[0, 0] + ys_ref[1, 0] + dsk_ref[...] * xc_ref[0]) * _silu(zg_ref[0])
    ys = _group_rms(y, SSD_G, nws_ref[...])
    hm = ym_ref[0, 0] + ym_ref[1, 0]
    ym = _group_rms(hm, ML_H, nwm_ref[...]) * _sigmoid(og_ref[0])
    yy = jnp.concatenate([ys, ym], axis=1).astype(BF16)
    o_ref[0] = x_ref[0] + mod_ref[0, 0][2:3] * _dot(yy, w_ref[...])


def odd_out(x, modtab, y_s, xbc, ps, y_m, pm, d_skip, nw_s, nw_m, w_out, ctx_len):
    bsz, lt, d = x.shape
    tm = TOK_TILE
    ct = ctx_len // tm
    seq = lt - ctx_len
    tok = lambda width, col=0: pl.BlockSpec((1, tm, width), lambda b, i: (b, i + ct, col))
    two = lambda width: pl.BlockSpec((2, 1, tm, width), lambda b, i: (0, b, i + ct, 0))
    row = lambda width: pl.BlockSpec((1, width), lambda b, i: (0, 0))
    return pl.pallas_call(
        _odd_out_kernel,
        out_shape=jax.ShapeDtypeStruct((bsz, seq, d), F32),
        grid=(bsz, seq // tm),
        in_specs=[tok(d), pl.BlockSpec((1, 1, N_MOD, d), lambda b, i: (b, 1, 0, 0)),
                  two(SSD_INNER), tok(SSD_INNER), tok(SSD_INNER), two(ML_VW), tok(ML_VW, 2),
                  row(SSD_INNER), row(SSD_INNER), row(ML_VW),
                  pl.BlockSpec(w_out.shape, lambda b, i: (0, 0))],
        out_specs=pl.BlockSpec((1, tm, d), lambda b, i: (b, i, 0)),
        compiler_params=_cparams(("parallel", "parallel")),
        name="odd_out",
    )(x, modtab, y_s, xbc, ps, y_m, pm, d_skip.reshape(1, SSD_INNER), nw_s.reshape(1, SSD_INNER),
      nw_m.reshape(1, ML_VW), w_out)


def _router_kernel(x_ref, mod_ref, nw_ref, rw_ref, rb_ref, h_ref, idx_ref, prob_ref, rank_ref, cnt_ref, carry_ref,
                   *, tm):
    first = jnp.logical_and(pl.program_id(0) == 0, pl.program_id(1) == 0)

    @pl.when(first)
    def _():
        carry_ref[...] = jnp.zeros_like(carry_ref)

    h = _norm_mod(x_ref[0], nw_ref[...], mod_ref[0, 0], 3)
    h_ref[0] = _pack_bf16_halves(h)
    logits = _dot(h, rw_ref[...], HI) + rb_ref[...]
    lane = lax.broadcasted_iota(I32, (tm, N_EXP), 1).astype(F32)
    work = logits
    sel = jnp.zeros((tm, N_EXP), F32)
    vals, idxs, hots = [], [], []
    for _ in range(TOP_K):
        m = jnp.max(work, axis=1, keepdims=True)
        idx = jnp.min(jnp.where(work == m, lane, float(N_EXP)), axis=1, keepdims=True)
        hot = lane == idx
        vals.append(m)
        idxs.append(idx)
        hots.append(hot.astype(F32))
        sel = sel + hots[-1]
        work = jnp.where(hot, -jnp.inf, work)
    ex = [jnp.exp(v - vals[0]) for v in vals]
    tot = ex[0] + ex[1] + ex[2] + ex[3]
    ti = lax.broadcasted_iota(I32, (tm, tm), 0)
    si = lax.broadcasted_iota(I32, (tm, tm), 1)
    before = _dot((si < ti).astype(BF16), sel.astype(BF16)) + carry_ref[...]
    ranks = [jnp.sum(before * hot, axis=1, keepdims=True) for hot in hots]
    idx_ref[0] = jnp.concatenate(idxs, axis=1).astype(I32)
    prob_ref[0] = jnp.concatenate([e / tot for e in ex], axis=1)
    rank_ref[0] = jnp.concatenate(ranks, axis=1).astype(I32)
    carry_ref[...] = carry_ref[...] + jnp.sum(sel, axis=0, keepdims=True)
    cnt_ref[...] = carry_ref[...]


def router(x, modtab, nw, rw, rb, ctx_len):
    bsz, lt, d = x.shape
    tm = TOK_TILE
    ctx_tiles = ctx_len // tm
    tok = lambda width, dt: (jax.ShapeDtypeStruct((bsz, lt, width), dt),
                             pl.BlockSpec((1, tm, width), lambda b, i: (b, i, 0)))
    outs = [tok(d // 2, U32), tok(TOP_K, I32), tok(TOP_K, F32), tok(TOP_K, I32),
            (jax.ShapeDtypeStruct((1, N_EXP), F32), pl.BlockSpec((1, N_EXP), lambda b, i: (0, 0)))]
    return pl.pallas_call(
        functools.partial(_router_kernel, tm=tm),
        out_shape=[o[0] for o in outs],
        grid=(bsz, lt // tm),
        in_specs=[pl.BlockSpec((1, tm, d), lambda b, i: (b, i, 0)),
                  pl.BlockSpec((1, 1, N_MOD, d), lambda b, i: (b, (i >= ctx_tiles).astype(I32), 0, 0)),
                  pl.BlockSpec((1, d), lambda b, i: (0, 0)),
                  pl.BlockSpec((d, N_EXP), lambda b, i: (0, 0)),
                  pl.BlockSpec((1, N_EXP), lambda b, i: (0, 0))],
        out_specs=[o[1] for o in outs],
        scratch_shapes=[pltpu.VMEM((1, N_EXP), F32)],
        compiler_params=_cparams(("arbitrary", "arbitrary")),
        name="router",
    )(x, modtab, nw.reshape(1, d), rw, rb.reshape(1, N_EXP))


def sc_gather_rows(table, idx):
    m = idx.shape[0]
    d = table.shape[1]
    workers = SC_CORES * SC_SUBCORES
    per_w, n_chunks = m // workers, m // workers // SC_GATHER_ROWS
    assert per_w * workers == m and n_chunks * SC_GATHER_ROWS == per_w and n_chunks % 2 == 0
    w = SC_GATHER_ROWS
    mesh = plsc.VectorSubcoreMesh(core_axis_name="c", subcore_axis_name="s")

    @functools.partial(
        pl.kernel, mesh=mesh, out_type=jax.ShapeDtypeStruct((m, d), table.dtype),
        scratch_types=[pltpu.VMEM((per_w,), I32), pltpu.VMEM((2, w, d), table.dtype),
                       pltpu.SemaphoreType.DMA((2,)), pltpu.SemaphoreType.DMA((2,))])
    def gather_kernel(table_hbm, idx_hbm, out_hbm, idx_v, rows_v, gsem, wsem):
        wid = lax.axis_index("s") * SC_CORES + lax.axis_index("c")
        base = pl.multiple_of(wid * per_w, 8)
        pltpu.sync_copy(idx_hbm.at[pl.ds(base, per_w)], idx_v)

        def gather(g, b):
            rows = idx_v.at[pl.ds(pl.multiple_of(g * w, 8), w)]
            return pltpu.make_async_copy(table_hbm.at[rows], rows_v.at[b], gsem.at[b])

        def write(g, b):
            dst = out_hbm.at[pl.ds(pl.multiple_of(base + g * w, 8), w)]
            return pltpu.make_async_copy(rows_v.at[b], dst, wsem.at[b])

        gather(0, 0).start()

        @pl.loop(0, n_chunks, step=2)
        def _(g):
            for b in (0, 1):
                cur = g + b
                gather(cur, b).wait()
                write(cur, b).start()

                @pl.when(cur >= 1)
                def _():
                    write(cur - 1, 1 - b).wait()

                @pl.when(cur + 1 < n_chunks)
                def _():
                    gather(cur + 1, 1 - b).start()

        write(n_chunks - 1, 1).wait()

    return gather_kernel(table, idx)


def sc_scatter_rows(src, idx, n_out):
    k_dup, t = idx.shape
    d = src.shape[1]
    workers = SC_CORES * SC_SUBCORES
    w = SC_GATHER_ROWS
    per_w, n_chunks = t // workers, t // workers // w
    assert per_w * workers == t and n_chunks * w == per_w and n_chunks % 2 == 0
    mesh = plsc.VectorSubcoreMesh(core_axis_name="c", subcore_axis_name="s")

    @functools.partial(
        pl.kernel, mesh=mesh, out_type=jax.ShapeDtypeStruct((n_out, d), src.dtype),
        scratch_types=[pltpu.VMEM((k_dup, n_chunks, w), I32), pltpu.VMEM((2, w, d), src.dtype),
                       pltpu.SemaphoreType.DMA((2,)), pltpu.SemaphoreType.DMA((2,))])
    def scatter_kernel(src_hbm, idx_hbm, out_hbm, idx_v, rows_v, rsem, wsem):
        wid = lax.axis_index("s") * SC_CORES + lax.axis_index("c")
        base = pl.multiple_of(wid * per_w, 8)
        for k in range(k_dup):
            pltpu.sync_copy(idx_hbm.at[k, wid], idx_v.at[k])

        def read(g, b):
            rows = src_hbm.at[pl.ds(pl.multiple_of(base + g * w, 8), w)]
            return pltpu.make_async_copy(rows, rows_v.at[b], rsem.at[b])

        def write(g, b, k):
            return pltpu.make_async_copy(rows_v.at[b], out_hbm.at[idx_v.at[k, g]], wsem.at[b])

        read(0, 0).start()

        @pl.loop(0, n_chunks, step=2)
        def _(g):
            for b in (0, 1):
                cur = g + b
                read(cur, b).wait()
                for k in range(k_dup):
                    write(cur, b, k).start()

                @pl.when(cur >= 1)
                def _():
                    for k in range(k_dup):
                        write(cur - 1, 1 - b, k).wait()

                @pl.when(cur + 1 < n_chunks)
                def _():
                    read(cur + 1, 1 - b).start()

        for k in range(k_dup):
            write(n_chunks - 1, 1, k).wait()

    return scatter_kernel(src, idx.reshape(k_dup, workers, n_chunks, w))


def _expert_kernel(be_ref, used_ref, x_ref, wg_ref, bg_ref, wu_ref, bu_ref, wd_ref, bd_ref, o_ref, wg_bf, wu_bf,
                   wd_bf):
    i = pl.program_id(0)

    @pl.when(jnp.logical_or(i == 0, be_ref[i] != be_ref[jnp.maximum(i - 1, 0)]))
    def _():
        wg_bf[...] = wg_ref[0, 0].astype(BF16)
        wu_bf[...] = wu_ref[0, 0].astype(BF16)
        wd_bf[...] = wd_ref[0, 0].astype(BF16)

    @pl.when(i < used_ref[0])
    def _():
        xb = _unpack_bf16_halves(x_ref[...]).astype(BF16)
        g = _dot(xb, wg_bf[...]) + bg_ref[0, 0]
        u = _dot(xb, wu_bf[...]) + bu_ref[0, 0]
        g = jnp.minimum(g, SWIGLU_LIMIT)
        u = jnp.clip(u, -SWIGLU_LIMIT, SWIGLU_LIMIT)
        act = (g * _sigmoid(SWIGLU_ALPHA * g) * (u + 1.0)).astype(BF16)
        o_ref[...] = _pack_bf16_halves(_dot(act, wd_bf[...]) + bd_ref[0, 0])


def expert_ffn(block_expert, blocks_used, x_sorted, layer, wg, bg, wu, bu, wd, bd):
    nblk = block_expert.shape[0]
    n_layers, _, d, f = wg.shape
    wspec = lambda r, c: pl.BlockSpec((1, 1, r, c), lambda i, be, used: (layer, be[i], 0, 0))
    grid_spec = pltpu.PrefetchScalarGridSpec(
        num_scalar_prefetch=2,
        grid=(nblk,),
        in_specs=[pl.BlockSpec((MOE_BLOCK, d // 2), lambda i, be, used: (i, 0)),
                  wspec(d, f), wspec(1, f), wspec(d, f), wspec(1, f), wspec(f, d), wspec(1, d)],
        out_specs=pl.BlockSpec((MOE_BLOCK, d // 2), lambda i, be, used: (i, 0)),
        scratch_shapes=[pltpu.VMEM((d, f), BF16), pltpu.VMEM((d, f), BF16), pltpu.VMEM((f, d), BF16)],
    )
    return pl.pallas_call(
        _expert_kernel,
        out_shape=jax.ShapeDtypeStruct((nblk * MOE_BLOCK, d // 2), U32),
        grid_spec=grid_spec,
        compiler_params=_cparams(("arbitrary",)),
        name="expert_ffn",
    )(block_expert, blocks_used, x_sorted, wg, bg.reshape(n_layers, N_EXP, 1, f), wu,
      bu.reshape(n_layers, N_EXP, 1, f), wd, bd.reshape(n_layers, N_EXP, 1, d))


def _combine_kernel(x_ref, mod_ref, prob_ref, y_ref, fw_ref, o_ref, *, final_norm):
    p = prob_ref[0]
    acc = _unpack_bf16_halves(y_ref[0, 0]) * p[:, 0:1]
    for k in range(1, TOP_K):
        acc = acc + _unpack_bf16_halves(y_ref[k, 0]) * p[:, k:k + 1]
    out = x_ref[0] + mod_ref[0, 0][5:6] * acc
    if final_norm:
        out = out * lax.rsqrt(jnp.mean(out * out, axis=-1, keepdims=True) + EPS) * fw_ref[...]
    o_ref[0] = out


def combine(x, modtab, probs, y_tok, final_w, ctx_len, final_norm):
    bsz, lt, d = x.shape
    tm = TOK_TILE
    ctx_tiles = ctx_len // tm
    return pl.pallas_call(
        functools.partial(_combine_kernel, final_norm=final_norm),
        out_shape=jax.ShapeDtypeStruct((bsz, lt, d), F32),
        grid=(bsz, lt // tm),
        in_specs=[pl.BlockSpec((1, tm, d), lambda b, i: (b, i, 0)),
                  pl.BlockSpec((1, 1, N_MOD, d), lambda b, i: (b, (i >= ctx_tiles).astype(I32), 0, 0)),
                  pl.BlockSpec((1, tm, TOP_K), lambda b, i: (b, i, 0)),
                  pl.BlockSpec((TOP_K, 1, tm, d // 2), lambda b, i: (0, b, i, 0)),
                  pl.BlockSpec((1, d), lambda b, i: (0, 0))],
        out_specs=pl.BlockSpec((1, tm, d), lambda b, i: (b, i, 0)),
        compiler_params=_cparams(("parallel", "parallel")),
        name="combine",
    )(x, modtab, probs, y_tok, final_w.reshape(1, d))


def moe_layer(x, modtab, layer, nw, rw, rb, wg, bg, wu, bu, wd, bd, final_w, ctx_len, final_norm):
    bsz, lt, d = x.shape
    t = bsz * lt
    h, idx, probs, rank, counts = router(x, modtab, nw, rw, rb, ctx_len)
    counts = counts.reshape(N_EXP).astype(I32)
    padded = (counts + MOE_BLOCK - 1) // MOE_BLOCK * MOE_BLOCK
    ends = jnp.cumsum(padded)
    pstart = ends - padded
    idx = idx.reshape(t, TOP_K)
    dest = pstart[idx] + rank.reshape(t, TOP_K)
    nblk = -(-t * TOP_K // MOE_BLOCK) + N_EXP
    block_row0 = jnp.arange(nblk, dtype=I32) * MOE_BLOCK
    block_expert = jnp.minimum(jnp.sum((ends[None, :] <= block_row0[:, None]).astype(I32), axis=1), N_EXP - 1)
    dest_k = dest.T
    x_sorted = sc_scatter_rows(h.reshape(t, d // 2), dest_k, nblk * MOE_BLOCK)
    blocks_used = (ends[-1:] // MOE_BLOCK).astype(I32)
    y_sorted = expert_ffn(block_expert, blocks_used, x_sorted, layer, wg, bg, wu, bu, wd, bd)
    y_tok = sc_gather_rows(y_sorted, dest_k.reshape(TOP_K * t)).reshape(TOP_K, bsz, lt, d // 2)
    return combine(x, modtab, probs, y_tok, final_w, ctx_len, final_norm)


def even_layer(x, modtab, norm1_w, w_in, w_out, mu, w0, w2, a0, a2, g2, k_k, k_a, r_k, ln_w, ln_b, lb, hg_norm_w,
               ctx_len):
    w_in = w_in.astype(BF16)
    pr, ph = in_proj(x, modtab, norm1_w, [w_in[:, :RW_PROJ], w_in[:, RW_PROJ:]], ctx_len)
    r, v, kk, g, bv, lw, km, bb = rwkv_prep(pr, mu, w0, w2, a0, a2, g2, k_k, k_a, r_k, ctx_len)
    o_r = rwkv_scan(r, v, kk, lw, km, bb, ctx_len)
    o_g = gla_scan(ph, lb, ctx_len)
    return even_out(x, modtab, o_r, bv, g, o_g, ph, ln_w, ln_b, hg_norm_w, w_out.astype(BF16), ctx_len)


def _dir_slab(cols, width):
    halves = []
    for z in range(2):
        parts = [c[:, z * (c.shape[1] // 2):(z + 1) * (c.shape[1] // 2)] for c in cols]
        used = sum(p.shape[1] for p in parts)
        parts.append(jnp.zeros((cols[0].shape[0], width - used), cols[0].dtype))
        halves.append(jnp.concatenate(parts, axis=1))
    return jnp.concatenate(halves, axis=1)


def _dir_rows(vals, width):
    rows = jnp.concatenate(vals, axis=1)
    return jnp.pad(rows, ((0, 0), (0, width - rows.shape[1]))).reshape(2, 1, width)


def odd_layer(x, modtab, norm1_w, w_in, w_out, s_conv_w, s_conv_b, dt_bias, a_log, d_skip, s_norm_w, m_conv_w,
              m_conv_b, i_bias, f_bias, m_norm_w, ctx_len):
    o = 0
    cols = {}
    for name, width in (("zg", SSD_INNER), ("xbc", SSD_CONV_CH), ("dt", 2 * SSD_H), ("qk", 2 * ML_QK),
                        ("v", ML_VW), ("og", ML_VW), ("ig", 2 * ML_H), ("fg", 2 * ML_H)):
        cols[name] = w_in[:, o:o + width]
        o += width
    slab_m = jnp.concatenate([cols["qk"], cols["v"], cols["og"]], axis=1).astype(BF16)
    slab_small = _dir_slab([cols["dt"], cols["ig"], cols["fg"]], SMALL_W).astype(BF16)
    pxbc, zg, pm, small = in_proj(x, modtab, norm1_w, [cols["xbc"].astype(BF16), cols["zg"].astype(BF16), slab_m,
                                                       slab_small], ctx_len)
    xbc, qk = odd_prep(pxbc, pm, s_conv_w, s_conv_b, m_conv_w, m_conv_b, ctx_len)
    zeros_g = jnp.zeros((2, 2 * ML_H), F32)
    y_s = ssd_scan(xbc, small, _dir_rows([dt_bias, zeros_g], SMALL_W), _dir_rows([a_log, zeros_g], SMALL_W), ctx_len)
    y_m = mlstm_scan(qk, pm, small, _dir_rows([jnp.zeros((2, SSD_H), F32), i_bias, f_bias], SMALL_W), ctx_len)
    return odd_out(x, modtab, y_s, xbc, zg, y_m, pm, jnp.repeat(d_skip, SSD_P), s_norm_w, m_norm_w,
                   w_out.astype(BF16), ctx_len)


def _to_col_major(t):
    b, s, ch = t.shape
    return t.reshape(b, s // GRID_W, GRID_W, ch).transpose(0, 2, 1, 3).reshape(b, s, ch)


def _to_row_major(t):
    b, s, ch = t.shape
    return t.reshape(b, GRID_W, s // GRID_W, ch).transpose(0, 2, 1, 3).reshape(b, s, ch)


def kernel(x, c, ctx, c_ctx, ada_w, ada_b, norm1_w, norm2_w, even_w_in, even_w_out, rwkv_mu, rwkv_w0, rwkv_w2, rwkv_a0, rwkv_a2, rwkv_g2, rwkv_k_k, rwkv_k_a, rwkv_r_k, rwkv_ln_w, rwkv_ln_b, hgrn_lower_bounds, hgrn_norm_w, odd_w_in, odd_w_out, ssd_conv_w, ssd_conv_b, ssd_dt_bias, ssd_a_log, ssd_d, ssd_norm_w, mlstm_conv_w, mlstm_conv_b, mlstm_i_bias, mlstm_f_bias, mlstm_norm_w, router_w, router_b, exp_w_gate, exp_b_gate, exp_w_up, exp_b_up, exp_w_down, exp_b_down, final_norm_w):
    lc = ctx.shape[1]
    lower_bounds = jnp.cumsum(jax.nn.softmax(hgrn_lower_bounds.astype(F32), axis=0), axis=0)
    moe = lambda l: (l, norm2_w[l], router_w[l], router_b[l], exp_w_gate, exp_b_gate, exp_w_up, exp_b_up, exp_w_down,
                     exp_b_down, final_norm_w)
    xa = jnp.concatenate([ctx, x], axis=1)
    modtab = _mod_table(c, c_ctx, ada_w[0], ada_b[0])
    xa = even_layer(xa, modtab, norm1_w[0], even_w_in[0], even_w_out[0], rwkv_mu[0], rwkv_w0[0], rwkv_w2[0],
                    rwkv_a0[0], rwkv_a2[0], rwkv_g2[0], rwkv_k_k[0], rwkv_k_a[0], rwkv_r_k[0], rwkv_ln_w[0],
                    rwkv_ln_b[0], lower_bounds[0], hgrn_norm_w[0], lc)
    xa = moe_layer(xa, modtab, *moe(0), lc, False)
    xa = jnp.concatenate([xa[:, :lc], _to_col_major(xa[:, lc:])], axis=1)
    modtab = _mod_table(c, c_ctx, ada_w[1], ada_b[1])
    xl = odd_layer(xa, modtab, norm1_w[1], odd_w_in[0], odd_w_out[0], ssd_conv_w[0], ssd_conv_b[0], ssd_dt_bias[0],
                   ssd_a_log[0], ssd_d[0], ssd_norm_w[0], mlstm_conv_w[0], mlstm_conv_b[0], mlstm_i_bias[0],
                   mlstm_f_bias[0], mlstm_norm_w[0], lc)
    xl = moe_layer(xl, modtab, *moe(1), 0, True)
    return _to_row_major(xl)
```

```python
import functools
import math

import jax
import jax.numpy as jnp
from jax import lax
from jax.experimental import pallas as pl
from jax.experimental.pallas import tpu as pltpu
from jax.experimental.pallas import tpu_sc as plsc

F32 = jnp.float32
BF16 = jnp.bfloat16
I32 = jnp.int32
U32 = jnp.uint32
HI = lax.Precision.HIGHEST

D_MODEL = 1024
GRID_W = 64
N_MOD = 6
EPS = 1e-6
RW_H, RW_N = 8, 64
RW_W = RW_H * RW_N
RW_RANK = 64
RW_GATE_RANK = 128
RW_GN_EPS = 64e-5
RW_PROJ = 3 * RW_W + 4 * RW_RANK + RW_GATE_RANK
RW_CHUNK = 64
RW_GROUP = 4
RW_PACK = 2
HG_H, HG_DK, HG_DV = 4, 128, 128
HG_W = HG_H * HG_DK
HG_CHUNK = 32
HG_PROJ = 5 * HG_W
SSD_H, SSD_P, SSD_G, SSD_N = 16, 64, 2, 128
SSD_INNER = SSD_H * SSD_P
SSD_CONV_CH = SSD_INNER + 2 * SSD_G * SSD_N
SSD_CHUNK = 64
ML_H, ML_DK, ML_DV = 4, 128, 256
ML_QK = ML_H * ML_DK
ML_VW = ML_H * ML_DV
ML_CHUNK = 64
ML_M_INIT = -1e30
CONV_K = 5
N_EXP = 32
TOP_K = 4
D_FF = 1024
SWIGLU_LIMIT = 7.0
SWIGLU_ALPHA = 1.702
MOE_BLOCK = 512

TOK_TILE = 256
SCAN_BLOCK = 256
HALO = 8
SMALL_W = 128
VMEM_LIMIT = 56 * 1024 * 1024
SC_CORES, SC_SUBCORES = 2, 16
SC_GATHER_ROWS = 32


def _cparams(sem):
    return pltpu.CompilerParams(dimension_semantics=sem, vmem_limit_bytes=VMEM_LIMIT)


def _dot(a, b, precision=None):
    return jnp.dot(a, b, preferred_element_type=F32, precision=precision)


def _dot_nt(a, b, precision=None):
    return lax.dot_general(a, b, (((1,), (1,)), ((), ())), preferred_element_type=F32, precision=precision)


def _dot_tn(a, b, precision=None):
    return lax.dot_general(a, b, (((0,), (0,)), ((), ())), preferred_element_type=F32, precision=precision)


def _bf16_parts(x):
    hi = x.astype(BF16)
    rest = x - hi.astype(F32)
    mid = rest.astype(BF16)
    return hi, mid, (rest - mid.astype(F32)).astype(BF16)


def _dot_into_01(x, sel):
    sel = sel.astype(BF16)
    p0, p1, p2 = _bf16_parts(x)
    return _dot(p0, sel) + _dot(p1, sel) + _dot(p2, sel)


def _dot_01_into(sel, x):
    sel = sel.astype(BF16)
    p0, p1, p2 = _bf16_parts(x)
    return _dot(sel, p0) + _dot(sel, p1) + _dot(sel, p2)


def _pack_bf16_halves(x):
    n = x.shape[1] // 2
    bits = lax.bitcast_convert_type(x.astype(BF16).astype(F32), U32)
    return bits[:, :n] | (bits[:, n:] >> 16)


def _unpack_bf16_halves(w):
    hi = lax.bitcast_convert_type(w & jnp.uint32(0xFFFF0000), F32)
    lo = lax.bitcast_convert_type(w << 16, F32)
    return jnp.concatenate([hi, lo], axis=1)


def _sigmoid(x):
    return 1.0 / (1.0 + jnp.exp(-x))


def _silu(x):
    return x * _sigmoid(x)


def _softplus(x):
    return jnp.maximum(x, 0.0) + jnp.log(1.0 + jnp.exp(-jnp.abs(x)))


def _dir_masks(z, c):
    ti = lax.broadcasted_iota(I32, (c, c), 0)
    si = lax.broadcasted_iota(I32, (c, c), 1)
    d = (si - ti) * jnp.where(z == 0, 1, -1)
    return d <= 0, d < 0


def _time_block(z, n, ctx_blocks, n_blocks):
    rev = jnp.where(n < ctx_blocks, ctx_blocks - 1 - n, n_blocks - 1 - (n - ctx_blocks))
    return jnp.where(z == 0, n, rev)


def _mod_kernel(c_ref, w_ref, b_ref, o_ref):
    o_ref[...] = _dot(_silu(c_ref[...]), w_ref[...]) + b_ref[...]


def ada_mod(cc, w, b):
    rows, d = cc.shape
    n = w.shape[1]
    tn = 1536
    return pl.pallas_call(
        _mod_kernel,
        out_shape=jax.ShapeDtypeStruct((rows, n), F32),
        grid=(n // tn,),
        in_specs=[pl.BlockSpec((rows, d), lambda j: (0, 0)),
                  pl.BlockSpec((d, tn), lambda j: (0, j)),
                  pl.BlockSpec((1, tn), lambda j: (0, j))],
        out_specs=pl.BlockSpec((rows, tn), lambda j: (0, j)),
        compiler_params=_cparams(("arbitrary",)),
        name="ada_mod",
    )(cc, w, b.reshape(1, n))


def _mod_table(c, c_ctx, w, b):
    bsz = c.shape[0]
    rows = -(-(bsz + 1) // 8) * 8
    cc = jnp.zeros((rows, D_MODEL), F32).at[:bsz].set(c).at[bsz].set(c_ctx)
    m = ada_mod(cc, w, b)
    mod = m[:bsz].reshape(bsz, 1, N_MOD, D_MODEL)
    mod_c = jnp.broadcast_to(m[bsz].reshape(1, 1, N_MOD, D_MODEL), (bsz, 1, N_MOD, D_MODEL))
    return jnp.concatenate([mod_c, mod], axis=1)


def _norm_mod(x, nw, m, shift_idx):
    y = x * lax.rsqrt(jnp.mean(x * x, axis=-1, keepdims=True) + EPS) * nw
    return y * (1.0 + m[shift_idx + 1:shift_idx + 2]) + m[shift_idx:shift_idx + 1]


def _in_proj_kernel(x_ref, mod_ref, nw_ref, *rest, n_out):
    w_refs, o_refs = rest[:n_out], rest[n_out:]
    h = _norm_mod(x_ref[0], nw_ref[...], mod_ref[0, 0], 0).astype(BF16)
    for w_ref, o_ref in zip(w_refs, o_refs):
        o_ref[0] = _dot(h, w_ref[...])


def in_proj(x, modtab, nw, slabs, ctx_len):
    bsz, lt, d = x.shape
    tm = TOK_TILE
    ctx_tiles = ctx_len // tm
    in_specs = [pl.BlockSpec((1, tm, d), lambda b, i: (b, i, 0)),
                pl.BlockSpec((1, 1, N_MOD, d), lambda b, i: (b, (i >= ctx_tiles).astype(I32), 0, 0)),
                pl.BlockSpec((1, d), lambda b, i: (0, 0))]
    in_specs += [pl.BlockSpec(w.shape, lambda b, i: (0, 0)) for w in slabs]
    return pl.pallas_call(
        functools.partial(_in_proj_kernel, n_out=len(slabs)),
        out_shape=[jax.ShapeDtypeStruct((bsz, lt, w.shape[1]), F32) for w in slabs],
        grid=(bsz, lt // tm),
        in_specs=in_specs,
        out_specs=[pl.BlockSpec((1, tm, w.shape[1]), lambda b, i: (b, i, 0)) for w in slabs],
        compiler_params=_cparams(("parallel", "parallel")),
        name="in_proj",
    )(x, modtab, nw.reshape(1, d), *slabs)


def _halo_specs(tm, width, lt):
    per = tm // HALO
    last = lt // HALO - 1
    return [pl.BlockSpec((1, tm, width), lambda b, i: (b, i, 0)),
            pl.BlockSpec((1, HALO, width), lambda b, i: (b, jnp.maximum(i * per - 1, 0), 0)),
            pl.BlockSpec((1, HALO, width), lambda b, i: (b, jnp.minimum((i + 1) * per, last), 0))]


def _segment_edges(i, tm, ctx_len, lt):
    row0 = i * tm
    has_prev = jnp.logical_and(row0 != 0, row0 != ctx_len)
    has_next = jnp.logical_and(row0 + tm != ctx_len, row0 + tm != lt)
    return has_prev, has_next


def _rwkv_prep_kernel(p_ref, pp_ref, pn_ref, mu_ref, w0_ref, w2_ref, a0_ref, a2_ref, g2_ref, kk_ref, ka_ref,
                      rk_ref, bd_ref, r_o, v_o, kk_o, g_o, bv_o, lw_o, km_o, bb_o, *, tm, ctx_len, lt):
    has_prev, has_next = _segment_edges(pl.program_id(1), tm, ctx_len, lt)
    p = p_ref[0]
    prev_row = jnp.where(has_prev, pp_ref[0, HALO - 1:HALO, :], 0.0)
    next_row = jnp.where(has_next, pn_ref[0, 0:1, :], 0.0)
    rid = lax.broadcasted_iota(I32, (tm, 1), 0)
    up = jnp.where(rid == 0, prev_row, pltpu.roll(p, 1, 0))
    dn = jnp.where(rid == tm - 1, next_row, pltpu.roll(p, tm - 1, 0))
    p = p + (0.5 * (up + dn) - p) * mu_ref[...]
    w = RW_W
    r, k, v = p[:, 0:w], p[:, w:2 * w], p[:, 2 * w:3 * w]
    o = 3 * w
    wd = jnp.tanh(p[:, o:o + 2 * RW_RANK])
    ad = p[:, o + 2 * RW_RANK:o + 4 * RW_RANK]
    gd = p[:, o + 4 * RW_RANK:o + 4 * RW_RANK + RW_GATE_RANK]
    bd = bd_ref[...]
    kk = k * kk_ref[...]
    kk = kk * lax.rsqrt(_dot_into_01(kk * kk, bd) + 1e-12)
    r_o[0] = r
    v_o[0] = v
    kk_o[0] = kk
    g_o[0] = _dot(_sigmoid(gd), g2_ref[...])
    km_sum = jnp.zeros_like(k)
    for z in range(2):
        sl = slice(z * RW_RANK, (z + 1) * RW_RANK)
        w_pre = w0_ref[z:z + 1, :] + _dot(wd[:, sl], w2_ref[z])
        lw_o[z, 0] = -_sigmoid(w_pre) * math.exp(-0.5)
        a = _sigmoid(a0_ref[z:z + 1, :] + _dot(ad[:, sl], a2_ref[z]))
        km = k * (1.0 + (a - 1.0) * ka_ref[...])
        km_o[z, 0] = km
        bb_o[z, 0] = kk * a
        km_sum = km_sum + km
    bv_o[0] = _dot_into_01(r * km_sum * rk_ref[...], bd) * v


def _head_block_ones(width, head):
    i = jnp.arange(width) // head
    return (i[:, None] == i[None, :]).astype(F32)


def rwkv_prep(p, mu, w0, w2, a0, a2, g2, k_k, k_a, r_k, ctx_len):
    bsz, lt, width = p.shape
    tm = TOK_TILE
    w = RW_W
    full = lambda a: pl.BlockSpec(a.shape, lambda b, i: (0,) * a.ndim)
    params = [mu.reshape(1, width), w0, w2, a0, a2, g2, k_k.reshape(1, w), k_a.reshape(1, w), r_k.reshape(1, w),
              _head_block_ones(w, RW_N)]
    one = jax.ShapeDtypeStruct((bsz, lt, w), F32)
    two = jax.ShapeDtypeStruct((2, bsz, lt, w), F32)
    one_spec = pl.BlockSpec((1, tm, w), lambda b, i: (b, i, 0))
    two_spec = pl.BlockSpec((2, 1, tm, w), lambda b, i: (0, b, i, 0))
    return pl.pallas_call(
        functools.partial(_rwkv_prep_kernel, tm=tm, ctx_len=ctx_len, lt=lt),
        out_shape=[one] * 5 + [two] * 3,
        grid=(bsz, lt // tm),
        in_specs=_halo_specs(tm, width, lt) + [full(a) for a in params],
        out_specs=[one_spec] * 5 + [two_spec] * 3,
        compiler_params=_cparams(("parallel", "parallel")),
        name="rwkv_prep",
    )(p, p, p, *params)


def _unit_lower_inverses(n_mats, eye, c):
    xs = [eye - n for n in n_mats]
    ps = [_dot(n, n) for n in n_mats]
    span = 4
    while True:
        xs = [x + _dot(x, p) for x, p in zip(xs, ps)]
        if span >= c:
            return xs
        ps = [_dot(p, p) for p in ps]
        span *= 2


def _rwkv_scan_kernel(r_ref, v_ref, kk_ref, lw_ref, km_ref, bb_ref, o_ref, s_ref, *, tb, c):
    z = pl.program_id(0)
    nsub = tb // c

    @pl.when(pl.program_id(2) == 0)
    def _():
        s_ref[...] = jnp.zeros_like(s_ref)

    incl, strict = _dir_masks(z, c)
    incl_f, strict_f = incl.astype(F32), strict.astype(F32)
    eye = incl_f - strict_f
    eye_pk = jnp.concatenate([eye] * RW_PACK, axis=1)
    pack_lane = lax.broadcasted_iota(I32, (c, RW_PACK * RW_N), 1) // RW_N

    heads = [slice(h * RW_N, (h + 1) * RW_N) for h in range(RW_H)]

    def group(g, s_old):
        subs = []
        for i in range(RW_GROUP):
            j = g * RW_GROUP + i
            rows = pl.ds(pl.multiple_of(jnp.where(z == 0, j, nsub - 1 - j) * c, c), c)
            r, v, kk = r_ref[0, rows, :], v_ref[0, rows, :], kk_ref[0, rows, :]
            lw, km, bb = lw_ref[0, 0, rows, :], km_ref[0, 0, rows, :], bb_ref[0, 0, rows, :]
            cum = _dot_01_into(incl_f, lw)
            tot = jnp.sum(lw, axis=0, keepdims=True)
            e_neg = jnp.exp(-cum)
            e_end = jnp.exp(tot - cum)
            a_t = kk * jnp.exp(cum - lw)
            r_t = r * jnp.exp(cum)
            b_t, k_t = bb * e_neg, km * e_neg
            b_p, k_p = bb * e_end, km * e_end
            subs.append(dict(
                rows=rows, e_tot=jnp.exp(tot), a_all=a_t, v_all=v, r=[r_t[:, hs] for hs in heads],
                ar=[jnp.concatenate([a_t[:, hs], r_t[:, hs]], axis=0) for hs in heads],
                bk=[jnp.concatenate([b_t[:, hs], k_t[:, hs]], axis=0) for hs in heads],
                bkp=[jnp.concatenate([b_p[:, hs], k_p[:, hs]], axis=0) for hs in heads],
                v=[v[:, hs] for hs in heads]))
        flat = [(i, h) for i in range(RW_GROUP) for h in range(RW_H)]
        sc = [_dot_nt(subs[i]["ar"][h], subs[i]["bk"][h]) for i, h in flat]
        n_rbk = [jnp.concatenate([x[c:, :c] * incl_f, x[c:, c:] * incl_f], axis=1) for x in sc]
        packs = [(i, p) for i in range(RW_GROUP) for p in range(RW_H // RW_PACK)]

        def side_by_side(mats):
            return jnp.concatenate(mats, axis=1)

        def block_diag(packed):
            return jnp.concatenate([jnp.where(pack_lane == r, packed, 0.0) for r in range(RW_PACK)], axis=0)

        def pack_of(per_head, i, p):
            return side_by_side([per_head[i * RW_H + p * RW_PACK + r] for r in range(RW_PACK)])

        lanes = lambda p: slice(p * RW_PACK * RW_N, (p + 1) * RW_PACK * RW_N)
        n_pk = [pack_of([x[:c, :c] * strict_f for x in sc], i, p) for i, p in packs]
        mak_pk = [pack_of([x[:c, c:] * strict_f for x in sc], i, p) for i, p in packs]
        makv_pk = [_dot(m, block_diag(subs[i]["v_all"][:, lanes(p)])) for m, (i, p) in zip(mak_pk, packs)]
        xs = [eye_pk - n for n in n_pk]
        ps = [_dot(n, block_diag(n)) for n in n_pk]
        span = 4
        while True:
            bds = [block_diag(pp) for pp in ps]
            xs = [x + _dot(x, b) for x, b in zip(xs, bds)]
            if span >= c:
                break
            ps = [_dot(pp, b) for pp, b in zip(ps, bds)]
            span *= 2
        w1_pk = [_dot(x, block_diag(subs[i]["a_all"][:, lanes(p)])) for x, (i, p) in zip(xs, packs)]
        w2_pk = [_dot(x, block_diag(m)) for x, m in zip(xs, makv_pk)]

        def head_of(packed_list, i, h):
            p, r = divmod(h, RW_PACK)
            return packed_list[i * (RW_H // RW_PACK) + p][:, r * RW_N:(r + 1) * RW_N]

        w1r = [jnp.concatenate([head_of(w1_pk, i, h), subs[i]["r"][h]], axis=0) for i, h in flat]
        w2 = [head_of(w2_pk, i, h) for i, h in flat]
        state = list(s_old)
        for i in range(RW_GROUP):
            sub = subs[i]
            base = i * RW_H
            ws = [_dot_nt(w1r[base + h], state[h]) for h in range(RW_H)]
            uv = [jnp.concatenate([-(ws[h][:c] + w2[base + h]), sub["v"][h]], axis=0) for h in range(RW_H)]
            outs = [ws[h][c:] + _dot(n_rbk[base + h], uv[h]) for h in range(RW_H)]
            o_ref[0, 0, sub["rows"], :] = jnp.concatenate(outs, axis=1)
            state = [state[h] * sub["e_tot"][:, heads[h]] + _dot_tn(uv[h], sub["bkp"][h]) for h in range(RW_H)]
        return tuple(state)

    s_fin = lax.fori_loop(0, nsub // RW_GROUP, group, tuple(s_ref[h] for h in range(RW_H)))
    for h in range(RW_H):
        s_ref[h] = s_fin[h]


def rwkv_scan(r, v, kk, lw, km, bb, ctx_len):
    bsz, lt, w = r.shape
    tb = SCAN_BLOCK
    nb, cb = lt // tb, ctx_len // tb
    tmap = lambda z, b, n: _time_block(z, n, cb, nb)
    one = pl.BlockSpec((1, tb, w), lambda z, b, n: (b, tmap(z, b, n), 0))
    two = pl.BlockSpec((1, 1, tb, w), lambda z, b, n: (z, b, tmap(z, b, n), 0))
    return pl.pallas_call(
        functools.partial(_rwkv_scan_kernel, tb=tb, c=RW_CHUNK),
        out_shape=jax.ShapeDtypeStruct((2, bsz, lt, w), F32),
        grid=(2, bsz, nb),
        in_specs=[one, one, one, two, two, two],
        out_specs=two,
        scratch_shapes=[pltpu.VMEM((RW_H, RW_N, RW_N), F32)],
        compiler_params=_cparams(("parallel", "parallel", "arbitrary")),
        name="rwkv_scan",
    )(r, v, kk, lw, km, bb)


def _gla_kernel(q_ref, f_ref, i_ref, lb_ref, o_ref, s_ref, *, tb, c):
    z = pl.program_id(0)
    nsub = tb // c

    @pl.when(pl.program_id(2) == 0)
    def _():
        s_ref[...] = jnp.zeros_like(s_ref)

    lb = lb_ref[...]
    heads = [slice(h * HG_DK, (h + 1) * HG_DK) for h in range(HG_H)]
    ti = lax.broadcasted_iota(I32, (tb, tb), 0)
    si = lax.broadcasted_iota(I32, (tb, tb), 1)
    same_chunk = (ti // c) == (si // c)

    def run(zs):
        incl = jnp.logical_and(same_chunk, (si <= ti) if zs == 0 else (si >= ti))
        q = _silu(q_ref[0]) * HG_DK ** -0.5
        f = f_ref[0]
        v = i_ref[0]
        sig_f = _sigmoid(f)
        log_f = jnp.log(lb + (1.0 - lb) * sig_f)
        k = (1.0 - lb) * (1.0 - sig_f)
        cum = _dot_01_into(incl.astype(F32), log_f)
        last = c - 1 if zs == 0 else 0
        tot = jnp.concatenate([jnp.broadcast_to(cum[j * c + last:j * c + last + 1, :], (c, cum.shape[1]))
                               for j in range(nsub)], axis=0)
        q_in = q * jnp.exp(cum)
        k_in = k * jnp.exp(-cum)
        k_end = k * jnp.exp(tot - cum)
        e_tot = jnp.exp(tot)
        intra = [_dot(jnp.where(incl, _dot_nt(q_in[:, hs], k_in[:, hs]), 0.0), v[:, hs]) for hs in heads]
        order = range(nsub) if zs == 0 else range(nsub - 1, -1, -1)
        grow = {(j, h): _dot_tn(v[j * c:(j + 1) * c, hs], k_end[j * c:(j + 1) * c, hs])
                for j in order for h, hs in enumerate(heads)}
        state = [s_ref[h] for h in range(HG_H)]
        inter = {}
        for j in order:
            rows = slice(j * c, (j + 1) * c)
            for h, hs in enumerate(heads):
                inter[(j, h)] = _dot_nt(q_in[rows, hs], state[h])
                state[h] = state[h] * e_tot[j * c:j * c + 1, hs] + grow[(j, h)]
        for h, hs in enumerate(heads):
            o_ref[0, 0, :, hs] = intra[h] + jnp.concatenate([inter[(j, h)] for j in range(nsub)], axis=0)
            s_ref[h] = state[h]

    for zs in range(2):
        pl.when(z == zs)(functools.partial(run, zs))


def gla_scan(ph, lb, ctx_len):
    bsz, lt, _ = ph.shape
    w = HG_W
    tb = SCAN_BLOCK
    nb, cb = lt // tb, ctx_len // tb
    tmap = lambda z, n: _time_block(z, n, cb, nb)
    return pl.pallas_call(
        functools.partial(_gla_kernel, tb=tb, c=HG_CHUNK),
        out_shape=jax.ShapeDtypeStruct((2, bsz, lt, w), F32),
        grid=(2, bsz, nb),
        in_specs=[pl.BlockSpec((1, tb, w), lambda z, b, n: (b, tmap(z, n), 0)),
                  pl.BlockSpec((1, tb, w), lambda z, b, n: (b, tmap(z, n), 1 + z)),
                  pl.BlockSpec((1, tb, w), lambda z, b, n: (b, tmap(z, n), 3)),
                  pl.BlockSpec((1, w), lambda z, b, n: (0, 0))],
        out_specs=pl.BlockSpec((1, 1, tb, w), lambda z, b, n: (z, b, tmap(z, n), 0)),
        scratch_shapes=[pltpu.VMEM((HG_H, HG_DV, HG_DK), F32)],
        compiler_params=_cparams(("parallel", "parallel", "arbitrary")),
        name="gla_scan",
    )(ph, ph, ph, lb.reshape(1, w))


def _group_rms(y, groups, gain):
    width = y.shape[-1] // groups
    parts = []
    for g in range(groups):
        yg = y[:, g * width:(g + 1) * width]
        parts.append(yg * lax.rsqrt(jnp.mean(yg * yg, axis=-1, keepdims=True) + EPS))
    return jnp.concatenate(parts, axis=1) * gain


def _even_out_kernel(x_ref, mod_ref, or_ref, bv_ref, g_ref, og_ref, gate_ref, lnw_ref, lnb_ref, nw_ref, bd_ref,
                     w_ref, o_ref):
    bd = bd_ref[...]
    o = or_ref[0, 0] + or_ref[1, 0]
    mu = _dot_into_01(o, bd) * (1.0 / RW_N)
    oc = o - mu
    var = _dot_into_01(oc * oc, bd) * (1.0 / RW_N)
    yr = (oc * lax.rsqrt(var + RW_GN_EPS) * lnw_ref[...] + lnb_ref[...] + bv_ref[0]) * g_ref[0]
    og = og_ref[0, 0] + og_ref[1, 0]
    yh = _group_rms(og, HG_H, nw_ref[...]) * _silu(gate_ref[0])
    y = jnp.concatenate([yr, yh], axis=1).astype(BF16)
    o_ref[0] = x_ref[0] + mod_ref[0, 0][2:3] * _dot(y, w_ref[...])


def even_out(x, modtab, o_r, bv, g, o_g, ph, ln_w, ln_b, norm_w, w_out, ctx_len):
    bsz, lt, d = x.shape
    tm = TOK_TILE
    ctx_tiles = ctx_len // tm
    w = RW_W
    tok = lambda width, col=0: pl.BlockSpec((1, tm, width), lambda b, i: (b, i, col))
    two = pl.BlockSpec((2, 1, tm, w), lambda b, i: (0, b, i, 0))
    row = lambda width: pl.BlockSpec((1, width), lambda b, i: (0, 0))
    return pl.pallas_call(
        _even_out_kernel,
        out_shape=jax.ShapeDtypeStruct((bsz, lt, d), F32),
        grid=(bsz, lt // tm),
        in_specs=[tok(d),
                  pl.BlockSpec((1, 1, N_MOD, d), lambda b, i: (b, (i >= ctx_tiles).astype(I32), 0, 0)),
                  two, tok(w), tok(w), two, tok(HG_W, 4), row(w), row(w), row(HG_W),
                  pl.BlockSpec((w, w), lambda b, i: (0, 0)),
                  pl.BlockSpec(w_out.shape, lambda b, i: (0, 0))],
        out_specs=tok(d),
        compiler_params=_cparams(("parallel", "parallel")),
        name="even_out",
    )(x, modtab, o_r, bv, g, o_g, ph, ln_w.reshape(1, w), ln_b.reshape(1, w), norm_w.reshape(1, HG_W),
      _head_block_ones(w, RW_N), w_out)


def _conv_silu(x_ref, xp_ref, xn_ref, w_ref, b_ref, has_prev, has_next, tm):
    prev = jnp.where(has_prev, xp_ref[0], 0.0)
    nxt = jnp.where(has_next, xn_ref[0], 0.0)
    ext = jnp.concatenate([prev, x_ref[0], nxt], axis=0)
    rows = tm + 2 * HALO
    pad = CONV_K // 2
    acc = b_ref[...] + jnp.zeros((tm, ext.shape[1]), F32)
    for j in range(CONV_K):
        d = j - pad
        sh = ext if d == 0 else pltpu.roll(ext, (-d) % rows, 0)
        acc = acc + sh[HALO:HALO + tm] * w_ref[j:j + 1, :]
    return _silu(acc)


def _odd_prep_kernel(xs_ref, xsp_ref, xsn_ref, xm_ref, xmp_ref, xmn_ref, ws_ref, bs_ref, wm_ref, bm_ref,
                     os_ref, om_ref, *, tm, ctx_len, lt):
    has_prev, has_next = _segment_edges(pl.program_id(1), tm, ctx_len, lt)
    os_ref[0] = _conv_silu(xs_ref, xsp_ref, xsn_ref, ws_ref, bs_ref, has_prev, has_next, tm)
    om_ref[0] = _conv_silu(xm_ref, xmp_ref, xmn_ref, wm_ref, bm_ref, has_prev, has_next, tm)


def odd_prep(xbc, qk, ws, bs, wm, bm, ctx_len):
    bsz, lt, cs = xbc.shape
    cm = wm.shape[1]
    tm = TOK_TILE
    full = lambda a: pl.BlockSpec(a.shape, lambda b, i: (0,) * a.ndim)
    params = [ws, bs.reshape(1, cs), wm, bm.reshape(1, cm)]
    return pl.pallas_call(
        functools.partial(_odd_prep_kernel, tm=tm, ctx_len=ctx_len, lt=lt),
        out_shape=[jax.ShapeDtypeStruct((bsz, lt, cs), F32), jax.ShapeDtypeStruct((bsz, lt, cm), F32)],
        grid=(bsz, lt // tm),
        in_specs=_halo_specs(tm, cs, lt) + _halo_specs(tm, cm, lt) + [full(a) for a in params],
        out_specs=[pl.BlockSpec((1, tm, cs), lambda b, i: (b, i, 0)),
                   pl.BlockSpec((1, tm, cm), lambda b, i: (b, i, 0))],
        compiler_params=_cparams(("parallel", "parallel")),
        name="odd_prep",
    )(xbc, xbc, xbc, qk, qk, qk, *params)


def _ssd_kernel(x_ref, b_ref, c_ref, sm_ref, dtb_ref, alog_ref, ex_ref, o_ref, s_ref, *, tb, c):
    z = pl.program_id(0)
    nsub = tb // c
    hg = SSD_H // SSD_G

    @pl.when(pl.program_id(2) == 0)
    def _():
        s_ref[...] = jnp.zeros_like(s_ref)

    incl, _ = _dir_masks(z, c)
    incl_f = incl.astype(F32)
    ex = ex_ref[...]

    gw = hg * SSD_P

    def local(j):
        jj = jnp.where(z == 0, j, nsub - 1 - j)
        rows = pl.ds(pl.multiple_of(jj * c, c), c)
        x = x_ref[0, rows, :]
        bm = b_ref[0, rows, :]
        cm = c_ref[0, rows, :]
        dt = _softplus(sm_ref[0, rows, :] + dtb_ref[0])
        la = -dt * jnp.exp(alog_ref[0])
        a_cum = _dot_01_into(incl_f, la)
        a_tot = jnp.sum(la, axis=0, keepdims=True)
        a_cum_t = a_cum.T
        e_tot_col = jnp.exp(jnp.where(z == 0, a_cum_t[:, c - 1:c], a_cum_t[:, 0:1]))
        wide = _dot(jnp.concatenate([dt, jnp.exp(a_cum), jnp.exp(a_tot - a_cum)], axis=0).astype(BF16),
                    ex.astype(BF16))
        xdt = x * wide[:c]
        din, dend = wide[c:2 * c], wide[2 * c:]
        xd_end = xdt * dend
        bgs = [bm[:, g * SSD_N:(g + 1) * SSD_N] for g in range(SSD_G)]
        cgs = [cm[:, g * SSD_N:(g + 1) * SSD_N] for g in range(SSD_G)]
        cb = [_dot_nt(cg, bg) for cg, bg in zip(cgs, bgs)]
        grow = [_dot_tn(xd_end[:, g * gw:(g + 1) * gw], bg) for g, bg in enumerate(bgs)]
        intra = []
        for h in range(SSD_H):
            seg = a_cum[:, h:h + 1] - a_cum_t[h:h + 1, :]
            wmat = cb[h // hg] * jnp.where(incl, jnp.exp(seg), 0.0)
            intra.append(_dot(wmat, xdt[:, h * SSD_P:(h + 1) * SSD_P]))
        decay = [jnp.concatenate([jnp.broadcast_to(e_tot_col[h:h + 1, :], (SSD_P, 1))
                                  for h in range(g * hg, (g + 1) * hg)], axis=0) for g in range(SSD_G)]
        return dict(rows=rows, cgs=cgs, din=din, intra=jnp.concatenate(intra, axis=1), grow=grow, decay=decay)

    subs = [local(j) for j in range(nsub)]
    state = [s_ref[g] for g in range(SSD_G)]
    for sub in subs:
        inter = [_dot_nt(cg, s) for cg, s in zip(sub["cgs"], state)]
        o_ref[0, 0, sub["rows"], :] = sub["intra"] + sub["din"] * jnp.concatenate(inter, axis=1)
        state = [s * d + g for s, d, g in zip(state, sub["decay"], sub["grow"])]
    for g in range(SSD_G):
        s_ref[g] = state[g]


def _head_expand(heads, width, rows):
    h = jnp.arange(rows)[:, None]
    col = jnp.arange(heads * width)[None, :] // width
    return (h == col).astype(F32)


def ssd_scan(xbc, small, dtb, alog, ctx_len):
    bsz, lt, _ = xbc.shape
    tb = SCAN_BLOCK
    nb, cb = lt // tb, ctx_len // tb
    tmap = lambda z, n: _time_block(z, n, cb, nb)
    gw = SSD_G * SSD_N
    return pl.pallas_call(
        functools.partial(_ssd_kernel, tb=tb, c=SSD_CHUNK),
        out_shape=jax.ShapeDtypeStruct((2, bsz, lt, SSD_INNER), F32),
        grid=(2, bsz, nb),
        in_specs=[pl.BlockSpec((1, tb, SSD_INNER), lambda z, b, n: (b, tmap(z, n), 0)),
                  pl.BlockSpec((1, tb, gw), lambda z, b, n: (b, tmap(z, n), SSD_INNER // gw)),
                  pl.BlockSpec((1, tb, gw), lambda z, b, n: (b, tmap(z, n), SSD_INNER // gw + 1)),
                  pl.BlockSpec((1, tb, SMALL_W), lambda z, b, n: (b, tmap(z, n), z)),
                  pl.BlockSpec((1, 1, SMALL_W), lambda z, b, n: (z, 0, 0)),
                  pl.BlockSpec((1, 1, SMALL_W), lambda z, b, n: (z, 0, 0)),
                  pl.BlockSpec((SMALL_W, SSD_INNER), lambda z, b, n: (0, 0))],
        out_specs=pl.BlockSpec((1, 1, tb, SSD_INNER), lambda z, b, n: (z, b, tmap(z, n), 0)),
        scratch_shapes=[pltpu.VMEM((SSD_G, SSD_H // SSD_G * SSD_P, SSD_N), F32)],
        compiler_params=_cparams(("parallel", "parallel", "arbitrary")),
        name="ssd_scan",
    )(xbc, xbc, xbc, small, dtb, alog, _head_expand(SSD_H, SSD_P, SMALL_W))


ML_I_LANE = SSD_H
ML_F_LANE = SSD_H + ML_H


def _mlstm_kernel(q_ref, k_ref, v_ref, sm_ref, gb_ref, o_ref, cs_ref, ns_ref, m_ref, *, tb, c):
    z = pl.program_id(0)
    nsub = tb // c

    @pl.when(pl.program_id(2) == 0)
    def _():
        cs_ref[...] = jnp.zeros_like(cs_ref)
        ns_ref[...] = jnp.zeros_like(ns_ref)
        m_ref[...] = jnp.full(m_ref.shape, ML_M_INIT, F32)

    incl, _ = _dir_masks(z, c)
    incl_f = incl.astype(F32)
    lane = lax.broadcasted_iota(I32, (1, SMALL_W), 1)
    is_f = jnp.logical_and(lane >= ML_F_LANE, lane < ML_F_LANE + ML_H)

    def local(j):
        rows = pl.ds(pl.multiple_of(jnp.where(z == 0, j, nsub - 1 - j) * c, c), c)
        q = q_ref[0, rows, :]
        k = k_ref[0, rows, :] * ML_DK ** -0.5
        v = v_ref[0, rows, :]
        pre = sm_ref[0, rows, :] + gb_ref[0]
        gates = jnp.where(is_f, -_softplus(-pre), pre)
        f_cum = _dot(incl_f, gates, HI)
        f_tot = jnp.sum(gates, axis=0, keepdims=True)
        f_cum_t = f_cum.T
        gates_t = gates.T
        per_head = []
        for h in range(ML_H):
            li, lf = ML_I_LANE + h, ML_F_LANE + h
            qh = q[:, h * ML_DK:(h + 1) * ML_DK]
            kh = k[:, h * ML_DK:(h + 1) * ML_DK]
            vh = v[:, h * ML_DV:(h + 1) * ML_DV]
            f_col = f_cum[:, lf:lf + 1]
            f_last = f_tot[:, lf:lf + 1]
            logw_end = f_last - f_col + gates[:, li:li + 1]
            m_end = jnp.max(logw_end, axis=0, keepdims=True)
            kw = kh * jnp.exp(logw_end - m_end)
            logw = jnp.where(incl, f_col - f_cum_t[lf:lf + 1, :] + gates_t[li:li + 1, :], -jnp.inf)
            m_loc = jnp.max(logw, axis=1, keepdims=True)
            scores = _dot_nt(qh, kh) * jnp.exp(logw - m_loc)
            per_head.append(dict(
                q=qh, f_col=f_col, f_last=f_last, m_end=m_end, m_loc=m_loc, num=_dot(scores, vh),
                den=jnp.sum(scores, axis=1, keepdims=True), kv=_dot_tn(kw, vh),
                ksum=jnp.sum(kw, axis=0, keepdims=True)))
        return rows, per_head

    subs = [local(j) for j in range(nsub)]
    cs = [cs_ref[h] for h in range(ML_H)]
    ns = [ns_ref[h] for h in range(ML_H)]
    m = [m_ref[h] for h in range(ML_H)]
    for rows, per_head in subs:
        outs = []
        for h, p in enumerate(per_head):
            m_t = jnp.maximum(p["m_loc"], p["f_col"] + m[h])
            w_loc = jnp.exp(p["m_loc"] - m_t)
            w_inter = jnp.exp(p["f_col"] + m[h] - m_t)
            num = p["num"] * w_loc + w_inter * _dot(p["q"], cs[h])
            den = p["den"] * w_loc + w_inter * jnp.sum(p["q"] * ns[h], axis=1, keepdims=True)
            outs.append(num / jnp.maximum(jnp.abs(den), jnp.exp(-m_t)))
            m_new = jnp.maximum(p["f_last"] + m[h], p["m_end"])
            s_keep = jnp.exp(p["f_last"] + m[h] - m_new)
            s_loc = jnp.exp(p["m_end"] - m_new)
            cs[h] = s_keep * cs[h] + s_loc * p["kv"]
            ns[h] = s_keep * ns[h] + s_loc * p["ksum"]
            m[h] = m_new
        o_ref[0, 0, rows, :] = jnp.concatenate(outs, axis=1)
    for h in range(ML_H):
        cs_ref[h] = cs[h]
        ns_ref[h] = ns[h]
        m_ref[h] = m[h]


def mlstm_scan(qk, pm, small, gate_bias, ctx_len):
    bsz, lt, _ = qk.shape
    tb = SCAN_BLOCK
    nb, cb = lt // tb, ctx_len // tb
    tmap = lambda z, n: _time_block(z, n, cb, nb)
    return pl.pallas_call(
        functools.partial(_mlstm_kernel, tb=tb, c=ML_CHUNK),
        out_shape=jax.ShapeDtypeStruct((2, bsz, lt, ML_VW), F32),
        grid=(2, bsz, nb),
        in_specs=[pl.BlockSpec((1, tb, ML_QK), lambda z, b, n: (b, tmap(z, n), 0)),
                  pl.BlockSpec((1, tb, ML_QK), lambda z, b, n: (b, tmap(z, n), 1)),
                  pl.BlockSpec((1, tb, ML_VW), lambda z, b, n: (b, tmap(z, n), 1)),
                  pl.BlockSpec((1, tb, SMALL_W), lambda z, b, n: (b, tmap(z, n), z)),
                  pl.BlockSpec((1, 1, SMALL_W), lambda z, b, n: (z, 0, 0))],
        out_specs=pl.BlockSpec((1, 1, tb, ML_VW), lambda z, b, n: (z, b, tmap(z, n), 0)),
        scratch_shapes=[pltpu.VMEM((ML_H, ML_DK, ML_DV), F32), pltpu.VMEM((ML_H, 1, ML_DK), F32),
                        pltpu.VMEM((ML_H, 1, 1), F32)],
        compiler_params=_cparams(("parallel", "parallel", "arbitrary")),
        name="mlstm_scan",
    )(qk, qk, pm, small, gate_bias)


def _odd_out_kernel(x_ref, mod_ref, ys_ref, xc_ref, zg_ref, ym_ref, og_ref, dsk_ref, nws_ref, nwm_ref, w_ref, o_ref):
    y = (ys_ref[0, 0] + ys_ref[1, 0] + dsk_ref[...] * xc_ref[0]) * _silu(zg_ref[0])
    ys = _group_rms(y, SSD_G, nws_ref[...])
    hm = ym_ref[0, 0] + ym_ref[1, 0]
    ym = _group_rms(hm, ML_H, nwm_ref[...]) * _sigmoid(og_ref[0])
    yy = jnp.concatenate([ys, ym], axis=1).astype(BF16)
    o_ref[0] = x_ref[0] + mod_ref[0, 0][2:3] * _dot(yy, w_ref[...])


def odd_out(x, modtab, y_s, xbc, ps, y_m, pm, d_skip, nw_s, nw_m, w_out, ctx_len):
    bsz, lt, d = x.shape
    tm = TOK_TILE
    ct = ctx_len // tm
    seq = lt - ctx_len
    tok = lambda width, col=0: pl.BlockSpec((1, tm, width), lambda b, i: (b, i + ct, col))
    two = lambda width: pl.BlockSpec((2, 1, tm, width), lambda b, i: (0, b, i + ct, 0))
    row = lambda width: pl.BlockSpec((1, width), lambda b, i: (0, 0))
    return pl.pallas_call(
        _odd_out_kernel,
        out_shape=jax.ShapeDtypeStruct((bsz, seq, d), F32),
        grid=(bsz, seq // tm),
        in_specs=[tok(d), pl.BlockSpec((1, 1, N_MOD, d), lambda b, i: (b, 1, 0, 0)),
                  two(SSD_INNER), tok(SSD_INNER), tok(SSD_INNER), two(ML_VW), tok(ML_VW, 2),
                  row(SSD_INNER), row(SSD_INNER), row(ML_VW),
                  pl.BlockSpec(w_out.shape, lambda b, i: (0, 0))],
        out_specs=pl.BlockSpec((1, tm, d), lambda b, i: (b, i, 0)),
        compiler_params=_cparams(("parallel", "parallel")),
        name="odd_out",
    )(x, modtab, y_s, xbc, ps, y_m, pm, d_skip.reshape(1, SSD_INNER), nw_s.reshape(1, SSD_INNER),
      nw_m.reshape(1, ML_VW), w_out)


def _router_kernel(x_ref, mod_ref, nw_ref, rw_ref, rb_ref, h_ref, idx_ref, prob_ref, rank_ref, cnt_ref, carry_ref,
                   *, tm):
    first = jnp.logical_and(pl.program_id(0) == 0, pl.program_id(1) == 0)

    @pl.when(first)
    def _():
        carry_ref[...] = jnp.zeros_like(carry_ref)

    h = _norm_mod(x_ref[0], nw_ref[...], mod_ref[0, 0], 3)
    h_ref[0] = _pack_bf16_halves(h)
    logits = _dot(h, rw_ref[...], HI) + rb_ref[...]
    lane = lax.broadcasted_iota(I32, (tm, N_EXP), 1).astype(F32)
    work = logits
    sel = jnp.zeros((tm, N_EXP), F32)
    vals, idxs, hots = [], [], []
    for _ in range(TOP_K):
        m = jnp.max(work, axis=1, keepdims=True)
        idx = jnp.min(jnp.where(work == m, lane, float(N_EXP)), axis=1, keepdims=True)
        hot = lane == idx
        vals.append(m)
        idxs.append(idx)
        hots.append(hot.astype(F32))
        sel = sel + hots[-1]
        work = jnp.where(hot, -jnp.inf, work)
    ex = [jnp.exp(v - vals[0]) for v in vals]
    tot = ex[0] + ex[1] + ex[2] + ex[3]
    ti = lax.broadcasted_iota(I32, (tm, tm), 0)
    si = lax.broadcasted_iota(I32, (tm, tm), 1)
    before = _dot((si < ti).astype(BF16), sel.astype(BF16)) + carry_ref[...]
    ranks = [jnp.sum(before * hot, axis=1, keepdims=True) for hot in hots]
    idx_ref[0] = jnp.concatenate(idxs, axis=1).astype(I32)
    prob_ref[0] = jnp.concatenate([e / tot for e in ex], axis=1)
    rank_ref[0] = jnp.concatenate(ranks, axis=1).astype(I32)
    carry_ref[...] = carry_ref[...] + jnp.sum(sel, axis=0, keepdims=True)
    cnt_ref[...] = carry_ref[...]


def router(x, modtab, nw, rw, rb, ctx_len):
    bsz, lt, d = x.shape
    tm = TOK_TILE
    ctx_tiles = ctx_len // tm
    tok = lambda width, dt: (jax.ShapeDtypeStruct((bsz, lt, width), dt),
                             pl.BlockSpec((1, tm, width), lambda b, i: (b, i, 0)))
    outs = [tok(d // 2, U32), tok(TOP_K, I32), tok(TOP_K, F32), tok(TOP_K, I32),
            (jax.ShapeDtypeStruct((1, N_EXP), F32), pl.BlockSpec((1, N_EXP), lambda b, i: (0, 0)))]
    return pl.pallas_call(
        functools.partial(_router_kernel, tm=tm),
        out_shape=[o[0] for o in outs],
        grid=(bsz, lt // tm),
        in_specs=[pl.BlockSpec((1, tm, d), lambda b, i: (b, i, 0)),
                  pl.BlockSpec((1, 1, N_MOD, d), lambda b, i: (b, (i >= ctx_tiles).astype(I32), 0, 0)),
                  pl.BlockSpec((1, d), lambda b, i: (0, 0)),
                  pl.BlockSpec((d, N_EXP), lambda b, i: (0, 0)),
                  pl.BlockSpec((1, N_EXP), lambda b, i: (0, 0))],
        out_specs=[o[1] for o in outs],
        scratch_shapes=[pltpu.VMEM((1, N_EXP), F32)],
        compiler_params=_cparams(("arbitrary", "arbitrary")),
        name="router",
    )(x, modtab, nw.reshape(1, d), rw, rb.reshape(1, N_EXP))


def sc_gather_rows(table, idx):
    m = idx.shape[0]
    d = table.shape[1]
    workers = SC_CORES * SC_SUBCORES
    per_w, n_chunks = m // workers, m // workers // SC_GATHER_ROWS
    assert per_w * workers == m and n_chunks * SC_GATHER_ROWS == per_w and n_chunks % 2 == 0
    w = SC_GATHER_ROWS
    mesh = plsc.VectorSubcoreMesh(core_axis_name="c", subcore_axis_name="s")

    @functools.partial(
        pl.kernel, mesh=mesh, out_type=jax.ShapeDtypeStruct((m, d), table.dtype),
        scratch_types=[pltpu.VMEM((per_w,), I32), pltpu.VMEM((2, w, d), table.dtype),
                       pltpu.SemaphoreType.DMA((2,)), pltpu.SemaphoreType.DMA((2,))])
    def gather_kernel(table_hbm, idx_hbm, out_hbm, idx_v, rows_v, gsem, wsem):
        wid = lax.axis_index("s") * SC_CORES + lax.axis_index("c")
        base = pl.multiple_of(wid * per_w, 8)
        pltpu.sync_copy(idx_hbm.at[pl.ds(base, per_w)], idx_v)

        def gather(g, b):
            rows = idx_v.at[pl.ds(pl.multiple_of(g * w, 8), w)]
            return pltpu.make_async_copy(table_hbm.at[rows], rows_v.at[b], gsem.at[b])

        def write(g, b):
            dst = out_hbm.at[pl.ds(pl.multiple_of(base + g * w, 8), w)]
            return pltpu.make_async_copy(rows_v.at[b], dst, wsem.at[b])

        gather(0, 0).start()

        @pl.loop(0, n_chunks, step=2)
        def _(g):
            for b in (0, 1):
                cur = g + b
                gather(cur, b).wait()
                write(cur, b).start()

                @pl.when(cur >= 1)
                def _():
                    write(cur - 1, 1 - b).wait()

                @pl.when(cur + 1 < n_chunks)
                def _():
                    gather(cur + 1, 1 - b).start()

        write(n_chunks - 1, 1).wait()

    return gather_kernel(table, idx)


def sc_scatter_rows(src, idx, n_out):
    k_dup, t = idx.shape
    d = src.shape[1]
    workers = SC_CORES * SC_SUBCORES
    w = SC_GATHER_ROWS
    per_w, n_chunks = t // workers, t // workers // w
    assert per_w * workers == t and n_chunks * w == per_w and n_chunks % 2 == 0
    mesh = plsc.VectorSubcoreMesh(core_axis_name="c", subcore_axis_name="s")

    @functools.partial(
        pl.kernel, mesh=mesh, out_type=jax.ShapeDtypeStruct((n_out, d), src.dtype),
        scratch_types=[pltpu.VMEM((k_dup, n_chunks, w), I32), pltpu.VMEM((2, w, d), src.dtype),
                       pltpu.SemaphoreType.DMA((2,)), pltpu.SemaphoreType.DMA((2,))])
    def scatter_kernel(src_hbm, idx_hbm, out_hbm, idx_v, rows_v, rsem, wsem):
        wid = lax.axis_index("s") * SC_CORES + lax.axis_index("c")
        base = pl.multiple_of(wid * per_w, 8)
        for k in range(k_dup):
            pltpu.sync_copy(idx_hbm.at[k, wid], idx_v.at[k])

        def read(g, b):
            rows = src_hbm.at[pl.ds(pl.multiple_of(base + g * w, 8), w)]
            return pltpu.make_async_copy(rows, rows_v.at[b], rsem.at[b])

        def write(g, b, k):
            return pltpu.make_async_copy(rows_v.at[b], out_hbm.at[idx_v.at[k, g]], wsem.at[b])

        read(0, 0).start()

        @pl.loop(0, n_chunks, step=2)
        def _(g):
            for b in (0, 1):
                cur = g + b
                read(cur, b).wait()
                for k in range(k_dup):
                    write(cur, b, k).start()

                @pl.when(cur >= 1)
                def _():
                    for k in range(k_dup):
                        write(cur - 1, 1 - b, k).wait()

                @pl.when(cur + 1 < n_chunks)
                def _():
                    read(cur + 1, 1 - b).start()

        for k in range(k_dup):
            write(n_chunks - 1, 1, k).wait()

    return scatter_kernel(src, idx.reshape(k_dup, workers, n_chunks, w))


def _expert_kernel(be_ref, used_ref, x_ref, wg_ref, bg_ref, wu_ref, bu_ref, wd_ref, bd_ref, o_ref, wg_bf, wu_bf,
                   wd_bf):
    i = pl.program_id(0)

    @pl.when(jnp.logical_or(i == 0, be_ref[i] != be_ref[jnp.maximum(i - 1, 0)]))
    def _():
        wg_bf[...] = wg_ref[0, 0].astype(BF16)
        wu_bf[...] = wu_ref[0, 0].astype(BF16)
        wd_bf[...] = wd_ref[0, 0].astype(BF16)

    @pl.when(i < used_ref[0])
    def _():
        xb = _unpack_bf16_halves(x_ref[...]).astype(BF16)
        g = _dot(xb, wg_bf[...]) + bg_ref[0, 0]
        u = _dot(xb, wu_bf[...]) + bu_ref[0, 0]
        g = jnp.minimum(g, SWIGLU_LIMIT)
        u = jnp.clip(u, -SWIGLU_LIMIT, SWIGLU_LIMIT)
        act = (g * _sigmoid(SWIGLU_ALPHA * g) * (u + 1.0)).astype(BF16)
        o_ref[...] = _pack_bf16_halves(_dot(act, wd_bf[...]) + bd_ref[0, 0])


def expert_ffn(block_expert, blocks_used, x_sorted, layer, wg, bg, wu, bu, wd, bd):
    nblk = block_expert.shape[0]
    n_layers, _, d, f = wg.shape
    wspec = lambda r, c: pl.BlockSpec((1, 1, r, c), lambda i, be, used: (layer, be[i], 0, 0))
    grid_spec = pltpu.PrefetchScalarGridSpec(
        num_scalar_prefetch=2,
        grid=(nblk,),
        in_specs=[pl.BlockSpec((MOE_BLOCK, d // 2), lambda i, be, used: (i, 0)),
                  wspec(d, f), wspec(1, f), wspec(d, f), wspec(1, f), wspec(f, d), wspec(1, d)],
        out_specs=pl.BlockSpec((MOE_BLOCK, d // 2), lambda i, be, used: (i, 0)),
        scratch_shapes=[pltpu.VMEM((d, f), BF16), pltpu.VMEM((d, f), BF16), pltpu.VMEM((f, d), BF16)],
    )
    return pl.pallas_call(
        _expert_kernel,
        out_shape=jax.ShapeDtypeStruct((nblk * MOE_BLOCK, d // 2), U32),
        grid_spec=grid_spec,
        compiler_params=_cparams(("arbitrary",)),
        name="expert_ffn",
    )(block_expert, blocks_used, x_sorted, wg, bg.reshape(n_layers, N_EXP, 1, f), wu,
      bu.reshape(n_layers, N_EXP, 1, f), wd, bd.reshape(n_layers, N_EXP, 1, d))


def _combine_kernel(x_ref, mod_ref, prob_ref, y_ref, fw_ref, o_ref, *, final_norm):
    p = prob_ref[0]
    acc = _unpack_bf16_halves(y_ref[0, 0]) * p[:, 0:1]
    for k in range(1, TOP_K):
        acc = acc + _unpack_bf16_halves(y_ref[k, 0]) * p[:, k:k + 1]
    out = x_ref[0] + mod_ref[0, 0][5:6] * acc
    if final_norm:
        out = out * lax.rsqrt(jnp.mean(out * out, axis=-1, keepdims=True) + EPS) * fw_ref[...]
    o_ref[0] = out


def combine(x, modtab, probs, y_tok, final_w, ctx_len, final_norm):
    bsz, lt, d = x.shape
    tm = TOK_TILE
    ctx_tiles = ctx_len // tm
    return pl.pallas_call(
        functools.partial(_combine_kernel, final_norm=final_norm),
        out_shape=jax.ShapeDtypeStruct((bsz, lt, d), F32),
        grid=(bsz, lt // tm),
        in_specs=[pl.BlockSpec((1, tm, d), lambda b, i: (b, i, 0)),
                  pl.BlockSpec((1, 1, N_MOD, d), lambda b, i: (b, (i >= ctx_tiles).astype(I32), 0, 0)),
                  pl.BlockSpec((1, tm, TOP_K), lambda b, i: (b, i, 0)),
                  pl.BlockSpec((TOP_K, 1, tm, d // 2), lambda b, i: (0, b, i, 0)),
                  pl.BlockSpec((1, d), lambda b, i: (0, 0))],
        out_specs=pl.BlockSpec((1, tm, d), lambda b, i: (b, i, 0)),
        compiler_params=_cparams(("parallel", "parallel")),
        name="combine",
    )(x, modtab, probs, y_tok, final_w.reshape(1, d))


def moe_layer(x, modtab, layer, nw, rw, rb, wg, bg, wu, bu, wd, bd, final_w, ctx_len, final_norm):
    bsz, lt, d = x.shape
    t = bsz * lt
    h, idx, probs, rank, counts = router(x, modtab, nw, rw, rb, ctx_len)
    counts = counts.reshape(N_EXP).astype(I32)
    padded = (counts + MOE_BLOCK - 1) // MOE_BLOCK * MOE_BLOCK
    ends = jnp.cumsum(padded)
    pstart = ends - padded
    idx = idx.reshape(t, TOP_K)
    dest = pstart[idx] + rank.reshape(t, TOP_K)
    nblk = -(-t * TOP_K // MOE_BLOCK) + N_EXP
    block_row0 = jnp.arange(nblk, dtype=I32) * MOE_BLOCK
    block_expert = jnp.minimum(jnp.sum((ends[None, :] <= block_row0[:, None]).astype(I32), axis=1), N_EXP - 1)
    dest_k = dest.T
    x_sorted = sc_scatter_rows(h.reshape(t, d // 2), dest_k, nblk * MOE_BLOCK)
    blocks_used = (ends[-1:] // MOE_BLOCK).astype(I32)
    y_sorted = expert_ffn(block_expert, blocks_used, x_sorted, layer, wg, bg, wu, bu, wd, bd)
    y_tok = sc_gather_rows(y_sorted, dest_k.reshape(TOP_K * t)).reshape(TOP_K, bsz, lt, d // 2)
    return combine(x, modtab, probs, y_tok, final_w, ctx_len, final_norm)


def even_layer(x, modtab, norm1_w, w_in, w_out, mu, w0, w2, a0, a2, g2, k_k, k_a, r_k, ln_w, ln_b, lb, hg_norm_w,
               ctx_len):
    w_in = w_in.astype(BF16)
    pr, ph = in_proj(x, modtab, norm1_w, [w_in[:, :RW_PROJ], w_in[:, RW_PROJ:]], ctx_len)
    r, v, kk, g, bv, lw, km, bb = rwkv_prep(pr, mu, w0, w2, a0, a2, g2, k_k, k_a, r_k, ctx_len)
    o_r = rwkv_scan(r, v, kk, lw, km, bb, ctx_len)
    o_g = gla_scan(ph, lb, ctx_len)
    return even_out(x, modtab, o_r, bv, g, o_g, ph, ln_w, ln_b, hg_norm_w, w_out.astype(BF16), ctx_len)


def _dir_slab(cols, width):
    halves = []
    for z in range(2):
        parts = [c[:, z * (c.shape[1] // 2):(z + 1) * (c.shape[1] // 2)] for c in cols]
        used = sum(p.shape[1] for p in parts)
        parts.append(jnp.zeros((cols[0].shape[0], width - used), cols[0].dtype))
        halves.append(jnp.concatenate(parts, axis=1))
    return jnp.concatenate(halves, axis=1)


def _dir_rows(vals, width):
    rows = jnp.concatenate(vals, axis=1)
    return jnp.pad(rows, ((0, 0), (0, width - rows.shape[1]))).reshape(2, 1, width)


def odd_layer(x, modtab, norm1_w, w_in, w_out, s_conv_w, s_conv_b, dt_bias, a_log, d_skip, s_norm_w, m_conv_w,
              m_conv_b, i_bias, f_bias, m_norm_w, ctx_len):
    o = 0
    cols = {}
    for name, width in (("zg", SSD_INNER), ("xbc", SSD_CONV_CH), ("dt", 2 * SSD_H), ("qk", 2 * ML_QK),
                        ("v", ML_VW), ("og", ML_VW), ("ig", 2 * ML_H), ("fg", 2 * ML_H)):
        cols[name] = w_in[:, o:o + width]
        o += width
    slab_m = jnp.concatenate([cols["qk"], cols["v"], cols["og"]], axis=1).astype(BF16)
    slab_small = _dir_slab([cols["dt"], cols["ig"], cols["fg"]], SMALL_W).astype(BF16)
    pxbc, zg, pm, small = in_proj(x, modtab, norm1_w, [cols["xbc"].astype(BF16), cols["zg"].astype(BF16), slab_m,
                                                       slab_small], ctx_len)
    xbc, qk = odd_prep(pxbc, pm, s_conv_w, s_conv_b, m_conv_w, m_conv_b, ctx_len)
    zeros_g = jnp.zeros((2, 2 * ML_H), F32)
    y_s = ssd_scan(xbc, small, _dir_rows([dt_bias, zeros_g], SMALL_W), _dir_rows([a_log, zeros_g], SMALL_W), ctx_len)
    y_m = mlstm_scan(qk, pm, small, _dir_rows([jnp.zeros((2, SSD_H), F32), i_bias, f_bias], SMALL_W), ctx_len)
    return odd_out(x, modtab, y_s, xbc, zg, y_m, pm, jnp.repeat(d_skip, SSD_P), s_norm_w, m_norm_w,
                   w_out.astype(BF16), ctx_len)


def _to_col_major(t):
    b, s, ch = t.shape
    return t.reshape(b, s // GRID_W, GRID_W, ch).transpose(0, 2, 1, 3).reshape(b, s, ch)


def _to_row_major(t):
    b, s, ch = t.shape
    return t.reshape(b, GRID_W, s // GRID_W, ch).transpose(0, 2, 1, 3).reshape(b, s, ch)


def kernel(x, c, ctx, c_ctx, ada_w, ada_b, norm1_w, norm2_w, even_w_in, even_w_out, rwkv_mu, rwkv_w0, rwkv_w2, rwkv_a0, rwkv_a2, rwkv_g2, rwkv_k_k, rwkv_k_a, rwkv_r_k, rwkv_ln_w, rwkv_ln_b, hgrn_lower_bounds, hgrn_norm_w, odd_w_in, odd_w_out, ssd_conv_w, ssd_conv_b, ssd_dt_bias, ssd_a_log, ssd_d, ssd_norm_w, mlstm_conv_w, mlstm_conv_b, mlstm_i_bias, mlstm_f_bias, mlstm_norm_w, router_w, router_b, exp_w_gate, exp_b_gate, exp_w_up, exp_b_up, exp_w_down, exp_b_down, final_norm_w):
    lc = ctx.shape[1]
    lower_bounds = jnp.cumsum(jax.nn.softmax(hgrn_lower_bounds.astype(F32), axis=0), axis=0)
    moe = lambda l: (l, norm2_w[l], router_w[l], router_b[l], exp_w_gate, exp_b_gate, exp_w_up, exp_b_up, exp_w_down,
                     exp_b_down, final_norm_w)
    xa = jnp.concatenate([ctx, x], axis=1)
    modtab = _mod_table(c, c_ctx, ada_w[0], ada_b[0])
    xa = even_layer(xa, modtab, norm1_w[0], even_w_in[0], even_w_out[0], rwkv_mu[0], rwkv_w0[0], rwkv_w2[0],
                    rwkv_a0[0], rwkv_a2[0], rwkv_g2[0], rwkv_k_k[0], rwkv_k_a[0], rwkv_r_k[0], rwkv_ln_w[0],
                    rwkv_ln_b[0], lower_bounds[0], hgrn_norm_w[0], lc)
    xa = moe_layer(xa, modtab, *moe(0), lc, False)
    xa = jnp.concatenate([xa[:, :lc], _to_col_major(xa[:, lc:])], axis=1)
    modtab = _mod_table(c, c_ctx, ada_w[1], ada_b[1])
    xl = odd_layer(xa, modtab, norm1_w[1], odd_w_in[0], odd_w_out[0], ssd_conv_w[0], ssd_conv_b[0], ssd_dt_bias[0],
                   ssd_a_log[0], ssd_d[0], ssd_norm_w[0], mlstm_conv_w[0], mlstm_conv_b[0], mlstm_i_bias[0],
                   mlstm_f_bias[0], mlstm_norm_w[0], lc)
    xl = moe_layer(xl, modtab, *moe(1), 0, True)
    return _to_row_major(xl)
```

```python
import functools
import math

import jax
import jax.numpy as jnp
from jax import lax
from jax.experimental import pallas as pl
from jax.experimental.pallas import tpu as pltpu
from jax.experimental.pallas import tpu_sc as plsc

F32 = jnp.float32
BF16 = jnp.bfloat16
I32 = jnp.int32
U32 = jnp.uint32
HI = lax.Precision.HIGHEST

D_MODEL = 1024
GRID_W = 64
N_MOD = 6
EPS = 1e-6
RW_H, RW_N = 8, 64
RW_W = RW_H * RW_N
RW_RANK = 64
RW_GATE_RANK = 128
RW_GN_EPS = 64e-5
RW_PROJ = 3 * RW_W + 4 * RW_RANK + RW_GATE_RANK
RW_CHUNK = 64
RW_GROUP = 4
RW_PACK = 2
HG_H, HG_DK, HG_DV = 4, 128, 128
HG_W = HG_H * HG_DK
HG_CHUNK = 32
HG_PROJ = 5 * HG_W
SSD_H, SSD_P, SSD_G, SSD_N = 16, 64, 2, 128
SSD_INNER = SSD_H * SSD_P
SSD_CONV_CH = SSD_INNER + 2 * SSD_G * SSD_N
SSD_CHUNK = 64
ML_H, ML_DK, ML_DV = 4, 128, 256
ML_QK = ML_H * ML_DK
ML_VW = ML_H * ML_DV
ML_CHUNK = 64
ML_M_INIT = -1e30
CONV_K = 5
N_EXP = 32
TOP_K = 4
D_FF = 1024
SWIGLU_LIMIT = 7.0
SWIGLU_ALPHA = 1.702
MOE_BLOCK = 512

TOK_TILE = 256
SCAN_BLOCK = 256
HALO = 8
SMALL_W = 128
VMEM_LIMIT = 56 * 1024 * 1024
SC_CORES, SC_SUBCORES = 2, 16
SC_GATHER_ROWS = 32


def _cparams(sem):
    return pltpu.CompilerParams(dimension_semantics=sem, vmem_limit_bytes=VMEM_LIMIT)


def _mxu_operands(a, b, precision):
    if precision is None:
        return a.astype(BF16), b.astype(BF16)
    return a, b


def _dot(a, b, precision=None):
    a, b = _mxu_operands(a, b, precision)
    return jnp.dot(a, b, preferred_element_type=F32, precision=precision)


def _dot_nt(a, b, precision=None):
    a, b = _mxu_operands(a, b, precision)
    return lax.dot_general(a, b, (((1,), (1,)), ((), ())), preferred_element_type=F32, precision=precision)


def _dot_tn(a, b, precision=None):
    a, b = _mxu_operands(a, b, precision)
    return lax.dot_general(a, b, (((0,), (0,)), ((), ())), preferred_element_type=F32, precision=precision)


def _bf16_parts(x):
    hi = x.astype(BF16)
    rest = x - hi.astype(F32)
    mid = rest.astype(BF16)
    return hi, mid, (rest - mid.astype(F32)).astype(BF16)


def _dot_into_01(x, sel):
    sel = sel.astype(BF16)
    p0, p1, p2 = _bf16_parts(x)
    return _dot(p0, sel) + _dot(p1, sel) + _dot(p2, sel)


def _dot_01_into(sel, x):
    sel = sel.astype(BF16)
    p0, p1, p2 = _bf16_parts(x)
    return _dot(sel, p0) + _dot(sel, p1) + _dot(sel, p2)


def _pack_bf16_halves(x):
    n = x.shape[1] // 2
    bits = lax.bitcast_convert_type(x.astype(BF16).astype(F32), U32)
    return bits[:, :n] | (bits[:, n:] >> 16)


def _unpack_bf16_halves(w):
    hi = lax.bitcast_convert_type(w & jnp.uint32(0xFFFF0000), F32)
    lo = lax.bitcast_convert_type(w << 16, F32)
    return jnp.concatenate([hi, lo], axis=1)


def _sigmoid(x):
    return 1.0 / (1.0 + jnp.exp(-x))


def _silu(x):
    return x * _sigmoid(x)


def _softplus(x):
    return jnp.maximum(x, 0.0) + jnp.log(1.0 + jnp.exp(-jnp.abs(x)))


def _dir_masks(z, c):
    ti = lax.broadcasted_iota(I32, (c, c), 0)
    si = lax.broadcasted_iota(I32, (c, c), 1)
    d = (si - ti) * jnp.where(z == 0, 1, -1)
    return d <= 0, d < 0


def _time_block(z, n, ctx_blocks, n_blocks):
    rev = jnp.where(n < ctx_blocks, ctx_blocks - 1 - n, n_blocks - 1 - (n - ctx_blocks))
    return jnp.where(z == 0, n, rev)


def _mod_kernel(c_ref, w_ref, b_ref, o_ref):
    o_ref[...] = _dot(_silu(c_ref[...]), w_ref[...]) + b_ref[...]


def ada_mod(cc, w, b):
    rows, d = cc.shape
    n = w.shape[1]
    tn = 1536
    return pl.pallas_call(
        _mod_kernel,
        out_shape=jax.ShapeDtypeStruct((rows, n), F32),
        grid=(n // tn,),
        in_specs=[pl.BlockSpec((rows, d), lambda j: (0, 0)),
                  pl.BlockSpec((d, tn), lambda j: (0, j)),
                  pl.BlockSpec((1, tn), lambda j: (0, j))],
        out_specs=pl.BlockSpec((rows, tn), lambda j: (0, j)),
        compiler_params=_cparams(("arbitrary",)),
        name="ada_mod",
    )(cc, w, b.reshape(1, n))


def _mod_table(c, c_ctx, w, b):
    bsz = c.shape[0]
    rows = -(-(bsz + 1) // 8) * 8
    cc = jnp.zeros((rows, D_MODEL), F32).at[:bsz].set(c).at[bsz].set(c_ctx)
    m = ada_mod(cc, w, b)
    mod = m[:bsz].reshape(bsz, 1, N_MOD, D_MODEL)
    mod_c = jnp.broadcast_to(m[bsz].reshape(1, 1, N_MOD, D_MODEL), (bsz, 1, N_MOD, D_MODEL))
    return jnp.concatenate([mod_c, mod], axis=1)


def _norm_mod(x, nw, m, shift_idx):
    y = x * lax.rsqrt(jnp.mean(x * x, axis=-1, keepdims=True) + EPS) * nw
    return y * (1.0 + m[shift_idx + 1:shift_idx + 2]) + m[shift_idx:shift_idx + 1]


def _in_proj_kernel(x_ref, mod_ref, nw_ref, *rest, n_out):
    w_refs, o_refs = rest[:n_out], rest[n_out:]
    h = _norm_mod(x_ref[0], nw_ref[...], mod_ref[0, 0], 0).astype(BF16)
    for w_ref, o_ref in zip(w_refs, o_refs):
        o_ref[0] = _dot(h, w_ref[...])


def in_proj(x, modtab, nw, slabs, ctx_len):
    bsz, lt, d = x.shape
    tm = TOK_TILE
    ctx_tiles = ctx_len // tm
    in_specs = [pl.BlockSpec((1, tm, d), lambda b, i: (b, i, 0)),
                pl.BlockSpec((1, 1, N_MOD, d), lambda b, i: (b, (i >= ctx_tiles).astype(I32), 0, 0)),
                pl.BlockSpec((1, d), lambda b, i: (0, 0))]
    in_specs += [pl.BlockSpec(w.shape, lambda b, i: (0, 0)) for w in slabs]
    return pl.pallas_call(
        functools.partial(_in_proj_kernel, n_out=len(slabs)),
        out_shape=[jax.ShapeDtypeStruct((bsz, lt, w.shape[1]), F32) for w in slabs],
        grid=(bsz, lt // tm),
        in_specs=in_specs,
        out_specs=[pl.BlockSpec((1, tm, w.shape[1]), lambda b, i: (b, i, 0)) for w in slabs],
        compiler_params=_cparams(("parallel", "parallel")),
        name="in_proj",
    )(x, modtab, nw.reshape(1, d), *slabs)


def _halo_specs(tm, width, lt):
    per = tm // HALO
    last = lt // HALO - 1
    return [pl.BlockSpec((1, tm, width), lambda b, i: (b, i, 0)),
            pl.BlockSpec((1, HALO, width), lambda b, i: (b, jnp.maximum(i * per - 1, 0), 0)),
            pl.BlockSpec((1, HALO, width), lambda b, i: (b, jnp.minimum((i + 1) * per, last), 0))]


def _segment_edges(i, tm, ctx_len, lt):
    row0 = i * tm
    has_prev = jnp.logical_and(row0 != 0, row0 != ctx_len)
    has_next = jnp.logical_and(row0 + tm != ctx_len, row0 + tm != lt)
    return has_prev, has_next


def _rwkv_prep_kernel(p_ref, pp_ref, pn_ref, mu_ref, w0_ref, w2_ref, a0_ref, a2_ref, g2_ref, kk_ref, ka_ref,
                      rk_ref, bd_ref, r_o, v_o, kk_o, g_o, bv_o, lw_o, km_o, bb_o, *, tm, ctx_len, lt):
    has_prev, has_next = _segment_edges(pl.program_id(1), tm, ctx_len, lt)
    p = p_ref[0]
    prev_row = jnp.where(has_prev, pp_ref[0, HALO - 1:HALO, :], 0.0)
    next_row = jnp.where(has_next, pn_ref[0, 0:1, :], 0.0)
    rid = lax.broadcasted_iota(I32, (tm, 1), 0)
    up = jnp.where(rid == 0, prev_row, pltpu.roll(p, 1, 0))
    dn = jnp.where(rid == tm - 1, next_row, pltpu.roll(p, tm - 1, 0))
    p = p + (0.5 * (up + dn) - p) * mu_ref[...]
    w = RW_W
    r, k, v = p[:, 0:w], p[:, w:2 * w], p[:, 2 * w:3 * w]
    o = 3 * w
    wd = jnp.tanh(p[:, o:o + 2 * RW_RANK])
    ad = p[:, o + 2 * RW_RANK:o + 4 * RW_RANK]
    gd = p[:, o + 4 * RW_RANK:o + 4 * RW_RANK + RW_GATE_RANK]
    bd = bd_ref[...]
    kk = k * kk_ref[...]
    kk = kk * lax.rsqrt(_dot_into_01(kk * kk, bd) + 1e-12)
    r_o[0] = r
    v_o[0] = v
    kk_o[0] = kk
    g_o[0] = _dot(_sigmoid(gd), g2_ref[...])
    km_sum = jnp.zeros_like(k)
    for z in range(2):
        sl = slice(z * RW_RANK, (z + 1) * RW_RANK)
        w_pre = w0_ref[z:z + 1, :] + _dot(wd[:, sl], w2_ref[z])
        lw_o[z, 0] = -_sigmoid(w_pre) * math.exp(-0.5)
        a = _sigmoid(a0_ref[z:z + 1, :] + _dot(ad[:, sl], a2_ref[z]))
        km = k * (1.0 + (a - 1.0) * ka_ref[...])
        km_o[z, 0] = km
        bb_o[z, 0] = kk * a
        km_sum = km_sum + km
    bv_o[0] = _dot_into_01(r * km_sum * rk_ref[...], bd) * v


def _head_block_ones(width, head):
    i = jnp.arange(width) // head
    return (i[:, None] == i[None, :]).astype(F32)


def rwkv_prep(p, mu, w0, w2, a0, a2, g2, k_k, k_a, r_k, ctx_len):
    bsz, lt, width = p.shape
    tm = TOK_TILE
    w = RW_W
    full = lambda a: pl.BlockSpec(a.shape, lambda b, i: (0,) * a.ndim)
    params = [mu.reshape(1, width), w0, w2, a0, a2, g2, k_k.reshape(1, w), k_a.reshape(1, w), r_k.reshape(1, w),
              _head_block_ones(w, RW_N)]
    one = jax.ShapeDtypeStruct((bsz, lt, w), F32)
    two = jax.ShapeDtypeStruct((2, bsz, lt, w), F32)
    one_spec = pl.BlockSpec((1, tm, w), lambda b, i: (b, i, 0))
    two_spec = pl.BlockSpec((2, 1, tm, w), lambda b, i: (0, b, i, 0))
    return pl.pallas_call(
        functools.partial(_rwkv_prep_kernel, tm=tm, ctx_len=ctx_len, lt=lt),
        out_shape=[one] * 5 + [two] * 3,
        grid=(bsz, lt // tm),
        in_specs=_halo_specs(tm, width, lt) + [full(a) for a in params],
        out_specs=[one_spec] * 5 + [two_spec] * 3,
        compiler_params=_cparams(("parallel", "parallel")),
        name="rwkv_prep",
    )(p, p, p, *params)


def _unit_lower_inverses(n_mats, eye, c):
    xs = [eye - n for n in n_mats]
    ps = [_dot(n, n) for n in n_mats]
    span = 4
    while True:
        xs = [x + _dot(x, p) for x, p in zip(xs, ps)]
        if span >= c:
            return xs
        ps = [_dot(p, p) for p in ps]
        span *= 2


def _rwkv_scan_kernel(r_ref, v_ref, kk_ref, lw_ref, km_ref, bb_ref, o_ref, s_ref, *, tb, c):
    z = pl.program_id(0)
    nsub = tb // c

    @pl.when(pl.program_id(2) == 0)
    def _():
        s_ref[...] = jnp.zeros_like(s_ref)

    incl, strict = _dir_masks(z, c)
    incl_f, strict_f = incl.astype(F32), strict.astype(F32)
    eye = incl_f - strict_f
    eye_pk = jnp.concatenate([eye] * RW_PACK, axis=1)
    pack_lane = lax.broadcasted_iota(I32, (c, RW_PACK * RW_N), 1) // RW_N

    heads = [slice(h * RW_N, (h + 1) * RW_N) for h in range(RW_H)]

    def group(g, s_old):
        subs = []
        for i in range(RW_GROUP):
            j = g * RW_GROUP + i
            rows = pl.ds(pl.multiple_of(jnp.where(z == 0, j, nsub - 1 - j) * c, c), c)
            r, v, kk = r_ref[0, rows, :], v_ref[0, rows, :], kk_ref[0, rows, :]
            lw, km, bb = lw_ref[0, 0, rows, :], km_ref[0, 0, rows, :], bb_ref[0, 0, rows, :]
            cum = _dot_01_into(incl_f, lw)
            tot = jnp.sum(lw, axis=0, keepdims=True)
            e_neg = jnp.exp(-cum)
            e_end = jnp.exp(tot - cum)
            a_t = kk * jnp.exp(cum - lw)
            r_t = r * jnp.exp(cum)
            b_t, k_t = bb * e_neg, km * e_neg
            b_p, k_p = bb * e_end, km * e_end
            subs.append(dict(
                rows=rows, e_tot=jnp.exp(tot), a_all=a_t, v_all=v, r=[r_t[:, hs] for hs in heads],
                ar=[jnp.concatenate([a_t[:, hs], r_t[:, hs]], axis=0) for hs in heads],
                bk=[jnp.concatenate([b_t[:, hs], k_t[:, hs]], axis=0) for hs in heads],
                bkp=[jnp.concatenate([b_p[:, hs], k_p[:, hs]], axis=0) for hs in heads],
                v=[v[:, hs] for hs in heads]))
        flat = [(i, h) for i in range(RW_GROUP) for h in range(RW_H)]
        sc = [_dot_nt(subs[i]["ar"][h], subs[i]["bk"][h]) for i, h in flat]
        n_rbk = [jnp.concatenate([x[c:, :c] * incl_f, x[c:, c:] * incl_f], axis=1) for x in sc]
        packs = [(i, p) for i in range(RW_GROUP) for p in range(RW_H // RW_PACK)]

        def side_by_side(mats):
            return jnp.concatenate(mats, axis=1)

        def block_diag(packed):
            return jnp.concatenate([jnp.where(pack_lane == r, packed, 0.0) for r in range(RW_PACK)], axis=0)

        def pack_of(per_head, i, p):
            return side_by_side([per_head[i * RW_H + p * RW_PACK + r] for r in range(RW_PACK)])

        lanes = lambda p: slice(p * RW_PACK * RW_N, (p + 1) * RW_PACK * RW_N)
        n_pk = [pack_of([x[:c, :c] * strict_f for x in sc], i, p) for i, p in packs]
        mak_pk = [pack_of([x[:c, c:] * strict_f for x in sc], i, p) for i, p in packs]
        makv_pk = [_dot(m, block_diag(subs[i]["v_all"][:, lanes(p)])) for m, (i, p) in zip(mak_pk, packs)]
        xs = [eye_pk - n for n in n_pk]
        ps = [_dot(n, block_diag(n)) for n in n_pk]
        span = 4
        while True:
            bds = [block_diag(pp) for pp in ps]
            xs = [x + _dot(x, b) for x, b in zip(xs, bds)]
            if span >= c:
                break
            ps = [_dot(pp, b) for pp, b in zip(ps, bds)]
            span *= 2
        w1_pk = [_dot(x, block_diag(subs[i]["a_all"][:, lanes(p)])) for x, (i, p) in zip(xs, packs)]
        w2_pk = [_dot(x, block_diag(m)) for x, m in zip(xs, makv_pk)]

        def head_of(packed_list, i, h):
            p, r = divmod(h, RW_PACK)
            return packed_list[i * (RW_H // RW_PACK) + p][:, r * RW_N:(r + 1) * RW_N]

        w1r = [jnp.concatenate([head_of(w1_pk, i, h), subs[i]["r"][h]], axis=0) for i, h in flat]
        w2 = [head_of(w2_pk, i, h) for i, h in flat]
        state = list(s_old)
        for i in range(RW_GROUP):
            sub = subs[i]
            base = i * RW_H
            ws = [_dot_nt(w1r[base + h], state[h]) for h in range(RW_H)]
            uv = [jnp.concatenate([-(ws[h][:c] + w2[base + h]), sub["v"][h]], axis=0) for h in range(RW_H)]
            outs = [ws[h][c:] + _dot(n_rbk[base + h], uv[h]) for h in range(RW_H)]
            o_ref[0, 0, sub["rows"], :] = jnp.concatenate(outs, axis=1)
            state = [state[h] * sub["e_tot"][:, heads[h]] + _dot_tn(uv[h], sub["bkp"][h]) for h in range(RW_H)]
        return tuple(state)

    s_fin = lax.fori_loop(0, nsub // RW_GROUP, group, tuple(s_ref[h] for h in range(RW_H)))
    for h in range(RW_H):
        s_ref[h] = s_fin[h]


def rwkv_scan(r, v, kk, lw, km, bb, ctx_len):
    bsz, lt, w = r.shape
    tb = SCAN_BLOCK
    nb, cb = lt // tb, ctx_len // tb
    tmap = lambda z, b, n: _time_block(z, n, cb, nb)
    one = pl.BlockSpec((1, tb, w), lambda z, b, n: (b, tmap(z, b, n), 0))
    two = pl.BlockSpec((1, 1, tb, w), lambda z, b, n: (z, b, tmap(z, b, n), 0))
    return pl.pallas_call(
        functools.partial(_rwkv_scan_kernel, tb=tb, c=RW_CHUNK),
        out_shape=jax.ShapeDtypeStruct((2, bsz, lt, w), F32),
        grid=(2, bsz, nb),
        in_specs=[one, one, one, two, two, two],
        out_specs=two,
        scratch_shapes=[pltpu.VMEM((RW_H, RW_N, RW_N), F32)],
        compiler_params=_cparams(("parallel", "parallel", "arbitrary")),
        name="rwkv_scan",
    )(r, v, kk, lw, km, bb)


def _gla_kernel(q_ref, f_ref, i_ref, lb_ref, o_ref, s_ref, *, tb, c):
    z = pl.program_id(0)
    nsub = tb // c

    @pl.when(pl.program_id(2) == 0)
    def _():
        s_ref[...] = jnp.zeros_like(s_ref)

    lb = lb_ref[...]
    heads = [slice(h * HG_DK, (h + 1) * HG_DK) for h in range(HG_H)]
    ti = lax.broadcasted_iota(I32, (tb, tb), 0)
    si = lax.broadcasted_iota(I32, (tb, tb), 1)
    same_chunk = (ti // c) == (si // c)

    def run(zs):
        incl = jnp.logical_and(same_chunk, (si <= ti) if zs == 0 else (si >= ti))
        q = _silu(q_ref[0]) * HG_DK ** -0.5
        f = f_ref[0]
        v = i_ref[0]
        sig_f = _sigmoid(f)
        log_f = jnp.log(lb + (1.0 - lb) * sig_f)
        k = (1.0 - lb) * (1.0 - sig_f)
        cum = _dot_01_into(incl.astype(F32), log_f)
        last = c - 1 if zs == 0 else 0
        tot = jnp.concatenate([jnp.broadcast_to(cum[j * c + last:j * c + last + 1, :], (c, cum.shape[1]))
                               for j in range(nsub)], axis=0)
        q_in = q * jnp.exp(cum)
        k_in = k * jnp.exp(-cum)
        k_end = k * jnp.exp(tot - cum)
        e_tot = jnp.exp(tot)
        intra = [_dot(jnp.where(incl, _dot_nt(q_in[:, hs], k_in[:, hs]), 0.0), v[:, hs]) for hs in heads]
        order = range(nsub) if zs == 0 else range(nsub - 1, -1, -1)
        grow = {(j, h): _dot_tn(v[j * c:(j + 1) * c, hs], k_end[j * c:(j + 1) * c, hs])
                for j in order for h, hs in enumerate(heads)}
        state = [s_ref[h] for h in range(HG_H)]
        inter = {}
        for j in order:
            rows = slice(j * c, (j + 1) * c)
            for h, hs in enumerate(heads):
                inter[(j, h)] = _dot_nt(q_in[rows, hs], state[h])
                state[h] = state[h] * e_tot[j * c:j * c + 1, hs] + grow[(j, h)]
        for h, hs in enumerate(heads):
            o_ref[0, 0, :, hs] = intra[h] + jnp.concatenate([inter[(j, h)] for j in range(nsub)], axis=0)
            s_ref[h] = state[h]

    for zs in range(2):
        pl.when(z == zs)(functools.partial(run, zs))


def gla_scan(ph, lb, ctx_len):
    bsz, lt, _ = ph.shape
    w = HG_W
    tb = SCAN_BLOCK
    nb, cb = lt // tb, ctx_len // tb
    tmap = lambda z, n: _time_block(z, n, cb, nb)
    return pl.pallas_call(
        functools.partial(_gla_kernel, tb=tb, c=HG_CHUNK),
        out_shape=jax.ShapeDtypeStruct((2, bsz, lt, w), F32),
        grid=(2, bsz, nb),
        in_specs=[pl.BlockSpec((1, tb, w), lambda z, b, n: (b, tmap(z, n), 0)),
                  pl.BlockSpec((1, tb, w), lambda z, b, n: (b, tmap(z, n), 1 + z)),
                  pl.BlockSpec((1, tb, w), lambda z, b, n: (b, tmap(z, n), 3)),
                  pl.BlockSpec((1, w), lambda z, b, n: (0, 0))],
        out_specs=pl.BlockSpec((1, 1, tb, w), lambda z, b, n: (z, b, tmap(z, n), 0)),
        scratch_shapes=[pltpu.VMEM((HG_H, HG_DV, HG_DK), F32)],
        compiler_params=_cparams(("parallel", "parallel", "arbitrary")),
        name="gla_scan",
    )(ph, ph, ph, lb.reshape(1, w))


def _group_rms(y, groups, gain):
    width = y.shape[-1] // groups
    parts = []
    for g in range(groups):
        yg = y[:, g * width:(g + 1) * width]
        parts.append(yg * lax.rsqrt(jnp.mean(yg * yg, axis=-1, keepdims=True) + EPS))
    return jnp.concatenate(parts, axis=1) * gain


def _even_out_kernel(x_ref, mod_ref, or_ref, bv_ref, g_ref, og_ref, gate_ref, lnw_ref, lnb_ref, nw_ref, bd_ref,
                     w_ref, o_ref):
    bd = bd_ref[...]
    o = or_ref[0, 0] + or_ref[1, 0]
    mu = _dot_into_01(o, bd) * (1.0 / RW_N)
    oc = o - mu
    var = _dot_into_01(oc * oc, bd) * (1.0 / RW_N)
    yr = (oc * lax.rsqrt(var + RW_GN_EPS) * lnw_ref[...] + lnb_ref[...] + bv_ref[0]) * g_ref[0]
    og = og_ref[0, 0] + og_ref[1, 0]
    yh = _group_rms(og, HG_H, nw_ref[...]) * _silu(gate_ref[0])
    y = jnp.concatenate([yr, yh], axis=1).astype(BF16)
    o_ref[0] = x_ref[0] + mod_ref[0, 0][2:3] * _dot(y, w_ref[...])


def even_out(x, modtab, o_r, bv, g, o_g, ph, ln_w, ln_b, norm_w, w_out, ctx_len):
    bsz, lt, d = x.shape
    tm = TOK_TILE
    ctx_tiles = ctx_len // tm
    w = RW_W
    tok = lambda width, col=0: pl.BlockSpec((1, tm, width), lambda b, i: (b, i, col))
    two = pl.BlockSpec((2, 1, tm, w), lambda b, i: (0, b, i, 0))
    row = lambda width: pl.BlockSpec((1, width), lambda b, i: (0, 0))
    return pl.pallas_call(
        _even_out_kernel,
        out_shape=jax.ShapeDtypeStruct((bsz, lt, d), F32),
        grid=(bsz, lt // tm),
        in_specs=[tok(d),
                  pl.BlockSpec((1, 1, N_MOD, d), lambda b, i: (b, (i >= ctx_tiles).astype(I32), 0, 0)),
                  two, tok(w), tok(w), two, tok(HG_W, 4), row(w), row(w), row(HG_W),
                  pl.BlockSpec((w, w), lambda b, i: (0, 0)),
                  pl.BlockSpec(w_out.shape, lambda b, i: (0, 0))],
        out_specs=tok(d),
        compiler_params=_cparams(("parallel", "parallel")),
        name="even_out",
    )(x, modtab, o_r, bv, g, o_g, ph, ln_w.reshape(1, w), ln_b.reshape(1, w), norm_w.reshape(1, HG_W),
      _head_block_ones(w, RW_N), w_out)


def _conv_silu(x_ref, xp_ref, xn_ref, w_ref, b_ref, has_prev, has_next, tm):
    prev = jnp.where(has_prev, xp_ref[0], 0.0)
    nxt = jnp.where(has_next, xn_ref[0], 0.0)
    ext = jnp.concatenate([prev, x_ref[0], nxt], axis=0)
    rows = tm + 2 * HALO
    pad = CONV_K // 2
    acc = b_ref[...] + jnp.zeros((tm, ext.shape[1]), F32)
    for j in range(CONV_K):
        d = j - pad
        sh = ext if d == 0 else pltpu.roll(ext, (-d) % rows, 0)
        acc = acc + sh[HALO:HALO + tm] * w_ref[j:j + 1, :]
    return _silu(acc)


def _odd_prep_kernel(xs_ref, xsp_ref, xsn_ref, xm_ref, xmp_ref, xmn_ref, ws_ref, bs_ref, wm_ref, bm_ref,
                     os_ref, om_ref, *, tm, ctx_len, lt):
    has_prev, has_next = _segment_edges(pl.program_id(1), tm, ctx_len, lt)
    os_ref[0] = _conv_silu(xs_ref, xsp_ref, xsn_ref, ws_ref, bs_ref, has_prev, has_next, tm)
    om_ref[0] = _conv_silu(xm_ref, xmp_ref, xmn_ref, wm_ref, bm_ref, has_prev, has_next, tm)


def odd_prep(xbc, qk, ws, bs, wm, bm, ctx_len):
    bsz, lt, cs = xbc.shape
    cm = wm.shape[1]
    tm = TOK_TILE
    full = lambda a: pl.BlockSpec(a.shape, lambda b, i: (0,) * a.ndim)
    params = [ws, bs.reshape(1, cs), wm, bm.reshape(1, cm)]
    return pl.pallas_call(
        functools.partial(_odd_prep_kernel, tm=tm, ctx_len=ctx_len, lt=lt),
        out_shape=[jax.ShapeDtypeStruct((bsz, lt, cs), F32), jax.ShapeDtypeStruct((bsz, lt, cm), F32)],
        grid=(bsz, lt // tm),
        in_specs=_halo_specs(tm, cs, lt) + _halo_specs(tm, cm, lt) + [full(a) for a in params],
        out_specs=[pl.BlockSpec((1, tm, cs), lambda b, i: (b, i, 0)),
                   pl.BlockSpec((1, tm, cm), lambda b, i: (b, i, 0))],
        compiler_params=_cparams(("parallel", "parallel")),
        name="odd_prep",
    )(xbc, xbc, xbc, qk, qk, qk, *params)


def _ssd_kernel(x_ref, b_ref, c_ref, sm_ref, dtb_ref, alog_ref, ex_ref, o_ref, s_ref, *, tb, c):
    z = pl.program_id(0)
    nsub = tb // c
    hg = SSD_H // SSD_G

    @pl.when(pl.program_id(2) == 0)
    def _():
        s_ref[...] = jnp.zeros_like(s_ref)

    incl, _ = _dir_masks(z, c)
    incl_f = incl.astype(F32)
    ex = ex_ref[...]

    gw = hg * SSD_P

    def local(j):
        jj = jnp.where(z == 0, j, nsub - 1 - j)
        rows = pl.ds(pl.multiple_of(jj * c, c), c)
        x = x_ref[0, rows, :]
        bm = b_ref[0, rows, :]
        cm = c_ref[0, rows, :]
        dt = _softplus(sm_ref[0, rows, :] + dtb_ref[0])
        la = -dt * jnp.exp(alog_ref[0])
        a_cum = _dot_01_into(incl_f, la)
        a_tot = jnp.sum(la, axis=0, keepdims=True)
        a_cum_t = a_cum.T
        e_tot_col = jnp.exp(jnp.where(z == 0, a_cum_t[:, c - 1:c], a_cum_t[:, 0:1]))
        wide = _dot(jnp.concatenate([dt, jnp.exp(a_cum), jnp.exp(a_tot - a_cum)], axis=0).astype(BF16),
                    ex.astype(BF16))
        xdt = x * wide[:c]
        din, dend = wide[c:2 * c], wide[2 * c:]
        xd_end = xdt * dend
        bgs = [bm[:, g * SSD_N:(g + 1) * SSD_N] for g in range(SSD_G)]
        cgs = [cm[:, g * SSD_N:(g + 1) * SSD_N] for g in range(SSD_G)]
        cb = [_dot_nt(cg, bg) for cg, bg in zip(cgs, bgs)]
        grow = [_dot_tn(xd_end[:, g * gw:(g + 1) * gw], bg) for g, bg in enumerate(bgs)]
        intra = []
        for h in range(SSD_H):
            seg = a_cum[:, h:h + 1] - a_cum_t[h:h + 1, :]
            wmat = cb[h // hg] * jnp.where(incl, jnp.exp(seg), 0.0)
            intra.append(_dot(wmat, xdt[:, h * SSD_P:(h + 1) * SSD_P]))
        decay = [jnp.concatenate([jnp.broadcast_to(e_tot_col[h:h + 1, :], (SSD_P, 1))
                                  for h in range(g * hg, (g + 1) * hg)], axis=0) for g in range(SSD_G)]
        return dict(rows=rows, cgs=cgs, din=din, intra=jnp.concatenate(intra, axis=1), grow=grow, decay=decay)

    subs = [local(j) for j in range(nsub)]
    state = [s_ref[g] for g in range(SSD_G)]
    for sub in subs:
        inter = [_dot_nt(cg, s) for cg, s in zip(sub["cgs"], state)]
        o_ref[0, 0, sub["rows"], :] = sub["intra"] + sub["din"] * jnp.concatenate(inter, axis=1)
        state = [s * d + g for s, d, g in zip(state, sub["decay"], sub["grow"])]
    for g in range(SSD_G):
        s_ref[g] = state[g]


def _head_expand(heads, width, rows):
    h = jnp.arange(rows)[:, None]
    col = jnp.arange(heads * width)[None, :] // width
    return (h == col).astype(F32)


def ssd_scan(xbc, small, dtb, alog, ctx_len):
    bsz, lt, _ = xbc.shape
    tb = SCAN_BLOCK
    nb, cb = lt // tb, ctx_len // tb
    tmap = lambda z, n: _time_block(z, n, cb, nb)
    gw = SSD_G * SSD_N
    return pl.pallas_call(
        functools.partial(_ssd_kernel, tb=tb, c=SSD_CHUNK),
        out_shape=jax.ShapeDtypeStruct((2, bsz, lt, SSD_INNER), F32),
        grid=(2, bsz, nb),
        in_specs=[pl.BlockSpec((1, tb, SSD_INNER), lambda z, b, n: (b, tmap(z, n), 0)),
                  pl.BlockSpec((1, tb, gw), lambda z, b, n: (b, tmap(z, n), SSD_INNER // gw)),
                  pl.BlockSpec((1, tb, gw), lambda z, b, n: (b, tmap(z, n), SSD_INNER // gw + 1)),
                  pl.BlockSpec((1, tb, SMALL_W), lambda z, b, n: (b, tmap(z, n), z)),
                  pl.BlockSpec((1, 1, SMALL_W), lambda z, b, n: (z, 0, 0)),
                  pl.BlockSpec((1, 1, SMALL_W), lambda z, b, n: (z, 0, 0)),
                  pl.BlockSpec((SMALL_W, SSD_INNER), lambda z, b, n: (0, 0))],
        out_specs=pl.BlockSpec((1, 1, tb, SSD_INNER), lambda z, b, n: (z, b, tmap(z, n), 0)),
        scratch_shapes=[pltpu.VMEM((SSD_G, SSD_H // SSD_G * SSD_P, SSD_N), F32)],
        compiler_params=_cparams(("parallel", "parallel", "arbitrary")),
        name="ssd_scan",
    )(xbc, xbc, xbc, small, dtb, alog, _head_expand(SSD_H, SSD_P, SMALL_W))


ML_I_LANE = SSD_H
ML_F_LANE = SSD_H + ML_H


def _mlstm_kernel(q_ref, k_ref, v_ref, sm_ref, gb_ref, o_ref, cs_ref, ns_ref, m_ref, *, tb, c):
    z = pl.program_id(0)
    nsub = tb // c

    @pl.when(pl.program_id(2) == 0)
    def _():
        cs_ref[...] = jnp.zeros_like(cs_ref)
        ns_ref[...] = jnp.zeros_like(ns_ref)
        m_ref[...] = jnp.full(m_ref.shape, ML_M_INIT, F32)

    incl, _ = _dir_masks(z, c)
    incl_f = incl.astype(F32)
    lane = lax.broadcasted_iota(I32, (1, SMALL_W), 1)
    is_f = jnp.logical_and(lane >= ML_F_LANE, lane < ML_F_LANE + ML_H)

    def local(j):
        rows = pl.ds(pl.multiple_of(jnp.where(z == 0, j, nsub - 1 - j) * c, c), c)
        q = q_ref[0, rows, :]
        k = k_ref[0, rows, :] * ML_DK ** -0.5
        v = v_ref[0, rows, :]
        pre = sm_ref[0, rows, :] + gb_ref[0]
        gates = jnp.where(is_f, -_softplus(-pre), pre)
        f_cum = _dot(incl_f, gates, HI)
        f_tot = jnp.sum(gates, axis=0, keepdims=True)
        f_cum_t = f_cum.T
        gates_t = gates.T
        per_head = []
        for h in range(ML_H):
            li, lf = ML_I_LANE + h, ML_F_LANE + h
            qh = q[:, h * ML_DK:(h + 1) * ML_DK]
            kh = k[:, h * ML_DK:(h + 1) * ML_DK]
            vh = v[:, h * ML_DV:(h + 1) * ML_DV]
            f_col = f_cum[:, lf:lf + 1]
            f_last = f_tot[:, lf:lf + 1]
            logw_end = f_last - f_col + gates[:, li:li + 1]
            m_end = jnp.max(logw_end, axis=0, keepdims=True)
            kw = kh * jnp.exp(logw_end - m_end)
            logw = jnp.where(incl, f_col - f_cum_t[lf:lf + 1, :] + gates_t[li:li + 1, :], -jnp.inf)
            m_loc = jnp.max(logw, axis=1, keepdims=True)
            scores = _dot_nt(qh, kh) * jnp.exp(logw - m_loc)
            per_head.append(dict(
                q=qh, f_col=f_col, f_last=f_last, m_end=m_end, m_loc=m_loc, num=_dot(scores, vh),
                den=jnp.sum(scores, axis=1, keepdims=True), kv=_dot_tn(kw, vh),
                ksum=jnp.sum(kw, axis=0, keepdims=True)))
        return rows, per_head

    subs = [local(j) for j in range(nsub)]
    cs = [cs_ref[h] for h in range(ML_H)]
    ns = [ns_ref[h] for h in range(ML_H)]
    m = [m_ref[h] for h in range(ML_H)]
    for rows, per_head in subs:
        outs = []
        for h, p in enumerate(per_head):
            m_t = jnp.maximum(p["m_loc"], p["f_col"] + m[h])
            w_loc = jnp.exp(p["m_loc"] - m_t)
            w_inter = jnp.exp(p["f_col"] + m[h] - m_t)
            num = p["num"] * w_loc + w_inter * _dot(p["q"], cs[h])
            den = p["den"] * w_loc + w_inter * jnp.sum(p["q"] * ns[h], axis=1, keepdims=True)
            outs.append(num / jnp.maximum(jnp.abs(den), jnp.exp(-m_t)))
            m_new = jnp.maximum(p["f_last"] + m[h], p["m_end"])
            s_keep = jnp.exp(p["f_last"] + m[h] - m_new)
            s_loc = jnp.exp(p["m_end"] - m_new)
            cs[h] = s_keep * cs[h] + s_loc * p["kv"]
            ns[h] = s_keep * ns[h] + s_loc * p["ksum"]
            m[h] = m_new
        o_ref[0, 0, rows, :] = jnp.concatenate(outs, axis=1)
    for h in range(ML_H):
        cs_ref[h] = cs[h]
        ns_ref[h] = ns[h]
        m_ref[h] = m[h]


def mlstm_scan(qk, pm, small, gate_bias, ctx_len):
    bsz, lt, _ = qk.shape
    tb = SCAN_BLOCK
    nb, cb = lt // tb, ctx_len // tb
    tmap = lambda z, n: _time_block(z, n, cb, nb)
    return pl.pallas_call(
        functools.partial(_mlstm_kernel, tb=tb, c=ML_CHUNK),
        out_shape=jax.ShapeDtypeStruct((2, bsz, lt, ML_VW), F32),
        grid=(2, bsz, nb),
        in_specs=[pl.BlockSpec((1, tb, ML_QK), lambda z, b, n: (b, tmap(z, n), 0)),
                  pl.BlockSpec((1, tb, ML_QK), lambda z, b, n: (b, tmap(z, n), 1)),
                  pl.BlockSpec((1, tb, ML_VW), lambda z, b, n: (b, tmap(z, n), 1)),
                  pl.BlockSpec((1, tb, SMALL_W), lambda z, b, n: (b, tmap(z, n), z)),
                  pl.BlockSpec((1, 1, SMALL_W), lambda z, b, n: (z, 0, 0))],
        out_specs=pl.BlockSpec((1, 1, tb, ML_VW), lambda z, b, n: (z, b, tmap(z, n), 0)),
        scratch_shapes=[pltpu.VMEM((ML_H, ML_DK, ML_DV), F32), pltpu.VMEM((ML_H, 1, ML_DK), F32),
                        pltpu.VMEM((ML_H, 1, 1), F32)],
        compiler_params=_cparams(("parallel", "parallel", "arbitrary")),
        name="mlstm_scan",
    )(qk, qk, pm, small, gate_bias)


def _odd_out_kernel(x_ref, mod_ref, ys_ref, xc_ref, zg_ref, ym_ref, og_ref, dsk_ref, nws_ref, nwm_ref, w_ref, o_ref):
    y = (ys_ref[0, 0] + ys_ref[1, 0] + dsk_ref[...] * xc_ref[0]) * _silu(zg_ref[0])
    ys = _group_rms(y, SSD_G, nws_ref[...])
    hm = ym_ref[0, 0] + ym_ref[1, 0]
    ym = _group_rms(hm, ML_H, nwm_ref[...]) * _sigmoid(og_ref[0])
    yy = jnp.concatenate([ys, ym], axis=1).astype(BF16)
    o_ref[0] = x_ref[0] + mod_ref[0, 0][2:3] * _dot(yy, w_ref[...])


def odd_out(x, modtab, y_s, xbc, ps, y_m, pm, d_skip, nw_s, nw_m, w_out, ctx_len):
    bsz, lt, d = x.shape
    tm = TOK_TILE
    ct = ctx_len // tm
    seq = lt - ctx_len
    tok = lambda width, col=0: pl.BlockSpec((1, tm, width), lambda b, i: (b, i + ct, col))
    two = lambda width: pl.BlockSpec((2, 1, tm, width), lambda b, i: (0, b, i + ct, 0))
    row = lambda width: pl.BlockSpec((1, width), lambda b, i: (0, 0))
    return pl.pallas_call(
        _odd_out_kernel,
        out_shape=jax.ShapeDtypeStruct((bsz, seq, d), F32),
        grid=(bsz, seq // tm),
        in_specs=[tok(d), pl.BlockSpec((1, 1, N_MOD, d), lambda b, i: (b, 1, 0, 0)),
                  two(SSD_INNER), tok(SSD_INNER), tok(SSD_INNER), two(ML_VW), tok(ML_VW, 2),
                  row(SSD_INNER), row(SSD_INNER), row(ML_VW),
                  pl.BlockSpec(w_out.shape, lambda b, i: (0, 0))],
        out_specs=pl.BlockSpec((1, tm, d), lambda b, i: (b, i, 0)),
        compiler_params=_cparams(("parallel", "parallel")),
        name="odd_out",
    )(x, modtab, y_s, xbc, ps, y_m, pm, d_skip.reshape(1, SSD_INNER), nw_s.reshape(1, SSD_INNER),
      nw_m.reshape(1, ML_VW), w_out)


def _router_kernel(x_ref, mod_ref, nw_ref, rw_ref, rb_ref, h_ref, idx_ref, prob_ref, rank_ref, cnt_ref, carry_ref,
                   *, tm):
    first = jnp.logical_and(pl.program_id(0) == 0, pl.program_id(1) == 0)

    @pl.when(first)
    def _():
        carry_ref[...] = jnp.zeros_like(carry_ref)

    h = _norm_mod(x_ref[0], nw_ref[...], mod_ref[0, 0], 3)
    h_ref[0] = _pack_bf16_halves(h)
    logits = _dot(h, rw_ref[...], HI) + rb_ref[...]
    lane = lax.broadcasted_iota(I32, (tm, N_EXP), 1).astype(F32)
    work = logits
    sel = jnp.zeros((tm, N_EXP), F32)
    vals, idxs, hots = [], [], []
    for _ in range(TOP_K):
        m = jnp.max(work, axis=1, keepdims=True)
        idx = jnp.min(jnp.where(work == m, lane, float(N_EXP)), axis=1, keepdims=True)
        hot = lane == idx
        vals.append(m)
        idxs.append(idx)
        hots.append(hot.astype(F32))
        sel = sel + hots[-1]
        work = jnp.where(hot, -jnp.inf, work)
    ex = [jnp.exp(v - vals[0]) for v in vals]
    tot = ex[0] + ex[1] + ex[2] + ex[3]
    ti = lax.broadcasted_iota(I32, (tm, tm), 0)
    si = lax.broadcasted_iota(I32, (tm, tm), 1)
    before = _dot((si < ti).astype(BF16), sel.astype(BF16)) + carry_ref[...]
    ranks = [jnp.sum(before * hot, axis=1, keepdims=True) for hot in hots]
    idx_ref[0] = jnp.concatenate(idxs, axis=1).astype(I32)
    prob_ref[0] = jnp.concatenate([e / tot for e in ex], axis=1)
    rank_ref[0] = jnp.concatenate(ranks, axis=1).astype(I32)
    carry_ref[...] = carry_ref[...] + jnp.sum(sel, axis=0, keepdims=True)
    cnt_ref[...] = carry_ref[...]


def router(x, modtab, nw, rw, rb, ctx_len):
    bsz, lt, d = x.shape
    tm = TOK_TILE
    ctx_tiles = ctx_len // tm
    tok = lambda width, dt: (jax.ShapeDtypeStruct((bsz, lt, width), dt),
                             pl.BlockSpec((1, tm, width), lambda b, i: (b, i, 0)))
    outs = [tok(d // 2, U32), tok(TOP_K, I32), tok(TOP_K, F32), tok(TOP_K, I32),
            (jax.ShapeDtypeStruct((1, N_EXP), F32), pl.BlockSpec((1, N_EXP), lambda b, i: (0, 0)))]
    return pl.pallas_call(
        functools.partial(_router_kernel, tm=tm),
        out_shape=[o[0] for o in outs],
        grid=(bsz, lt // tm),
        in_specs=[pl.BlockSpec((1, tm, d), lambda b, i: (b, i, 0)),
                  pl.BlockSpec((1, 1, N_MOD, d), lambda b, i: (b, (i >= ctx_tiles).astype(I32), 0, 0)),
                  pl.BlockSpec((1, d), lambda b, i: (0, 0)),
                  pl.BlockSpec((d, N_EXP), lambda b, i: (0, 0)),
                  pl.BlockSpec((1, N_EXP), lambda b, i: (0, 0))],
        out_specs=[o[1] for o in outs],
        scratch_shapes=[pltpu.VMEM((1, N_EXP), F32)],
        compiler_params=_cparams(("arbitrary", "arbitrary")),
        name="router",
    )(x, modtab, nw.reshape(1, d), rw, rb.reshape(1, N_EXP))


def sc_gather_rows(table, idx):
    m = idx.shape[0]
    d = table.shape[1]
    workers = SC_CORES * SC_SUBCORES
    per_w, n_chunks = m // workers, m // workers // SC_GATHER_ROWS
    assert per_w * workers == m and n_chunks * SC_GATHER_ROWS == per_w and n_chunks % 2 == 0
    w = SC_GATHER_ROWS
    mesh = plsc.VectorSubcoreMesh(core_axis_name="c", subcore_axis_name="s")

    @functools.partial(
        pl.kernel, mesh=mesh, out_type=jax.ShapeDtypeStruct((m, d), table.dtype),
        scratch_types=[pltpu.VMEM((per_w,), I32), pltpu.VMEM((2, w, d), table.dtype),
                       pltpu.SemaphoreType.DMA((2,)), pltpu.SemaphoreType.DMA((2,))])
    def gather_kernel(table_hbm, idx_hbm, out_hbm, idx_v, rows_v, gsem, wsem):
        wid = lax.axis_index("s") * SC_CORES + lax.axis_index("c")
        base = pl.multiple_of(wid * per_w, 8)
        pltpu.sync_copy(idx_hbm.at[pl.ds(base, per_w)], idx_v)

        def gather(g, b):
            rows = idx_v.at[pl.ds(pl.multiple_of(g * w, 8), w)]
            return pltpu.make_async_copy(table_hbm.at[rows], rows_v.at[b], gsem.at[b])

        def write(g, b):
            dst = out_hbm.at[pl.ds(pl.multiple_of(base + g * w, 8), w)]
            return pltpu.make_async_copy(rows_v.at[b], dst, wsem.at[b])

        gather(0, 0).start()

        @pl.loop(0, n_chunks, step=2)
        def _(g):
            for b in (0, 1):
                cur = g + b
                gather(cur, b).wait()
                write(cur, b).start()

                @pl.when(cur >= 1)
                def _():
                    write(cur - 1, 1 - b).wait()

                @pl.when(cur + 1 < n_chunks)
                def _():
                    gather(cur + 1, 1 - b).start()

        write(n_chunks - 1, 1).wait()

    return gather_kernel(table, idx)


def sc_scatter_rows(src, idx, n_out):
    k_dup, t = idx.shape
    d = src.shape[1]
    workers = SC_CORES * SC_SUBCORES
    w = SC_GATHER_ROWS
    per_w, n_chunks = t // workers, t // workers // w
    assert per_w * workers == t and n_chunks * w == per_w and n_chunks % 2 == 0
    mesh = plsc.VectorSubcoreMesh(core_axis_name="c", subcore_axis_name="s")

    @functools.partial(
        pl.kernel, mesh=mesh, out_type=jax.ShapeDtypeStruct((n_out, d), src.dtype),
        scratch_types=[pltpu.VMEM((k_dup, n_chunks, w), I32), pltpu.VMEM((2, w, d), src.dtype),
                       pltpu.SemaphoreType.DMA((2,)), pltpu.SemaphoreType.DMA((2,))])
    def scatter_kernel(src_hbm, idx_hbm, out_hbm, idx_v, rows_v, rsem, wsem):
        wid = lax.axis_index("s") * SC_CORES + lax.axis_index("c")
        base = pl.multiple_of(wid * per_w, 8)
        for k in range(k_dup):
            pltpu.sync_copy(idx_hbm.at[k, wid], idx_v.at[k])

        def read(g, b):
            rows = src_hbm.at[pl.ds(pl.multiple_of(base + g * w, 8), w)]
            return pltpu.make_async_copy(rows, rows_v.at[b], rsem.at[b])

        def write(g, b, k):
            return pltpu.make_async_copy(rows_v.at[b], out_hbm.at[idx_v.at[k, g]], wsem.at[b])

        read(0, 0).start()

        @pl.loop(0, n_chunks, step=2)
        def _(g):
            for b in (0, 1):
                cur = g + b
                read(cur, b).wait()
                for k in range(k_dup):
                    write(cur, b, k).start()

                @pl.when(cur >= 1)
                def _():
                    for k in range(k_dup):
                        write(cur - 1, 1 - b, k).wait()

                @pl.when(cur + 1 < n_chunks)
                def _():
                    read(cur + 1, 1 - b).start()

        for k in range(k_dup):
            write(n_chunks - 1, 1, k).wait()

    return scatter_kernel(src, idx.reshape(k_dup, workers, n_chunks, w))


def _expert_kernel(be_ref, used_ref, x_ref, wg_ref, bg_ref, wu_ref, bu_ref, wd_ref, bd_ref, o_ref, wg_bf, wu_bf,
                   wd_bf):
    i = pl.program_id(0)

    @pl.when(jnp.logical_or(i == 0, be_ref[i] != be_ref[jnp.maximum(i - 1, 0)]))
    def _():
        wg_bf[...] = wg_ref[0, 0].astype(BF16)
        wu_bf[...] = wu_ref[0, 0].astype(BF16)
        wd_bf[...] = wd_ref[0, 0].astype(BF16)

    @pl.when(i < used_ref[0])
    def _():
        xb = _unpack_bf16_halves(x_ref[...]).astype(BF16)
        g = _dot(xb, wg_bf[...]) + bg_ref[0, 0]
        u = _dot(xb, wu_bf[...]) + bu_ref[0, 0]
        g = jnp.minimum(g, SWIGLU_LIMIT)
        u = jnp.clip(u, -SWIGLU_LIMIT, SWIGLU_LIMIT)
        act = (g * _sigmoid(SWIGLU_ALPHA * g) * (u + 1.0)).astype(BF16)
        o_ref[...] = _pack_bf16_halves(_dot(act, wd_bf[...]) + bd_ref[0, 0])


def expert_ffn(block_expert, blocks_used, x_sorted, layer, wg, bg, wu, bu, wd, bd):
    nblk = block_expert.shape[0]
    n_layers, _, d, f = wg.shape
    wspec = lambda r, c: pl.BlockSpec((1, 1, r, c), lambda i, be, used: (layer, be[i], 0, 0))
    grid_spec = pltpu.PrefetchScalarGridSpec(
        num_scalar_prefetch=2,
        grid=(nblk,),
        in_specs=[pl.BlockSpec((MOE_BLOCK, d // 2), lambda i, be, used: (i, 0)),
                  wspec(d, f), wspec(1, f), wspec(d, f), wspec(1, f), wspec(f, d), wspec(1, d)],
        out_specs=pl.BlockSpec((MOE_BLOCK, d // 2), lambda i, be, used: (i, 0)),
        scratch_shapes=[pltpu.VMEM((d, f), BF16), pltpu.VMEM((d, f), BF16), pltpu.VMEM((f, d), BF16)],
    )
    return pl.pallas_call(
        _expert_kernel,
        out_shape=jax.ShapeDtypeStruct((nblk * MOE_BLOCK, d // 2), U32),
        grid_spec=grid_spec,
        compiler_params=_cparams(("arbitrary",)),
        name="expert_ffn",
    )(block_expert, blocks_used, x_sorted, wg, bg.reshape(n_layers, N_EXP, 1, f), wu,
      bu.reshape(n_layers, N_EXP, 1, f), wd, bd.reshape(n_layers, N_EXP, 1, d))


def _combine_kernel(x_ref, mod_ref, prob_ref, y_ref, fw_ref, o_ref, *, final_norm):
    p = prob_ref[0]
    acc = _unpack_bf16_halves(y_ref[0, 0]) * p[:, 0:1]
    for k in range(1, TOP_K):
        acc = acc + _unpack_bf16_halves(y_ref[k, 0]) * p[:, k:k + 1]
    out = x_ref[0] + mod_ref[0, 0][5:6] * acc
    if final_norm:
        out = out * lax.rsqrt(jnp.mean(out * out, axis=-1, keepdims=True) + EPS) * fw_ref[...]
    o_ref[0] = out


def combine(x, modtab, probs, y_tok, final_w, ctx_len, final_norm):
    bsz, lt, d = x.shape
    tm = TOK_TILE
    ctx_tiles = ctx_len // tm
    return pl.pallas_call(
        functools.partial(_combine_kernel, final_norm=final_norm),
        out_shape=jax.ShapeDtypeStruct((bsz, lt, d), F32),
        grid=(bsz, lt // tm),
        in_specs=[pl.BlockSpec((1, tm, d), lambda b, i: (b, i, 0)),
                  pl.BlockSpec((1, 1, N_MOD, d), lambda b, i: (b, (i >= ctx_tiles).astype(I32), 0, 0)),
                  pl.BlockSpec((1, tm, TOP_K), lambda b, i: (b, i, 0)),
                  pl.BlockSpec((TOP_K, 1, tm, d // 2), lambda b, i: (0, b, i, 0)),
                  pl.BlockSpec((1, d), lambda b, i: (0, 0))],
        out_specs=pl.BlockSpec((1, tm, d), lambda b, i: (b, i, 0)),
        compiler_params=_cparams(("parallel", "parallel")),
        name="combine",
    )(x, modtab, probs, y_tok, final_w.reshape(1, d))


def moe_layer(x, modtab, layer, nw, rw, rb, wg, bg, wu, bu, wd, bd, final_w, ctx_len, final_norm):
    bsz, lt, d = x.shape
    t = bsz * lt
    h, idx, probs, rank, counts = router(x, modtab, nw, rw, rb, ctx_len)
    counts = counts.reshape(N_EXP).astype(I32)
    padded = (counts + MOE_BLOCK - 1) // MOE_BLOCK * MOE_BLOCK
    ends = jnp.cumsum(padded)
    pstart = ends - padded
    idx = idx.reshape(t, TOP_K)
    dest = pstart[idx] + rank.reshape(t, TOP_K)
    nblk = -(-t * TOP_K // MOE_BLOCK) + N_EXP
    block_row0 = jnp.arange(nblk, dtype=I32) * MOE_BLOCK
    block_expert = jnp.minimum(jnp.sum((ends[None, :] <= block_row0[:, None]).astype(I32), axis=1), N_EXP - 1)
    dest_k = dest.T
    x_sorted = sc_scatter_rows(h.reshape(t, d // 2), dest_k, nblk * MOE_BLOCK)
    blocks_used = (ends[-1:] // MOE_BLOCK).astype(I32)
    y_sorted = expert_ffn(block_expert, blocks_used, x_sorted, layer, wg, bg, wu, bu, wd, bd)
    y_tok = sc_gather_rows(y_sorted, dest_k.reshape(TOP_K * t)).reshape(TOP_K, bsz, lt, d // 2)
    return combine(x, modtab, probs, y_tok, final_w, ctx_len, final_norm)


def even_layer(x, modtab, norm1_w, w_in, w_out, mu, w0, w2, a0, a2, g2, k_k, k_a, r_k, ln_w, ln_b, lb, hg_norm_w,
               ctx_len):
    w_in = w_in.astype(BF16)
    pr, ph = in_proj(x, modtab, norm1_w, [w_in[:, :RW_PROJ], w_in[:, RW_PROJ:]], ctx_len)
    r, v, kk, g, bv, lw, km, bb = rwkv_prep(pr, mu, w0, w2, a0, a2, g2, k_k, k_a, r_k, ctx_len)
    o_r = rwkv_scan(r, v, kk, lw, km, bb, ctx_len)
    o_g = gla_scan(ph, lb, ctx_len)
    return even_out(x, modtab, o_r, bv, g, o_g, ph, ln_w, ln_b, hg_norm_w, w_out.astype(BF16), ctx_len)


def _dir_slab(cols, width):
    halves = []
    for z in range(2):
        parts = [c[:, z * (c.shape[1] // 2):(z + 1) * (c.shape[1] // 2)] for c in cols]
        used = sum(p.shape[1] for p in parts)
        parts.append(jnp.zeros((cols[0].shape[0], width - used), cols[0].dtype))
        halves.append(jnp.concatenate(parts, axis=1))
    return jnp.concatenate(halves, axis=1)


def _dir_rows(vals, width):
    rows = jnp.concatenate(vals, axis=1)
    return jnp.pad(rows, ((0, 0), (0, width - rows.shape[1]))).reshape(2, 1, width)


def odd_layer(x, modtab, norm1_w, w_in, w_out, s_conv_w, s_conv_b, dt_bias, a_log, d_skip, s_norm_w, m_conv_w,
              m_conv_b, i_bias, f_bias, m_norm_w, ctx_len):
    o = 0
    cols = {}
    for name, width in (("zg", SSD_INNER), ("xbc", SSD_CONV_CH), ("dt", 2 * SSD_H), ("qk", 2 * ML_QK),
                        ("v", ML_VW), ("og", ML_VW), ("ig", 2 * ML_H), ("fg", 2 * ML_H)):
        cols[name] = w_in[:, o:o + width]
        o += width
    slab_m = jnp.concatenate([cols["qk"], cols["v"], cols["og"]], axis=1).astype(BF16)
    slab_small = _dir_slab([cols["dt"], cols["ig"], cols["fg"]], SMALL_W).astype(BF16)
    pxbc, zg, pm, small = in_proj(x, modtab, norm1_w, [cols["xbc"].astype(BF16), cols["zg"].astype(BF16), slab_m,
                                                       slab_small], ctx_len)
    xbc, qk = odd_prep(pxbc, pm, s_conv_w, s_conv_b, m_conv_w, m_conv_b, ctx_len)
    zeros_g = jnp.zeros((2, 2 * ML_H), F32)
    y_s = ssd_scan(xbc, small, _dir_rows([dt_bias, zeros_g], SMALL_W), _dir_rows([a_log, zeros_g], SMALL_W), ctx_len)
    y_m = mlstm_scan(qk, pm, small, _dir_rows([jnp.zeros((2, SSD_H), F32), i_bias, f_bias], SMALL_W), ctx_len)
    return odd_out(x, modtab, y_s, xbc, zg, y_m, pm, jnp.repeat(d_skip, SSD_P), s_norm_w, m_norm_w,
                   w_out.astype(BF16), ctx_len)


def _to_col_major(t):
    b, s, ch = t.shape
    return t.reshape(b, s // GRID_W, GRID_W, ch).transpose(0, 2, 1, 3).reshape(b, s, ch)


def _to_row_major(t):
    b, s, ch = t.shape
    return t.reshape(b, GRID_W, s // GRID_W, ch).transpose(0, 2, 1, 3).reshape(b, s, ch)


def kernel(x, c, ctx, c_ctx, ada_w, ada_b, norm1_w, norm2_w, even_w_in, even_w_out, rwkv_mu, rwkv_w0, rwkv_w2, rwkv_a0, rwkv_a2, rwkv_g2, rwkv_k_k, rwkv_k_a, rwkv_r_k, rwkv_ln_w, rwkv_ln_b, hgrn_lower_bounds, hgrn_norm_w, odd_w_in, odd_w_out, ssd_conv_w, ssd_conv_b, ssd_dt_bias, ssd_a_log, ssd_d, ssd_norm_w, mlstm_conv_w, mlstm_conv_b, mlstm_i_bias, mlstm_f_bias, mlstm_norm_w, router_w, router_b, exp_w_gate, exp_b_gate, exp_w_up, exp_b_up, exp_w_down, exp_b_down, final_norm_w):
    lc = ctx.shape[1]
    lower_bounds = jnp.cumsum(jax.nn.softmax(hgrn_lower_bounds.astype(F32), axis=0), axis=0)
    moe = lambda l: (l, norm2_w[l], router_w[l], router_b[l], exp_w_gate, exp_b_gate, exp_w_up, exp_b_up, exp_w_down,
                     exp_b_down, final_norm_w)
    xa = jnp.concatenate([ctx, x], axis=1)
    modtab = _mod_table(c, c_ctx, ada_w[0], ada_b[0])
    xa = even_layer(xa, modtab, norm1_w[0], even_w_in[0], even_w_out[0], rwkv_mu[0], rwkv_w0[0], rwkv_w2[0],
                    rwkv_a0[0], rwkv_a2[0], rwkv_g2[0], rwkv_k_k[0], rwkv_k_a[0], rwkv_r_k[0], rwkv_ln_w[0],
                    rwkv_ln_b[0], lower_bounds[0], hgrn_norm_w[0], lc)
    xa = moe_layer(xa, modtab, *moe(0), lc, False)
    xa = jnp.concatenate([xa[:, :lc], _to_col_major(xa[:, lc:])], axis=1)
    modtab = _mod_table(c, c_ctx, ada_w[1], ada_b[1])
    xl = odd_layer(xa, modtab, norm1_w[1], odd_w_in[0], odd_w_out[0], ssd_conv_w[0], ssd_conv_b[0], ssd_dt_bias[0],
                   ssd_a_log[0], ssd_d[0], ssd_norm_w[0], mlstm_conv_w[0], mlstm_conv_b[0], mlstm_i_bias[0],
                   mlstm_f_bias[0], mlstm_norm_w[0], lc)
    xl = moe_layer(xl, modtab, *moe(1), 0, True)
    return _to_row_major(xl)
```
